```python
import jax, jax.numpy as jnp
from jax import lax
import numpy as np

D_MODEL = 1024
BATCH = 2
SEQ = 16384
DEPTH = 2
DEC_BATCH = 32
DEC_SEQ = 16
PAST_LEN = 1024

CHUNK = 64
Q_BLOCK = 128
RET_HEADS = 4
RET_DK = 64
RET_DV = 128
ROPE_BASE = 10000.0
RWKV_HEADS = 8
RWKV_HD = 64
RWKV_W_LORA = 64
RWKV_A_LORA = 64
RWKV_G_LORA = 128
MLSTM_HEADS = 4
MLSTM_DK = 64
MLSTM_DV = 128
MLSTM_CONV = 4
FOX_HEADS = 4
FOX_HD = 128
MEM_LEN = 256
MEM_HEADS = 4
MEM_HD = D_MODEL // MEM_HEADS
N_EXPERTS = 16
N_GROUPS = 4
EXPERTS_PER_GROUP = N_EXPERTS // N_GROUPS
TOP_K = 2
D_EXPERT = 512
MOE_BLOCK = 128
DEEPNORM_ALPHA = (2 * DEPTH) ** 0.25
DEEPNORM_BETA = (8 * DEPTH) ** -0.25
EPS = 1e-5
NEG_INF = -1e30

RET_W = RET_HEADS * RET_DV
RWKV_W = RWKV_HEADS * RWKV_HD
RET_PROJ = 2 * RET_HEADS * RET_DK + 2 * RET_W
RWKV_PROJ = 3 * RWKV_W + RWKV_W_LORA + RWKV_A_LORA + RWKV_G_LORA
MLSTM_W = MLSTM_HEADS * MLSTM_DV
MLSTM_QK = 2 * MLSTM_HEADS * MLSTM_DK
MLSTM_PROJ = MLSTM_QK + MLSTM_W + 2 * MLSTM_HEADS + MLSTM_W
FOX_W = FOX_HEADS * FOX_HD
FOX_PROJ = 3 * FOX_W + FOX_HEADS

kernel_name = 'hybrid_streaming_encoder_step'

F32 = jnp.float32


def layer_norm(x, g, b):
    xf = x.astype(F32)
    mu = xf.mean(-1, keepdims=True)
    var = jnp.square(xf - mu).mean(-1, keepdims=True)
    return ((xf - mu) * lax.rsqrt(var + EPS) * g + b).astype(x.dtype)


def head_norm(x, g, b=None):
    xf = x.astype(F32)
    mu = xf.mean(-1, keepdims=True)
    var = jnp.square(xf - mu).mean(-1, keepdims=True)
    y = ((xf - mu) * lax.rsqrt(var + EPS)).reshape(x.shape[:2] + (-1,)) * g
    return y if b is None else y + b


def rms_norm(x, g):
    xf = x.astype(F32)
    return xf * lax.rsqrt(jnp.square(xf).mean(-1, keepdims=True) + EPS) * g


def rope(x, pos):
    half = x.shape[-1] // 2
    inv = ROPE_BASE ** (-jnp.arange(half, dtype=F32) / half)
    ang = pos.astype(F32)[:, None] * inv[None, :]
    cos = jnp.cos(ang)[None, :, None, :]
    sin = jnp.sin(ang)[None, :, None, :]
    x1 = x[..., :half].astype(F32)
    x2 = x[..., half:].astype(F32)
    return jnp.concatenate([x1 * cos - x2 * sin, x1 * sin + x2 * cos], axis=-1)


def scan_chunks(step, carry, xs):
    B, T = xs[0].shape[:2]
    n = T // CHUNK

    def split(a):
        return jnp.swapaxes(a.reshape((B, n, CHUNK) + a.shape[2:]), 0, 1)

    carry, ys = lax.scan(lambda c, xc: step(c, *xc), carry, tuple(split(a) for a in xs))
    return carry, jnp.swapaxes(ys, 0, 1).reshape((B, T) + ys.shape[3:])


def retention_chunk(S, q, k, v):
    L = q.shape[1]
    lg = jnp.log(1.0 - 2.0 ** (-5.0 - jnp.arange(RET_HEADS, dtype=F32)))
    i = jnp.arange(L, dtype=F32)
    rel = i[:, None] - i[None, :]
    decay = jnp.where(rel >= 0, jnp.exp(jnp.maximum(rel, 0.0)[None] * lg[:, None, None]), 0.0)
    q, k, v, S = q.astype(F32), k.astype(F32), v.astype(F32), S.astype(F32)
    inner = jnp.einsum('bhlm,bmhe->blhe', jnp.einsum('blhd,bmhd->bhlm', q, k) * decay[None], v)
    q_dec = jnp.exp((i + 1.0)[:, None] * lg[None, :])
    cross = jnp.einsum('blhd,bhde->blhe', q, S) * q_dec[None, :, :, None]
    k_dec = jnp.exp((L - 1.0 - i)[:, None] * lg[None, :])
    S_new = jnp.exp(L * lg)[None, :, None, None] * S + jnp.einsum('blhd,blhe->bhde', k * k_dec[None, :, :, None], v)
    return S_new, inner + cross


def rwkv7_scan(S, r, w, k, v, kk, a):
    def step(S, inp):
        r_t, w_t, k_t, v_t, kk_t, a_t = inp
        sa = jnp.einsum('bhvk,bhk->bhv', S, -kk_t)
        S = S * w_t[:, :, None, :] + sa[..., None] * (kk_t * a_t)[:, :, None, :] + v_t[..., None] * k_t[:, :, None, :]
        return S, jnp.einsum('bhvk,bhk->bhv', S, r_t)

    xs = tuple(jnp.swapaxes(t.astype(F32), 0, 1) for t in (r, w, k, v, kk, a))
    S, ys = lax.scan(step, S.astype(F32), xs)
    return S, jnp.swapaxes(ys, 0, 1)


def mlstm_chunk(state, q, k, v, ig, lf):
    C, n, m = (s.astype(F32) for s in state)
    q, k, v = q.astype(F32), k.astype(F32), v.astype(F32)
    L = q.shape[1]
    igh = jnp.swapaxes(ig.astype(F32), 1, 2)
    b = jnp.cumsum(jnp.swapaxes(lf.astype(F32), 1, 2), axis=-1)
    causal = jnp.tril(jnp.ones((L, L), dtype=bool))
    logD = jnp.where(causal, b[..., :, None] - b[..., None, :] + igh[..., None, :], NEG_INF)
    m_inter = b + m[..., None]
    m_t = jnp.maximum(m_inter, logD.max(-1))
    wD = jnp.exp(logD - m_t[..., None])
    wI = jnp.exp(m_inter - m_t)
    s = jnp.einsum('blhd,bshd->bhls', q, k) * wD
    num = jnp.einsum('bhls,bshe->blhe', s, v) + jnp.einsum('blhd,bhde->blhe', q, C) * jnp.swapaxes(wI, 1, 2)[..., None]
    den = s.sum(-1) + wI * jnp.einsum('blhd,bhd->bhl', q, n)
    denom = jnp.maximum(jnp.abs(den), jnp.exp(-m_t))
    h = num / jnp.swapaxes(denom, 1, 2)[..., None]
    m_new = m_t[..., -1]
    bL = b[..., -1]
    wS = jnp.exp(bL + m - m_new)
    wK = jnp.exp(bL[..., None] - b + igh - m_new[..., None])
    C_new = wS[..., None, None] * C + jnp.einsum('blhd,blhe,bhl->bhde', k, v, wK)
    n_new = wS[..., None] * n + jnp.einsum('blhd,bhl->bhd', k, wK)
    return (C_new, n_new, m_new), h


def fox_block(q, c_q, q_pos, k, v, c_k, k_pos):
    logits = jnp.einsum('bqhd,bkhd->bhqk', q.astype(F32), k.astype(F32)) * FOX_HD ** -0.5
    logits = logits + jnp.swapaxes(c_q, 1, 2)[..., :, None] - jnp.swapaxes(c_k, 1, 2)[..., None, :]
    logits = jnp.where(k_pos[None, :] <= q_pos[:, None], logits, NEG_INF)
    p = jax.nn.softmax(logits, axis=-1)
    return jnp.einsum('bhqk,bkhd->bqhd', p, v.astype(F32))


def fox_prompt(q, k, v, logf):
    B, T = q.shape[:2]
    c = jnp.cumsum(logf, axis=1)
    k_pos = jnp.arange(T)
    k32, v32 = k.astype(F32), v.astype(F32)

    def block(s):
        q_b = lax.dynamic_slice_in_dim(q, s, Q_BLOCK, axis=1)
        c_b = lax.dynamic_slice_in_dim(c, s, Q_BLOCK, axis=1)
        return fox_block(q_b, c_b, s + jnp.arange(Q_BLOCK), k32, v32, c, k_pos)

    o = lax.map(block, jnp.arange(T // Q_BLOCK) * Q_BLOCK)
    return jnp.swapaxes(o, 0, 1).reshape(B, T, FOX_HEADS, FOX_HD)


def mixer_ab(x, pos0, ret_S, rwkv_S, shift_prev, P, is_prompt):
    B, T, _ = x.shape
    h = x @ P['w_in0']
    hA, hB = h[..., :RET_PROJ], h[..., RET_PROJ:]
    qk_w = RET_HEADS * RET_DK
    pos = pos0 + jnp.arange(T)
    qA = rope(hA[..., :qk_w].reshape(B, T, RET_HEADS, RET_DK), pos)
    kA = rope(hA[..., qk_w:2 * qk_w].reshape(B, T, RET_HEADS, RET_DK), pos) * RET_DK ** -0.5
    vA = hA[..., 2 * qk_w:2 * qk_w + RET_W].reshape(B, T, RET_HEADS, RET_DV)
    gA = hA[..., 2 * qk_w + RET_W:]
    if is_prompt:
        ret_S, ret_o = scan_chunks(retention_chunk, ret_S, (qA, kA, vA))
    else:
        ret_S, ret_o = retention_chunk(ret_S, qA, kA, vA)
    out_a = jax.nn.silu(gA.astype(F32)) * head_norm(ret_o, P['ret_gn_g'])
    prev = jnp.concatenate([shift_prev.astype(hB.dtype), hB[:, :-1]], axis=1)
    xs = hB + (prev - hB) * P['rwkv_mu']
    new_shift = hB[:, -1:]
    o1, o2, o3 = RWKV_W, 2 * RWKV_W, 3 * RWKV_W
    r, kB, vB = xs[..., :o1], xs[..., o1:o2], xs[..., o2:o3]
    w_lo = xs[..., o3:o3 + RWKV_W_LORA]
    a_lo = xs[..., o3 + RWKV_W_LORA:o3 + RWKV_W_LORA + RWKV_A_LORA]
    g_lo = xs[..., o3 + RWKV_W_LORA + RWKV_A_LORA:]
    w_log = -jax.nn.softplus(-(P['rwkv_w0'] + jnp.tanh(w_lo) @ P['rwkv_w2']).astype(F32)) - 0.5
    w = jnp.exp(-jnp.exp(w_log))
    a = jax.nn.sigmoid((P['rwkv_a0'] + a_lo @ P['rwkv_a2']).astype(F32))
    g = jax.nn.sigmoid(g_lo) @ P['rwkv_g2']

    def heads(t):
        return t.reshape(B, T, RWKV_HEADS, RWKV_HD)

    kk = heads((kB * P['rwkv_k_k']).astype(F32))
    kk = kk / jnp.maximum(jnp.linalg.norm(kk, axis=-1, keepdims=True), 1e-12)
    kB = kB.astype(F32) * (1.0 + (a - 1.0) * P['rwkv_k_a'])
    r, kB, vB, w, a = heads(r.astype(F32)), heads(kB), heads(vB.astype(F32)), heads(w), heads(a)
    rwkv_S, y = rwkv7_scan(rwkv_S, r, w, kB, vB, kk, a)
    bonus = jnp.sum(r * kB * P['rwkv_r_k'], axis=-1, keepdims=True) * vB
    out_b = (head_norm(y, P['rwkv_gn_g'], P['rwkv_gn_b']) + bonus.reshape(B, T, RWKV_W)) * g
    out = jnp.concatenate([out_a, out_b], axis=-1).astype(x.dtype) @ P['w_out0']
    return out, ret_S, rwkv_S, new_shift


def mixer_cd(x, conv_prev, mlstm_st, fox_cache, P, is_prompt):
    B, T, _ = x.shape
    h = x @ P['w_in1']
    hC, hD = h[..., :MLSTM_PROJ], h[..., MLSTM_PROJ:]
    qk_raw = hC[..., :MLSTM_QK]
    vC = hC[..., MLSTM_QK:MLSTM_QK + MLSTM_W].reshape(B, T, MLSTM_HEADS, MLSTM_DV)
    off = MLSTM_QK + MLSTM_W
    i_pre = hC[..., off:off + MLSTM_HEADS]
    f_pre = hC[..., off + MLSTM_HEADS:off + 2 * MLSTM_HEADS]
    o_pre = hC[..., off + 2 * MLSTM_HEADS:]
    xpad = jnp.concatenate([conv_prev.astype(qk_raw.dtype), qk_raw], axis=1)
    conv = P['mlstm_conv_b'].astype(F32)
    for j in range(MLSTM_CONV):
        conv = conv + xpad[:, j:j + T].astype(F32) * P['mlstm_conv_w'][j]
    new_conv = xpad[:, T:]
    qk = jax.nn.silu(conv)
    qC = qk[..., :MLSTM_QK // 2].reshape(B, T, MLSTM_HEADS, MLSTM_DK) * MLSTM_DK ** -0.5
    kC = qk[..., MLSTM_QK // 2:].reshape(B, T, MLSTM_HEADS, MLSTM_DK)
    ig = (i_pre + P['mlstm_b_i']).astype(F32)
    lf = jax.nn.log_sigmoid((f_pre + P['mlstm_b_f']).astype(F32))
    if is_prompt:
        mlstm_st, hm = scan_chunks(mlstm_chunk, mlstm_st, (qC, kC, vC, ig, lf))
    else:
        mlstm_st, hm = mlstm_chunk(mlstm_st, qC, kC, vC, ig, lf)
    out_c = head_norm(hm, P['mlstm_gn_g']) * jax.nn.sigmoid(o_pre.astype(F32))
    qD = rms_norm(hD[..., :FOX_W].reshape(B, T, FOX_HEADS, FOX_HD), P['fox_q_g'])
    kD = rms_norm(hD[..., FOX_W:2 * FOX_W].reshape(B, T, FOX_HEADS, FOX_HD), P['fox_k_g'])
    vD = hD[..., 2 * FOX_W:3 * FOX_W].reshape(B, T, FOX_HEADS, FOX_HD)
    logf = jax.nn.log_sigmoid((hD[..., 3 * FOX_W:] + P['fox_b_f']).astype(F32))
    if is_prompt:
        out_d = fox_prompt(qD, kD, vD, logf)
    else:
        k_cache, v_cache, lf_cache = fox_cache
        past = k_cache.shape[1]
        k_all = jnp.concatenate([k_cache.astype(F32), kD], axis=1)
        v_all = jnp.concatenate([v_cache.astype(F32), vD.astype(F32)], axis=1)
        c = jnp.cumsum(jnp.concatenate([lf_cache.astype(F32), logf], axis=1), axis=1)
        out_d = fox_block(qD, c[:, past:], past + jnp.arange(T), k_all, v_all, c, jnp.arange(past + T))
    out = jnp.concatenate([out_c, out_d.reshape(B, T, FOX_W)], axis=-1).astype(x.dtype) @ P['w_out1']
    return out, new_conv, mlstm_st, kD, vD, logf


def mem_attend(x, mk, mv, wq, wo):
    B, T, _ = x.shape
    q = (x @ wq).reshape(B, T, MEM_HEADS, MEM_HD)
    logits = jnp.einsum('bthd,bmhd->bhtm', q.astype(F32), mk.astype(F32)) * MEM_HD ** -0.5
    p = jax.nn.softmax(logits, axis=-1)
    o = jnp.einsum('bhtm,bmhd->bthd', p, mv.astype(F32)).reshape(B, T, D_MODEL)
    return o.astype(x.dtype) @ wo


def moe_ffn(x, w_router, b_router, wg, wu, wd):
    B, T, D = x.shape
    xt = x.reshape(B * T, D)
    n_tok = B * T
    probs = jax.nn.softmax((xt @ w_router).astype(F32) + b_router, axis=-1)
    pg = probs.reshape(n_tok, N_GROUPS, EXPERTS_PER_GROUP)
    g_score = lax.top_k(pg, 2)[0].sum(-1)
    g_sel = jnp.argmax(g_score, axis=-1)
    p_in = jnp.einsum('ngk,ng->nk', pg, jax.nn.one_hot(g_sel, N_GROUPS, dtype=F32))
    top_v, top_i = lax.top_k(p_in, TOP_K)
    gates = top_v / top_v.sum(-1, keepdims=True)
    e_flat = (g_sel[:, None] * EXPERTS_PER_GROUP + top_i).reshape(-1)
    tok_flat = jnp.repeat(jnp.arange(n_tok, dtype=jnp.int32), TOP_K)
    gate_flat = gates.reshape(-1)
    n_assign = n_tok * TOP_K
    order = jnp.argsort(e_flat)
    e_s, tok_s, gate_s = e_flat[order], tok_flat[order], gate_flat[order]
    counts = jnp.zeros((N_EXPERTS,), jnp.int32).at[e_flat].add(1)
    starts = jnp.cumsum(counts) - counts
    padded = (counts + MOE_BLOCK - 1) // MOE_BLOCK * MOE_BLOCK
    p_ends = jnp.cumsum(padded)
    p_starts = p_ends - padded
    dest = p_starts[e_s] + jnp.arange(n_assign, dtype=jnp.int32) - starts[e_s]
    n_blocks = -(-n_assign // MOE_BLOCK) + N_EXPERTS
    buf_tok = jnp.zeros((n_blocks * MOE_BLOCK,), jnp.int32).at[dest].set(tok_s)
    buf_gate = jnp.zeros((n_blocks * MOE_BLOCK,), F32).at[dest].set(gate_s)
    blk_exp = jnp.minimum(jnp.searchsorted(p_ends, jnp.arange(n_blocks, dtype=jnp.int32) * MOE_BLOCK, side='right'), N_EXPERTS - 1)

    def expert_block(args):
        tok_b, gate_b, e_b = args
        xb = xt[tok_b]
        hb = jax.nn.silu(xb @ wg[e_b]) * (xb @ wu[e_b])
        return (hb @ wd[e_b]) * gate_b[:, None].astype(x.dtype)

    yb = lax.map(expert_block, (buf_tok.reshape(n_blocks, MOE_BLOCK), buf_gate.reshape(n_blocks, MOE_BLOCK), blk_exp))
    y = jnp.zeros_like(xt).at[buf_tok].add(yb.reshape(-1, D))
    return y.reshape(B, T, D)


def trunk(x, pos0, mem_k, mem_v, states, P, is_prompt):
    ret_S, rwkv_S, shift, mC, mn, mm, conv, fk, fv, flf = states
    fk_new, fv_new, flf_new = fk, fv, flf
    for l in range(DEPTH):
        if l % 2 == 0:
            mix, ret_S, rwkv_S, shift = mixer_ab(x, pos0, ret_S, rwkv_S, shift, P, is_prompt)
        else:
            mix, conv, (mC, mn, mm), fk_new, fv_new, flf_new = mixer_cd(x, conv, (mC, mn, mm), (fk, fv, flf), P, is_prompt)
        x = layer_norm(DEEPNORM_ALPHA * x + mix, P['ln_g'][l, 0], P['ln_b'][l, 0])
        att = mem_attend(x, mem_k[l], mem_v[l], P['w_mem_q'][l], P['w_mem_o'][l])
        x = layer_norm(DEEPNORM_ALPHA * x + att, P['ln_g'][l, 1], P['ln_b'][l, 1])
        ffn = moe_ffn(x, P['w_router'], P['b_router'], P['w_exp_gate'][l], P['w_exp_up'][l], P['w_exp_down'][l])
        x = layer_norm(DEEPNORM_ALPHA * x + ffn, P['ln_g'][l, 2], P['ln_b'][l, 2])
    return x, ret_S, rwkv_S, shift, mC, mn, mm, conv, fk_new, fv_new, flf_new


def setup_inputs(seed: int = 0) -> dict:
    key = jax.random.key(seed)
    ks = iter(jax.random.split(key, 64))

    def nrm(shape, scale=1.0):
        return scale * jax.random.normal(next(ks), shape, F32)

    def uni(shape, lo, hi):
        return jax.random.uniform(next(ks), shape, F32, lo, hi)

    D = D_MODEL
    beta = DEEPNORM_BETA
    return {
        'x_prompt': nrm((BATCH, SEQ, D)),
        'x_sample': nrm((DEC_BATCH, DEC_SEQ, D)),
        'mem_prompt': nrm((BATCH, MEM_LEN, D)),
        'state_ret': nrm((DEC_BATCH, RET_HEADS, RET_DK, RET_DV), 0.5),
        'state_rwkv': nrm((DEC_BATCH, RWKV_HEADS, RWKV_HD, RWKV_HD), 0.3),
        'cache_rwkv_shift': nrm((DEC_BATCH, 1, RWKV_PROJ)),
        'state_mlstm_c': nrm((DEC_BATCH, MLSTM_HEADS, MLSTM_DK, MLSTM_DV), 0.5),
        'state_mlstm_n': nrm((DEC_BATCH, MLSTM_HEADS, MLSTM_DK), 0.5),
        'state_mlstm_m': nrm((DEC_BATCH, MLSTM_HEADS)),
        'cache_mlstm_conv': nrm((DEC_BATCH, MLSTM_CONV - 1, MLSTM_QK)),
        'cache_fox_k': nrm((DEC_BATCH, PAST_LEN, FOX_HEADS, FOX_HD)),
        'cache_fox_v': nrm((DEC_BATCH, PAST_LEN, FOX_HEADS, FOX_HD)),
        'cache_fox_logf': jax.nn.log_sigmoid(4.0 + nrm((DEC_BATCH, PAST_LEN, FOX_HEADS))),
        'cache_mem_k': nrm((DEPTH, DEC_BATCH, MEM_LEN, MEM_HEADS, MEM_HD)),
        'cache_mem_v': nrm((DEPTH, DEC_BATCH, MEM_LEN, MEM_HEADS, MEM_HD)),
        'w_in0': nrm((D, RET_PROJ + RWKV_PROJ), D ** -0.5),
        'ret_gn_g': 1.0 + nrm((RET_W,), 0.1),
        'rwkv_mu': uni((RWKV_PROJ,), 0.0, 1.0),
        'rwkv_w0': uni((RWKV_W,), -6.0, -1.0),
        'rwkv_w2': nrm((RWKV_W_LORA, RWKV_W), 0.1),
        'rwkv_a0': nrm((RWKV_W,), 0.1),
        'rwkv_a2': nrm((RWKV_A_LORA, RWKV_W), RWKV_A_LORA ** -0.5),
        'rwkv_g2': nrm((RWKV_G_LORA, RWKV_W), RWKV_G_LORA ** -0.5),
        'rwkv_k_k': 1.0 + nrm((RWKV_W,), 0.1),
        'rwkv_k_a': 1.0 + nrm((RWKV_W,), 0.1),
        'rwkv_r_k': nrm((RWKV_HEADS, RWKV_HD), 0.1),
        'rwkv_gn_g': 1.0 + nrm((RWKV_W,), 0.1),
        'rwkv_gn_b': nrm((RWKV_W,), 0.01),
        'w_out0': nrm((RET_W + RWKV_W, D), (RET_W + RWKV_W) ** -0.5 * beta),
        'w_in1': nrm((D, MLSTM_PROJ + FOX_PROJ), D ** -0.5),
        'mlstm_conv_w': nrm((MLSTM_CONV, MLSTM_QK), MLSTM_CONV ** -0.5),
        'mlstm_conv_b': nrm((MLSTM_QK,), 0.01),
        'mlstm_b_i': nrm((MLSTM_HEADS,), 0.1),
        'mlstm_b_f': uni((MLSTM_HEADS,), 3.0, 6.0),
        'mlstm_gn_g': 1.0 + nrm((MLSTM_W,), 0.1),
        'fox_q_g': 1.0 + nrm((FOX_HD,), 0.1),
        'fox_k_g': 1.0 + nrm((FOX_HD,), 0.1),
        'fox_b_f': uni((FOX_HEADS,), 3.0, 5.0),
        'w_out1': nrm((MLSTM_W + FOX_W, D), (MLSTM_W + FOX_W) ** -0.5 * beta),
        'w_mem_q': nrm((DEPTH, D, D), D ** -0.5),
        'w_mem_k': nrm((DEPTH, D, D), D ** -0.5),
        'w_mem_v': nrm((DEPTH, D, D), D ** -0.5),
        'w_mem_o': nrm((DEPTH, D, D), D ** -0.5 * beta),
        'w_router': nrm((D, N_EXPERTS), D ** -0.5),
        'b_router': nrm((N_EXPERTS,), 0.01),
        'w_exp_gate': nrm((DEPTH, N_EXPERTS, D, D_EXPERT), D ** -0.5),
        'w_exp_up': nrm((DEPTH, N_EXPERTS, D, D_EXPERT), D ** -0.5),
        'w_exp_down': nrm((DEPTH, N_EXPERTS, D_EXPERT, D), D_EXPERT ** -0.5 * beta),
        'ln_g': 1.0 + nrm((DEPTH, 3, D), 0.1),
        'ln_b': nrm((DEPTH, 3, D), 0.01),
    }


def reference(x_prompt, x_sample, mem_prompt, state_ret, state_rwkv, cache_rwkv_shift, state_mlstm_c, state_mlstm_n,
              state_mlstm_m, cache_mlstm_conv, cache_fox_k, cache_fox_v, cache_fox_logf, cache_mem_k, cache_mem_v,
              w_in0, ret_gn_g, rwkv_mu, rwkv_w0, rwkv_w2, rwkv_a0, rwkv_a2, rwkv_g2, rwkv_k_k, rwkv_k_a, rwkv_r_k,
              rwkv_gn_g, rwkv_gn_b, w_out0, w_in1, mlstm_conv_w, mlstm_conv_b, mlstm_b_i, mlstm_b_f, mlstm_gn_g,
              fox_q_g, fox_k_g, fox_b_f, w_out1, w_mem_q, w_mem_k, w_mem_v, w_mem_o, w_router, b_router,
              w_exp_gate, w_exp_up, w_exp_down, ln_g, ln_b):
    P = dict(w_in0=w_in0, ret_gn_g=ret_gn_g, rwkv_mu=rwkv_mu, rwkv_w0=rwkv_w0, rwkv_w2=rwkv_w2, rwkv_a0=rwkv_a0,
             rwkv_a2=rwkv_a2, rwkv_g2=rwkv_g2, rwkv_k_k=rwkv_k_k, rwkv_k_a=rwkv_k_a, rwkv_r_k=rwkv_r_k,
             rwkv_gn_g=rwkv_gn_g, rwkv_gn_b=rwkv_gn_b, w_out0=w_out0, w_in1=w_in1, mlstm_conv_w=mlstm_conv_w,
             mlstm_conv_b=mlstm_conv_b, mlstm_b_i=mlstm_b_i, mlstm_b_f=mlstm_b_f, mlstm_gn_g=mlstm_gn_g,
             fox_q_g=fox_q_g, fox_k_g=fox_k_g, fox_b_f=fox_b_f, w_out1=w_out1, w_mem_q=w_mem_q, w_mem_o=w_mem_o,
             w_router=w_router, b_router=b_router, w_exp_gate=w_exp_gate, w_exp_up=w_exp_up,
             w_exp_down=w_exp_down, ln_g=ln_g, ln_b=ln_b)
    B = x_prompt.shape[0]
    M = mem_prompt.shape[1]
    p_mem_k = jnp.einsum('bmd,ldf->lbmf', mem_prompt, w_mem_k).reshape(DEPTH, B, M, MEM_HEADS, MEM_HD)
    p_mem_v = jnp.einsum('bmd,ldf->lbmf', mem_prompt, w_mem_v).reshape(DEPTH, B, M, MEM_HEADS, MEM_HD)
    prompt_states = (jnp.zeros((B, RET_HEADS, RET_DK, RET_DV), F32),
                     jnp.zeros((B, RWKV_HEADS, RWKV_HD, RWKV_HD), F32),
                     jnp.zeros((B, 1, RWKV_PROJ), x_prompt.dtype),
                     jnp.zeros((B, MLSTM_HEADS, MLSTM_DK, MLSTM_DV), F32),
                     jnp.zeros((B, MLSTM_HEADS, MLSTM_DK), F32),
                     jnp.zeros((B, MLSTM_HEADS), F32),
                     jnp.zeros((B, MLSTM_CONV - 1, MLSTM_QK), x_prompt.dtype),
                     None, None, None)
    (y_prompt, p_ret, p_rwkv, p_shift, p_mc, p_mn, p_mm, p_conv, p_fk, p_fv, p_flf) = trunk(
        x_prompt, 0, p_mem_k, p_mem_v, prompt_states, P, True)
    sample_states = (state_ret, state_rwkv, cache_rwkv_shift, state_mlstm_c, state_mlstm_n, state_mlstm_m,
                     cache_mlstm_conv, cache_fox_k, cache_fox_v, cache_fox_logf)
    (y_sample, s_ret, s_rwkv, s_shift, s_mc, s_mn, s_mm, s_conv, s_fk, s_fv, s_flf) = trunk(
        x_sample, cache_fox_k.shape[1], cache_mem_k, cache_mem_v, sample_states, P, False)
    return (y_prompt, y_sample,
            p_ret, p_rwkv, p_shift, p_mc, p_mn, p_mm, p_conv, p_fk, p_fv, p_flf, p_mem_k, p_mem_v,
            s_ret, s_rwkv, s_shift, s_mc, s_mn, s_mm, s_conv, s_fk, s_fv, s_flf)
```

```python
import functools
import math

import numpy as np
import jax
import jax.numpy as jnp
from jax import lax
from jax.experimental import pallas as pl
from jax.experimental.pallas import tpu as pltpu

F32 = jnp.float32
BF16 = jnp.bfloat16

D_MODEL = 1024
DEPTH = 2
RET_HEADS, RET_DK, RET_DV = 4, 64, 128
ROPE_BASE = 10000.0
RWKV_HEADS, RWKV_HD = 8, 64
RWKV_W = RWKV_HEADS * RWKV_HD
RWKV_PROJ = 3 * RWKV_W + 64 + 64 + 128
MLSTM_HEADS, MLSTM_DK, MLSTM_DV, MLSTM_CONV = 4, 64, 128, 4
FOX_HEADS, FOX_HD = 4, 128
MEM_HEADS, MEM_HD = 4, 256
N_EXPERTS, N_GROUPS, EXPERTS_PER_GROUP = 16, 4, 4
D_EXPERT = 512
ALPHA = (2 * DEPTH) ** 0.25
EPS = 1e-5
NEG_INF = -1e30

VMEM_LIMIT_BYTES = 56 * 1024 * 1024
ROW_TILE = 512
RET_CHUNK = 256
RWKV_CHUNK = 64
RWKV_ROWS = 512
FOX_TQ = 512
MOE_BLOCK = 256
PROMPT_PREC0 = False


def _cparams(sem):
    return pltpu.CompilerParams(dimension_semantics=sem, vmem_limit_bytes=VMEM_LIMIT_BYTES)


def _tile(n, pref):
    if n <= pref:
        return n
    t = pref
    while t >= 8:
        if n % t == 0:
            return t
        t -= 8
    return n


_NN = (((1,), (0,)), ((), ()))
_NT = (((1,), (1,)), ((), ()))
_TN = (((0,), (0,)), ((), ()))


def _split2(a):
    a = a.astype(F32)
    hi = a.astype(BF16)
    return hi, (a - hi.astype(F32)).astype(BF16)


def _dg(a, b, dims, prec):
    if not prec:
        return lax.dot_general(a.astype(BF16), b.astype(BF16), dims, preferred_element_type=F32)
    a1, a2 = _split2(a)
    b1, b2 = _split2(b)
    d = lambda p, q: lax.dot_general(p, q, dims, preferred_element_type=F32)
    return d(a1, b1) + (d(a1, b2) + d(a2, b1))


def _dot(a, b, prec=False):
    return _dg(a, b, _NN, prec)


def _wdot(x, w_ref, prec):
    if not prec:
        return jnp.dot(x.astype(BF16), w_ref[...], preferred_element_type=F32)
    x1, x2 = _split2(x)
    d = lambda p, q: jnp.dot(p, q, preferred_element_type=F32)
    return d(x1, w_ref[0]) + (d(x1, w_ref[1]) + d(x2, w_ref[0]))


def _wspec(w):
    return pl.BlockSpec(w.shape, lambda *_: (0,) * w.ndim)


def _hi_lo(w):
    hi = w.astype(BF16)
    return jnp.stack([hi, (w - hi.astype(F32)).astype(BF16)])


def _dot_nt(a, b, prec=False):
    return _dg(a, b, _NT, prec)


def _dot_tn(a, b, prec=False):
    return _dg(a, b, _TN, prec)


def _split3(a):
    a1 = a.astype(BF16)
    r1 = a - a1.astype(F32)
    a2 = r1.astype(BF16)
    a3 = (r1 - a2.astype(F32)).astype(BF16)
    return a1, a2, a3


def _exact_left_dot(e, a):
    a1, a2, a3 = _split3(a)
    d = lambda p: jnp.dot(e, p, preferred_element_type=F32)
    return d(a1) + d(a2) + d(a3)


def _exact_right_dot(a, e):
    a1, a2, a3 = _split3(a)
    d = lambda p: jnp.dot(p, e, preferred_element_type=F32)
    return d(a1) + d(a2) + d(a3)


def _sigmoid(x):
    return 1.0 / (1.0 + jnp.exp(-x))


def _silu(x):
    return x * _sigmoid(x)


def _softplus(x):
    return jnp.maximum(x, 0.0) + jnp.log1p(jnp.exp(-jnp.abs(x)))


def _log_sigmoid(x):
    return -_softplus(-x)


def _layer_norm(z, g, b):
    mu = jnp.mean(z, axis=-1, keepdims=True)
    d = z - mu
    var = jnp.mean(d * d, axis=-1, keepdims=True)
    return d * lax.rsqrt(var + EPS) * g + b


def _lane_norm(y):
    mu = jnp.mean(y, axis=-1, keepdims=True)
    d = y - mu
    var = jnp.mean(d * d, axis=-1, keepdims=True)
    return d * lax.rsqrt(var + EPS)


def _proj_body(x_ref, w_ref, *o_refs, widths, prec):
    h = _wdot(x_ref[...], w_ref, prec)
    off = 0
    for o_ref, wd in zip(o_refs, widths):
        o_ref[...] = h[:, off:off + wd]
        off += wd


def _project(x, w, widths, prec=False):
    M, K = x.shape
    tm = _tile(M, ROW_TILE)
    return pl.pallas_call(
        functools.partial(_proj_body, widths=widths, prec=prec),
        grid=(M // tm,),
        in_specs=[pl.BlockSpec((tm, K), lambda i: (i, 0)), _wspec(w)],
        out_specs=[pl.BlockSpec((tm, wd), lambda i: (i, 0)) for wd in widths],
        out_shape=[jax.ShapeDtypeStruct((M, wd), F32) for wd in widths],
        compiler_params=_cparams(("parallel",)),
        name="project",
    )(x, w)


def _mem_proj_body(x_ref, w_ref, o_ref):
    o_ref[0] = _wdot(x_ref[...], w_ref.at[0], True)


def _mem_project(x, w):
    M, K = x.shape
    J, _, _, N = w.shape
    return pl.pallas_call(
        _mem_proj_body,
        grid=(J,),
        in_specs=[pl.BlockSpec((M, K), lambda j: (0, 0)),
                  pl.BlockSpec((1, 2, K, N), lambda j: (j, 0, 0, 0))],
        out_specs=pl.BlockSpec((1, M, N), lambda j: (j, 0, 0)),
        out_shape=jax.ShapeDtypeStruct((J, M, N), F32),
        compiler_params=_cparams(("parallel",)),
        name="mem_project",
    )(x, w)


def _out_ln_body(x_ref, a_ref, b_ref, wa_ref, wb_ref, g_ref, bias_ref, o_ref, *, prec):
    mix = _wdot(a_ref[...], wa_ref, prec) + _wdot(b_ref[...], wb_ref, prec)
    o_ref[...] = _layer_norm(ALPHA * x_ref[...] + mix, g_ref[...], bias_ref[...])


def _out_proj_ln(x, a, b, wa, wb, g, bias, prec=False):
    M, D = x.shape
    Ka, Kb = a.shape[1], b.shape[1]
    tm = _tile(M, ROW_TILE)
    row = lambda i: (i, 0)
    fix = lambda i: (0, 0)
    return pl.pallas_call(
        functools.partial(_out_ln_body, prec=prec),
        grid=(M // tm,),
        in_specs=[pl.BlockSpec((tm, D), row), pl.BlockSpec((tm, Ka), row), pl.BlockSpec((tm, Kb), row),
                  _wspec(wa), _wspec(wb),
                  pl.BlockSpec((1, D), fix), pl.BlockSpec((1, D), fix)],
        out_specs=pl.BlockSpec((tm, D), row),
        out_shape=jax.ShapeDtypeStruct((M, D), F32),
        compiler_params=_cparams(("parallel",)),
        name="out_proj_ln",
    )(x, a, b, wa, wb, g.reshape(1, D), bias.reshape(1, D))


def _ret_body(q_ref, k_ref, v_ref, g_ref, cos_ref, sa_ref, sb_ref, dmat_ref, qdec_ref, kdec_ref,
              s0_ref, gn_ref, o_ref, sout_ref, s_scr, *, sdec, prec):
    i = pl.program_id(1)

    @pl.when(i == 0)
    def _():
        s_scr[...] = s0_ref[0]

    cos, sin_a, sin_b = cos_ref[...], sa_ref[...], sb_ref[...]
    width = RET_HEADS * RET_DK
    half = RET_DK // 2

    def rope(x):
        return x * cos + pltpu.roll(x, width - half, 1) * sin_a + pltpu.roll(x, half, 1) * sin_b

    q = rope(q_ref[...])
    k = rope(k_ref[...]) * (RET_DK ** -0.5)
    v = v_ref[...]
    gate = g_ref[...]
    outs = []
    for h in range(RET_HEADS):
        qh = q[:, h * RET_DK:(h + 1) * RET_DK]
        kh = k[:, h * RET_DK:(h + 1) * RET_DK]
        vh = v[:, h * RET_DV:(h + 1) * RET_DV]
        s_old = s_scr[h]
        s = _dot_nt(qh, kh, prec) * dmat_ref[h]
        o = _dot(s, vh, prec) + _dot(qh, s_old, prec) * qdec_ref[h]
        s_scr[h] = sdec[h] * s_old + _dot_tn(kh * kdec_ref[h], vh, prec)
        outs.append(_lane_norm(o))
    hn = jnp.concatenate(outs, axis=1) * gn_ref[...]
    o_ref[...] = _silu(gate) * hn

    @pl.when(i == pl.num_programs(1) - 1)
    def _():
        sout_ref[0] = s_scr[...]


def _retention(hq, hk, hv, hg, s0, gn_g, B, T, pos0, L, prec=False):
    nT = T // L
    lg = np.log(1.0 - 2.0 ** (-5.0 - np.arange(RET_HEADS)))
    idx = np.arange(L, dtype=np.float64)
    rel = idx[:, None] - idx[None, :]
    dmat = np.where(rel >= 0, np.exp(np.maximum(rel, 0.0)[None] * lg[:, None, None]), 0.0)
    qdec = np.exp((idx + 1.0)[None, :, None] * lg[:, None, None])
    kdec = np.exp((L - 1.0 - idx)[None, :, None] * lg[:, None, None])
    sdec = tuple(float(np.exp(L * x)) for x in lg)
    half = RET_DK // 2
    inv = ROPE_BASE ** (-jnp.arange(half, dtype=F32) / half)
    ang = (pos0 + jnp.arange(T)).astype(F32)[:, None] * inv[None, :]
    cos, sin = jnp.cos(ang), jnp.sin(ang)
    zero = jnp.zeros_like(sin)
    cos_t = jnp.tile(jnp.concatenate([cos, cos], axis=1), (1, RET_HEADS))
    sin_a = jnp.tile(jnp.concatenate([-sin, zero], axis=1), (1, RET_HEADS))
    sin_b = jnp.tile(jnp.concatenate([zero, sin], axis=1), (1, RET_HEADS))
    qk_w = RET_HEADS * RET_DK
    v_w = RET_HEADS * RET_DV
    row = lambda b, i: (b * nT + i, 0)
    tab = lambda b, i: (i, 0)
    fix3 = lambda b, i: (0, 0, 0)
    out, s_out = pl.pallas_call(
        functools.partial(_ret_body, sdec=sdec, prec=prec),
        grid=(B, nT),
        in_specs=[pl.BlockSpec((L, qk_w), row), pl.BlockSpec((L, qk_w), row),
                  pl.BlockSpec((L, v_w), row), pl.BlockSpec((L, v_w), row),
                  pl.BlockSpec((L, qk_w), tab), pl.BlockSpec((L, qk_w), tab), pl.BlockSpec((L, qk_w), tab),
                  pl.BlockSpec((RET_HEADS, L, L), fix3), pl.BlockSpec((RET_HEADS, L, 1), fix3),
                  pl.BlockSpec((RET_HEADS, L, 1), fix3),
                  pl.BlockSpec((1, RET_HEADS, RET_DK, RET_DV), lambda b, i: (b, 0, 0, 0)),
                  pl.BlockSpec((1, v_w), lambda b, i: (0, 0))],
        out_specs=[pl.BlockSpec((L, v_w), row),
                   pl.BlockSpec((1, RET_HEADS, RET_DK, RET_DV), lambda b, i: (b, 0, 0, 0))],
        out_shape=[jax.ShapeDtypeStruct((B * T, v_w), F32),
                   jax.ShapeDtypeStruct((B, RET_HEADS, RET_DK, RET_DV), F32)],
        scratch_shapes=[pltpu.VMEM((RET_HEADS, RET_DK, RET_DV), F32)],
        compiler_params=_cparams(("parallel", "arbitrary")),
        name="retention",
    )(hq, hk, hv, hg, cos_t, sin_a, sin_b, jnp.asarray(dmat, F32), jnp.asarray(qdec, F32),
      jnp.asarray(kdec, F32), s0, gn_g.reshape(1, v_w))
    return out, s_out


def _rwkv_prep_body(h_ref, p_ref, mu_ref, w0_ref, w2_ref, a0_ref, a2_ref, g2_ref, kk_ref, ka_ref, rk_ref,
                    ones_ref, tri_ref, blk_ref,
                    kq_ref, rq_ref, kt_ref, bt_ref, ke_ref, be_ref, v_ref, gt_ref, g_ref, bon_ref, *, prec):
    hb = h_ref[...]
    xs = hb + (p_ref[...] - hb) * mu_ref[...]
    W = RWKV_W
    r, k, v = xs[:, :W], xs[:, W:2 * W], xs[:, 2 * W:3 * W]
    w_lo = xs[:, 3 * W:3 * W + 64]
    a_lo = xs[:, 3 * W + 64:3 * W + 128]
    g_lo = xs[:, 3 * W + 128:]
    w_log = -_softplus(-(w0_ref[...] + _dot(jnp.tanh(w_lo), w2_ref[...], prec))) - 0.5
    lw = -jnp.exp(w_log)
    a = _sigmoid(a0_ref[...] + _dot(a_lo, a2_ref[...], prec))
    g = _dot(_sigmoid(g_lo), g2_ref[...], prec)
    ones = ones_ref[...]
    kk = k * kk_ref[...]
    nrm = jnp.sqrt(_exact_right_dot(kk * kk, ones))
    kk = kk / jnp.maximum(nrm, 1e-12)
    k2 = k * (1.0 + (a - 1.0) * ka_ref[...])
    beta = kk * a
    cl = _exact_left_dot(tri_ref[...], lw)
    tot = _exact_left_dot(blk_ref[...], lw)
    ginv = jnp.exp(-cl)
    gend = jnp.exp(tot - cl)
    kq_ref[...] = (kk * jnp.exp(cl - lw)).astype(kq_ref.dtype)
    rq_ref[...] = (r * jnp.exp(cl)).astype(rq_ref.dtype)
    kt_ref[...] = (k2 * ginv).astype(kt_ref.dtype)
    bt_ref[...] = (beta * ginv).astype(bt_ref.dtype)
    ke_ref[...] = (k2 * gend).astype(ke_ref.dtype)
    be_ref[...] = (beta * gend).astype(be_ref.dtype)
    v_ref[...] = v.astype(v_ref.dtype)
    gt_ref[...] = jnp.exp(tot)
    g_ref[...] = g
    bon_ref[...] = _exact_right_dot(r * k2 * rk_ref[...], ones) * v


def _rwkv_prep(hb, prev, P, L, prec=False):
    M = hb.shape[0]
    tm = _tile(M, ROW_TILE)
    assert tm % L == 0
    W = RWKV_W
    idx = np.arange(tm)
    same = (idx[:, None] // L) == (idx[None, :] // L)
    tri = jnp.asarray(same & (idx[:, None] >= idx[None, :]), BF16)
    blk = jnp.asarray(same, BF16)
    lane = np.arange(W)
    ones = jnp.asarray((lane[:, None] // RWKV_HD) == (lane[None, :] // RWKV_HD), BF16)
    row = lambda i: (i, 0)
    fix = lambda i: (0, 0)
    vec = lambda a: a.reshape(1, -1)
    full = lambda a: pl.BlockSpec(a.shape, fix)
    params = [vec(P['rwkv_mu']), vec(P['rwkv_w0']), P['rwkv_w2'], vec(P['rwkv_a0']),
              P['rwkv_a2'], P['rwkv_g2'], vec(P['rwkv_k_k']), vec(P['rwkv_k_a']),
              vec(P['rwkv_r_k']), ones, tri, blk]
    out_dt = [F32 if prec else BF16] * 7 + [F32] * 3
    return pl.pallas_call(
        functools.partial(_rwkv_prep_body, prec=prec),
        grid=(M // tm,),
        in_specs=[pl.BlockSpec((tm, RWKV_PROJ), row), pl.BlockSpec((tm, RWKV_PROJ), row)]
                 + [full(a) for a in params],
        out_specs=[pl.BlockSpec((tm, W), row) for _ in out_dt],
        out_shape=[jax.ShapeDtypeStruct((M, W), dt) for dt in out_dt],
        compiler_params=_cparams(("parallel",)),
        name="rwkv_prep",
    )(hb, prev, *params)


def _rwkv_scan_body(kq_ref, rq_ref, kt_ref, bt_ref, ke_ref, be_ref, v_ref, gt_ref, g_ref, bon_ref,
                    s0_ref, gng_ref, gnb_ref, lmask_ref, o_ref, sout_ref, s_scr, *, L, nchunk, nlev, prec):
    i = pl.program_id(2)
    op_dt = F32 if prec else BF16

    @pl.when(i == 0)
    def _():
        s_scr[...] = s0_ref[0]

    ii = lax.broadcasted_iota(jnp.int32, (L, L), 0)
    jj = lax.broadcasted_iota(jnp.int32, (L, L), 1)
    strict = ii > jj
    incl = ii >= jj
    eye = (ii == jj).astype(F32)
    hd = RWKV_HD

    def chunk(c, carry):
        r0 = pl.multiple_of(c * L, L)
        rows = pl.ds(r0, L)
        kq, rq, kt, bt = kq_ref[rows, :], rq_ref[rows, :], kt_ref[rows, :], bt_ref[rows, :]
        ke, be, vv = ke_ref[rows, :], be_ref[rows, :], v_ref[rows, :]
        g_end = gt_ref[pl.ds(r0, 1), :]
        outs = []
        for hh in range(2):
            cs = slice(hh * hd, (hh + 1) * hd)
            Kq, Rq, Kt, Bt, Ke, Be, V = kq[:, cs], rq[:, cs], kt[:, cs], bt[:, cs], ke[:, cs], be[:, cs], vv[:, cs]
            s_old = s_scr[hh]
            s_b = s_old.astype(op_dt)
            n_m = jnp.where(strict, _dot_nt(Kq, Kt, prec), 0.0)
            m_m = jnp.where(strict, _dot_nt(Kq, Bt, prec), 0.0)
            a_k = jnp.where(incl, _dot_nt(Rq, Kt, prec), 0.0)
            a_b = jnp.where(incl, _dot_nt(Rq, Bt, prec), 0.0)
            t_m = eye
            for lv in range(nlev):
                c_m = m_m * lmask_ref[lv]
                t_m = t_m - _dot(_dot(t_m, c_m, prec), t_m, prec)
            w_m = _dot(n_m, V, prec) + _dot_nt(Kq, s_b, prec)
            u_b = _dot(t_m, w_m, prec).astype(op_dt)
            y = _dot_nt(Rq, s_b, prec) + _dot(a_k, V, prec) - _dot(a_b, u_b, prec)
            s_scr[hh] = s_old * g_end[:, cs] + _dot_tn(V, Ke, prec) - _dot_tn(u_b, Be, prec)
            outs.append(_lane_norm(y))
        yn = jnp.concatenate(outs, axis=1)
        o_ref[rows, :] = (yn * gng_ref[...] + gnb_ref[...] + bon_ref[rows, :]) * g_ref[rows, :]
        return carry

    lax.fori_loop(0, nchunk, chunk, 0)

    @pl.when(i == pl.num_programs(2) - 1)
    def _():
        sout_ref[0] = s_scr[...]


def _rwkv_scan(pre, s0, gn_g, gn_b, B, T, L, prec=False):
    W = RWKV_W
    tb = _tile(T, RWKV_ROWS)
    assert tb % L == 0
    nT = T // tb
    nlev = int(math.log2(L))
    idx = np.arange(L)
    ii, jj = idx[:, None], idx[None, :]
    lmask = np.stack([((ii >> (lv + 1)) == (jj >> (lv + 1))) & ((ii & (1 << lv)) != 0) & ((jj & (1 << lv)) == 0)
                      for lv in range(nlev)]).astype(np.float32)
    npair = RWKV_HEADS // 2
    pw = 2 * RWKV_HD
    row = lambda b, p, i: (b * nT + i, p)
    st = lambda b, p, i: (b, p, 0, 0)
    vec = lambda b, p, i: (0, p)
    out, s_out = pl.pallas_call(
        functools.partial(_rwkv_scan_body, L=L, nchunk=tb // L, nlev=nlev, prec=prec),
        grid=(B, npair, nT),
        in_specs=[pl.BlockSpec((tb, pw), row) for _ in range(10)]
                 + [pl.BlockSpec((1, 2, RWKV_HD, RWKV_HD), st),
                    pl.BlockSpec((1, pw), vec), pl.BlockSpec((1, pw), vec),
                    pl.BlockSpec((nlev, L, L), lambda b, p, i: (0, 0, 0))],
        out_specs=[pl.BlockSpec((tb, pw), row), pl.BlockSpec((1, 2, RWKV_HD, RWKV_HD), st)],
        out_shape=[jax.ShapeDtypeStruct((B * T, W), F32),
                   jax.ShapeDtypeStruct((B, RWKV_HEADS, RWKV_HD, RWKV_HD), F32)],
        scratch_shapes=[pltpu.VMEM((2, RWKV_HD, RWKV_HD), F32)],
        compiler_params=_cparams(("parallel", "parallel", "arbitrary")),
        name="rwkv_scan",
    )(*pre, s0, gn_g.reshape(1, W), gn_b.reshape(1, W), jnp.asarray(lmask))
    return out, s_out


def _mlstm_body(qk_ref, v_ref, o_ref, gt_ref, cprev_ref, c0_ref, n0_ref, m0_ref, cw_ref, cb_ref, gb_ref,
                gn_ref, tri_ref, out_ref, cout_ref, nout_ref, mout_ref, convout_ref,
                xpad, c_scr, m_scr, *, L):
    i = pl.program_id(1)
    H, DK, DV = MLSTM_HEADS, MLSTM_DK, MLSTM_DV
    K = MLSTM_CONV - 1
    base = 8 - K

    @pl.when(i == 0)
    def _():
        xpad[base:8, :] = cprev_ref[0]
        for h in range(H):
            c_scr[h, :, 0:DV] = c0_ref[0, h]
            c_scr[h, :, DV:2 * DV] = jnp.broadcast_to(n0_ref[0, h], (DK, DV))
        m_scr[...] = m0_ref[0]

    xpad[8:8 + L, :] = qk_ref[...]
    conv = cb_ref[...] + xpad[pl.ds(base, L), :] * cw_ref[0:1, :]
    for j in range(1, MLSTM_CONV):
        conv = conv + xpad[pl.ds(base + j, L), :] * cw_ref[j:j + 1, :]
    tail = xpad[pl.ds(8 + L - K, K), :]
    xpad[base:8, :] = tail
    qk = _silu(conv)
    q = qk[:, :H * DK] * (DK ** -0.5)
    k = qk[:, H * DK:]
    v = v_ref[...]
    z = gt_ref[...] + gb_ref[...]
    lf = _log_sigmoid(z)
    bcum = _exact_left_dot(tri_ref[...], lf)
    z_t = z.T
    b_t = bcum.T
    ii = lax.broadcasted_iota(jnp.int32, (L, L), 0)
    jj = lax.broadcasted_iota(jnp.int32, (L, L), 1)
    causal = ii >= jj
    ones = jnp.ones((L, DV), F32)
    m_all = m_scr[...]
    outs = []
    m_new_all = m_all
    lane = lax.broadcasted_iota(jnp.int32, (1, 128), 1)
    for h in range(H):
        qh = q[:, h * DK:(h + 1) * DK]
        kh = k[:, h * DK:(h + 1) * DK]
        vh = jnp.concatenate([v[:, h * DV:(h + 1) * DV], ones], axis=1)
        b_col = bcum[:, H + h:H + h + 1]
        ig_col = z[:, h:h + 1]
        row_term = z_t[h:h + 1, :] - b_t[H + h:H + h + 1, :]
        m0 = m_all[:, h:h + 1]
        log_d = jnp.where(causal, b_col + row_term, NEG_INF)
        m_inter = b_col + m0
        m_t = jnp.maximum(m_inter, jnp.max(log_d, axis=-1, keepdims=True))
        w_d = jnp.exp(log_d - m_t)
        w_i = jnp.exp(m_inter - m_t)
        c_old = c_scr[h]
        s = _dot_nt(qh, kh) * w_d
        num = _dot(s, vh) + _dot(qh, c_old) * w_i
        den = num[:, DV:DV + 1]
        denom = jnp.maximum(jnp.abs(den), jnp.exp(-m_t))
        outs.append(_lane_norm(num[:, :DV] / denom))
        m_new = m_t[L - 1:L, :]
        b_last = b_col[L - 1:L, :]
        w_s = jnp.exp(b_last + m0 - m_new)
        w_k = jnp.exp(b_last - b_col + ig_col - m_new)
        c_scr[h] = w_s * c_old + _dot_tn(kh * w_k, vh)
        m_new_all = jnp.where(lane == h, m_new, m_new_all)
    m_scr[...] = m_new_all
    hn = jnp.concatenate(outs, axis=1) * gn_ref[...]
    out_ref[...] = hn * _sigmoid(o_ref[...])

    @pl.when(i == pl.num_programs(1) - 1)
    def _():
        for h in range(H):
            cout_ref[0, h] = c_scr[h, :, 0:DV]
            nout_ref[0, h] = c_scr[h, :, DV:DV + 1]
        mout_ref[0] = m_scr[...]
        convout_ref[0] = tail


def _mlstm(hqk, hv, ho, hgate, conv_prev, c0, n0, m0, P, gate_bias, B, T, L):
    H, DK, DV = MLSTM_HEADS, MLSTM_DK, MLSTM_DV
    nT = T // L
    K = MLSTM_CONV - 1
    W = H * DV
    idx = np.arange(L)
    tri = jnp.asarray(idx[:, None] >= idx[None, :], BF16)
    m0p = jnp.zeros((B, 1, 128), F32).at[:, 0, :H].set(m0)
    row = lambda b, i: (b * nT + i, 0)
    fix = lambda b, i: (0, 0)
    perb3 = lambda b, i: (b, 0, 0)
    perb4 = lambda b, i: (b, 0, 0, 0)
    out, c_out, n_out, m_out, conv_out = pl.pallas_call(
        functools.partial(_mlstm_body, L=L),
        grid=(B, nT),
        in_specs=[pl.BlockSpec((L, W), row), pl.BlockSpec((L, W), row), pl.BlockSpec((L, W), row),
                  pl.BlockSpec((L, 128), row),
                  pl.BlockSpec((1, K, W), perb3),
                  pl.BlockSpec((1, H, DK, DV), perb4), pl.BlockSpec((1, H, DK, 1), perb4),
                  pl.BlockSpec((1, 1, 128), perb3),
                  pl.BlockSpec((MLSTM_CONV, W), fix), pl.BlockSpec((1, W), fix), pl.BlockSpec((1, 128), fix),
                  pl.BlockSpec((1, W), fix), pl.BlockSpec((L, L), fix)],
        out_specs=[pl.BlockSpec((L, W), row),
                   pl.BlockSpec((1, H, DK, DV), perb4), pl.BlockSpec((1, H, DK, 1), perb4),
                   pl.BlockSpec((1, 1, 128), perb3), pl.BlockSpec((1, K, W), perb3)],
        out_shape=[jax.ShapeDtypeStruct((B * T, W), F32),
                   jax.ShapeDtypeStruct((B, H, DK, DV), F32), jax.ShapeDtypeStruct((B, H, DK, 1), F32),
                   jax.ShapeDtypeStruct((B, 1, 128), F32), jax.ShapeDtypeStruct((B, K, W), F32)],
        scratch_shapes=[pltpu.VMEM((L + 8, W), F32), pltpu.VMEM((H, DK, 2 * DV), F32), pltpu.VMEM((1, 128), F32)],
        compiler_params=_cparams(("parallel", "arbitrary")),
        name="mlstm",
    )(hqk, hv, ho, hgate, conv_prev, c0, n0.reshape(B, H, DK, 1), m0p,
      P['mlstm_conv_w'], P['mlstm_conv_b'].reshape(1, W), gate_bias, P['mlstm_gn_g'].reshape(1, W), tri)
    return out, c_out, n_out.reshape(B, H, DK), m_out[:, 0, :H], conv_out


def _fox_prep_body(q_ref, k_ref, v_ref, gt_ref, qg_ref, kg_ref, gb_ref, qn_ref, kn_ref, knb_ref, vb_ref, lf_ref):
    def rms(x, g):
        outs = []
        for h in range(FOX_HEADS):
            xh = x[:, h * FOX_HD:(h + 1) * FOX_HD]
            outs.append(xh * lax.rsqrt(jnp.mean(xh * xh, axis=-1, keepdims=True) + EPS) * g)
        return jnp.concatenate(outs, axis=1)

    qn_ref[...] = (rms(q_ref[...], qg_ref[...]) * (FOX_HD ** -0.5)).astype(BF16)
    kn = rms(k_ref[...], kg_ref[...])
    kn_ref[...] = kn
    knb_ref[...] = kn.astype(BF16)
    vb_ref[...] = v_ref[...].astype(BF16)
    lf_ref[...] = _log_sigmoid(gt_ref[...] + gb_ref[...])


def _fox_prep(hq, hk, hv, hgate, P, gate_bias):
    M, W = hq.shape
    tm = _tile(M, ROW_TILE)
    row = lambda i: (i, 0)
    fix = lambda i: (0, 0)
    return pl.pallas_call(
        _fox_prep_body,
        grid=(M // tm,),
        in_specs=[pl.BlockSpec((tm, W), row), pl.BlockSpec((tm, W), row), pl.BlockSpec((tm, W), row),
                  pl.BlockSpec((tm, 128), row),
                  pl.BlockSpec((1, FOX_HD), fix), pl.BlockSpec((1, FOX_HD), fix), pl.BlockSpec((1, 128), fix)],
        out_specs=[pl.BlockSpec((tm, W), row), pl.BlockSpec((tm, W), row), pl.BlockSpec((tm, W), row),
                   pl.BlockSpec((tm, W), row), pl.BlockSpec((tm, 128), row)],
        out_shape=[jax.ShapeDtypeStruct((M, W), BF16), jax.ShapeDtypeStruct((M, W), F32),
                   jax.ShapeDtypeStruct((M, W), BF16), jax.ShapeDtypeStruct((M, W), BF16),
                   jax.ShapeDtypeStruct((M, 128), F32)],
        compiler_params=_cparams(("parallel",)),
        name="fox_prep",
    )(hq, hk, hv, hgate, P['fox_q_g'].reshape(1, FOX_HD), P['fox_k_g'].reshape(1, FOX_HD), gate_bias)


def _fox_prompt_body(q_ref, k_ref, v_ref, cq_ref, ck_ref, o_ref, m_scr, l_scr, acc_scr, *, tq):
    qi = pl.program_id(2)
    q = q_ref[0]
    cq = cq_ref[0, 0]
    m_scr[...] = jnp.full_like(m_scr, NEG_INF)
    l_scr[...] = jnp.zeros_like(l_scr)
    acc_scr[...] = jnp.zeros_like(acc_scr)

    def step(kj, masked):
        r0 = pl.multiple_of(kj * tq, tq)
        kb = k_ref[0, pl.ds(r0, tq), :]
        vb = v_ref[0, pl.ds(r0, tq), :]
        ck = ck_ref[0, 0, pl.ds(kj, 1), :]
        s = lax.dot_general(q, kb, (((1,), (1,)), ((), ())), preferred_element_type=F32) + (cq - ck)
        if masked:
            ii = lax.broadcasted_iota(jnp.int32, (tq, tq), 0)
            jj = lax.broadcasted_iota(jnp.int32, (tq, tq), 1)
            s = jnp.where(jj <= ii, s, NEG_INF)
        m_old = m_scr[...]
        m_new = jnp.maximum(m_old, jnp.max(s, axis=-1, keepdims=True))
        a = jnp.exp(m_old - m_new)
        p = jnp.exp(s - m_new)
        l_scr[...] = a * l_scr[...] + jnp.sum(p, axis=-1, keepdims=True)
        acc_scr[...] = a * acc_scr[...] + jnp.dot(p.astype(BF16), vb, preferred_element_type=F32)
        m_scr[...] = m_new

    def body(kj, carry):
        step(kj, False)
        return carry

    lax.fori_loop(0, qi, body, 0)
    step(qi, True)
    o_ref[0] = acc_scr[...] / l_scr[...]


def _fox_prompt(qn, kn, v, c, B, T):
    H, HD = FOX_HEADS, FOX_HD
    W = H * HD
    tq = _tile(T, FOX_TQ)
    nQ = T // tq
    cq = jnp.transpose(c, (0, 2, 1)).reshape(B, H, T, 1)
    ck = jnp.transpose(c, (0, 2, 1)).reshape(B, H, nQ, tq)
    out = pl.pallas_call(
        functools.partial(_fox_prompt_body, tq=tq),
        grid=(B, H, nQ),
        in_specs=[pl.BlockSpec((1, tq, HD), lambda b, h, i: (b, i, h)),
                  pl.BlockSpec((1, T, HD), lambda b, h, i: (b, 0, h)),
                  pl.BlockSpec((1, T, HD), lambda b, h, i: (b, 0, h)),
                  pl.BlockSpec((1, 1, tq, 1), lambda b, h, i: (b, h, i, 0)),
                  pl.BlockSpec((1, 1, nQ, tq), lambda b, h, i: (b, h, 0, 0))],
        out_specs=pl.BlockSpec((1, tq, HD), lambda b, h, i: (b, i, h)),
        out_shape=jax.ShapeDtypeStruct((B, T, W), F32),
        scratch_shapes=[pltpu.VMEM((tq, 1), F32), pltpu.VMEM((tq, 1), F32), pltpu.VMEM((tq, HD), F32)],
        compiler_params=_cparams(("parallel", "parallel", "arbitrary")),
        name="fox_prompt",
    )(qn.reshape(B, T, W), kn.reshape(B, T, W), v.reshape(B, T, W), cq, ck)
    return out.reshape(B * T, W)


def _fox_sample_body(q_ref, kc_ref, vc_ref, kn_ref, vn_ref, cq_ref, ckc_ref, ckn_ref, o_ref, *, T):
    H, HD = FOX_HEADS, FOX_HD
    q = q_ref[0]
    kc, vc, kn, vn = kc_ref[0], vc_ref[0], kn_ref[0], vn_ref[0]
    ii = lax.broadcasted_iota(jnp.int32, (T, T), 0)
    jj = lax.broadcasted_iota(jnp.int32, (T, T), 1)
    outs = []
    for h in range(H):
        cs = slice(h * HD, (h + 1) * HD)
        cq = cq_ref[0, h]
        s1 = _dot_nt(q[:, cs], kc[:, cs]) + (cq - ckc_ref[0, h])
        s2 = _dot_nt(q[:, cs], kn[:, cs]) + (cq - ckn_ref[0, h])
        s2 = jnp.where(jj <= ii, s2, NEG_INF)
        m = jnp.maximum(jnp.max(s1, axis=-1, keepdims=True), jnp.max(s2, axis=-1, keepdims=True))
        p1 = jnp.exp(s1 - m)
        p2 = jnp.exp(s2 - m)
        den = jnp.sum(p1, axis=-1, keepdims=True) + jnp.sum(p2, axis=-1, keepdims=True)
        outs.append((_dot(p1, vc[:, cs]) + _dot(p2, vn[:, cs])) / den)
    o_ref[0] = jnp.concatenate(outs, axis=1)


def _fox_sample(qn, kn, vn, k_cache, v_cache, c_all, B, T):
    H, HD = FOX_HEADS, FOX_HD
    W = H * HD
    past = k_cache.shape[1]
    ct = jnp.transpose(c_all, (0, 2, 1))
    cq = ct[:, :, past:].reshape(B, H, T, 1)
    ckc = ct[:, :, :past].reshape(B, H, 1, past)
    ckn = ct[:, :, past:].reshape(B, H, 1, T)
    b3 = lambda b: (b, 0, 0)
    b4 = lambda b: (b, 0, 0, 0)
    out = pl.pallas_call(
        functools.partial(_fox_sample_body, T=T),
        grid=(B,),
        in_specs=[pl.BlockSpec((1, T, W), b3), pl.BlockSpec((1, past, W), b3), pl.BlockSpec((1, past, W), b3),
                  pl.BlockSpec((1, T, W), b3), pl.BlockSpec((1, T, W), b3),
                  pl.BlockSpec((1, H, T, 1), b4), pl.BlockSpec((1, H, 1, past), b4), pl.BlockSpec((1, H, 1, T), b4)],
        out_specs=pl.BlockSpec((1, T, W), b3),
        out_shape=jax.ShapeDtypeStruct((B, T, W), F32),
        compiler_params=_cparams(("parallel",)),
        name="fox_sample",
    )(qn.reshape(B, T, W), k_cache.reshape(B, past, W), v_cache.reshape(B, past, W),
      kn.reshape(B, T, W), vn.reshape(B, T, W), cq, ckc, ckn)
    return out.reshape(B * T, W)


def _mem_attn_body(x_ref, mk_ref, mv_ref, wq_ref, wo_ref, g_ref, b_ref, o_ref, *, prec):
    x = x_ref[...]
    q = _wdot(x, wq_ref, prec)
    mk, mv = mk_ref[0], mv_ref[0]
    outs = []
    for h in range(MEM_HEADS):
        cs = slice(h * MEM_HD, (h + 1) * MEM_HD)
        s = _dot_nt(q[:, cs], mk[:, cs], prec) * (MEM_HD ** -0.5)
        m = jnp.max(s, axis=-1, keepdims=True)
        p = jnp.exp(s - m)
        outs.append(_dot(p, mv[:, cs], prec) / jnp.sum(p, axis=-1, keepdims=True))
    o = jnp.concatenate(outs, axis=1)
    att = _wdot(o, wo_ref, prec)
    o_ref[...] = _layer_norm(ALPHA * x + att, g_ref[...], b_ref[...])


def _mem_attn_ln(x, mk, mv, l, wq, wo, g, bias, B, T, prec=False):
    D = D_MODEL
    Mm = mk.shape[1]
    tm = _tile(T, ROW_TILE)
    nT = T // tm
    row = lambda b, i: (b * nT + i, 0)
    fix = lambda b, i: (0, 0)
    mem = lambda b, i: (l * B + b, 0, 0)
    return pl.pallas_call(
        functools.partial(_mem_attn_body, prec=prec),
        grid=(B, nT),
        in_specs=[pl.BlockSpec((tm, D), row), pl.BlockSpec((1, Mm, D), mem), pl.BlockSpec((1, Mm, D), mem),
                  _wspec(wq), _wspec(wo),
                  pl.BlockSpec((1, D), fix), pl.BlockSpec((1, D), fix)],
        out_specs=pl.BlockSpec((tm, D), row),
        out_shape=jax.ShapeDtypeStruct((B * T, D), F32),
        compiler_params=_cparams(("parallel", "parallel")),
        name="mem_attn_ln",
    )(x, mk, mv, wq, wo, g.reshape(1, D), bias.reshape(1, D))


def _router_body(x_ref, w_ref, b_ref, o_ref):
    x = x_ref[...]
    w = w_ref[...]
    x1 = x.astype(BF16)
    x2 = (x - x1.astype(F32)).astype(BF16)
    w1 = w.astype(BF16)
    w2 = (w - w1.astype(F32)).astype(BF16)
    nt = lambda a, c: lax.dot_general(a, c, (((1,), (1,)), ((), ())), preferred_element_type=F32)
    logits = nt(w1, x1) + nt(w1, x2) + nt(w2, x1) + b_ref[...]
    m = jnp.max(logits, axis=0, keepdims=True)
    e = jnp.exp(logits - m)
    p = e / jnp.sum(e, axis=0, keepdims=True)
    rows = [p[j:j + 1, :] for j in range(N_EXPERTS)]
    best = None
    sel = None
    for g in range(N_GROUPS):
        a, b, c, d = rows[4 * g:4 * g + 4]
        top2 = jnp.maximum(jnp.maximum(jnp.maximum(a + b, a + c), jnp.maximum(a + d, b + c)),
                           jnp.maximum(b + d, c + d))
        if g == 0:
            best, sel = top2, jnp.zeros_like(top2, dtype=jnp.int32)
        else:
            upd = top2 > best
            sel = jnp.where(upd, g, sel)
            best = jnp.maximum(best, top2)
    pin = []
    for kk in range(EXPERTS_PER_GROUP):
        v = rows[kk]
        for g in range(1, N_GROUPS):
            v = jnp.where(sel == g, rows[4 * g + kk], v)
        pin.append(v)
    v1, i1 = pin[0], jnp.zeros_like(sel)
    for kk in range(1, EXPERTS_PER_GROUP):
        upd = pin[kk] > v1
        i1 = jnp.where(upd, kk, i1)
        v1 = jnp.maximum(v1, pin[kk])
    v2, i2 = None, None
    for kk in range(EXPERTS_PER_GROUP):
        cand = jnp.where(i1 == kk, -1.0, pin[kk])
        if v2 is None:
            v2, i2 = cand, jnp.zeros_like(sel)
        else:
            upd = cand > v2
            i2 = jnp.where(upd, kk, i2)
            v2 = jnp.maximum(v2, cand)
    tot = v1 + v2
    e1 = (sel * EXPERTS_PER_GROUP + i1).astype(F32)
    e2 = (sel * EXPERTS_PER_GROUP + i2).astype(F32)
    zero = jnp.zeros_like(v1)
    o_ref[...] = jnp.concatenate([e1, e2, v1 / tot, v2 / tot, zero, zero, zero, zero], axis=0)


def _router(x, w_router, b_router):
    M, D = x.shape
    tm = _tile(M, ROW_TILE)
    return pl.pallas_call(
        _router_body,
        grid=(M // tm,),
        in_specs=[pl.BlockSpec((tm, D), lambda i: (i, 0)),
                  pl.BlockSpec((N_EXPERTS, D), lambda i: (0, 0)),
                  pl.BlockSpec((N_EXPERTS, 1), lambda i: (0, 0))],
        out_specs=pl.BlockSpec((8, tm), lambda i: (0, i)),
        out_shape=jax.ShapeDtypeStruct((8, M), F32),
        compiler_params=_cparams(("parallel",)),
        name="router",
    )(x, w_router.T, b_router.reshape(N_EXPERTS, 1))


def _expert_body(be_ref, x_ref, wg_ref, wu_ref, wd_ref, o_ref, wg_s, wu_s, wd_s):
    i = pl.program_id(0)
    prev = be_ref[jnp.maximum(i - 1, 0)]

    @pl.when((i == 0) | (be_ref[i] != prev))
    def _():
        wg_s[...] = wg_ref[0].astype(BF16)
        wu_s[...] = wu_ref[0].astype(BF16)
        wd_s[...] = wd_ref[0].astype(BF16)

    x = x_ref[...].astype(BF16)
    hg = jnp.dot(x, wg_s[...], preferred_element_type=F32)
    hu = jnp.dot(x, wu_s[...], preferred_element_type=F32)
    hb = (_silu(hg) * hu).astype(BF16)
    o_ref[...] = jnp.dot(hb, wd_s[...], preferred_element_type=F32)


def _experts(xg, blk_exp, wg, wu, wd):
    n_blocks = blk_exp.shape[0]
    D, DE = D_MODEL, D_EXPERT
    grid_spec = pltpu.PrefetchScalarGridSpec(
        num_scalar_prefetch=1,
        grid=(n_blocks,),
        in_specs=[pl.BlockSpec((MOE_BLOCK, D), lambda i, be: (i, 0)),
                  pl.BlockSpec((1, D, DE), lambda i, be: (be[i], 0, 0)),
                  pl.BlockSpec((1, D, DE), lambda i, be: (be[i], 0, 0)),
                  pl.BlockSpec((1, DE, D), lambda i, be: (be[i], 0, 0))],
        out_specs=pl.BlockSpec((MOE_BLOCK, D), lambda i, be: (i, 0)),
        scratch_shapes=[pltpu.VMEM((D, DE), BF16), pltpu.VMEM((D, DE), BF16), pltpu.VMEM((DE, D), BF16)],
    )
    return pl.pallas_call(
        _expert_body,
        grid_spec=grid_spec,
        out_shape=jax.ShapeDtypeStruct((n_blocks * MOE_BLOCK, D), F32),
        compiler_params=_cparams(("arbitrary",)),
        name="experts",
    )(blk_exp, xg, wg, wu, wd)


def _combine_ln_body(x_ref, y0_ref, y1_ref, gt_ref, g_ref, b_ref, o_ref):
    gt = gt_ref[...]
    ffn = y0_ref[...] * gt[:, 0:1] + y1_ref[...] * gt[:, 1:2]
    o_ref[...] = _layer_norm(ALPHA * x_ref[...] + ffn, g_ref[...], b_ref[...])


def _combine_ln(x, y0, y1, gates, g, bias):
    M, D = x.shape
    tm = _tile(M, ROW_TILE)
    row = lambda i: (i, 0)
    fix = lambda i: (0, 0)
    return pl.pallas_call(
        _combine_ln_body,
        grid=(M // tm,),
        in_specs=[pl.BlockSpec((tm, D), row), pl.BlockSpec((tm, D), row), pl.BlockSpec((tm, D), row),
                  pl.BlockSpec((tm, 2), row), pl.BlockSpec((1, D), fix), pl.BlockSpec((1, D), fix)],
        out_specs=pl.BlockSpec((tm, D), row),
        out_shape=jax.ShapeDtypeStruct((M, D), F32),
        compiler_params=_cparams(("parallel",)),
        name="combine_ln",
    )(x, y0, y1, gates, g.reshape(1, D), bias.reshape(1, D))


def _moe_ln(x, P, l):
    M = x.shape[0]
    r = _router(x, P['w_router'], P['b_router'])
    e = r[0:2].astype(jnp.int32).T.reshape(-1)
    gates = r[2:4].T
    oh = (e[:, None] == jnp.arange(N_EXPERTS, dtype=jnp.int32)[None, :]).astype(jnp.int32)
    csum = jnp.cumsum(oh, axis=0)
    rank = jnp.sum((csum - oh) * oh, axis=1)
    counts = csum[-1]
    padded = (counts + MOE_BLOCK - 1) // MOE_BLOCK * MOE_BLOCK
    p_ends = jnp.cumsum(padded)
    p_starts = p_ends - padded
    dest = p_starts[e] + rank
    n_blocks = -(-2 * M // MOE_BLOCK) + N_EXPERTS
    tok = jnp.arange(2 * M, dtype=jnp.int32) // 2
    slot_tok = jnp.zeros((n_blocks * MOE_BLOCK,), jnp.int32).at[dest].set(tok)
    blk_exp = jnp.minimum(jnp.searchsorted(p_ends, jnp.arange(n_blocks, dtype=jnp.int32) * MOE_BLOCK, side='right'),
                          N_EXPERTS - 1).astype(jnp.int32)
    xg = x[slot_tok]
    yb = _experts(xg, blk_exp, P['w_exp_gate'][l], P['w_exp_up'][l], P['w_exp_down'][l])
    d2 = dest.reshape(M, 2)
    return _combine_ln(x, yb[d2[:, 0]], yb[d2[:, 1]], gates, P['ln_g'][l, 2], P['ln_b'][l, 2])


def _trunk(x, pos0, mem_k, mem_v, states, P, Wc, Wf, is_prompt, prec0):
    B, T, D = x.shape
    ret_S, rwkv_S, shift, mC, mn, mm, conv, fk, fv, flf = states
    xf = x.reshape(B * T, D)
    W0 = Wf if prec0 else Wc
    hq, hk, hv, hg, hb = _project(xf, W0['w_in0'], (256, 256, 512, 512, RWKV_PROJ), prec0)
    L_ret = _tile(T, RET_CHUNK)
    out_a, ret_S = _retention(hq, hk, hv, hg, ret_S, P['ret_gn_g'], B, T, pos0, L_ret, prec0)
    hb3 = hb.reshape(B, T, RWKV_PROJ)
    prev = jnp.concatenate([shift, hb3[:, :-1]], axis=1).reshape(B * T, RWKV_PROJ)
    new_shift = hb3[:, -1:]
    L_rwkv = min(RWKV_CHUNK, T)
    pre = _rwkv_prep(hb, prev, P, L_rwkv, prec0)
    out_b, rwkv_S = _rwkv_scan(pre, rwkv_S, P['rwkv_gn_g'], P['rwkv_gn_b'], B, T, L_rwkv, prec0)
    xf = _out_proj_ln(xf, out_a, out_b, W0['w_out0a'], W0['w_out0b'], P['ln_g'][0, 0], P['ln_b'][0, 0], prec0)
    xf = _mem_attn_ln(xf, mem_k, mem_v, 0, W0['w_mem_q'][0], W0['w_mem_o'][0], P['ln_g'][0, 1], P['ln_b'][0, 1],
                      B, T, prec0)
    xf = _moe_ln(xf, P, 0)
    hqk, hv1, ho, fq, fkk, fvv, hgate = _project(xf, Wc['w_in1'], (512, 512, 512, 512, 512, 512, 128))
    out_c, mC, mn, mm, conv = _mlstm(hqk, hv1, ho, hgate, conv, mC, mn, mm, P, Wc['gate_bias'], B, T, L_ret)
    qn, kn, knb, vb, lf = _fox_prep(fq, fkk, fvv, hgate, P, Wc['gate_bias'])
    logf = lf[:, 2 * MLSTM_HEADS:2 * MLSTM_HEADS + FOX_HEADS].reshape(B, T, FOX_HEADS)
    if is_prompt:
        out_d = _fox_prompt(qn, knb, vb, jnp.cumsum(logf, axis=1), B, T)
    else:
        c_all = jnp.cumsum(jnp.concatenate([flf, logf], axis=1), axis=1)
        out_d = _fox_sample(qn, kn, fvv, fk, fv, c_all, B, T)
    xf = _out_proj_ln(xf, out_c, out_d, Wc['w_out1a'], Wc['w_out1b'], P['ln_g'][1, 0], P['ln_b'][1, 0])
    xf = _mem_attn_ln(xf, mem_k, mem_v, 1, Wc['w_mem_q'][1], Wc['w_mem_o'][1], P['ln_g'][1, 1], P['ln_b'][1, 1], B, T)
    xf = _moe_ln(xf, P, 1)
    fk_new = kn.reshape(B, T, FOX_HEADS, FOX_HD)
    fv_new = fvv.reshape(B, T, FOX_HEADS, FOX_HD)
    return (xf.reshape(B, T, D), ret_S, rwkv_S, new_shift, mC, mn, mm, conv, fk_new, fv_new, logf)


def kernel(x_prompt, x_sample, mem_prompt, state_ret, state_rwkv, cache_rwkv_shift, state_mlstm_c, state_mlstm_n,
           state_mlstm_m, cache_mlstm_conv, cache_fox_k, cache_fox_v, cache_fox_logf, cache_mem_k, cache_mem_v,
           w_in0, ret_gn_g, rwkv_mu, rwkv_w0, rwkv_w2, rwkv_a0, rwkv_a2, rwkv_g2, rwkv_k_k, rwkv_k_a, rwkv_r_k,
           rwkv_gn_g, rwkv_gn_b, w_out0, w_in1, mlstm_conv_w, mlstm_conv_b, mlstm_b_i, mlstm_b_f, mlstm_gn_g,
           fox_q_g, fox_k_g, fox_b_f, w_out1, w_mem_q, w_mem_k, w_mem_v, w_mem_o, w_router, b_router,
           w_exp_gate, w_exp_up, w_exp_down, ln_g, ln_b):
    P = dict(ret_gn_g=ret_gn_g, rwkv_mu=rwkv_mu, rwkv_w0=rwkv_w0, rwkv_w2=rwkv_w2, rwkv_a0=rwkv_a0,
             rwkv_a2=rwkv_a2, rwkv_g2=rwkv_g2, rwkv_k_k=rwkv_k_k, rwkv_k_a=rwkv_k_a, rwkv_r_k=rwkv_r_k,
             rwkv_gn_g=rwkv_gn_g, rwkv_gn_b=rwkv_gn_b, mlstm_conv_w=mlstm_conv_w, mlstm_conv_b=mlstm_conv_b,
             mlstm_gn_g=mlstm_gn_g, fox_q_g=fox_q_g, fox_k_g=fox_k_g, w_router=w_router, b_router=b_router,
             w_exp_gate=w_exp_gate, w_exp_up=w_exp_up, w_exp_down=w_exp_down, ln_g=ln_g, ln_b=ln_b)
    B, M = mem_prompt.shape[0], mem_prompt.shape[1]
    D = D_MODEL
    H = MLSTM_HEADS
    ret_proj = 2 * RET_HEADS * RET_DK + 2 * RET_HEADS * RET_DV
    qk1, w1 = 2 * H * MLSTM_DK, H * MLSTM_DV
    off = qk1 + w1
    mlstm_proj = off + 2 * H + w1
    fw = FOX_HEADS * FOX_HD
    gate_cols = jnp.concatenate([w_in1[:, off:off + 2 * H], w_in1[:, mlstm_proj + 3 * fw:],
                                 jnp.zeros((D, 128 - 2 * H - FOX_HEADS), F32)], axis=1)
    w_in1_c = jnp.concatenate([w_in1[:, :off], w_in1[:, off + 2 * H:mlstm_proj],
                               w_in1[:, mlstm_proj:mlstm_proj + 3 * fw], gate_cols], axis=1)
    gate_bias = jnp.concatenate([mlstm_b_i, mlstm_b_f, fox_b_f, jnp.zeros((128 - 2 * H - FOX_HEADS,), F32)]).reshape(1, 128)
    wa0, wb0 = w_out0[:RET_HEADS * RET_DV], w_out0[RET_HEADS * RET_DV:]
    Wf = dict(w_in0=_hi_lo(w_in0), w_out0a=_hi_lo(wa0), w_out0b=_hi_lo(wb0),
              w_mem_q=[_hi_lo(w_mem_q[0])], w_mem_o=[_hi_lo(w_mem_o[0])])
    Wc = dict(w_in0=w_in0.astype(BF16), w_in1=w_in1_c.astype(BF16), gate_bias=gate_bias,
              w_out0a=wa0.astype(BF16), w_out0b=wb0.astype(BF16),
              w_out1a=w_out1[:w1].astype(BF16), w_out1b=w_out1[w1:].astype(BF16),
              w_mem_q=w_mem_q.astype(BF16), w_mem_o=w_mem_o.astype(BF16))
    w_kv = jnp.concatenate([w_mem_k, w_mem_v], axis=0)
    memkv = _mem_project(mem_prompt.reshape(B * M, D), jnp.swapaxes(_hi_lo(w_kv), 0, 1))
    p_mem_k = memkv[:DEPTH].reshape(DEPTH * B, M, D)
    p_mem_v = memkv[DEPTH:].reshape(DEPTH * B, M, D)
    zeros = lambda *s: jnp.zeros(s, F32)
    prompt_states = (zeros(B, RET_HEADS, RET_DK, RET_DV), zeros(B, RWKV_HEADS, RWKV_HD, RWKV_HD),
                     zeros(B, 1, RWKV_PROJ), zeros(B, H, MLSTM_DK, MLSTM_DV), zeros(B, H, MLSTM_DK), zeros(B, H),
                     zeros(B, MLSTM_CONV - 1, qk1), None, None, None)
    p_out = _trunk(x_prompt, 0, p_mem_k, p_mem_v, prompt_states, P, Wc, Wf, True, PROMPT_PREC0)
    DB = x_sample.shape[0]
    sample_states = (state_ret, state_rwkv, cache_rwkv_shift, state_mlstm_c, state_mlstm_n, state_mlstm_m,
                     cache_mlstm_conv, cache_fox_k, cache_fox_v, cache_fox_logf)
    s_out = _trunk(x_sample, cache_fox_k.shape[1], cache_mem_k.reshape(DEPTH * DB, M, D),
                   cache_mem_v.reshape(DEPTH * DB, M, D), sample_states, P, Wc, Wf, False, True)
    mem_shape = (DEPTH, B, M, MEM_HEADS, MEM_HD)
    return ((p_out[0], s_out[0]) + p_out[1:] + (p_mem_k.reshape(mem_shape), p_mem_v.reshape(mem_shape)) + s_out[1:])
```

```python
import functools
import math

import numpy as np
import jax
import jax.numpy as jnp
from jax import lax
from jax.experimental import pallas as pl
from jax.experimental.pallas import tpu as pltpu

F32 = jnp.float32
BF16 = jnp.bfloat16

D_MODEL = 1024
DEPTH = 2
RET_HEADS, RET_DK, RET_DV = 4, 64, 128
ROPE_BASE = 10000.0
RWKV_HEADS, RWKV_HD = 8, 64
RWKV_W = RWKV_HEADS * RWKV_HD
RWKV_PROJ = 3 * RWKV_W + 64 + 64 + 128
MLSTM_HEADS, MLSTM_DK, MLSTM_DV, MLSTM_CONV = 4, 64, 128, 4
FOX_HEADS, FOX_HD = 4, 128
MEM_HEADS, MEM_HD = 4, 256
N_EXPERTS, N_GROUPS, EXPERTS_PER_GROUP = 16, 4, 4
D_EXPERT = 512
ALPHA = (2 * DEPTH) ** 0.25
EPS = 1e-5
NEG_INF = -1e30
LOG2E = 1.4426950408889634

VMEM_LIMIT_BYTES = 56 * 1024 * 1024
ROW_TILE = 512
RET_CHUNK = 256
RWKV_CHUNK = 64
RWKV_ROWS = 512
RWKV_SEQS = 8
FOX_TQ = 1024
MOE_BLOCK = 256
PROMPT_PREC0 = False


def _cparams(sem):
    return pltpu.CompilerParams(dimension_semantics=sem, vmem_limit_bytes=VMEM_LIMIT_BYTES)


def _tile(n, pref):
    if n <= pref:
        return n
    t = pref
    while t >= 8:
        if n % t == 0:
            return t
        t -= 8
    return n


_NN = (((1,), (0,)), ((), ()))
_NT = (((1,), (1,)), ((), ()))
_TN = (((0,), (0,)), ((), ()))
_BNN = (((2,), (1,)), ((0,), (0,)))
_BNT = (((2,), (2,)), ((0,), (0,)))
_BTN = (((1,), (1,)), ((0,), (0,)))


def _split2(a):
    a = a.astype(F32)
    hi = a.astype(BF16)
    return hi, (a - hi.astype(F32)).astype(BF16)


def _dg(a, b, dims, prec):
    if not prec:
        return lax.dot_general(a.astype(BF16), b.astype(BF16), dims, preferred_element_type=F32)
    a1, a2 = _split2(a)
    b1, b2 = _split2(b)
    d = lambda p, q: lax.dot_general(p, q, dims, preferred_element_type=F32)
    return d(a1, b1) + (d(a1, b2) + d(a2, b1))


def _dot(a, b, prec=False):
    return _dg(a, b, _NN, prec)


def _wdot(x, w_ref, prec):
    if not prec:
        return jnp.dot(x.astype(BF16), w_ref[...], preferred_element_type=F32)
    x1, x2 = _split2(x)
    d = lambda p, q: jnp.dot(p, q, preferred_element_type=F32)
    return d(x1, w_ref[0]) + (d(x1, w_ref[1]) + d(x2, w_ref[0]))


def _wspec(w):
    return pl.BlockSpec(w.shape, lambda *_: (0,) * w.ndim)


def _hi_lo(w):
    bits = lax.bitcast_convert_type(w, jnp.uint32) & jnp.uint32(0xFFFF0000)
    hi = lax.bitcast_convert_type(bits, F32)
    return jnp.stack([hi.astype(BF16), (w - hi).astype(BF16)])


def _dot_nt(a, b, prec=False):
    return _dg(a, b, _NT, prec)


def _dot_tn(a, b, prec=False):
    return _dg(a, b, _TN, prec)


def _split3(a):
    a1 = a.astype(BF16)
    r1 = a - a1.astype(F32)
    a2 = r1.astype(BF16)
    a3 = (r1 - a2.astype(F32)).astype(BF16)
    return a1, a2, a3


def _exact_left_dot(e, a):
    a1, a2, a3 = _split3(a)
    d = lambda p: jnp.dot(e, p, preferred_element_type=F32)
    return d(a1) + d(a2) + d(a3)


def _exact_right_dot(a, e):
    a1, a2, a3 = _split3(a)
    d = lambda p: jnp.dot(p, e, preferred_element_type=F32)
    return d(a1) + d(a2) + d(a3)


def _sigmoid(x):
    return 1.0 / (1.0 + jnp.exp(-x))


def _silu(x):
    return x * _sigmoid(x)


def _softplus(x):
    return jnp.maximum(x, 0.0) + jnp.log1p(jnp.exp(-jnp.abs(x)))


def _log_sigmoid(x):
    return -_softplus(-x)


def _layer_norm(z, g, b):
    mu = jnp.mean(z, axis=-1, keepdims=True)
    d = z - mu
    var = jnp.mean(d * d, axis=-1, keepdims=True)
    return d * lax.rsqrt(var + EPS) * g + b


def _lane_norm(y):
    mu = jnp.mean(y, axis=-1, keepdims=True)
    d = y - mu
    var = jnp.mean(d * d, axis=-1, keepdims=True)
    return d * lax.rsqrt(var + EPS)


def _proj_body(x_ref, w_ref, *o_refs, widths, prec):
    h = _wdot(x_ref[...], w_ref, prec)
    off = 0
    for o_ref, wd in zip(o_refs, widths):
        o_ref[...] = h[:, off:off + wd]
        off += wd


def _project(x, w, widths, prec=False):
    M, K = x.shape
    tm = _tile(M, ROW_TILE)
    return pl.pallas_call(
        functools.partial(_proj_body, widths=widths, prec=prec),
        grid=(M // tm,),
        in_specs=[pl.BlockSpec((tm, K), lambda i: (i, 0)), _wspec(w)],
        out_specs=[pl.BlockSpec((tm, wd), lambda i: (i, 0)) for wd in widths],
        out_shape=[jax.ShapeDtypeStruct((M, wd), F32) for wd in widths],
        compiler_params=_cparams(("parallel",)),
        name="project",
    )(x, w)


def _mem_proj_body(x_ref, w_ref, o_ref):
    o_ref[0] = _wdot(x_ref[...], w_ref.at[0], True)


def _mem_project(x, w):
    M, K = x.shape
    J, _, _, N = w.shape
    return pl.pallas_call(
        _mem_proj_body,
        grid=(J,),
        in_specs=[pl.BlockSpec((M, K), lambda j: (0, 0)),
                  pl.BlockSpec((1, 2, K, N), lambda j: (j, 0, 0, 0))],
        out_specs=pl.BlockSpec((1, M, N), lambda j: (j, 0, 0)),
        out_shape=jax.ShapeDtypeStruct((J, M, N), F32),
        compiler_params=_cparams(("parallel",)),
        name="mem_project",
    )(x, w)


def _out_ln_body(x_ref, a_ref, b_ref, wa_ref, wb_ref, g_ref, bias_ref, o_ref, *, prec):
    mix = _wdot(a_ref[...], wa_ref, prec) + _wdot(b_ref[...], wb_ref, prec)
    o_ref[...] = _layer_norm(ALPHA * x_ref[...] + mix, g_ref[...], bias_ref[...])


def _out_proj_ln(x, a, b, wa, wb, g, bias, prec=False):
    M, D = x.shape
    Ka, Kb = a.shape[1], b.shape[1]
    tm = _tile(M, ROW_TILE)
    row = lambda i: (i, 0)
    fix = lambda i: (0, 0)
    return pl.pallas_call(
        functools.partial(_out_ln_body, prec=prec),
        grid=(M // tm,),
        in_specs=[pl.BlockSpec((tm, D), row), pl.BlockSpec((tm, Ka), row), pl.BlockSpec((tm, Kb), row),
                  _wspec(wa), _wspec(wb),
                  pl.BlockSpec((1, D), fix), pl.BlockSpec((1, D), fix)],
        out_specs=pl.BlockSpec((tm, D), row),
        out_shape=jax.ShapeDtypeStruct((M, D), F32),
        compiler_params=_cparams(("parallel",)),
        name="out_proj_ln",
    )(x, a, b, wa, wb, g.reshape(1, D), bias.reshape(1, D))


def _ret_body(q_ref, k_ref, v_ref, g_ref, cos_ref, sa_ref, sb_ref, dmat_ref, qdec_ref, kdec_ref,
              s0_ref, gn_ref, o_ref, sout_ref, s_scr, *, sdec, prec):
    i = pl.program_id(1)

    @pl.when(i == 0)
    def _():
        s_scr[...] = s0_ref[0]

    cos, sin_a, sin_b = cos_ref[...], sa_ref[...], sb_ref[...]
    width = RET_HEADS * RET_DK
    half = RET_DK // 2

    def rope(x):
        return x * cos + pltpu.roll(x, width - half, 1) * sin_a + pltpu.roll(x, half, 1) * sin_b

    q = rope(q_ref[...])
    k = rope(k_ref[...]) * (RET_DK ** -0.5)
    v = v_ref[...]
    gate = g_ref[...]
    outs = []
    for h in range(RET_HEADS):
        qh = q[:, h * RET_DK:(h + 1) * RET_DK]
        kh = k[:, h * RET_DK:(h + 1) * RET_DK]
        vh = v[:, h * RET_DV:(h + 1) * RET_DV]
        s_old = s_scr[h]
        s = _dot_nt(qh, kh, prec) * dmat_ref[h]
        o = _dot(s, vh, prec) + _dot(qh, s_old, prec) * qdec_ref[h]
        s_scr[h] = sdec[h] * s_old + _dot_tn(kh * kdec_ref[h], vh, prec)
        outs.append(_lane_norm(o))
    hn = jnp.concatenate(outs, axis=1) * gn_ref[...]
    o_ref[...] = _silu(gate) * hn

    @pl.when(i == pl.num_programs(1) - 1)
    def _():
        sout_ref[0] = s_scr[...]


def _retention(hq, hk, hv, hg, s0, gn_g, B, T, pos0, L, prec=False):
    nT = T // L
    lg = np.log(1.0 - 2.0 ** (-5.0 - np.arange(RET_HEADS)))
    idx = np.arange(L, dtype=np.float64)
    rel = idx[:, None] - idx[None, :]
    dmat = np.where(rel >= 0, np.exp(np.maximum(rel, 0.0)[None] * lg[:, None, None]), 0.0)
    qdec = np.exp((idx + 1.0)[None, :, None] * lg[:, None, None])
    kdec = np.exp((L - 1.0 - idx)[None, :, None] * lg[:, None, None])
    sdec = tuple(float(np.exp(L * x)) for x in lg)
    half = RET_DK // 2
    inv = ROPE_BASE ** (-jnp.arange(half, dtype=F32) / half)
    ang = (pos0 + jnp.arange(T)).astype(F32)[:, None] * inv[None, :]
    cos, sin = jnp.cos(ang), jnp.sin(ang)
    zero = jnp.zeros_like(sin)
    cos_t = jnp.tile(jnp.concatenate([cos, cos], axis=1), (1, RET_HEADS))
    sin_a = jnp.tile(jnp.concatenate([-sin, zero], axis=1), (1, RET_HEADS))
    sin_b = jnp.tile(jnp.concatenate([zero, sin], axis=1), (1, RET_HEADS))
    qk_w = RET_HEADS * RET_DK
    v_w = RET_HEADS * RET_DV
    row = lambda b, i: (b * nT + i, 0)
    tab = lambda b, i: (i, 0)
    fix3 = lambda b, i: (0, 0, 0)
    out, s_out = pl.pallas_call(
        functools.partial(_ret_body, sdec=sdec, prec=prec),
        grid=(B, nT),
        in_specs=[pl.BlockSpec((L, qk_w), row), pl.BlockSpec((L, qk_w), row),
                  pl.BlockSpec((L, v_w), row), pl.BlockSpec((L, v_w), row),
                  pl.BlockSpec((L, qk_w), tab), pl.BlockSpec((L, qk_w), tab), pl.BlockSpec((L, qk_w), tab),
                  pl.BlockSpec((RET_HEADS, L, L), fix3), pl.BlockSpec((RET_HEADS, L, 1), fix3),
                  pl.BlockSpec((RET_HEADS, L, 1), fix3),
                  pl.BlockSpec((1, RET_HEADS, RET_DK, RET_DV), lambda b, i: (b, 0, 0, 0)),
                  pl.BlockSpec((1, v_w), lambda b, i: (0, 0))],
        out_specs=[pl.BlockSpec((L, v_w), row),
                   pl.BlockSpec((1, RET_HEADS, RET_DK, RET_DV), lambda b, i: (b, 0, 0, 0))],
        out_shape=[jax.ShapeDtypeStruct((B * T, v_w), F32),
                   jax.ShapeDtypeStruct((B, RET_HEADS, RET_DK, RET_DV), F32)],
        scratch_shapes=[pltpu.VMEM((RET_HEADS, RET_DK, RET_DV), F32)],
        compiler_params=_cparams(("parallel", "arbitrary")),
        name="retention",
    )(hq, hk, hv, hg, cos_t, sin_a, sin_b, jnp.asarray(dmat, F32), jnp.asarray(qdec, F32),
      jnp.asarray(kdec, F32), s0, gn_g.reshape(1, v_w))
    return out, s_out


def _rwkv_prep_body(h_ref, p_ref, mu_ref, w0_ref, w2_ref, a0_ref, a2_ref, g2_ref, kk_ref, ka_ref, rk_ref,
                    ones_ref, tri_ref, blk_ref, sel_ref,
                    kq_ref, rq_ref, kt_ref, bt_ref, ke_ref, be_ref, v_ref, g_ref, bon_ref, gt_ref, *, prec):
    hb = h_ref[...]
    xs = hb + (p_ref[...] - hb) * mu_ref[...]
    W = RWKV_W
    r, k, v = xs[:, :W], xs[:, W:2 * W], xs[:, 2 * W:3 * W]
    w_lo = xs[:, 3 * W:3 * W + 64]
    a_lo = xs[:, 3 * W + 64:3 * W + 128]
    g_lo = xs[:, 3 * W + 128:]
    w_log = -_softplus(-(w0_ref[...] + _dot(jnp.tanh(w_lo), w2_ref[...], prec))) - 0.5
    lw = -jnp.exp(w_log)
    a = _sigmoid(a0_ref[...] + _dot(a_lo, a2_ref[...], prec))
    g = _dot(_sigmoid(g_lo), g2_ref[...], prec)
    ones = ones_ref[...]
    kk = k * kk_ref[...]
    nrm = jnp.sqrt(_exact_right_dot(kk * kk, ones))
    kk = kk / jnp.maximum(nrm, 1e-12)
    k2 = k * (1.0 + (a - 1.0) * ka_ref[...])
    beta = kk * a
    cl = _exact_left_dot(tri_ref[...], lw)
    tot = _exact_left_dot(blk_ref[...], lw)
    ginv = jnp.exp(-cl)
    gend = jnp.exp(tot - cl)
    kq_ref[...] = (kk * jnp.exp(cl - lw)).astype(kq_ref.dtype)
    rq_ref[...] = (r * jnp.exp(cl)).astype(rq_ref.dtype)
    kt_ref[...] = (k2 * ginv).astype(kt_ref.dtype)
    bt_ref[...] = (beta * ginv).astype(bt_ref.dtype)
    ke_ref[...] = (k2 * gend).astype(ke_ref.dtype)
    be_ref[...] = (beta * gend).astype(be_ref.dtype)
    v_ref[...] = v.astype(v_ref.dtype)
    gt_ref[...] = jnp.exp(_exact_left_dot(sel_ref[...], lw))
    g_ref[...] = g
    bon_ref[...] = _exact_right_dot(r * k2 * rk_ref[...], ones) * v


def _rwkv_prep(hb, prev, P, L, prec=False):
    M = hb.shape[0]
    tm = _tile(M, ROW_TILE)
    assert tm % L == 0
    W = RWKV_W
    idx = np.arange(tm)
    same = (idx[:, None] // L) == (idx[None, :] // L)
    tri = jnp.asarray(same & (idx[:, None] >= idx[None, :]), BF16)
    blk = jnp.asarray(same, BF16)
    sel = jnp.asarray((idx[None, :] // L) == np.arange(tm // L)[:, None], BF16)
    lane = np.arange(W)
    ones = jnp.asarray((lane[:, None] // RWKV_HD) == (lane[None, :] // RWKV_HD), BF16)
    row = lambda i: (i, 0)
    fix = lambda i: (0, 0)
    vec = lambda a: a.reshape(1, -1)
    full = lambda a: pl.BlockSpec(a.shape, fix)
    params = [vec(P['rwkv_mu']), vec(P['rwkv_w0']), P['rwkv_w2'], vec(P['rwkv_a0']),
              P['rwkv_a2'], P['rwkv_g2'], vec(P['rwkv_k_k']), vec(P['rwkv_k_a']),
              vec(P['rwkv_r_k']), ones, tri, blk, sel]
    out_dt = [F32 if prec else BF16] * 7 + [F32] * 2
    return pl.pallas_call(
        functools.partial(_rwkv_prep_body, prec=prec),
        grid=(M // tm,),
        in_specs=[pl.BlockSpec((tm, RWKV_PROJ), row), pl.BlockSpec((tm, RWKV_PROJ), row)]
                 + [full(a) for a in params],
        out_specs=[pl.BlockSpec((tm, W), row) for _ in out_dt] + [pl.BlockSpec((tm // L, W), row)],
        out_shape=[jax.ShapeDtypeStruct((M, W), dt) for dt in out_dt] + [jax.ShapeDtypeStruct((M // L, W), F32)],
        compiler_params=_cparams(("parallel",)),
        name="rwkv_prep",
    )(hb, prev, *params)


def _rwkv_masks(L):
    ii = lax.broadcasted_iota(jnp.int32, (L, L), 0)
    jj = lax.broadcasted_iota(jnp.int32, (L, L), 1)
    return ii > jj, ii >= jj, (ii == jj).astype(F32)


def _rwkv_local(kq, rq, kt, bt, ke, be, v, masks, lmask_ref, nlev, prec):
    strict, incl, eye = masks
    L = kq.shape[1]
    x = jnp.concatenate([kq, rq], axis=1)
    sk = _dg(x, kt, _BNT, prec)
    sb = _dg(x, bt, _BNT, prec)
    n_m = jnp.where(strict, sk[:, :L], 0.0)
    a_k = jnp.where(incl, sk[:, L:], 0.0)
    m_m = jnp.where(strict, sb[:, :L], 0.0)
    a_b = jnp.where(incl, sb[:, L:], 0.0)
    t_m = eye - m_m * lmask_ref[0]
    for lv in range(1, nlev):
        c_m = m_m * lmask_ref[lv]
        t_m = t_m - _dg(_dg(t_m, c_m, _BNN, prec), t_m, _BNN, prec)
    kqp = _dg(t_m, kq, _BNN, prec)
    u0 = _dg(t_m, _dg(n_m, v, _BNN, prec), _BNN, prec)
    rqp = rq.astype(F32) - _dg(a_b, kqp, _BNN, prec)
    y0 = _dg(a_k, v, _BNN, prec) - _dg(a_b, u0, _BNN, prec)
    p_m = _dg(kqp, be, _BTN, prec)
    b_c = _dg(v, ke, _BTN, prec) - _dg(u0, be, _BTN, prec)
    return rqp, y0, p_m, b_c


def _rwkv_apply(s_old, rqp, y0, p_m, b_c, g_end, prec):
    y = _dg(rqp, s_old, _BNT, prec) + y0
    return y, s_old * g_end - _dg(s_old, p_m, _BNN, prec) + b_c


def _head_stack(tile, nblk):
    hd = RWKV_HD
    L = tile.shape[0] // nblk
    return jnp.concatenate([tile[:, hh * hd:(hh + 1) * hd].reshape(nblk, L, hd) for hh in range(2)], axis=0)


def _head_unstack(x, nblk):
    L, hd = x.shape[1], x.shape[2]
    return jnp.concatenate([x[hh * nblk:(hh + 1) * nblk].reshape(nblk * L, hd) for hh in range(2)], axis=1)


def _rwkv_local_body(kq_ref, rq_ref, kt_ref, bt_ref, ke_ref, be_ref, v_ref, lmask_ref,
                     rqp_ref, y0_ref, pm_ref, bc_ref, *, L, nchunk, nlev):
    ops = [_head_stack(r[...], nchunk) for r in (kq_ref, rq_ref, kt_ref, bt_ref, ke_ref, be_ref, v_ref)]
    res = _rwkv_local(*ops, _rwkv_masks(L), lmask_ref, nlev, False)
    for o_ref, a in zip((rqp_ref, y0_ref, pm_ref, bc_ref), res):
        o_ref[...] = _head_unstack(a, nchunk).astype(o_ref.dtype)


def _rwkv_apply_body(rqp_ref, y0_ref, pm_ref, bc_ref, gt_ref, g_ref, bon_ref, s0_ref, gng_ref, gnb_ref,
                     o_ref, sout_ref, s_scr, *, L, nchunk, B):
    i = pl.program_id(0)
    H, hd = RWKV_HEADS, RWKV_HD

    @pl.when(i == 0)
    def _():
        s_scr[...] = s0_ref[...].reshape(B * H, hd, hd)

    def heads(x):
        return jnp.stack([x[b][:, h * hd:(h + 1) * hd] for b in range(B) for h in range(H)], axis=0)

    def chunk(c, carry):
        rows = pl.ds(pl.multiple_of(c * L, L), L)
        ops = [heads(r[:, rows, :]) for r in (rqp_ref, y0_ref, pm_ref, bc_ref)]
        g_end = heads(gt_ref[:, pl.ds(c, 1), :])
        y, s_new = _rwkv_apply(s_scr[...], *ops, g_end, False)
        s_scr[...] = s_new
        yn = _lane_norm(y)
        for b in range(B):
            ynb = jnp.concatenate([yn[b * H + h] for h in range(H)], axis=1)
            o_ref[b, rows, :] = (ynb * gng_ref[...] + gnb_ref[...] + bon_ref[b, rows, :]) * g_ref[b, rows, :]
        return carry

    lax.fori_loop(0, nchunk, chunk, 0)

    @pl.when(i == pl.num_programs(0) - 1)
    def _():
        sout_ref[...] = s_scr[...].reshape(B, H, hd, hd)


def _rwkv_fused_body(kq_ref, rq_ref, kt_ref, bt_ref, ke_ref, be_ref, v_ref, g_ref, bon_ref, gt_ref,
                     s0_ref, gng_ref, gnb_ref, lmask_ref, o_ref, sout_ref, *, L, nseq, nlev, prec):
    hd = RWKV_HD
    ops = [_head_stack(r[...], nseq) for r in (kq_ref, rq_ref, kt_ref, bt_ref, ke_ref, be_ref, v_ref)]
    loc = _rwkv_local(*ops, _rwkv_masks(L), lmask_ref, nlev, prec)
    gt = gt_ref[...]
    g_end = jnp.concatenate([gt[:, :, hh * hd:(hh + 1) * hd] for hh in range(2)], axis=0)
    s_old = jnp.concatenate([s0_ref[:, hh] for hh in range(2)], axis=0)
    y, s_new = _rwkv_apply(s_old, *loc, g_end, prec)
    for hh in range(2):
        sout_ref[:, hh] = s_new[hh * nseq:(hh + 1) * nseq]
    yn = _head_unstack(_lane_norm(y), nseq)
    o_ref[...] = (yn * gng_ref[...] + gnb_ref[...] + bon_ref[...]) * g_ref[...]


def _rwkv_level_masks(L):
    nlev = int(math.log2(L))
    idx = np.arange(L)
    ii, jj = idx[:, None], idx[None, :]
    lmask = np.stack([((ii >> (lv + 1)) == (jj >> (lv + 1))) & ((ii & (1 << lv)) != 0) & ((jj & (1 << lv)) == 0)
                      for lv in range(nlev)]).astype(np.float32)
    return nlev, jnp.asarray(lmask)


def _rwkv_scan(pre, s0, gn_g, gn_b, B, T, L):
    W, H, hd = RWKV_W, RWKV_HEADS, RWKV_HD
    assert L == hd
    kq, rq, kt, bt, ke, be, v, g, bon, gt = pre
    nlev, lmask = _rwkv_level_masks(L)
    tb = _tile(T, RWKV_ROWS)
    nT = T // tb
    nchunk = tb // L
    pw = 2 * hd
    row = lambda b, p, i: (b * nT + i, p)
    rqp, y0, p_m, b_c = pl.pallas_call(
        functools.partial(_rwkv_local_body, L=L, nchunk=nchunk, nlev=nlev),
        grid=(B, H // 2, nT),
        in_specs=[pl.BlockSpec((tb, pw), row) for _ in range(7)]
                 + [pl.BlockSpec((nlev, L, L), lambda b, p, i: (0, 0, 0))],
        out_specs=[pl.BlockSpec((tb, pw), row) for _ in range(4)],
        out_shape=[jax.ShapeDtypeStruct((B * T, W), dt) for dt in (BF16, F32, BF16, F32)],
        compiler_params=_cparams(("parallel", "parallel", "parallel")),
        name="rwkv_local",
    )(kq, rq, kt, bt, ke, be, v, lmask)
    r3 = lambda a: a.reshape(B, T, W)
    blk = lambda i: (0, i, 0)
    fix2 = lambda i: (0, 0)
    fix4 = lambda i: (0, 0, 0, 0)
    out, s_out = pl.pallas_call(
        functools.partial(_rwkv_apply_body, L=L, nchunk=nchunk, B=B),
        grid=(nT,),
        in_specs=[pl.BlockSpec((B, tb, W), blk) for _ in range(4)]
                 + [pl.BlockSpec((B, nchunk, W), blk), pl.BlockSpec((B, tb, W), blk), pl.BlockSpec((B, tb, W), blk),
                    pl.BlockSpec((B, H, hd, hd), fix4), pl.BlockSpec((1, W), fix2), pl.BlockSpec((1, W), fix2)],
        out_specs=[pl.BlockSpec((B, tb, W), blk), pl.BlockSpec((B, H, hd, hd), fix4)],
        out_shape=[jax.ShapeDtypeStruct((B, T, W), F32), jax.ShapeDtypeStruct((B, H, hd, hd), F32)],
        scratch_shapes=[pltpu.VMEM((B * H, hd, hd), F32)],
        compiler_params=_cparams(("arbitrary",)),
        name="rwkv_apply",
    )(r3(rqp), r3(y0), r3(p_m), r3(b_c), gt.reshape(B, T // L, W), r3(g), r3(bon), s0,
      gn_g.reshape(1, W), gn_b.reshape(1, W))
    return out.reshape(B * T, W), s_out


def _rwkv_single_chunk(pre, s0, gn_g, gn_b, B, T, prec):
    W, H, hd = RWKV_W, RWKV_HEADS, RWKV_HD
    kq, rq, kt, bt, ke, be, v, g, bon, gt = pre
    nlev, lmask = _rwkv_level_masks(T)
    nseq = RWKV_SEQS if B % RWKV_SEQS == 0 else B
    pw = 2 * hd
    row = lambda i, p: (i, p)
    st = lambda i, p: (i, p, 0, 0)
    vec = lambda i, p: (0, p)
    out, s_out = pl.pallas_call(
        functools.partial(_rwkv_fused_body, L=T, nseq=nseq, nlev=nlev, prec=prec),
        grid=(B // nseq, H // 2),
        in_specs=[pl.BlockSpec((nseq * T, pw), row) for _ in range(9)]
                 + [pl.BlockSpec((nseq, 1, pw), lambda i, p: (i, 0, p)), pl.BlockSpec((nseq, 2, hd, hd), st),
                    pl.BlockSpec((1, pw), vec), pl.BlockSpec((1, pw), vec),
                    pl.BlockSpec((nlev, T, T), lambda i, p: (0, 0, 0))],
        out_specs=[pl.BlockSpec((nseq * T, pw), row), pl.BlockSpec((nseq, 2, hd, hd), st)],
        out_shape=[jax.ShapeDtypeStruct((B * T, W), F32), jax.ShapeDtypeStruct((B, H, hd, hd), F32)],
        compiler_params=_cparams(("parallel", "parallel")),
        name="rwkv_single_chunk",
    )(kq, rq, kt, bt, ke, be, v, g, bon, gt.reshape(B, 1, W), s0, gn_g.reshape(1, W), gn_b.reshape(1, W), lmask)
    return out, s_out


def _mlstm_body(qk_ref, v_ref, o_ref, gt_ref, cprev_ref, c0_ref, n0_ref, m0_ref, cw_ref, cb_ref, gb_ref,
                gn_ref, tri_ref, out_ref, cout_ref, nout_ref, mout_ref, convout_ref,
                xpad, c_scr, m_scr, *, L):
    i = pl.program_id(1)
    H, DK, DV = MLSTM_HEADS, MLSTM_DK, MLSTM_DV
    K = MLSTM_CONV - 1
    base = 8 - K

    @pl.when(i == 0)
    def _():
        xpad[base:8, :] = cprev_ref[0]
        for h in range(H):
            c_scr[h, :, 0:DV] = c0_ref[0, h]
            c_scr[h, :, DV:2 * DV] = jnp.broadcast_to(n0_ref[0, h], (DK, DV))
        m_scr[...] = m0_ref[0]

    xpad[8:8 + L, :] = qk_ref[...]
    conv = cb_ref[...] + xpad[pl.ds(base, L), :] * cw_ref[0:1, :]
    for j in range(1, MLSTM_CONV):
        conv = conv + xpad[pl.ds(base + j, L), :] * cw_ref[j:j + 1, :]
    tail = xpad[pl.ds(8 + L - K, K), :]
    xpad[base:8, :] = tail
    qk = _silu(conv)
    q = qk[:, :H * DK] * (DK ** -0.5)
    k = qk[:, H * DK:]
    v = v_ref[...]
    z = gt_ref[...] + gb_ref[...]
    lf = _log_sigmoid(z)
    bcum = _exact_left_dot(tri_ref[...], lf)
    z_t = z.T
    b_t = bcum.T
    ii = lax.broadcasted_iota(jnp.int32, (L, L), 0)
    jj = lax.broadcasted_iota(jnp.int32, (L, L), 1)
    causal = ii >= jj
    ones = jnp.ones((L, DV), F32)
    m_all = m_scr[...]
    outs = []
    m_new_all = m_all
    lane = lax.broadcasted_iota(jnp.int32, (1, 128), 1)
    for h in range(H):
        qh = q[:, h * DK:(h + 1) * DK]
        kh = k[:, h * DK:(h + 1) * DK]
        vh = jnp.concatenate([v[:, h * DV:(h + 1) * DV], ones], axis=1)
        b_col = bcum[:, H + h:H + h + 1]
        ig_col = z[:, h:h + 1]
        row_term = z_t[h:h + 1, :] - b_t[H + h:H + h + 1, :]
        m0 = m_all[:, h:h + 1]
        log_d = jnp.where(causal, b_col + row_term, NEG_INF)
        m_inter = b_col + m0
        m_t = jnp.maximum(m_inter, jnp.max(log_d, axis=-1, keepdims=True))
        w_d = jnp.exp(log_d - m_t)
        w_i = jnp.exp(m_inter - m_t)
        c_old = c_scr[h]
        s = _dot_nt(qh, kh) * w_d
        num = _dot(s, vh) + _dot(qh, c_old) * w_i
        den = num[:, DV:DV + 1]
        denom = jnp.maximum(jnp.abs(den), jnp.exp(-m_t))
        outs.append(_lane_norm(num[:, :DV] / denom))
        m_new = m_t[L - 1:L, :]
        b_last = b_col[L - 1:L, :]
        w_s = jnp.exp(b_last + m0 - m_new)
        w_k = jnp.exp(b_last - b_col + ig_col - m_new)
        c_scr[h] = w_s * c_old + _dot_tn(kh * w_k, vh)
        m_new_all = jnp.where(lane == h, m_new, m_new_all)
    m_scr[...] = m_new_all
    hn = jnp.concatenate(outs, axis=1) * gn_ref[...]
    out_ref[...] = hn * _sigmoid(o_ref[...])

    @pl.when(i == pl.num_programs(1) - 1)
    def _():
        for h in range(H):
            cout_ref[0, h] = c_scr[h, :, 0:DV]
            nout_ref[0, h] = c_scr[h, :, DV:DV + 1]
        mout_ref[0] = m_scr[...]
        convout_ref[0] = tail


def _mlstm(hqk, hv, ho, hgate, conv_prev, c0, n0, m0, P, gate_bias, B, T, L):
    H, DK, DV = MLSTM_HEADS, MLSTM_DK, MLSTM_DV
    nT = T // L
    K = MLSTM_CONV - 1
    W = H * DV
    idx = np.arange(L)
    tri = jnp.asarray(idx[:, None] >= idx[None, :], BF16)
    m0p = jnp.zeros((B, 1, 128), F32).at[:, 0, :H].set(m0)
    row = lambda b, i: (b * nT + i, 0)
    fix = lambda b, i: (0, 0)
    perb3 = lambda b, i: (b, 0, 0)
    perb4 = lambda b, i: (b, 0, 0, 0)
    out, c_out, n_out, m_out, conv_out = pl.pallas_call(
        functools.partial(_mlstm_body, L=L),
        grid=(B, nT),
        in_specs=[pl.BlockSpec((L, W), row), pl.BlockSpec((L, W), row), pl.BlockSpec((L, W), row),
                  pl.BlockSpec((L, 128), row),
                  pl.BlockSpec((1, K, W), perb3),
                  pl.BlockSpec((1, H, DK, DV), perb4), pl.BlockSpec((1, H, DK, 1), perb4),
                  pl.BlockSpec((1, 1, 128), perb3),
                  pl.BlockSpec((MLSTM_CONV, W), fix), pl.BlockSpec((1, W), fix), pl.BlockSpec((1, 128), fix),
                  pl.BlockSpec((1, W), fix), pl.BlockSpec((L, L), fix)],
        out_specs=[pl.BlockSpec((L, W), row),
                   pl.BlockSpec((1, H, DK, DV), perb4), pl.BlockSpec((1, H, DK, 1), perb4),
                   pl.BlockSpec((1, 1, 128), perb3), pl.BlockSpec((1, K, W), perb3)],
        out_shape=[jax.ShapeDtypeStruct((B * T, W), F32),
                   jax.ShapeDtypeStruct((B, H, DK, DV), F32), jax.ShapeDtypeStruct((B, H, DK, 1), F32),
                   jax.ShapeDtypeStruct((B, 1, 128), F32), jax.ShapeDtypeStruct((B, K, W), F32)],
        scratch_shapes=[pltpu.VMEM((L + 8, W), F32), pltpu.VMEM((H, DK, 2 * DV), F32), pltpu.VMEM((1, 128), F32)],
        compiler_params=_cparams(("parallel", "arbitrary")),
        name="mlstm",
    )(hqk, hv, ho, hgate, conv_prev, c0, n0.reshape(B, H, DK, 1), m0p,
      P['mlstm_conv_w'], P['mlstm_conv_b'].reshape(1, W), gate_bias, P['mlstm_gn_g'].reshape(1, W), tri)
    return out, c_out, n_out.reshape(B, H, DK), m_out[:, 0, :H], conv_out


def _fox_prep_body(q_ref, k_ref, gt_ref, qg_ref, kg_ref, gb_ref, qn_ref, kn_ref, lf_ref):
    def rms(x, g):
        outs = []
        for h in range(FOX_HEADS):
            xh = x[:, h * FOX_HD:(h + 1) * FOX_HD]
            outs.append(xh * lax.rsqrt(jnp.mean(xh * xh, axis=-1, keepdims=True) + EPS) * g)
        return jnp.concatenate(outs, axis=1)

    qn_ref[...] = (rms(q_ref[...], qg_ref[...]) * (FOX_HD ** -0.5)).astype(BF16)
    kn_ref[...] = rms(k_ref[...], kg_ref[...])
    lf_ref[...] = _log_sigmoid(gt_ref[...] + gb_ref[...])


def _fox_prep(hq, hk, hgate, P, gate_bias):
    M, W = hq.shape
    tm = _tile(M, ROW_TILE)
    row = lambda i: (i, 0)
    fix = lambda i: (0, 0)
    return pl.pallas_call(
        _fox_prep_body,
        grid=(M // tm,),
        in_specs=[pl.BlockSpec((tm, W), row), pl.BlockSpec((tm, W), row), pl.BlockSpec((tm, 128), row),
                  pl.BlockSpec((1, FOX_HD), fix), pl.BlockSpec((1, FOX_HD), fix), pl.BlockSpec((1, 128), fix)],
        out_specs=[pl.BlockSpec((tm, W), row), pl.BlockSpec((tm, W), row), pl.BlockSpec((tm, 128), row)],
        out_shape=[jax.ShapeDtypeStruct((M, W), BF16), jax.ShapeDtypeStruct((M, W), F32),
                   jax.ShapeDtypeStruct((M, 128), F32)],
        compiler_params=_cparams(("parallel",)),
        name="fox_prep",
    )(hq, hk, hgate, P['fox_q_g'].reshape(1, FOX_HD), P['fox_k_g'].reshape(1, FOX_HD), gate_bias)


def _fox_prep_prompt_body(q_ref, k_ref, v_ref, gt_ref, qg_ref, kg_ref, gb_ref, tri_ref,
                          qa_ref, kn_ref, kt_ref, va_ref, lf_ref, carry):
    i = pl.program_id(1)
    tm = q_ref.shape[0]
    HD = FOX_HD

    @pl.when(i == 0)
    def _():
        carry[...] = jnp.zeros_like(carry)

    lf = _log_sigmoid(gt_ref[...] + gb_ref[...])
    lf_ref[...] = lf
    c = _exact_left_dot(tri_ref[...], lf) + carry[...]
    carry[...] = c[tm - 1:tm, :]
    c2t = (c * (-LOG2E)).T
    lane = lax.broadcasted_iota(jnp.int32, (tm, HD), 1)
    q_ones = jnp.where(lane < 3, 1.0, 0.0).astype(BF16)
    v_ones = jnp.where(lane < 1, 1.0, 0.0).astype(BF16)
    row16 = lax.broadcasted_iota(jnp.int32, (16, tm), 0)
    q, k, v = q_ref[...], k_ref[...], v_ref[...]
    for h in range(FOX_HEADS):
        hs = slice(h * HD, (h + 1) * HD)
        qh, kh = q[:, hs], k[:, hs]
        qh = qh * lax.rsqrt(jnp.mean(qh * qh, axis=-1, keepdims=True) + EPS) * qg_ref[...]
        kh = kh * lax.rsqrt(jnp.mean(kh * kh, axis=-1, keepdims=True) + EPS) * kg_ref[...]
        qa_ref[:, 2 * h * HD:(2 * h + 1) * HD] = (qh * (HD ** -0.5 * LOG2E)).astype(BF16)
        qa_ref[:, (2 * h + 1) * HD:(2 * h + 2) * HD] = q_ones
        kn_ref[:, hs] = kh
        kt_ref[0, h, 0, 0:HD, :] = kh.T.astype(BF16)
        bias = c2t[2 * MLSTM_HEADS + h:2 * MLSTM_HEADS + h + 1, :]
        hi = bias.astype(BF16).astype(F32)
        mid = (bias - hi).astype(BF16).astype(F32)
        lo = bias - hi - mid
        blk = jnp.where(row16 == 0, hi, jnp.where(row16 == 1, mid, jnp.where(row16 == 2, lo, 0.0)))
        kt_ref[0, h, 0, HD:HD + 16, :] = blk.astype(BF16)
        kt_ref[0, h, 0, HD + 16:2 * HD, :] = jnp.zeros((HD - 16, tm), BF16)
        va_ref[:, 2 * h * HD:(2 * h + 1) * HD] = v[:, hs].astype(BF16)
        va_ref[:, (2 * h + 1) * HD:(2 * h + 2) * HD] = v_ones


def _fox_prep_prompt(hq, hk, hv, hgate, P, gate_bias, B, T):
    M, W = hq.shape
    H, HD = FOX_HEADS, FOX_HD
    tm = _tile(T, FOX_TQ)
    nT = T // tm
    idx = np.arange(tm)
    tri = jnp.asarray(idx[:, None] >= idx[None, :], BF16)
    row = lambda b, i: (b * nT + i, 0)
    fix = lambda b, i: (0, 0)
    return pl.pallas_call(
        _fox_prep_prompt_body,
        grid=(B, nT),
        in_specs=[pl.BlockSpec((tm, W), row), pl.BlockSpec((tm, W), row), pl.BlockSpec((tm, W), row),
                  pl.BlockSpec((tm, 128), row),
                  pl.BlockSpec((1, HD), fix), pl.BlockSpec((1, HD), fix), pl.BlockSpec((1, 128), fix),
                  pl.BlockSpec((tm, tm), fix)],
        out_specs=[pl.BlockSpec((tm, 2 * W), row), pl.BlockSpec((tm, W), row),
                   pl.BlockSpec((1, H, 1, 2 * HD, tm), lambda b, i: (b, 0, i, 0, 0)),
                   pl.BlockSpec((tm, 2 * W), row), pl.BlockSpec((tm, 128), row)],
        out_shape=[jax.ShapeDtypeStruct((M, 2 * W), BF16), jax.ShapeDtypeStruct((M, W), F32),
                   jax.ShapeDtypeStruct((B, H, nT, 2 * HD, tm), BF16),
                   jax.ShapeDtypeStruct((M, 2 * W), BF16), jax.ShapeDtypeStruct((M, 128), F32)],
        scratch_shapes=[pltpu.VMEM((1, 128), F32)],
        compiler_params=_cparams(("parallel", "arbitrary")),
        name="fox_prep_prompt",
    )(hq, hk, hv, hgate, P['fox_q_g'].reshape(1, HD), P['fox_k_g'].reshape(1, HD), gate_bias, tri)


def _fox_prompt_body(q_ref, kt_ref, va_ref, o_ref, m_scr, acc_scr, *, tq):
    qi = pl.program_id(2)
    q = q_ref[0]
    m_scr[...] = jnp.full_like(m_scr, NEG_INF)
    acc_scr[...] = jnp.zeros_like(acc_scr)

    def step(kj, masked):
        kb = kt_ref[0, 0, kj]
        vb = va_ref[0, pl.ds(pl.multiple_of(kj * tq, tq), tq), :]
        s = jnp.dot(q, kb, preferred_element_type=F32)
        if masked:
            ii = lax.broadcasted_iota(jnp.int32, (tq, tq), 0)
            jj = lax.broadcasted_iota(jnp.int32, (tq, tq), 1)
            s = jnp.where(jj <= ii, s, NEG_INF)
        m_old = m_scr[...]
        m_new = jnp.maximum(m_old, jnp.max(s, axis=-1, keepdims=True))
        p = jnp.exp2(s - m_new)
        acc_scr[...] = jnp.exp2(m_old - m_new) * acc_scr[...] + jnp.dot(p.astype(BF16), vb, preferred_element_type=F32)
        m_scr[...] = m_new

    def body(j, carry):
        step(2 * j, False)
        step(2 * j + 1, False)
        return carry

    lax.fori_loop(0, lax.shift_right_logical(qi, 1), body, 0)

    @pl.when((qi & 1) == 1)
    def _():
        step(qi - 1, False)

    step(qi, True)
    acc = acc_scr[...]
    o_ref[0] = acc[:, :FOX_HD] / acc[:, FOX_HD:FOX_HD + 1]


def _fox_prompt(qa, kt, va, B, T):
    H, HD = FOX_HEADS, FOX_HD
    W = H * HD
    tq = kt.shape[-1]
    nQ = T // tq
    out = pl.pallas_call(
        functools.partial(_fox_prompt_body, tq=tq),
        grid=(B, H, nQ),
        in_specs=[pl.BlockSpec((1, tq, 2 * HD), lambda b, h, i: (b, i, h)),
                  pl.BlockSpec((1, 1, nQ, 2 * HD, tq), lambda b, h, i: (b, h, 0, 0, 0),
                               pipeline_mode=pl.Buffered(1)),
                  pl.BlockSpec((1, T, 2 * HD), lambda b, h, i: (b, 0, h), pipeline_mode=pl.Buffered(1))],
        out_specs=pl.BlockSpec((1, tq, HD), lambda b, h, i: (b, i, h)),
        out_shape=jax.ShapeDtypeStruct((B, T, W), F32),
        scratch_shapes=[pltpu.VMEM((tq, 1), F32), pltpu.VMEM((tq, 2 * HD), F32)],
        compiler_params=_cparams(("parallel", "parallel", "arbitrary")),
        name="fox_prompt",
    )(qa.reshape(B, T, 2 * W), kt, va.reshape(B, T, 2 * W))
    return out.reshape(B * T, W)


def _fox_sample_body(q_ref, kc_ref, vc_ref, kn_ref, vn_ref, cq_ref, ckc_ref, ckn_ref, o_ref, *, T):
    H, HD = FOX_HEADS, FOX_HD
    q = q_ref[0]
    kc, vc, kn, vn = kc_ref[0], vc_ref[0], kn_ref[0], vn_ref[0]
    ii = lax.broadcasted_iota(jnp.int32, (T, T), 0)
    jj = lax.broadcasted_iota(jnp.int32, (T, T), 1)
    outs = []
    for h in range(H):
        cs = slice(h * HD, (h + 1) * HD)
        cq = cq_ref[0, h]
        s1 = _dot_nt(q[:, cs], kc[:, cs]) + (cq - ckc_ref[0, h])
        s2 = _dot_nt(q[:, cs], kn[:, cs]) + (cq - ckn_ref[0, h])
        s2 = jnp.where(jj <= ii, s2, NEG_INF)
        m = jnp.maximum(jnp.max(s1, axis=-1, keepdims=True), jnp.max(s2, axis=-1, keepdims=True))
        p1 = jnp.exp(s1 - m)
        p2 = jnp.exp(s2 - m)
        den = jnp.sum(p1, axis=-1, keepdims=True) + jnp.sum(p2, axis=-1, keepdims=True)
        outs.append((_dot(p1, vc[:, cs]) + _dot(p2, vn[:, cs])) / den)
    o_ref[0] = jnp.concatenate(outs, axis=1)


def _fox_sample(qn, kn, vn, k_cache, v_cache, c_all, B, T):
    H, HD = FOX_HEADS, FOX_HD
    W = H * HD
    past = k_cache.shape[1]
    ct = jnp.transpose(c_all, (0, 2, 1))
    cq = ct[:, :, past:].reshape(B, H, T, 1)
    ckc = ct[:, :, :past].reshape(B, H, 1, past)
    ckn = ct[:, :, past:].reshape(B, H, 1, T)
    b3 = lambda b: (b, 0, 0)
    b4 = lambda b: (b, 0, 0, 0)
    out = pl.pallas_call(
        functools.partial(_fox_sample_body, T=T),
        grid=(B,),
        in_specs=[pl.BlockSpec((1, T, W), b3), pl.BlockSpec((1, past, W), b3), pl.BlockSpec((1, past, W), b3),
                  pl.BlockSpec((1, T, W), b3), pl.BlockSpec((1, T, W), b3),
                  pl.BlockSpec((1, H, T, 1), b4), pl.BlockSpec((1, H, 1, past), b4), pl.BlockSpec((1, H, 1, T), b4)],
        out_specs=pl.BlockSpec((1, T, W), b3),
        out_shape=jax.ShapeDtypeStruct((B, T, W), F32),
        compiler_params=_cparams(("parallel",)),
        name="fox_sample",
    )(qn.reshape(B, T, W), k_cache.reshape(B, past, W), v_cache.reshape(B, past, W),
      kn.reshape(B, T, W), vn.reshape(B, T, W), cq, ckc, ckn)
    return out.reshape(B * T, W)


def _mem_attn_body(x_ref, mk_ref, mv_ref, wq_ref, wo_ref, g_ref, b_ref, o_ref, *, prec):
    x = x_ref[...]
    q = _wdot(x, wq_ref, prec)
    mk, mv = mk_ref[0], mv_ref[0]
    outs = []
    for h in range(MEM_HEADS):
        cs = slice(h * MEM_HD, (h + 1) * MEM_HD)
        s = _dot_nt(q[:, cs], mk[:, cs], prec) * (MEM_HD ** -0.5)
        m = jnp.max(s, axis=-1, keepdims=True)
        p = jnp.exp(s - m)
        outs.append(_dot(p, mv[:, cs], prec) / jnp.sum(p, axis=-1, keepdims=True))
    o = jnp.concatenate(outs, axis=1)
    att = _wdot(o, wo_ref, prec)
    o_ref[...] = _layer_norm(ALPHA * x + att, g_ref[...], b_ref[...])


def _mem_attn_ln(x, mk, mv, l, wq, wo, g, bias, B, T, prec=False):
    D = D_MODEL
    Mm = mk.shape[1]
    tm = _tile(T, ROW_TILE)
    nT = T // tm
    row = lambda b, i: (b * nT + i, 0)
    fix = lambda b, i: (0, 0)
    mem = lambda b, i: (l * B + b, 0, 0)
    return pl.pallas_call(
        functools.partial(_mem_attn_body, prec=prec),
        grid=(B, nT),
        in_specs=[pl.BlockSpec((tm, D), row), pl.BlockSpec((1, Mm, D), mem), pl.BlockSpec((1, Mm, D), mem),
                  _wspec(wq), _wspec(wo),
                  pl.BlockSpec((1, D), fix), pl.BlockSpec((1, D), fix)],
        out_specs=pl.BlockSpec((tm, D), row),
        out_shape=jax.ShapeDtypeStruct((B * T, D), F32),
        compiler_params=_cparams(("parallel", "parallel")),
        name="mem_attn_ln",
    )(x, mk, mv, wq, wo, g.reshape(1, D), bias.reshape(1, D))


def _router_body(x_ref, w_ref, b_ref, o_ref):
    x = x_ref[...]
    w = w_ref[...]
    x1, x2, x3 = _split3(x)
    w1, w2, w3 = _split3(w)
    nt = lambda a, c: lax.dot_general(a, c, _NT, preferred_element_type=F32)
    logits = (nt(w1, x1) + (nt(w1, x2) + nt(w2, x1)) + (nt(w1, x3) + nt(w2, x2) + nt(w3, x1))) + b_ref[...]
    m = jnp.max(logits, axis=0, keepdims=True)
    e = jnp.exp(logits - m)
    p = e / jnp.sum(e, axis=0, keepdims=True)
    rows = [p[j:j + 1, :] for j in range(N_EXPERTS)]
    best = None
    sel = None
    for g in range(N_GROUPS):
        a, b, c, d = rows[4 * g:4 * g + 4]
        top2 = jnp.maximum(jnp.maximum(jnp.maximum(a + b, a + c), jnp.maximum(a + d, b + c)),
                           jnp.maximum(b + d, c + d))
        if g == 0:
            best, sel = top2, jnp.zeros_like(top2, dtype=jnp.int32)
        else:
            upd = top2 > best
            sel = jnp.where(upd, g, sel)
            best = jnp.maximum(best, top2)
    pin = []
    for kk in range(EXPERTS_PER_GROUP):
        v = rows[kk]
        for g in range(1, N_GROUPS):
            v = jnp.where(sel == g, rows[4 * g + kk], v)
        pin.append(v)
    v1, i1 = pin[0], jnp.zeros_like(sel)
    for kk in range(1, EXPERTS_PER_GROUP):
        upd = pin[kk] > v1
        i1 = jnp.where(upd, kk, i1)
        v1 = jnp.maximum(v1, pin[kk])
    v2, i2 = None, None
    for kk in range(EXPERTS_PER_GROUP):
        cand = jnp.where(i1 == kk, -1.0, pin[kk])
        if v2 is None:
            v2, i2 = cand, jnp.zeros_like(sel)
        else:
            upd = cand > v2
            i2 = jnp.where(upd, kk, i2)
            v2 = jnp.maximum(v2, cand)
    tot = v1 + v2
    e1 = (sel * EXPERTS_PER_GROUP + i1).astype(F32)
    e2 = (sel * EXPERTS_PER_GROUP + i2).astype(F32)
    zero = jnp.zeros_like(v1)
    o_ref[...] = jnp.concatenate([e1, e2, v1 / tot, v2 / tot, zero, zero, zero, zero], axis=0)


def _router(x, w_router, b_router):
    M, D = x.shape
    tm = _tile(M, ROW_TILE)
    return pl.pallas_call(
        _router_body,
        grid=(M // tm,),
        in_specs=[pl.BlockSpec((tm, D), lambda i: (i, 0)),
                  pl.BlockSpec((N_EXPERTS, D), lambda i: (0, 0)),
                  pl.BlockSpec((N_EXPERTS, 1), lambda i: (0, 0))],
        out_specs=pl.BlockSpec((8, tm), lambda i: (0, i)),
        out_shape=jax.ShapeDtypeStruct((8, M), F32),
        compiler_params=_cparams(("parallel",)),
        name="router",
    )(x, w_router.T, b_router.reshape(N_EXPERTS, 1))


def _expert_body(be_ref, x_ref, wg_ref, wu_ref, wd_ref, o_ref, wg_s, wu_s, wd_s):
    i = pl.program_id(0)
    prev = be_ref[jnp.maximum(i - 1, 0)]

    @pl.when((i == 0) | (be_ref[i] != prev))
    def _():
        wg_s[...] = wg_ref[0].astype(BF16)
        wu_s[...] = wu_ref[0].astype(BF16)
        wd_s[...] = wd_ref[0].astype(BF16)

    x = x_ref[...].astype(BF16)
    hg = jnp.dot(x, wg_s[...], preferred_element_type=F32)
    hu = jnp.dot(x, wu_s[...], preferred_element_type=F32)
    hb = (_silu(hg) * hu).astype(BF16)
    o_ref[...] = jnp.dot(hb, wd_s[...], preferred_element_type=F32)


def _experts(xg, blk_exp, wg, wu, wd):
    n_blocks = blk_exp.shape[0]
    D, DE = D_MODEL, D_EXPERT
    grid_spec = pltpu.PrefetchScalarGridSpec(
        num_scalar_prefetch=1,
        grid=(n_blocks,),
        in_specs=[pl.BlockSpec((MOE_BLOCK, D), lambda i, be: (i, 0)),
                  pl.BlockSpec((1, D, DE), lambda i, be: (be[i], 0, 0)),
                  pl.BlockSpec((1, D, DE), lambda i, be: (be[i], 0, 0)),
                  pl.BlockSpec((1, DE, D), lambda i, be: (be[i], 0, 0))],
        out_specs=pl.BlockSpec((MOE_BLOCK, D), lambda i, be: (i, 0)),
        scratch_shapes=[pltpu.VMEM((D, DE), BF16), pltpu.VMEM((D, DE), BF16), pltpu.VMEM((DE, D), BF16)],
    )
    return pl.pallas_call(
        _expert_body,
        grid_spec=grid_spec,
        out_shape=jax.ShapeDtypeStruct((n_blocks * MOE_BLOCK, D), F32),
        compiler_params=_cparams(("arbitrary",)),
        name="experts",
    )(blk_exp, xg, wg, wu, wd)


def _combine_ln_body(x_ref, y0_ref, y1_ref, gt_ref, g_ref, b_ref, o_ref):
    gt = gt_ref[...]
    ffn = y0_ref[...] * gt[:, 0:1] + y1_ref[...] * gt[:, 1:2]
    o_ref[...] = _layer_norm(ALPHA * x_ref[...] + ffn, g_ref[...], b_ref[...])


def _combine_ln(x, y0, y1, gates, g, bias):
    M, D = x.shape
    tm = _tile(M, ROW_TILE)
    row = lambda i: (i, 0)
    fix = lambda i: (0, 0)
    return pl.pallas_call(
        _combine_ln_body,
        grid=(M // tm,),
        in_specs=[pl.BlockSpec((tm, D), row), pl.BlockSpec((tm, D), row), pl.BlockSpec((tm, D), row),
                  pl.BlockSpec((tm, 2), row), pl.BlockSpec((1, D), fix), pl.BlockSpec((1, D), fix)],
        out_specs=pl.BlockSpec((tm, D), row),
        out_shape=jax.ShapeDtypeStruct((M, D), F32),
        compiler_params=_cparams(("parallel",)),
        name="combine_ln",
    )(x, y0, y1, gates, g.reshape(1, D), bias.reshape(1, D))


def _moe_ln(x, P, l):
    M = x.shape[0]
    r = _router(x, P['w_router'], P['b_router'])
    e = r[0:2].astype(jnp.int32).T.reshape(-1)
    gates = r[2:4].T
    oh = (e[:, None] == jnp.arange(N_EXPERTS, dtype=jnp.int32)[None, :]).astype(jnp.int32)
    csum = jnp.cumsum(oh, axis=0)
    rank = jnp.sum((csum - oh) * oh, axis=1)
    counts = csum[-1]
    padded = (counts + MOE_BLOCK - 1) // MOE_BLOCK * MOE_BLOCK
    p_ends = jnp.cumsum(padded)
    p_starts = p_ends - padded
    dest = p_starts[e] + rank
    n_blocks = -(-2 * M // MOE_BLOCK) + N_EXPERTS
    tok = jnp.arange(2 * M, dtype=jnp.int32) // 2
    slot_tok = jnp.zeros((n_blocks * MOE_BLOCK,), jnp.int32).at[dest].set(tok)
    blk_start = jnp.arange(n_blocks, dtype=jnp.int32) * MOE_BLOCK
    blk_exp = jnp.minimum(jnp.sum((p_ends[None, :] <= blk_start[:, None]).astype(jnp.int32), axis=1), N_EXPERTS - 1)
    xg = x[slot_tok]
    yb = _experts(xg, blk_exp, P['w_exp_gate'][l], P['w_exp_up'][l], P['w_exp_down'][l])
    d2 = dest.reshape(M, 2)
    return _combine_ln(x, yb[d2[:, 0]], yb[d2[:, 1]], gates, P['ln_g'][l, 2], P['ln_b'][l, 2])


def _trunk(x, pos0, mem_k, mem_v, states, P, Wc, Wf, is_prompt, prec0):
    B, T, D = x.shape
    ret_S, rwkv_S, shift, mC, mn, mm, conv, fk, fv, flf = states
    xf = x.reshape(B * T, D)
    W0 = Wf if prec0 else Wc
    hq, hk, hv, hg, hb = _project(xf, W0['w_in0'], (256, 256, 512, 512, RWKV_PROJ), prec0)
    L_ret = _tile(T, RET_CHUNK)
    out_a, ret_S = _retention(hq, hk, hv, hg, ret_S, P['ret_gn_g'], B, T, pos0, L_ret, prec0)
    hb3 = hb.reshape(B, T, RWKV_PROJ)
    prev = jnp.concatenate([shift, hb3[:, :-1]], axis=1).reshape(B * T, RWKV_PROJ)
    new_shift = hb3[:, -1:]
    L_rwkv = min(RWKV_CHUNK, T)
    if T > L_rwkv:
        assert not prec0
        pre = _rwkv_prep(hb, prev, P, L_rwkv, False)
        out_b, rwkv_S = _rwkv_scan(pre, rwkv_S, P['rwkv_gn_g'], P['rwkv_gn_b'], B, T, L_rwkv)
    else:
        pre = _rwkv_prep(hb, prev, P, T, prec0)
        out_b, rwkv_S = _rwkv_single_chunk(pre, rwkv_S, P['rwkv_gn_g'], P['rwkv_gn_b'], B, T, prec0)
    xf = _out_proj_ln(xf, out_a, out_b, W0['w_out0a'], W0['w_out0b'], P['ln_g'][0, 0], P['ln_b'][0, 0], prec0)
    xf = _mem_attn_ln(xf, mem_k, mem_v, 0, W0['w_mem_q'][0], W0['w_mem_o'][0], P['ln_g'][0, 1], P['ln_b'][0, 1],
                      B, T, prec0)
    xf = _moe_ln(xf, P, 0)
    hqk, hv1, ho, fq, fkk, fvv, hgate = _project(xf, Wc['w_in1'], (512, 512, 512, 512, 512, 512, 128))
    out_c, mC, mn, mm, conv = _mlstm(hqk, hv1, ho, hgate, conv, mC, mn, mm, P, Wc['gate_bias'], B, T, L_ret)
    if is_prompt:
        qa, kn, kt, va, lf = _fox_prep_prompt(fq, fkk, fvv, hgate, P, Wc['gate_bias'], B, T)
        out_d = _fox_prompt(qa, kt, va, B, T)
        logf = lf[:, 2 * MLSTM_HEADS:2 * MLSTM_HEADS + FOX_HEADS].reshape(B, T, FOX_HEADS)
    else:
        qn, kn, lf = _fox_prep(fq, fkk, hgate, P, Wc['gate_bias'])
        logf = lf[:, 2 * MLSTM_HEADS:2 * MLSTM_HEADS + FOX_HEADS].reshape(B, T, FOX_HEADS)
        c_all = jnp.cumsum(jnp.concatenate([flf, logf], axis=1), axis=1)
        out_d = _fox_sample(qn, kn, fvv, fk, fv, c_all, B, T)
    xf = _out_proj_ln(xf, out_c, out_d, Wc['w_out1a'], Wc['w_out1b'], P['ln_g'][1, 0], P['ln_b'][1, 0])
    xf = _mem_attn_ln(xf, mem_k, mem_v, 1, Wc['w_mem_q'][1], Wc['w_mem_o'][1], P['ln_g'][1, 1], P['ln_b'][1, 1], B, T)
    xf = _moe_ln(xf, P, 1)
    fk_new = kn.reshape(B, T, FOX_HEADS, FOX_HD)
    fv_new = fvv.reshape(B, T, FOX_HEADS, FOX_HD)
    return (xf.reshape(B, T, D), ret_S, rwkv_S, new_shift, mC, mn, mm, conv, fk_new, fv_new, logf)


def kernel(x_prompt, x_sample, mem_prompt, state_ret, state_rwkv, cache_rwkv_shift, state_mlstm_c, state_mlstm_n,
           state_mlstm_m, cache_mlstm_conv, cache_fox_k, cache_fox_v, cache_fox_logf, cache_mem_k, cache_mem_v,
           w_in0, ret_gn_g, rwkv_mu, rwkv_w0, rwkv_w2, rwkv_a0, rwkv_a2, rwkv_g2, rwkv_k_k, rwkv_k_a, rwkv_r_k,
           rwkv_gn_g, rwkv_gn_b, w_out0, w_in1, mlstm_conv_w, mlstm_conv_b, mlstm_b_i, mlstm_b_f, mlstm_gn_g,
           fox_q_g, fox_k_g, fox_b_f, w_out1, w_mem_q, w_mem_k, w_mem_v, w_mem_o, w_router, b_router,
           w_exp_gate, w_exp_up, w_exp_down, ln_g, ln_b):
    P = dict(ret_gn_g=ret_gn_g, rwkv_mu=rwkv_mu, rwkv_w0=rwkv_w0, rwkv_w2=rwkv_w2, rwkv_a0=rwkv_a0,
             rwkv_a2=rwkv_a2, rwkv_g2=rwkv_g2, rwkv_k_k=rwkv_k_k, rwkv_k_a=rwkv_k_a, rwkv_r_k=rwkv_r_k,
             rwkv_gn_g=rwkv_gn_g, rwkv_gn_b=rwkv_gn_b, mlstm_conv_w=mlstm_conv_w, mlstm_conv_b=mlstm_conv_b,
             mlstm_gn_g=mlstm_gn_g, fox_q_g=fox_q_g, fox_k_g=fox_k_g, w_router=w_router, b_router=b_router,
             w_exp_gate=w_exp_gate, w_exp_up=w_exp_up, w_exp_down=w_exp_down, ln_g=ln_g, ln_b=ln_b)
    B, M = mem_prompt.shape[0], mem_prompt.shape[1]
    D = D_MODEL
    H = MLSTM_HEADS
    ret_proj = 2 * RET_HEADS * RET_DK + 2 * RET_HEADS * RET_DV
    qk1, w1 = 2 * H * MLSTM_DK, H * MLSTM_DV
    off = qk1 + w1
    mlstm_proj = off + 2 * H + w1
    fw = FOX_HEADS * FOX_HD
    gate_cols = jnp.concatenate([w_in1[:, off:off + 2 * H], w_in1[:, mlstm_proj + 3 * fw:],
                                 jnp.zeros((D, 128 - 2 * H - FOX_HEADS), F32)], axis=1)
    w_in1_c = jnp.concatenate([w_in1[:, :off], w_in1[:, off + 2 * H:mlstm_proj],
                               w_in1[:, mlstm_proj:mlstm_proj + 3 * fw], gate_cols], axis=1)
    gate_bias = jnp.concatenate([mlstm_b_i, mlstm_b_f, fox_b_f, jnp.zeros((128 - 2 * H - FOX_HEADS,), F32)]).reshape(1, 128)
    wa0, wb0 = w_out0[:RET_HEADS * RET_DV], w_out0[RET_HEADS * RET_DV:]
    Wf = dict(w_in0=_hi_lo(w_in0), w_out0a=_hi_lo(wa0), w_out0b=_hi_lo(wb0),
              w_mem_q=[_hi_lo(w_mem_q[0])], w_mem_o=[_hi_lo(w_mem_o[0])])
    Wc = dict(w_in0=w_in0.astype(BF16), w_in1=w_in1_c.astype(BF16), gate_bias=gate_bias,
              w_out0a=wa0.astype(BF16), w_out0b=wb0.astype(BF16),
              w_out1a=w_out1[:w1].astype(BF16), w_out1b=w_out1[w1:].astype(BF16),
              w_mem_q=w_mem_q.astype(BF16), w_mem_o=w_mem_o.astype(BF16))
    w_kv = jnp.concatenate([w_mem_k, w_mem_v], axis=0)
    memkv = _mem_project(mem_prompt.reshape(B * M, D), jnp.swapaxes(_hi_lo(w_kv), 0, 1))
    p_mem_k = memkv[:DEPTH].reshape(DEPTH * B, M, D)
    p_mem_v = memkv[DEPTH:].reshape(DEPTH * B, M, D)
    zeros = lambda *s: jnp.zeros(s, F32)
    prompt_states = (zeros(B, RET_HEADS, RET_DK, RET_DV), zeros(B, RWKV_HEADS, RWKV_HD, RWKV_HD),
                     zeros(B, 1, RWKV_PROJ), zeros(B, H, MLSTM_DK, MLSTM_DV), zeros(B, H, MLSTM_DK), zeros(B, H),
                     zeros(B, MLSTM_CONV - 1, qk1), None, None, None)
    p_out = _trunk(x_prompt, 0, p_mem_k, p_mem_v, prompt_states, P, Wc, Wf, True, PROMPT_PREC0)
    DB = x_sample.shape[0]
    sample_states = (state_ret, state_rwkv, cache_rwkv_shift, state_mlstm_c, state_mlstm_n, state_mlstm_m,
                     cache_mlstm_conv, cache_fox_k, cache_fox_v, cache_fox_logf)
    s_out = _trunk(x_sample, cache_fox_k.shape[1], cache_mem_k.reshape(DEPTH * DB, M, D),
                   cache_mem_v.reshape(DEPTH * DB, M, D), sample_states, P, Wc, Wf, False, True)
    mem_shape = (DEPTH, B, M, MEM_HEADS, MEM_HD)
    return ((p_out[0], s_out[0]) + p_out[1:] + (p_mem_k.reshape(mem_shape), p_mem_v.reshape(mem_shape)) + s_out[1:])
```

```python
import functools
import math

import numpy as np
import jax
import jax.numpy as jnp
from jax import lax
from jax.experimental import pallas as pl
from jax.experimental.pallas import tpu as pltpu

F32 = jnp.float32
BF16 = jnp.bfloat16

D_MODEL = 1024
DEPTH = 2
RET_HEADS, RET_DK, RET_DV = 4, 64, 128
ROPE_BASE = 10000.0
RWKV_HEADS, RWKV_HD = 8, 64
RWKV_W = RWKV_HEADS * RWKV_HD
RWKV_PROJ = 3 * RWKV_W + 64 + 64 + 128
MLSTM_HEADS, MLSTM_DK, MLSTM_DV, MLSTM_CONV = 4, 64, 128, 4
FOX_HEADS, FOX_HD = 4, 128
MEM_HEADS, MEM_HD = 4, 256
N_EXPERTS, N_GROUPS, EXPERTS_PER_GROUP = 16, 4, 4
D_EXPERT = 512
ALPHA = (2 * DEPTH) ** 0.25
EPS = 1e-5
NEG_INF = -1e30
LOG2E = 1.4426950408889634

VMEM_LIMIT_BYTES = 56 * 1024 * 1024
ROW_TILE = 512
RET_CHUNK = 256
RWKV_CHUNK = 64
RWKV_ROWS = 512
RWKV_SEQS = 8
FOX_TQ = 1024
MOE_BLOCK = 512
PROMPT_PREC0 = False


def _cparams(sem):
    return pltpu.CompilerParams(dimension_semantics=sem, vmem_limit_bytes=VMEM_LIMIT_BYTES)


def _tile(n, pref):
    if n <= pref:
        return n
    t = pref
    while t >= 8:
        if n % t == 0:
            return t
        t -= 8
    return n


_NN = (((1,), (0,)), ((), ()))
_NT = (((1,), (1,)), ((), ()))
_TN = (((0,), (0,)), ((), ()))
_BNN = (((2,), (1,)), ((0,), (0,)))
_BNT = (((2,), (2,)), ((0,), (0,)))
_BTN = (((1,), (1,)), ((0,), (0,)))


def _split2(a):
    a = a.astype(F32)
    hi = a.astype(BF16)
    return hi, (a - hi.astype(F32)).astype(BF16)


def _dg(a, b, dims, prec):
    if not prec:
        return lax.dot_general(a.astype(BF16), b.astype(BF16), dims, preferred_element_type=F32)
    a1, a2 = _split2(a)
    b1, b2 = _split2(b)
    d = lambda p, q: lax.dot_general(p, q, dims, preferred_element_type=F32)
    return d(a1, b1) + (d(a1, b2) + d(a2, b1))


def _dot(a, b, prec=False):
    return _dg(a, b, _NN, prec)


def _wdot(x, w_ref, prec):
    if not prec:
        return jnp.dot(x.astype(BF16), w_ref[...], preferred_element_type=F32)
    x1, x2 = _split2(x)
    d = lambda p, q: jnp.dot(p, q, preferred_element_type=F32)
    return d(x1, w_ref[0]) + (d(x1, w_ref[1]) + d(x2, w_ref[0]))


def _wspec(w):
    return pl.BlockSpec(w.shape, lambda *_: (0,) * w.ndim)


def _hi_lo(w):
    bits = lax.bitcast_convert_type(w, jnp.uint32) & jnp.uint32(0xFFFF0000)
    hi = lax.bitcast_convert_type(bits, F32)
    return jnp.stack([hi.astype(BF16), (w - hi).astype(BF16)])


def _dot_nt(a, b, prec=False):
    return _dg(a, b, _NT, prec)


def _dot_tn(a, b, prec=False):
    return _dg(a, b, _TN, prec)


def _split3(a):
    a1 = a.astype(BF16)
    r1 = a - a1.astype(F32)
    a2 = r1.astype(BF16)
    a3 = (r1 - a2.astype(F32)).astype(BF16)
    return a1, a2, a3


def _exact_left_dot(e, a):
    a1, a2, a3 = _split3(a)
    d = lambda p: jnp.dot(e, p, preferred_element_type=F32)
    return d(a1) + d(a2) + d(a3)


def _exact_right_dot(a, e):
    a1, a2, a3 = _split3(a)
    d = lambda p: jnp.dot(p, e, preferred_element_type=F32)
    return d(a1) + d(a2) + d(a3)


def _sigmoid(x):
    return 1.0 / (1.0 + jnp.exp(-x))


def _silu(x):
    return x * _sigmoid(x)


def _softplus(x):
    return jnp.maximum(x, 0.0) + jnp.log1p(jnp.exp(-jnp.abs(x)))


def _log_sigmoid(x):
    return -_softplus(-x)


def _layer_norm(z, g, b):
    mu = jnp.mean(z, axis=-1, keepdims=True)
    d = z - mu
    var = jnp.mean(d * d, axis=-1, keepdims=True)
    return d * lax.rsqrt(var + EPS) * g + b


def _lane_norm(y):
    mu = jnp.mean(y, axis=-1, keepdims=True)
    d = y - mu
    var = jnp.mean(d * d, axis=-1, keepdims=True)
    return d * lax.rsqrt(var + EPS)


def _proj_body(x_ref, w_ref, *o_refs, widths, prec):
    h = _wdot(x_ref[...], w_ref, prec)
    off = 0
    for o_ref, wd in zip(o_refs, widths):
        o_ref[...] = h[:, off:off + wd]
        off += wd


def _project(x, w, widths, prec=False):
    M, K = x.shape
    tm = _tile(M, ROW_TILE)
    return pl.pallas_call(
        functools.partial(_proj_body, widths=widths, prec=prec),
        grid=(M // tm,),
        in_specs=[pl.BlockSpec((tm, K), lambda i: (i, 0)), _wspec(w)],
        out_specs=[pl.BlockSpec((tm, wd), lambda i: (i, 0)) for wd in widths],
        out_shape=[jax.ShapeDtypeStruct((M, wd), F32) for wd in widths],
        compiler_params=_cparams(("parallel",)),
        name="project",
    )(x, w)


def _mem_proj_body(x_ref, w_ref, o_ref):
    o_ref[0] = _wdot(x_ref[...], w_ref.at[0], True)


def _mem_project(x, w):
    M, K = x.shape
    J, _, _, N = w.shape
    return pl.pallas_call(
        _mem_proj_body,
        grid=(J,),
        in_specs=[pl.BlockSpec((M, K), lambda j: (0, 0)),
                  pl.BlockSpec((1, 2, K, N), lambda j: (j, 0, 0, 0))],
        out_specs=pl.BlockSpec((1, M, N), lambda j: (j, 0, 0)),
        out_shape=jax.ShapeDtypeStruct((J, M, N), F32),
        compiler_params=_cparams(("parallel",)),
        name="mem_project",
    )(x, w)


def _out_ln_body(x_ref, a_ref, b_ref, wa_ref, wb_ref, g_ref, bias_ref, o_ref, *, prec):
    mix = _wdot(a_ref[...], wa_ref, prec) + _wdot(b_ref[...], wb_ref, prec)
    o_ref[...] = _layer_norm(ALPHA * x_ref[...] + mix, g_ref[...], bias_ref[...])


def _out_proj_ln(x, a, b, wa, wb, g, bias, prec=False):
    M, D = x.shape
    Ka, Kb = a.shape[1], b.shape[1]
    tm = _tile(M, ROW_TILE)
    row = lambda i: (i, 0)
    fix = lambda i: (0, 0)
    return pl.pallas_call(
        functools.partial(_out_ln_body, prec=prec),
        grid=(M // tm,),
        in_specs=[pl.BlockSpec((tm, D), row), pl.BlockSpec((tm, Ka), row), pl.BlockSpec((tm, Kb), row),
                  _wspec(wa), _wspec(wb),
                  pl.BlockSpec((1, D), fix), pl.BlockSpec((1, D), fix)],
        out_specs=pl.BlockSpec((tm, D), row),
        out_shape=jax.ShapeDtypeStruct((M, D), F32),
        compiler_params=_cparams(("parallel",)),
        name="out_proj_ln",
    )(x, a, b, wa, wb, g.reshape(1, D), bias.reshape(1, D))


def _ret_body(q_ref, k_ref, v_ref, g_ref, cos_ref, sa_ref, sb_ref, dmat_ref, qdec_ref, kdec_ref, sdec_ref,
              s0_ref, gn_ref, o_ref, sout_ref, s_scr, *, prec):
    i = pl.program_id(1)

    @pl.when(i == 0)
    def _():
        s_scr[...] = s0_ref[0]

    cos, sin_a, sin_b = cos_ref[...], sa_ref[...], sb_ref[...]
    width = RET_HEADS * RET_DK
    half = RET_DK // 2

    def rope(x):
        return x * cos + pltpu.roll(x, width - half, 1) * sin_a + pltpu.roll(x, half, 1) * sin_b

    q = rope(q_ref[...])
    k = rope(k_ref[...]) * (RET_DK ** -0.5)
    v = v_ref[...]
    gate = g_ref[...]
    H = RET_HEADS
    qh = jnp.stack([q[:, h * RET_DK:(h + 1) * RET_DK] for h in range(H)], axis=0)
    kh = jnp.stack([k[:, h * RET_DK:(h + 1) * RET_DK] for h in range(H)], axis=0)
    vh = jnp.stack([v[:, h * RET_DV:(h + 1) * RET_DV] for h in range(H)], axis=0)
    s_old = s_scr[...]
    s = _dg(qh, kh, _BNT, prec) * dmat_ref[...]
    o = _dg(s, vh, _BNN, prec) + _dg(qh, s_old, _BNN, prec) * qdec_ref[...]
    s_scr[...] = sdec_ref[...] * s_old + _dg(kh * kdec_ref[...], vh, _BTN, prec)
    on = _lane_norm(o)
    hn = jnp.concatenate([on[h] for h in range(H)], axis=1) * gn_ref[...]
    o_ref[...] = _silu(gate) * hn

    @pl.when(i == pl.num_programs(1) - 1)
    def _():
        sout_ref[0] = s_scr[...]


def _retention(hq, hk, hv, hg, s0, gn_g, B, T, pos0, L, prec=False):
    nT = T // L
    lg = np.log(1.0 - 2.0 ** (-5.0 - np.arange(RET_HEADS)))
    idx = np.arange(L, dtype=np.float64)
    rel = idx[:, None] - idx[None, :]
    dmat = np.where(rel >= 0, np.exp(np.maximum(rel, 0.0)[None] * lg[:, None, None]), 0.0)
    qdec = np.exp((idx + 1.0)[None, :, None] * lg[:, None, None])
    kdec = np.exp((L - 1.0 - idx)[None, :, None] * lg[:, None, None])
    sdec = np.exp(L * lg)[:, None, None]
    half = RET_DK // 2
    inv = ROPE_BASE ** (-jnp.arange(half, dtype=F32) / half)
    ang = (pos0 + jnp.arange(T)).astype(F32)[:, None] * inv[None, :]
    cos, sin = jnp.cos(ang), jnp.sin(ang)
    zero = jnp.zeros_like(sin)
    cos_t = jnp.tile(jnp.concatenate([cos, cos], axis=1), (1, RET_HEADS))
    sin_a = jnp.tile(jnp.concatenate([-sin, zero], axis=1), (1, RET_HEADS))
    sin_b = jnp.tile(jnp.concatenate([zero, sin], axis=1), (1, RET_HEADS))
    qk_w = RET_HEADS * RET_DK
    v_w = RET_HEADS * RET_DV
    row = lambda b, i: (b * nT + i, 0)
    tab = lambda b, i: (i, 0)
    fix3 = lambda b, i: (0, 0, 0)
    out, s_out = pl.pallas_call(
        functools.partial(_ret_body, prec=prec),
        grid=(B, nT),
        in_specs=[pl.BlockSpec((L, qk_w), row), pl.BlockSpec((L, qk_w), row),
                  pl.BlockSpec((L, v_w), row), pl.BlockSpec((L, v_w), row),
                  pl.BlockSpec((L, qk_w), tab), pl.BlockSpec((L, qk_w), tab), pl.BlockSpec((L, qk_w), tab),
                  pl.BlockSpec((RET_HEADS, L, L), fix3), pl.BlockSpec((RET_HEADS, L, 1), fix3),
                  pl.BlockSpec((RET_HEADS, L, 1), fix3), pl.BlockSpec((RET_HEADS, 1, 1), fix3),
                  pl.BlockSpec((1, RET_HEADS, RET_DK, RET_DV), lambda b, i: (b, 0, 0, 0)),
                  pl.BlockSpec((1, v_w), lambda b, i: (0, 0))],
        out_specs=[pl.BlockSpec((L, v_w), row),
                   pl.BlockSpec((1, RET_HEADS, RET_DK, RET_DV), lambda b, i: (b, 0, 0, 0))],
        out_shape=[jax.ShapeDtypeStruct((B * T, v_w), F32),
                   jax.ShapeDtypeStruct((B, RET_HEADS, RET_DK, RET_DV), F32)],
        scratch_shapes=[pltpu.VMEM((RET_HEADS, RET_DK, RET_DV), F32)],
        compiler_params=_cparams(("parallel", "arbitrary")),
        name="retention",
    )(hq, hk, hv, hg, cos_t, sin_a, sin_b, jnp.asarray(dmat, F32), jnp.asarray(qdec, F32),
      jnp.asarray(kdec, F32), jnp.asarray(sdec, F32), s0, gn_g.reshape(1, v_w))
    return out, s_out


def _rwkv_prep_body(h_ref, p_ref, mu_ref, w0_ref, w2_ref, a0_ref, a2_ref, g2_ref, kk_ref, ka_ref, rk_ref,
                    ones_ref, tri_ref, blk_ref, sel_ref,
                    kq_ref, rq_ref, kt_ref, bt_ref, ke_ref, be_ref, v_ref, g_ref, bon_ref, gt_ref, *scratch,
                    prec, shift_rows):
    hb = h_ref[...]
    if shift_rows:
        carry, = scratch
        tm = hb.shape[0]
        first = jnp.where(pl.program_id(1) == 0, p_ref[0], carry[...])
        row = lax.broadcasted_iota(jnp.int32, (tm, 1), 0)
        prev = jnp.where(row == 0, first, pltpu.roll(hb, 1, 0))
        carry[...] = hb[tm - 1:tm, :]
    else:
        prev = p_ref[...]
    xs = hb + (prev - hb) * mu_ref[...]
    W = RWKV_W
    r, k, v = xs[:, :W], xs[:, W:2 * W], xs[:, 2 * W:3 * W]
    w_lo = xs[:, 3 * W:3 * W + 64]
    a_lo = xs[:, 3 * W + 64:3 * W + 128]
    g_lo = xs[:, 3 * W + 128:]
    w_log = -_softplus(-(w0_ref[...] + _dot(jnp.tanh(w_lo), w2_ref[...], prec))) - 0.5
    lw = -jnp.exp(w_log)
    a = _sigmoid(a0_ref[...] + _dot(a_lo, a2_ref[...], prec))
    g = _dot(_sigmoid(g_lo), g2_ref[...], prec)
    ones = ones_ref[...]
    kk = k * kk_ref[...]
    nrm = jnp.sqrt(_exact_right_dot(kk * kk, ones))
    kk = kk / jnp.maximum(nrm, 1e-12)
    k2 = k * (1.0 + (a - 1.0) * ka_ref[...])
    beta = kk * a
    cl = _exact_left_dot(tri_ref[...], lw)
    tot = _exact_left_dot(blk_ref[...], lw)
    ginv = jnp.exp(-cl)
    gend = jnp.exp(tot - cl)
    kq_ref[...] = (kk * jnp.exp(cl - lw)).astype(kq_ref.dtype)
    rq_ref[...] = (r * jnp.exp(cl)).astype(rq_ref.dtype)
    kt_ref[...] = (k2 * ginv).astype(kt_ref.dtype)
    bt_ref[...] = (beta * ginv).astype(bt_ref.dtype)
    ke_ref[...] = (k2 * gend).astype(ke_ref.dtype)
    be_ref[...] = (beta * gend).astype(be_ref.dtype)
    v_ref[...] = v.astype(v_ref.dtype)
    gt_ref[...] = jnp.exp(_exact_left_dot(sel_ref[...], lw))
    g_ref[...] = g
    bon_ref[...] = _exact_right_dot(r * k2 * rk_ref[...], ones) * v


def _rwkv_prep(hb, prev, P, L, prec=False, seqs=None):
    M = hb.shape[0]
    tm = _tile(M if seqs is None else seqs[1], ROW_TILE)
    assert tm % L == 0
    W = RWKV_W
    idx = np.arange(tm)
    same = (idx[:, None] // L) == (idx[None, :] // L)
    tri = jnp.asarray(same & (idx[:, None] >= idx[None, :]), BF16)
    blk = jnp.asarray(same, BF16)
    sel = jnp.asarray((idx[None, :] // L) == np.arange(tm // L)[:, None], BF16)
    lane = np.arange(W)
    ones = jnp.asarray((lane[:, None] // RWKV_HD) == (lane[None, :] // RWKV_HD), BF16)
    if seqs is None:
        grid, nT = (1, M // tm), M // tm
        prev_spec = pl.BlockSpec((tm, RWKV_PROJ), lambda b, i: (i, 0))
        scratch = []
    else:
        grid, nT = (seqs[0], seqs[1] // tm), seqs[1] // tm
        prev_spec = pl.BlockSpec((1, 1, RWKV_PROJ), lambda b, i: (b, 0, 0))
        scratch = [pltpu.VMEM((1, RWKV_PROJ), F32)]
    row = lambda b, i: (b * nT + i, 0)
    vec = lambda a: a.reshape(1, -1)
    params = [vec(P['rwkv_mu']), vec(P['rwkv_w0']), P['rwkv_w2'], vec(P['rwkv_a0']),
              P['rwkv_a2'], P['rwkv_g2'], vec(P['rwkv_k_k']), vec(P['rwkv_k_a']),
              vec(P['rwkv_r_k']), ones, tri, blk, sel]
    out_dt = [F32 if prec else BF16] * 7 + [F32] * 2
    return pl.pallas_call(
        functools.partial(_rwkv_prep_body, prec=prec, shift_rows=seqs is not None),
        grid=grid,
        in_specs=[pl.BlockSpec((tm, RWKV_PROJ), row), prev_spec] + [_wspec(a) for a in params],
        out_specs=[pl.BlockSpec((tm, W), row) for _ in out_dt] + [pl.BlockSpec((tm // L, W), row)],
        out_shape=[jax.ShapeDtypeStruct((M, W), dt) for dt in out_dt] + [jax.ShapeDtypeStruct((M // L, W), F32)],
        scratch_shapes=scratch,
        compiler_params=_cparams(("parallel", "arbitrary")),
        name="rwkv_prep",
    )(hb, prev, *params)


def _rwkv_masks(L):
    ii = lax.broadcasted_iota(jnp.int32, (L, L), 0)
    jj = lax.broadcasted_iota(jnp.int32, (L, L), 1)
    return ii > jj, ii >= jj, (ii == jj).astype(F32)


def _rwkv_local(kq, rq, kt, bt, ke, be, v, masks, lmask_ref, nlev, prec):
    strict, incl, eye = masks
    L = kq.shape[1]
    x = jnp.concatenate([kq, rq], axis=1)
    sk = _dg(x, kt, _BNT, prec)
    sb = _dg(x, bt, _BNT, prec)
    n_m = jnp.where(strict, sk[:, :L], 0.0)
    a_k = jnp.where(incl, sk[:, L:], 0.0)
    m_m = jnp.where(strict, sb[:, :L], 0.0)
    a_b = jnp.where(incl, sb[:, L:], 0.0)
    t_m = eye - m_m * lmask_ref[0]
    for lv in range(1, nlev):
        c_m = m_m * lmask_ref[lv]
        t_m = t_m - _dg(_dg(t_m, c_m, _BNN, prec), t_m, _BNN, prec)
    kqp = _dg(t_m, kq, _BNN, prec)
    u0 = _dg(t_m, _dg(n_m, v, _BNN, prec), _BNN, prec)
    rqp = rq.astype(F32) - _dg(a_b, kqp, _BNN, prec)
    y0 = _dg(a_k, v, _BNN, prec) - _dg(a_b, u0, _BNN, prec)
    p_m = _dg(kqp, be, _BTN, prec)
    b_c = _dg(v, ke, _BTN, prec) - _dg(u0, be, _BTN, prec)
    return rqp, y0, p_m, b_c


def _rwkv_apply(s_old, rqp, y0, p_m, b_c, g_end, prec):
    y = _dg(rqp, s_old, _BNT, prec) + y0
    return y, s_old * g_end - _dg(s_old, p_m, _BNN, prec) + b_c


def _head_stack(tile, nblk):
    hd = RWKV_HD
    L = tile.shape[0] // nblk
    return jnp.concatenate([tile[:, hh * hd:(hh + 1) * hd].reshape(nblk, L, hd) for hh in range(2)], axis=0)


def _head_unstack(x, nblk):
    L, hd = x.shape[1], x.shape[2]
    return jnp.concatenate([x[hh * nblk:(hh + 1) * nblk].reshape(nblk * L, hd) for hh in range(2)], axis=1)


def _rwkv_local_body(kq_ref, rq_ref, kt_ref, bt_ref, ke_ref, be_ref, v_ref, lmask_ref,
                     rqp_ref, y0_ref, pm_ref, bc_ref, *, L, nchunk, nlev):
    ops = [_head_stack(r[...], nchunk) for r in (kq_ref, rq_ref, kt_ref, bt_ref, ke_ref, be_ref, v_ref)]
    res = _rwkv_local(*ops, _rwkv_masks(L), lmask_ref, nlev, False)
    for o_ref, a in zip((rqp_ref, y0_ref, pm_ref, bc_ref), res):
        o_ref[...] = _head_unstack(a, nchunk).astype(o_ref.dtype)


def _rwkv_apply_body(rqp_ref, y0_ref, pm_ref, bc_ref, gt_ref, g_ref, bon_ref, s0_ref, gng_ref, gnb_ref,
                     o_ref, sout_ref, s_scr, *, L, nchunk, B):
    i = pl.program_id(0)
    H, hd = RWKV_HEADS, RWKV_HD

    @pl.when(i == 0)
    def _():
        s_scr[...] = s0_ref[...].reshape(B * H, hd, hd)

    def heads(x):
        return jnp.stack([x[b][:, h * hd:(h + 1) * hd] for b in range(B) for h in range(H)], axis=0)

    def chunk(c, carry):
        rows = pl.ds(pl.multiple_of(c * L, L), L)
        ops = [heads(r[:, rows, :]) for r in (rqp_ref, y0_ref, pm_ref, bc_ref)]
        g_end = heads(gt_ref[:, pl.ds(c, 1), :])
        y, s_new = _rwkv_apply(s_scr[...], *ops, g_end, False)
        s_scr[...] = s_new
        yn = _lane_norm(y)
        for b in range(B):
            ynb = jnp.concatenate([yn[b * H + h] for h in range(H)], axis=1)
            o_ref[b, rows, :] = (ynb * gng_ref[...] + gnb_ref[...] + bon_ref[b, rows, :]) * g_ref[b, rows, :]
        return carry

    lax.fori_loop(0, nchunk, chunk, 0)

    @pl.when(i == pl.num_programs(0) - 1)
    def _():
        sout_ref[...] = s_scr[...].reshape(B, H, hd, hd)


def _rwkv_fused_body(kq_ref, rq_ref, kt_ref, bt_ref, ke_ref, be_ref, v_ref, g_ref, bon_ref, gt_ref,
                     s0_ref, gng_ref, gnb_ref, lmask_ref, o_ref, sout_ref, *, L, nseq, nlev, prec):
    hd = RWKV_HD
    ops = [_head_stack(r[...], nseq) for r in (kq_ref, rq_ref, kt_ref, bt_ref, ke_ref, be_ref, v_ref)]
    loc = _rwkv_local(*ops, _rwkv_masks(L), lmask_ref, nlev, prec)
    gt = gt_ref[...]
    g_end = jnp.concatenate([gt[:, :, hh * hd:(hh + 1) * hd] for hh in range(2)], axis=0)
    s_old = jnp.concatenate([s0_ref[:, hh] for hh in range(2)], axis=0)
    y, s_new = _rwkv_apply(s_old, *loc, g_end, prec)
    for hh in range(2):
        sout_ref[:, hh] = s_new[hh * nseq:(hh + 1) * nseq]
    yn = _head_unstack(_lane_norm(y), nseq)
    o_ref[...] = (yn * gng_ref[...] + gnb_ref[...] + bon_ref[...]) * g_ref[...]


def _rwkv_level_masks(L):
    nlev = int(math.log2(L))
    idx = np.arange(L)
    ii, jj = idx[:, None], idx[None, :]
    lmask = np.stack([((ii >> (lv + 1)) == (jj >> (lv + 1))) & ((ii & (1 << lv)) != 0) & ((jj & (1 << lv)) == 0)
                      for lv in range(nlev)]).astype(np.float32)
    return nlev, jnp.asarray(lmask)


def _rwkv_scan(pre, s0, gn_g, gn_b, B, T, L):
    W, H, hd = RWKV_W, RWKV_HEADS, RWKV_HD
    assert L == hd
    kq, rq, kt, bt, ke, be, v, g, bon, gt = pre
    nlev, lmask = _rwkv_level_masks(L)
    tb = _tile(T, RWKV_ROWS)
    nT = T // tb
    nchunk = tb // L
    pw = 2 * hd
    row = lambda b, p, i: (b * nT + i, p)
    rqp, y0, p_m, b_c = pl.pallas_call(
        functools.partial(_rwkv_local_body, L=L, nchunk=nchunk, nlev=nlev),
        grid=(B, H // 2, nT),
        in_specs=[pl.BlockSpec((tb, pw), row) for _ in range(7)]
                 + [pl.BlockSpec((nlev, L, L), lambda b, p, i: (0, 0, 0))],
        out_specs=[pl.BlockSpec((tb, pw), row) for _ in range(4)],
        out_shape=[jax.ShapeDtypeStruct((B * T, W), dt) for dt in (BF16, F32, BF16, F32)],
        compiler_params=_cparams(("parallel", "parallel", "parallel")),
        name="rwkv_local",
    )(kq, rq, kt, bt, ke, be, v, lmask)
    r3 = lambda a: a.reshape(B, T, W)
    blk = lambda i: (0, i, 0)
    fix2 = lambda i: (0, 0)
    fix4 = lambda i: (0, 0, 0, 0)
    out, s_out = pl.pallas_call(
        functools.partial(_rwkv_apply_body, L=L, nchunk=nchunk, B=B),
        grid=(nT,),
        in_specs=[pl.BlockSpec((B, tb, W), blk) for _ in range(4)]
                 + [pl.BlockSpec((B, nchunk, W), blk), pl.BlockSpec((B, tb, W), blk), pl.BlockSpec((B, tb, W), blk),
                    pl.BlockSpec((B, H, hd, hd), fix4), pl.BlockSpec((1, W), fix2), pl.BlockSpec((1, W), fix2)],
        out_specs=[pl.BlockSpec((B, tb, W), blk), pl.BlockSpec((B, H, hd, hd), fix4)],
        out_shape=[jax.ShapeDtypeStruct((B, T, W), F32), jax.ShapeDtypeStruct((B, H, hd, hd), F32)],
        scratch_shapes=[pltpu.VMEM((B * H, hd, hd), F32)],
        compiler_params=_cparams(("arbitrary",)),
        name="rwkv_apply",
    )(r3(rqp), r3(y0), r3(p_m), r3(b_c), gt.reshape(B, T // L, W), r3(g), r3(bon), s0,
      gn_g.reshape(1, W), gn_b.reshape(1, W))
    return out.reshape(B * T, W), s_out


def _rwkv_single_chunk(pre, s0, gn_g, gn_b, B, T, prec):
    W, H, hd = RWKV_W, RWKV_HEADS, RWKV_HD
    kq, rq, kt, bt, ke, be, v, g, bon, gt = pre
    nlev, lmask = _rwkv_level_masks(T)
    nseq = RWKV_SEQS if B % RWKV_SEQS == 0 else B
    pw = 2 * hd
    row = lambda i, p: (i, p)
    st = lambda i, p: (i, p, 0, 0)
    vec = lambda i, p: (0, p)
    out, s_out = pl.pallas_call(
        functools.partial(_rwkv_fused_body, L=T, nseq=nseq, nlev=nlev, prec=prec),
        grid=(B // nseq, H // 2),
        in_specs=[pl.BlockSpec((nseq * T, pw), row) for _ in range(9)]
                 + [pl.BlockSpec((nseq, 1, pw), lambda i, p: (i, 0, p)), pl.BlockSpec((nseq, 2, hd, hd), st),
                    pl.BlockSpec((1, pw), vec), pl.BlockSpec((1, pw), vec),
                    pl.BlockSpec((nlev, T, T), lambda i, p: (0, 0, 0))],
        out_specs=[pl.BlockSpec((nseq * T, pw), row), pl.BlockSpec((nseq, 2, hd, hd), st)],
        out_shape=[jax.ShapeDtypeStruct((B * T, W), F32), jax.ShapeDtypeStruct((B, H, hd, hd), F32)],
        compiler_params=_cparams(("parallel", "parallel")),
        name="rwkv_single_chunk",
    )(kq, rq, kt, bt, ke, be, v, g, bon, gt.reshape(B, 1, W), s0, gn_g.reshape(1, W), gn_b.reshape(1, W), lmask)
    return out, s_out


def _mlstm_body(qk_ref, v_ref, o_ref, gt_ref, cprev_ref, c0_ref, n0_ref, m0_ref, cw_ref, cb_ref, gb_ref,
                gn_ref, tri_ref, out_ref, cout_ref, nout_ref, mout_ref, convout_ref,
                xpad, c_scr, m_scr, *, L):
    i = pl.program_id(1)
    H, DK, DV = MLSTM_HEADS, MLSTM_DK, MLSTM_DV
    K = MLSTM_CONV - 1
    base = 8 - K

    @pl.when(i == 0)
    def _():
        xpad[base:8, :] = cprev_ref[0]
        for h in range(H):
            c_scr[h, :, 0:DV] = c0_ref[0, h]
            c_scr[h, :, DV:2 * DV] = jnp.broadcast_to(n0_ref[0, h], (DK, DV))
        m_scr[...] = m0_ref[0]

    xpad[8:8 + L, :] = qk_ref[...]
    conv = cb_ref[...] + xpad[pl.ds(base, L), :] * cw_ref[0:1, :]
    for j in range(1, MLSTM_CONV):
        conv = conv + xpad[pl.ds(base + j, L), :] * cw_ref[j:j + 1, :]
    tail = xpad[pl.ds(8 + L - K, K), :]
    xpad[base:8, :] = tail
    qk = _silu(conv)
    q = qk[:, :H * DK] * (DK ** -0.5)
    k = qk[:, H * DK:]
    v = v_ref[...]
    z = gt_ref[...] + gb_ref[...]
    lf = _log_sigmoid(z)
    bcum = _exact_left_dot(tri_ref[...], lf)
    z_t = z.T
    b_t = bcum.T
    ii = lax.broadcasted_iota(jnp.int32, (L, L), 0)
    jj = lax.broadcasted_iota(jnp.int32, (L, L), 1)
    causal = ii >= jj
    ones = jnp.ones((L, DV), F32)
    m_all = m_scr[...]
    lane = lax.broadcasted_iota(jnp.int32, (1, 128), 1)
    stack = lambda f: jnp.stack([f(h) for h in range(H)], axis=0)
    qh = stack(lambda h: q[:, h * DK:(h + 1) * DK])
    kh = stack(lambda h: k[:, h * DK:(h + 1) * DK])
    vh = stack(lambda h: jnp.concatenate([v[:, h * DV:(h + 1) * DV], ones], axis=1))
    b_col = stack(lambda h: bcum[:, H + h:H + h + 1])
    ig_col = stack(lambda h: z[:, h:h + 1])
    row_term = stack(lambda h: z_t[h:h + 1, :] - b_t[H + h:H + h + 1, :])
    m0 = stack(lambda h: m_all[:, h:h + 1])
    log_d = jnp.where(causal, b_col + row_term, NEG_INF)
    m_inter = b_col + m0
    m_t = jnp.maximum(m_inter, jnp.max(log_d, axis=-1, keepdims=True))
    w_d = jnp.exp(log_d - m_t)
    w_i = jnp.exp(m_inter - m_t)
    c_old = c_scr[...]
    s = _dg(qh, kh, _BNT, False) * w_d
    num = _dg(s, vh, _BNN, False) + _dg(qh, c_old, _BNN, False) * w_i
    den = num[:, :, DV:DV + 1]
    denom = jnp.maximum(jnp.abs(den), jnp.exp(-m_t))
    hn = _lane_norm(num[:, :, :DV] / denom)
    m_new = m_t[:, L - 1:L, :]
    b_last = b_col[:, L - 1:L, :]
    w_s = jnp.exp(b_last + m0 - m_new)
    w_k = jnp.exp(b_last - b_col + ig_col - m_new)
    c_scr[...] = w_s * c_old + _dg(kh * w_k, vh, _BTN, False)
    m_new_all = m_all
    for h in range(H):
        m_new_all = jnp.where(lane == h, m_new[h], m_new_all)
    m_scr[...] = m_new_all
    out_ref[...] = jnp.concatenate([hn[h] for h in range(H)], axis=1) * gn_ref[...] * _sigmoid(o_ref[...])

    @pl.when(i == pl.num_programs(1) - 1)
    def _():
        for h in range(H):
            cout_ref[0, h] = c_scr[h, :, 0:DV]
            nout_ref[0, h] = c_scr[h, :, DV:DV + 1]
        mout_ref[0] = m_scr[...]
        convout_ref[0] = tail


def _mlstm(hqk, hv, ho, hgate, conv_prev, c0, n0, m0, P, gate_bias, B, T, L):
    H, DK, DV = MLSTM_HEADS, MLSTM_DK, MLSTM_DV
    nT = T // L
    K = MLSTM_CONV - 1
    W = H * DV
    idx = np.arange(L)
    tri = jnp.asarray(idx[:, None] >= idx[None, :], BF16)
    m0p = jnp.zeros((B, 1, 128), F32).at[:, 0, :H].set(m0)
    row = lambda b, i: (b * nT + i, 0)
    fix = lambda b, i: (0, 0)
    perb3 = lambda b, i: (b, 0, 0)
    perb4 = lambda b, i: (b, 0, 0, 0)
    out, c_out, n_out, m_out, conv_out = pl.pallas_call(
        functools.partial(_mlstm_body, L=L),
        grid=(B, nT),
        in_specs=[pl.BlockSpec((L, W), row), pl.BlockSpec((L, W), row), pl.BlockSpec((L, W), row),
                  pl.BlockSpec((L, 128), row),
                  pl.BlockSpec((1, K, W), perb3),
                  pl.BlockSpec((1, H, DK, DV), perb4), pl.BlockSpec((1, H, DK, 1), perb4),
                  pl.BlockSpec((1, 1, 128), perb3),
                  pl.BlockSpec((MLSTM_CONV, W), fix), pl.BlockSpec((1, W), fix), pl.BlockSpec((1, 128), fix),
                  pl.BlockSpec((1, W), fix), pl.BlockSpec((L, L), fix)],
        out_specs=[pl.BlockSpec((L, W), row),
                   pl.BlockSpec((1, H, DK, DV), perb4), pl.BlockSpec((1, H, DK, 1), perb4),
                   pl.BlockSpec((1, 1, 128), perb3), pl.BlockSpec((1, K, W), perb3)],
        out_shape=[jax.ShapeDtypeStruct((B * T, W), F32),
                   jax.ShapeDtypeStruct((B, H, DK, DV), F32), jax.ShapeDtypeStruct((B, H, DK, 1), F32),
                   jax.ShapeDtypeStruct((B, 1, 128), F32), jax.ShapeDtypeStruct((B, K, W), F32)],
        scratch_shapes=[pltpu.VMEM((L + 8, W), F32), pltpu.VMEM((H, DK, 2 * DV), F32), pltpu.VMEM((1, 128), F32)],
        compiler_params=_cparams(("parallel", "arbitrary")),
        name="mlstm",
    )(hqk, hv, ho, hgate, conv_prev, c0, n0.reshape(B, H, DK, 1), m0p,
      P['mlstm_conv_w'], P['mlstm_conv_b'].reshape(1, W), gate_bias, P['mlstm_gn_g'].reshape(1, W), tri)
    return out, c_out, n_out.reshape(B, H, DK), m_out[:, 0, :H], conv_out


def _fox_prep_body(q_ref, k_ref, gt_ref, qg_ref, kg_ref, gb_ref, qn_ref, kn_ref, lf_ref):
    def rms(x, g):
        outs = []
        for h in range(FOX_HEADS):
            xh = x[:, h * FOX_HD:(h + 1) * FOX_HD]
            outs.append(xh * lax.rsqrt(jnp.mean(xh * xh, axis=-1, keepdims=True) + EPS) * g)
        return jnp.concatenate(outs, axis=1)

    qn_ref[...] = (rms(q_ref[...], qg_ref[...]) * (FOX_HD ** -0.5)).astype(BF16)
    kn_ref[...] = rms(k_ref[...], kg_ref[...])
    lf_ref[...] = _log_sigmoid(gt_ref[...] + gb_ref[...])


def _fox_prep(hq, hk, hgate, P, gate_bias):
    M, W = hq.shape
    tm = _tile(M, ROW_TILE)
    row = lambda i: (i, 0)
    fix = lambda i: (0, 0)
    return pl.pallas_call(
        _fox_prep_body,
        grid=(M // tm,),
        in_specs=[pl.BlockSpec((tm, W), row), pl.BlockSpec((tm, W), row), pl.BlockSpec((tm, 128), row),
                  pl.BlockSpec((1, FOX_HD), fix), pl.BlockSpec((1, FOX_HD), fix), pl.BlockSpec((1, 128), fix)],
        out_specs=[pl.BlockSpec((tm, W), row), pl.BlockSpec((tm, W), row), pl.BlockSpec((tm, 128), row)],
        out_shape=[jax.ShapeDtypeStruct((M, W), BF16), jax.ShapeDtypeStruct((M, W), F32),
                   jax.ShapeDtypeStruct((M, 128), F32)],
        compiler_params=_cparams(("parallel",)),
        name="fox_prep",
    )(hq, hk, hgate, P['fox_q_g'].reshape(1, FOX_HD), P['fox_k_g'].reshape(1, FOX_HD), gate_bias)


def _fox_prep_prompt_body(q_ref, k_ref, v_ref, gt_ref, qg_ref, kg_ref, gb_ref, tri_ref,
                          qa_ref, kn_ref, vc_ref, kt_ref, va_ref, lf_ref, carry):
    i = pl.program_id(1)
    tm = q_ref.shape[0]
    HD = FOX_HD

    @pl.when(i == 0)
    def _():
        carry[...] = jnp.zeros_like(carry)

    lf = _log_sigmoid(gt_ref[...] + gb_ref[...])
    lf_ref[...] = lf
    c = _exact_left_dot(tri_ref[...], lf) + carry[...]
    carry[...] = c[tm - 1:tm, :]
    c2t = (c * (-LOG2E)).T
    lane = lax.broadcasted_iota(jnp.int32, (tm, HD), 1)
    q_ones = jnp.where(lane < 3, 1.0, 0.0).astype(BF16)
    v_ones = jnp.where(lane < 1, 1.0, 0.0).astype(BF16)
    row16 = lax.broadcasted_iota(jnp.int32, (16, tm), 0)
    q, k, v = q_ref[...], k_ref[...], v_ref[...]
    for h in range(FOX_HEADS):
        hs = slice(h * HD, (h + 1) * HD)
        qh, kh = q[:, hs], k[:, hs]
        qh = qh * lax.rsqrt(jnp.mean(qh * qh, axis=-1, keepdims=True) + EPS) * qg_ref[...]
        kh = kh * lax.rsqrt(jnp.mean(kh * kh, axis=-1, keepdims=True) + EPS) * kg_ref[...]
        qa_ref[:, 2 * h * HD:(2 * h + 1) * HD] = (qh * (HD ** -0.5 * LOG2E)).astype(BF16)
        qa_ref[:, (2 * h + 1) * HD:(2 * h + 2) * HD] = q_ones
        kn_ref[pl.ds(h, tm, stride=FOX_HEADS), :] = kh
        vc_ref[pl.ds(h, tm, stride=FOX_HEADS), :] = v[:, hs]
        kt_ref[0, h, 0, 0:HD, :] = kh.T.astype(BF16)
        bias = c2t[2 * MLSTM_HEADS + h:2 * MLSTM_HEADS + h + 1, :]
        hi = bias.astype(BF16).astype(F32)
        mid = (bias - hi).astype(BF16).astype(F32)
        lo = bias - hi - mid
        blk = jnp.where(row16 == 0, hi, jnp.where(row16 == 1, mid, jnp.where(row16 == 2, lo, 0.0)))
        kt_ref[0, h, 0, HD:HD + 16, :] = blk.astype(BF16)
        kt_ref[0, h, 0, HD + 16:2 * HD, :] = jnp.zeros((HD - 16, tm), BF16)
        va_ref[:, 2 * h * HD:(2 * h + 1) * HD] = v[:, hs].astype(BF16)
        va_ref[:, (2 * h + 1) * HD:(2 * h + 2) * HD] = v_ones


def _fox_prep_prompt(hq, hk, hv, hgate, P, gate_bias, B, T):
    M, W = hq.shape
    H, HD = FOX_HEADS, FOX_HD
    tm = _tile(T, FOX_TQ)
    nT = T // tm
    idx = np.arange(tm)
    tri = jnp.asarray(idx[:, None] >= idx[None, :], BF16)
    row = lambda b, i: (b * nT + i, 0)
    fix = lambda b, i: (0, 0)
    return pl.pallas_call(
        _fox_prep_prompt_body,
        grid=(B, nT),
        in_specs=[pl.BlockSpec((tm, W), row), pl.BlockSpec((tm, W), row), pl.BlockSpec((tm, W), row),
                  pl.BlockSpec((tm, 128), row),
                  pl.BlockSpec((1, HD), fix), pl.BlockSpec((1, HD), fix), pl.BlockSpec((1, 128), fix),
                  pl.BlockSpec((tm, tm), fix)],
        out_specs=[pl.BlockSpec((tm, 2 * W), row), pl.BlockSpec((tm * H, HD), row), pl.BlockSpec((tm * H, HD), row),
                   pl.BlockSpec((1, H, 1, 2 * HD, tm), lambda b, i: (b, 0, i, 0, 0)),
                   pl.BlockSpec((tm, 2 * W), row), pl.BlockSpec((tm, 128), row)],
        out_shape=[jax.ShapeDtypeStruct((M, 2 * W), BF16), jax.ShapeDtypeStruct((M * H, HD), F32),
                   jax.ShapeDtypeStruct((M * H, HD), F32),
                   jax.ShapeDtypeStruct((B, H, nT, 2 * HD, tm), BF16),
                   jax.ShapeDtypeStruct((M, 2 * W), BF16), jax.ShapeDtypeStruct((M, 128), F32)],
        scratch_shapes=[pltpu.VMEM((1, 128), F32)],
        compiler_params=_cparams(("parallel", "arbitrary")),
        name="fox_prep_prompt",
    )(hq, hk, hv, hgate, P['fox_q_g'].reshape(1, HD), P['fox_k_g'].reshape(1, HD), gate_bias, tri)


def _fox_prompt_body(q_ref, kt_ref, va_ref, o_ref, sa_scr, sb_scr, m_scr, acc_scr, *, tq):
    qi = pl.program_id(2)
    m_scr[...] = jnp.full_like(m_scr, NEG_INF)
    acc_scr[...] = jnp.zeros_like(acc_scr)

    def scores(s_ref, kj):
        s_ref[...] = jnp.dot(q_ref[0], kt_ref[0, 0, kj], preferred_element_type=F32)

    def update(s_ref, kj, masked):
        vb = va_ref[0, pl.ds(pl.multiple_of(kj * tq, tq), tq), :]
        s = s_ref[...]
        if masked:
            ii = lax.broadcasted_iota(jnp.int32, (tq, tq), 0)
            jj = lax.broadcasted_iota(jnp.int32, (tq, tq), 1)
            s = jnp.where(jj <= ii, s, NEG_INF)
        m_old = m_scr[...]
        m_new = jnp.maximum(m_old, jnp.max(s, axis=-1, keepdims=True))
        p = jnp.exp2(s - m_new)
        acc_scr[...] = jnp.exp2(m_old - m_new) * acc_scr[...] + jnp.dot(p.astype(BF16), vb, preferred_element_type=F32)
        m_scr[...] = m_new

    scores(sa_scr, 0)

    def body(j, carry):
        scores(sb_scr, 2 * j + 1)
        update(sa_scr, 2 * j, False)
        scores(sa_scr, 2 * j + 2)
        update(sb_scr, 2 * j + 1, False)
        return carry

    lax.fori_loop(0, lax.shift_right_logical(qi, 1), body, 0)

    @pl.when((qi & 1) == 0)
    def _():
        update(sa_scr, qi, True)

    @pl.when((qi & 1) == 1)
    def _():
        scores(sb_scr, qi)
        update(sa_scr, qi - 1, False)
        update(sb_scr, qi, True)

    acc = acc_scr[...]
    o_ref[0] = acc[:, :FOX_HD] / acc[:, FOX_HD:FOX_HD + 1]


def _fox_prompt(qa, kt, va, B, T):
    H, HD = FOX_HEADS, FOX_HD
    W = H * HD
    tq = kt.shape[-1]
    nQ = T // tq
    out = pl.pallas_call(
        functools.partial(_fox_prompt_body, tq=tq),
        grid=(B, H, nQ),
        in_specs=[pl.BlockSpec((1, tq, 2 * HD), lambda b, h, i: (b, i, h)),
                  pl.BlockSpec((1, 1, nQ, 2 * HD, tq), lambda b, h, i: (b, h, 0, 0, 0),
                               pipeline_mode=pl.Buffered(1)),
                  pl.BlockSpec((1, T, 2 * HD), lambda b, h, i: (b, 0, h), pipeline_mode=pl.Buffered(1))],
        out_specs=pl.BlockSpec((1, tq, HD), lambda b, h, i: (b, i, h)),
        out_shape=jax.ShapeDtypeStruct((B, T, W), F32),
        scratch_shapes=[pltpu.VMEM((tq, tq), F32), pltpu.VMEM((tq, tq), F32),
                        pltpu.VMEM((tq, 1), F32), pltpu.VMEM((tq, 2 * HD), F32)],
        compiler_params=_cparams(("parallel", "parallel", "arbitrary")),
        name="fox_prompt",
    )(qa.reshape(B, T, 2 * W), kt, va.reshape(B, T, 2 * W))
    return out.reshape(B * T, W)


def _fox_sample_body(q_ref, kc_ref, vc_ref, kn_ref, vn_ref, cq_ref, ckc_ref, ckn_ref, o_ref, *, T, past):
    H, HD = FOX_HEADS, FOX_HD
    q = q_ref[0]
    kn, vn = kn_ref[0], vn_ref[0]
    ii = lax.broadcasted_iota(jnp.int32, (T, T), 0)
    jj = lax.broadcasted_iota(jnp.int32, (T, T), 1)
    outs = []
    for h in range(H):
        cs = slice(h * HD, (h + 1) * HD)
        kc = kc_ref[0, pl.ds(h, past, stride=H), :]
        vc = vc_ref[0, pl.ds(h, past, stride=H), :]
        cq = cq_ref[0, h]
        s1 = _dot_nt(q[:, cs], kc) + (cq - ckc_ref[0, h])
        s2 = _dot_nt(q[:, cs], kn[:, cs]) + (cq - ckn_ref[0, h])
        s2 = jnp.where(jj <= ii, s2, NEG_INF)
        m = jnp.maximum(jnp.max(s1, axis=-1, keepdims=True), jnp.max(s2, axis=-1, keepdims=True))
        p1 = jnp.exp(s1 - m)
        p2 = jnp.exp(s2 - m)
        den = jnp.sum(p1, axis=-1, keepdims=True) + jnp.sum(p2, axis=-1, keepdims=True)
        outs.append((_dot(p1, vc) + _dot(p2, vn[:, cs])) / den)
    o_ref[0] = jnp.concatenate(outs, axis=1)


def _fox_sample(qn, kn, vn, k_cache, v_cache, c_all, B, T):
    H, HD = FOX_HEADS, FOX_HD
    W = H * HD
    past = k_cache.shape[1]
    ct = jnp.transpose(c_all, (0, 2, 1))
    cq = ct[:, :, past:].reshape(B, H, T, 1)
    ckc = ct[:, :, :past].reshape(B, H, 1, past)
    ckn = ct[:, :, past:].reshape(B, H, 1, T)
    b3 = lambda b: (b, 0, 0)
    b4 = lambda b: (b, 0, 0, 0)
    out = pl.pallas_call(
        functools.partial(_fox_sample_body, T=T, past=past),
        grid=(B,),
        in_specs=[pl.BlockSpec((1, T, W), b3), pl.BlockSpec((1, past * H, HD), b3), pl.BlockSpec((1, past * H, HD), b3),
                  pl.BlockSpec((1, T, W), b3), pl.BlockSpec((1, T, W), b3),
                  pl.BlockSpec((1, H, T, 1), b4), pl.BlockSpec((1, H, 1, past), b4), pl.BlockSpec((1, H, 1, T), b4)],
        out_specs=pl.BlockSpec((1, T, W), b3),
        out_shape=jax.ShapeDtypeStruct((B, T, W), F32),
        compiler_params=_cparams(("parallel",)),
        name="fox_sample",
    )(qn.reshape(B, T, W), k_cache.reshape(B, past * H, HD), v_cache.reshape(B, past * H, HD),
      kn.reshape(B, T, W), vn.reshape(B, T, W), cq, ckc, ckn)
    return out.reshape(B * T, W)


def _mem_attn_body(x_ref, mk_ref, mv_ref, wq_ref, wo_ref, g_ref, b_ref, o_ref, ob_ref, *, prec):
    x = x_ref[...]
    q = _wdot(x, wq_ref, prec)
    mk, mv = mk_ref[0], mv_ref[0]
    outs = []
    for h in range(MEM_HEADS):
        cs = slice(h * MEM_HD, (h + 1) * MEM_HD)
        s = _dot_nt(q[:, cs], mk[:, cs], prec) * (MEM_HD ** -0.5)
        m = jnp.max(s, axis=-1, keepdims=True)
        p = jnp.exp(s - m)
        outs.append(_dot(p, mv[:, cs], prec) / jnp.sum(p, axis=-1, keepdims=True))
    o = jnp.concatenate(outs, axis=1)
    att = _wdot(o, wo_ref, prec)
    y = _layer_norm(ALPHA * x + att, g_ref[...], b_ref[...])
    o_ref[...] = y
    ob_ref[...] = y.astype(BF16)


def _mem_attn_ln(x, mk, mv, l, wq, wo, g, bias, B, T, prec=False):
    D = D_MODEL
    Mm = mk.shape[1]
    tm = _tile(T, ROW_TILE)
    nT = T // tm
    row = lambda b, i: (b * nT + i, 0)
    fix = lambda b, i: (0, 0)
    mem = lambda b, i: (l * B + b, 0, 0)
    return pl.pallas_call(
        functools.partial(_mem_attn_body, prec=prec),
        grid=(B, nT),
        in_specs=[pl.BlockSpec((tm, D), row), pl.BlockSpec((1, Mm, D), mem), pl.BlockSpec((1, Mm, D), mem),
                  _wspec(wq), _wspec(wo),
                  pl.BlockSpec((1, D), fix), pl.BlockSpec((1, D), fix)],
        out_specs=[pl.BlockSpec((tm, D), row), pl.BlockSpec((tm, D), row)],
        out_shape=[jax.ShapeDtypeStruct((B * T, D), F32), jax.ShapeDtypeStruct((B * T, D), BF16)],
        compiler_params=_cparams(("parallel", "parallel")),
        name="mem_attn_ln",
    )(x, mk, mv, wq, wo, g.reshape(1, D), bias.reshape(1, D))


def _router_body(x_ref, w_ref, b_ref, o_ref):
    x = x_ref[...]
    w = w_ref[...]
    x1, x2, x3 = _split3(x)
    w1, w2, w3 = _split3(w)
    nt = lambda a, c: lax.dot_general(a, c, _NT, preferred_element_type=F32)
    logits = (nt(w1, x1) + (nt(w1, x2) + nt(w2, x1)) + (nt(w1, x3) + nt(w2, x2) + nt(w3, x1))) + b_ref[...]
    m = jnp.max(logits, axis=0, keepdims=True)
    e = jnp.exp(logits - m)
    p = e / jnp.sum(e, axis=0, keepdims=True)
    rows = [p[j:j + 1, :] for j in range(N_EXPERTS)]
    best = None
    sel = None
    for g in range(N_GROUPS):
        a, b, c, d = rows[4 * g:4 * g + 4]
        top2 = jnp.maximum(jnp.maximum(jnp.maximum(a + b, a + c), jnp.maximum(a + d, b + c)),
                           jnp.maximum(b + d, c + d))
        if g == 0:
            best, sel = top2, jnp.zeros_like(top2, dtype=jnp.int32)
        else:
            upd = top2 > best
            sel = jnp.where(upd, g, sel)
            best = jnp.maximum(best, top2)
    pin = []
    for kk in range(EXPERTS_PER_GROUP):
        v = rows[kk]
        for g in range(1, N_GROUPS):
            v = jnp.where(sel == g, rows[4 * g + kk], v)
        pin.append(v)
    v1, i1 = pin[0], jnp.zeros_like(sel)
    for kk in range(1, EXPERTS_PER_GROUP):
        upd = pin[kk] > v1
        i1 = jnp.where(upd, kk, i1)
        v1 = jnp.maximum(v1, pin[kk])
    v2, i2 = None, None
    for kk in range(EXPERTS_PER_GROUP):
        cand = jnp.where(i1 == kk, -1.0, pin[kk])
        if v2 is None:
            v2, i2 = cand, jnp.zeros_like(sel)
        else:
            upd = cand > v2
            i2 = jnp.where(upd, kk, i2)
            v2 = jnp.maximum(v2, cand)
    tot = v1 + v2
    e1 = (sel * EXPERTS_PER_GROUP + i1).astype(F32)
    e2 = (sel * EXPERTS_PER_GROUP + i2).astype(F32)
    zero = jnp.zeros_like(v1)
    o_ref[...] = jnp.concatenate([e1, e2, v1 / tot, v2 / tot, zero, zero, zero, zero], axis=0)


def _router(x, w_router, b_router):
    M, D = x.shape
    tm = _tile(M, ROW_TILE)
    return pl.pallas_call(
        _router_body,
        grid=(M // tm,),
        in_specs=[pl.BlockSpec((tm, D), lambda i: (i, 0)),
                  pl.BlockSpec((N_EXPERTS, D), lambda i: (0, 0)),
                  pl.BlockSpec((N_EXPERTS, 1), lambda i: (0, 0))],
        out_specs=pl.BlockSpec((8, tm), lambda i: (0, i)),
        out_shape=jax.ShapeDtypeStruct((8, M), F32),
        compiler_params=_cparams(("parallel",)),
        name="router",
    )(x, w_router.T, b_router.reshape(N_EXPERTS, 1))


def _expert_body(be_ref, x_ref, wg_ref, wu_ref, wd_ref, o_ref, wg_s, wu_s, wd_s):
    i = pl.program_id(0)
    prev = be_ref[jnp.maximum(i - 1, 0)]

    @pl.when((i == 0) | (be_ref[i] != prev))
    def _():
        wg_s[...] = wg_ref[0, 0].astype(BF16)
        wu_s[...] = wu_ref[0, 0].astype(BF16)
        wd_s[...] = wd_ref[0, 0].astype(BF16)

    x = x_ref[...]
    hg = jnp.dot(x, wg_s[...], preferred_element_type=F32)
    hu = jnp.dot(x, wu_s[...], preferred_element_type=F32)
    hb = (_silu(hg) * hu).astype(BF16)
    o_ref[...] = jnp.dot(hb, wd_s[...], preferred_element_type=F32).astype(o_ref.dtype)


def _experts(xg, blk_exp, wg, wu, wd, l):
    n_blocks = blk_exp.shape[0]
    D, DE = D_MODEL, D_EXPERT
    grid_spec = pltpu.PrefetchScalarGridSpec(
        num_scalar_prefetch=1,
        grid=(n_blocks,),
        in_specs=[pl.BlockSpec((MOE_BLOCK, D), lambda i, be: (i, 0)),
                  pl.BlockSpec((1, 1, D, DE), lambda i, be: (l, be[i], 0, 0)),
                  pl.BlockSpec((1, 1, D, DE), lambda i, be: (l, be[i], 0, 0)),
                  pl.BlockSpec((1, 1, DE, D), lambda i, be: (l, be[i], 0, 0))],
        out_specs=pl.BlockSpec((MOE_BLOCK, D), lambda i, be: (i, 0)),
        scratch_shapes=[pltpu.VMEM((D, DE), BF16), pltpu.VMEM((D, DE), BF16), pltpu.VMEM((DE, D), BF16)],
    )
    return pl.pallas_call(
        _expert_body,
        grid_spec=grid_spec,
        out_shape=jax.ShapeDtypeStruct((n_blocks * MOE_BLOCK, D), BF16),
        compiler_params=_cparams(("arbitrary",)),
        name="experts",
    )(blk_exp, xg, wg, wu, wd)


def _combine_ln_body(x_ref, y0_ref, y1_ref, gt_ref, g_ref, b_ref, o_ref):
    gt = gt_ref[...]
    ffn = y0_ref[...].astype(F32) * gt[:, 0:1] + y1_ref[...].astype(F32) * gt[:, 1:2]
    o_ref[...] = _layer_norm(ALPHA * x_ref[...] + ffn, g_ref[...], b_ref[...])


def _combine_ln(x, y0, y1, gates, g, bias):
    M, D = x.shape
    tm = _tile(M, ROW_TILE)
    row = lambda i: (i, 0)
    fix = lambda i: (0, 0)
    return pl.pallas_call(
        _combine_ln_body,
        grid=(M // tm,),
        in_specs=[pl.BlockSpec((tm, D), row), pl.BlockSpec((tm, D), row), pl.BlockSpec((tm, D), row),
                  pl.BlockSpec((tm, 2), row), pl.BlockSpec((1, D), fix), pl.BlockSpec((1, D), fix)],
        out_specs=pl.BlockSpec((tm, D), row),
        out_shape=jax.ShapeDtypeStruct((M, D), F32),
        compiler_params=_cparams(("parallel",)),
        name="combine_ln",
    )(x, y0, y1, gates, g.reshape(1, D), bias.reshape(1, D))


def _moe_ln(x, xb, P, l):
    M = x.shape[0]
    r = _router(x, P['w_router'], P['b_router'])
    e = r[0:2].astype(jnp.int32).T.reshape(-1)
    gates = r[2:4].T
    oh = (e[:, None] == jnp.arange(N_EXPERTS, dtype=jnp.int32)[None, :]).astype(jnp.int32)
    csum = jnp.cumsum(oh, axis=0)
    rank = jnp.sum((csum - oh) * oh, axis=1)
    counts = csum[-1]
    padded = (counts + MOE_BLOCK - 1) // MOE_BLOCK * MOE_BLOCK
    p_ends = jnp.cumsum(padded)
    p_starts = p_ends - padded
    dest = p_starts[e] + rank
    n_blocks = -(-2 * M // MOE_BLOCK) + N_EXPERTS
    tok = jnp.arange(2 * M, dtype=jnp.int32) // 2
    slot_tok = jnp.zeros((n_blocks * MOE_BLOCK,), jnp.int32).at[dest].set(tok)
    blk_start = jnp.arange(n_blocks, dtype=jnp.int32) * MOE_BLOCK
    blk_exp = jnp.minimum(jnp.sum((p_ends[None, :] <= blk_start[:, None]).astype(jnp.int32), axis=1), N_EXPERTS - 1)
    xg = xb[slot_tok]
    yb = _experts(xg, blk_exp, P['w_exp_gate'], P['w_exp_up'], P['w_exp_down'], l)
    d2 = dest.reshape(M, 2)
    return _combine_ln(x, yb[d2[:, 0]], yb[d2[:, 1]], gates, P['ln_g'][l, 2], P['ln_b'][l, 2])


def _trunk(x, pos0, mem_k, mem_v, states, P, Wc, Wf, is_prompt, prec0):
    B, T, D = x.shape
    ret_S, rwkv_S, shift, mC, mn, mm, conv, fk, fv, flf = states
    xf = x.reshape(B * T, D)
    W0 = Wf if prec0 else Wc
    hq, hk, hv, hg, hb = _project(xf, W0['w_in0'], (256, 256, 512, 512, RWKV_PROJ), prec0)
    L_ret = _tile(T, RET_CHUNK)
    out_a, ret_S = _retention(hq, hk, hv, hg, ret_S, P['ret_gn_g'], B, T, pos0, L_ret, prec0)
    hb3 = hb.reshape(B, T, RWKV_PROJ)
    new_shift = hb3[:, -1:]
    L_rwkv = min(RWKV_CHUNK, T)
    if T > L_rwkv:
        assert not prec0
        pre = _rwkv_prep(hb, shift, P, L_rwkv, False, seqs=(B, T))
        out_b, rwkv_S = _rwkv_scan(pre, rwkv_S, P['rwkv_gn_g'], P['rwkv_gn_b'], B, T, L_rwkv)
    else:
        prev = jnp.concatenate([shift, hb3[:, :-1]], axis=1).reshape(B * T, RWKV_PROJ)
        pre = _rwkv_prep(hb, prev, P, T, prec0)
        out_b, rwkv_S = _rwkv_single_chunk(pre, rwkv_S, P['rwkv_gn_g'], P['rwkv_gn_b'], B, T, prec0)
    xf = _out_proj_ln(xf, out_a, out_b, W0['w_out0a'], W0['w_out0b'], P['ln_g'][0, 0], P['ln_b'][0, 0], prec0)
    xf, xb = _mem_attn_ln(xf, mem_k, mem_v, 0, W0['w_mem_q'][0], W0['w_mem_o'][0], P['ln_g'][0, 1], P['ln_b'][0, 1],
                          B, T, prec0)
    xf = _moe_ln(xf, xb, P, 0)
    hqk, hv1, ho, fq, fkk, fvv, hgate = _project(xf, Wc['w_in1'], (512, 512, 512, 512, 512, 512, 128))
    out_c, mC, mn, mm, conv = _mlstm(hqk, hv1, ho, hgate, conv, mC, mn, mm, P, Wc['gate_bias'], B, T, L_ret)
    if is_prompt:
        qa, kn, vc, kt, va, lf = _fox_prep_prompt(fq, fkk, fvv, hgate, P, Wc['gate_bias'], B, T)
        out_d = _fox_prompt(qa, kt, va, B, T)
        logf = lf[:, 2 * MLSTM_HEADS:2 * MLSTM_HEADS + FOX_HEADS].reshape(B, T, FOX_HEADS)
    else:
        qn, kn, lf = _fox_prep(fq, fkk, hgate, P, Wc['gate_bias'])
        vc = fvv
        logf = lf[:, 2 * MLSTM_HEADS:2 * MLSTM_HEADS + FOX_HEADS].reshape(B, T, FOX_HEADS)
        c_all = jnp.cumsum(jnp.concatenate([flf, logf], axis=1), axis=1)
        out_d = _fox_sample(qn, kn, fvv, fk, fv, c_all, B, T)
    xf = _out_proj_ln(xf, out_c, out_d, Wc['w_out1a'], Wc['w_out1b'], P['ln_g'][1, 0], P['ln_b'][1, 0])
    xf, xb = _mem_attn_ln(xf, mem_k, mem_v, 1, Wc['w_mem_q'][1], Wc['w_mem_o'][1], P['ln_g'][1, 1], P['ln_b'][1, 1],
                          B, T)
    xf = _moe_ln(xf, xb, P, 1)
    fk_new = kn.reshape(B, T, FOX_HEADS, FOX_HD)
    fv_new = vc.reshape(B, T, FOX_HEADS, FOX_HD)
    return (xf.reshape(B, T, D), ret_S, rwkv_S, new_shift, mC, mn, mm, conv, fk_new, fv_new, logf)


def kernel(x_prompt, x_sample, mem_prompt, state_ret, state_rwkv, cache_rwkv_shift, state_mlstm_c, state_mlstm_n,
           state_mlstm_m, cache_mlstm_conv, cache_fox_k, cache_fox_v, cache_fox_logf, cache_mem_k, cache_mem_v,
           w_in0, ret_gn_g, rwkv_mu, rwkv_w0, rwkv_w2, rwkv_a0, rwkv_a2, rwkv_g2, rwkv_k_k, rwkv_k_a, rwkv_r_k,
           rwkv_gn_g, rwkv_gn_b, w_out0, w_in1, mlstm_conv_w, mlstm_conv_b, mlstm_b_i, mlstm_b_f, mlstm_gn_g,
           fox_q_g, fox_k_g, fox_b_f, w_out1, w_mem_q, w_mem_k, w_mem_v, w_mem_o, w_router, b_router,
           w_exp_gate, w_exp_up, w_exp_down, ln_g, ln_b):
    P = dict(ret_gn_g=ret_gn_g, rwkv_mu=rwkv_mu, rwkv_w0=rwkv_w0, rwkv_w2=rwkv_w2, rwkv_a0=rwkv_a0,
             rwkv_a2=rwkv_a2, rwkv_g2=rwkv_g2, rwkv_k_k=rwkv_k_k, rwkv_k_a=rwkv_k_a, rwkv_r_k=rwkv_r_k,
             rwkv_gn_g=rwkv_gn_g, rwkv_gn_b=rwkv_gn_b, mlstm_conv_w=mlstm_conv_w, mlstm_conv_b=mlstm_conv_b,
             mlstm_gn_g=mlstm_gn_g, fox_q_g=fox_q_g, fox_k_g=fox_k_g, w_router=w_router, b_router=b_router,
             w_exp_gate=w_exp_gate, w_exp_up=w_exp_up, w_exp_down=w_exp_down, ln_g=ln_g, ln_b=ln_b)
    B, M = mem_prompt.shape[0], mem_prompt.shape[1]
    D = D_MODEL
    H = MLSTM_HEADS
    ret_proj = 2 * RET_HEADS * RET_DK + 2 * RET_HEADS * RET_DV
    qk1, w1 = 2 * H * MLSTM_DK, H * MLSTM_DV
    off = qk1 + w1
    mlstm_proj = off + 2 * H + w1
    fw = FOX_HEADS * FOX_HD
    gate_cols = jnp.concatenate([w_in1[:, off:off + 2 * H], w_in1[:, mlstm_proj + 3 * fw:],
                                 jnp.zeros((D, 128 - 2 * H - FOX_HEADS), F32)], axis=1)
    w_in1_c = jnp.concatenate([w_in1[:, :off], w_in1[:, off + 2 * H:mlstm_proj],
                               w_in1[:, mlstm_proj:mlstm_proj + 3 * fw], gate_cols], axis=1)
    gate_bias = jnp.concatenate([mlstm_b_i, mlstm_b_f, fox_b_f, jnp.zeros((128 - 2 * H - FOX_HEADS,), F32)]).reshape(1, 128)
    wa0, wb0 = w_out0[:RET_HEADS * RET_DV], w_out0[RET_HEADS * RET_DV:]
    Wf = dict(w_in0=_hi_lo(w_in0), w_out0a=_hi_lo(wa0), w_out0b=_hi_lo(wb0),
              w_mem_q=[_hi_lo(w_mem_q[0])], w_mem_o=[_hi_lo(w_mem_o[0])])
    Wc = dict(w_in0=w_in0.astype(BF16), w_in1=w_in1_c.astype(BF16), gate_bias=gate_bias,
              w_out0a=wa0.astype(BF16), w_out0b=wb0.astype(BF16),
              w_out1a=w_out1[:w1].astype(BF16), w_out1b=w_out1[w1:].astype(BF16),
              w_mem_q=w_mem_q.astype(BF16), w_mem_o=w_mem_o.astype(BF16))
    w_kv = jnp.concatenate([w_mem_k, w_mem_v], axis=0)
    memkv = _mem_project(mem_prompt.reshape(B * M, D), jnp.swapaxes(_hi_lo(w_kv), 0, 1))
    p_mem_k = memkv[:DEPTH].reshape(DEPTH * B, M, D)
    p_mem_v = memkv[DEPTH:].reshape(DEPTH * B, M, D)
    zeros = lambda *s: jnp.zeros(s, F32)
    prompt_states = (zeros(B, RET_HEADS, RET_DK, RET_DV), zeros(B, RWKV_HEADS, RWKV_HD, RWKV_HD),
                     zeros(B, 1, RWKV_PROJ), zeros(B, H, MLSTM_DK, MLSTM_DV), zeros(B, H, MLSTM_DK), zeros(B, H),
                     zeros(B, MLSTM_CONV - 1, qk1), None, None, None)
    p_out = _trunk(x_prompt, 0, p_mem_k, p_mem_v, prompt_states, P, Wc, Wf, True, PROMPT_PREC0)
    DB = x_sample.shape[0]
    sample_states = (state_ret, state_rwkv, cache_rwkv_shift, state_mlstm_c, state_mlstm_n, state_mlstm_m,
                     cache_mlstm_conv, cache_fox_k, cache_fox_v, cache_fox_logf)
    s_out = _trunk(x_sample, cache_fox_k.shape[1], cache_mem_k.reshape(DEPTH * DB, M, D),
                   cache_mem_v.reshape(DEPTH * DB, M, D), sample_states, P, Wc, Wf, False, True)
    mem_shape = (DEPTH, B, M, MEM_HEADS, MEM_HD)
    return ((p_out[0], s_out[0]) + p_out[1:] + (p_mem_k.reshape(mem_shape), p_mem_v.reshape(mem_shape)) + s_out[1:])
```

```python
import functools
import math

import numpy as np
import jax
import jax.numpy as jnp
from jax import lax
from jax.experimental import pallas as pl
from jax.experimental.pallas import tpu as pltpu

F32 = jnp.float32
BF16 = jnp.bfloat16

D_MODEL = 1024
DEPTH = 2
RET_HEADS, RET_DK, RET_DV = 4, 64, 128
ROPE_BASE = 10000.0
RWKV_HEADS, RWKV_HD = 8, 64
RWKV_W = RWKV_HEADS * RWKV_HD
RWKV_PROJ = 3 * RWKV_W + 64 + 64 + 128
MLSTM_HEADS, MLSTM_DK, MLSTM_DV, MLSTM_CONV = 4, 64, 128, 4
FOX_HEADS, FOX_HD = 4, 128
MEM_HEADS, MEM_HD = 4, 256
N_EXPERTS, N_GROUPS, EXPERTS_PER_GROUP = 16, 4, 4
D_EXPERT = 512
ALPHA = (2 * DEPTH) ** 0.25
EPS = 1e-5
NEG_INF = -1e30
LOG2E = 1.4426950408889634

VMEM_LIMIT_BYTES = 56 * 1024 * 1024
ROW_TILE = 512
RET_CHUNK = 256
RWKV_CHUNK = 64
RWKV_ROWS = 512
RWKV_LOCAL_ROWS = 1024
RWKV_SEQS = 8
FOX_TQ = 1024
MOE_BLOCK = 512
PROMPT_PREC0 = False


def _cparams(sem):
    return pltpu.CompilerParams(dimension_semantics=sem, vmem_limit_bytes=VMEM_LIMIT_BYTES)


def _tile(n, pref):
    if n <= pref:
        return n
    t = pref
    while t >= 8:
        if n % t == 0:
            return t
        t -= 8
    return n


_NN = (((1,), (0,)), ((), ()))
_NT = (((1,), (1,)), ((), ()))
_TN = (((0,), (0,)), ((), ()))
_BNN = (((2,), (1,)), ((0,), (0,)))
_BNT = (((2,), (2,)), ((0,), (0,)))
_BTN = (((1,), (1,)), ((0,), (0,)))


def _split2(a):
    a = a.astype(F32)
    hi = a.astype(BF16)
    return hi, (a - hi.astype(F32)).astype(BF16)


def _dg(a, b, dims, prec):
    if not prec:
        return lax.dot_general(a.astype(BF16), b.astype(BF16), dims, preferred_element_type=F32)
    a1, a2 = _split2(a)
    b1, b2 = _split2(b)
    d = lambda p, q: lax.dot_general(p, q, dims, preferred_element_type=F32)
    return d(a1, b1) + (d(a1, b2) + d(a2, b1))


def _dot(a, b, prec=False):
    return _dg(a, b, _NN, prec)


def _wdot(x, w_ref, prec):
    if not prec:
        return jnp.dot(x.astype(BF16), w_ref[...], preferred_element_type=F32)
    x1, x2 = _split2(x)
    d = lambda p, q: jnp.dot(p, q, preferred_element_type=F32)
    return d(x1, w_ref[0]) + (d(x1, w_ref[1]) + d(x2, w_ref[0]))


def _wspec(w):
    return pl.BlockSpec(w.shape, lambda *_: (0,) * w.ndim)


def _hi_lo(w):
    bits = lax.bitcast_convert_type(w, jnp.uint32) & jnp.uint32(0xFFFF0000)
    hi = lax.bitcast_convert_type(bits, F32)
    return jnp.stack([hi.astype(BF16), (w - hi).astype(BF16)])


def _dot_nt(a, b, prec=False):
    return _dg(a, b, _NT, prec)


def _dot_tn(a, b, prec=False):
    return _dg(a, b, _TN, prec)


def _split3(a):
    a1 = a.astype(BF16)
    r1 = a - a1.astype(F32)
    a2 = r1.astype(BF16)
    a3 = (r1 - a2.astype(F32)).astype(BF16)
    return a1, a2, a3


def _exact_left_dot(e, a):
    a1, a2, a3 = _split3(a)
    d = lambda p: jnp.dot(e, p, preferred_element_type=F32)
    return d(a1) + d(a2) + d(a3)


def _exact_right_dot(a, e):
    a1, a2, a3 = _split3(a)
    d = lambda p: jnp.dot(p, e, preferred_element_type=F32)
    return d(a1) + d(a2) + d(a3)


def _sigmoid(x):
    return 1.0 / (1.0 + jnp.exp(-x))


def _silu(x):
    return x * _sigmoid(x)


def _softplus(x):
    return jnp.maximum(x, 0.0) + jnp.log1p(jnp.exp(-jnp.abs(x)))


def _log_sigmoid(x):
    return -_softplus(-x)


def _layer_norm(z, g, b):
    mu = jnp.mean(z, axis=-1, keepdims=True)
    d = z - mu
    var = jnp.mean(d * d, axis=-1, keepdims=True)
    return d * lax.rsqrt(var + EPS) * g + b


def _lane_norm(y):
    mu = jnp.mean(y, axis=-1, keepdims=True)
    d = y - mu
    var = jnp.mean(d * d, axis=-1, keepdims=True)
    return d * lax.rsqrt(var + EPS)


def _proj_body(x_ref, w_ref, *o_refs, widths, prec):
    h = _wdot(x_ref[...], w_ref, prec)
    off = 0
    for o_ref, wd in zip(o_refs, widths):
        o_ref[...] = h[:, off:off + wd]
        off += wd


def _project(x, w, widths, prec=False):
    M, K = x.shape
    tm = _tile(M, ROW_TILE)
    return pl.pallas_call(
        functools.partial(_proj_body, widths=widths, prec=prec),
        grid=(M // tm,),
        in_specs=[pl.BlockSpec((tm, K), lambda i: (i, 0)), _wspec(w)],
        out_specs=[pl.BlockSpec((tm, wd), lambda i: (i, 0)) for wd in widths],
        out_shape=[jax.ShapeDtypeStruct((M, wd), F32) for wd in widths],
        compiler_params=_cparams(("parallel",)),
        name="project",
    )(x, w)


def _mem_proj_body(x_ref, w_ref, o_ref):
    o_ref[0] = _wdot(x_ref[...], w_ref.at[0], True)


def _mem_project(x, w):
    M, K = x.shape
    J, _, _, N = w.shape
    return pl.pallas_call(
        _mem_proj_body,
        grid=(J,),
        in_specs=[pl.BlockSpec((M, K), lambda j: (0, 0)),
                  pl.BlockSpec((1, 2, K, N), lambda j: (j, 0, 0, 0))],
        out_specs=pl.BlockSpec((1, M, N), lambda j: (j, 0, 0)),
        out_shape=jax.ShapeDtypeStruct((J, M, N), F32),
        compiler_params=_cparams(("parallel",)),
        name="mem_project",
    )(x, w)


def _out_ln_body(x_ref, a_ref, b_ref, wa_ref, wb_ref, g_ref, bias_ref, o_ref, *, prec):
    mix = _wdot(a_ref[...], wa_ref, prec) + _wdot(b_ref[...], wb_ref, prec)
    o_ref[...] = _layer_norm(ALPHA * x_ref[...] + mix, g_ref[...], bias_ref[...])


def _out_proj_ln(x, a, b, wa, wb, g, bias, prec=False):
    M, D = x.shape
    Ka, Kb = a.shape[1], b.shape[1]
    tm = _tile(M, ROW_TILE)
    row = lambda i: (i, 0)
    fix = lambda i: (0, 0)
    return pl.pallas_call(
        functools.partial(_out_ln_body, prec=prec),
        grid=(M // tm,),
        in_specs=[pl.BlockSpec((tm, D), row), pl.BlockSpec((tm, Ka), row), pl.BlockSpec((tm, Kb), row),
                  _wspec(wa), _wspec(wb),
                  pl.BlockSpec((1, D), fix), pl.BlockSpec((1, D), fix)],
        out_specs=pl.BlockSpec((tm, D), row),
        out_shape=jax.ShapeDtypeStruct((M, D), F32),
        compiler_params=_cparams(("parallel",)),
        name="out_proj_ln",
    )(x, a, b, wa, wb, g.reshape(1, D), bias.reshape(1, D))


def _ret_body(q_ref, k_ref, v_ref, g_ref, cos_ref, sa_ref, sb_ref, dmat_ref, qdec_ref, kdec_ref, sdec_ref,
              s0_ref, gn_ref, o_ref, sout_ref, s_scr, *, prec):
    i = pl.program_id(1)

    @pl.when(i == 0)
    def _():
        s_scr[...] = s0_ref[0]

    cos, sin_a, sin_b = cos_ref[...], sa_ref[...], sb_ref[...]
    width = RET_HEADS * RET_DK
    half = RET_DK // 2

    def rope(x):
        return x * cos + pltpu.roll(x, width - half, 1) * sin_a + pltpu.roll(x, half, 1) * sin_b

    q = rope(q_ref[...])
    k = rope(k_ref[...]) * (RET_DK ** -0.5)
    v = v_ref[...]
    gate = g_ref[...]
    H = RET_HEADS
    qh = jnp.stack([q[:, h * RET_DK:(h + 1) * RET_DK] for h in range(H)], axis=0)
    kh = jnp.stack([k[:, h * RET_DK:(h + 1) * RET_DK] for h in range(H)], axis=0)
    vh = jnp.stack([v[:, h * RET_DV:(h + 1) * RET_DV] for h in range(H)], axis=0)
    s_old = s_scr[...]
    s = _dg(qh, kh, _BNT, prec) * dmat_ref[...]
    o = _dg(s, vh, _BNN, prec) + _dg(qh, s_old, _BNN, prec) * qdec_ref[...]
    s_scr[...] = sdec_ref[...] * s_old + _dg(kh * kdec_ref[...], vh, _BTN, prec)
    on = _lane_norm(o)
    hn = jnp.concatenate([on[h] for h in range(H)], axis=1) * gn_ref[...]
    o_ref[...] = _silu(gate) * hn

    @pl.when(i == pl.num_programs(1) - 1)
    def _():
        sout_ref[0] = s_scr[...]


def _retention(hq, hk, hv, hg, s0, gn_g, B, T, pos0, L, prec=False):
    nT = T // L
    lg = np.log(1.0 - 2.0 ** (-5.0 - np.arange(RET_HEADS)))
    idx = np.arange(L, dtype=np.float64)
    rel = idx[:, None] - idx[None, :]
    dmat = np.where(rel >= 0, np.exp(np.maximum(rel, 0.0)[None] * lg[:, None, None]), 0.0)
    qdec = np.exp((idx + 1.0)[None, :, None] * lg[:, None, None])
    kdec = np.exp((L - 1.0 - idx)[None, :, None] * lg[:, None, None])
    sdec = np.exp(L * lg)[:, None, None]
    half = RET_DK // 2
    inv = ROPE_BASE ** (-jnp.arange(half, dtype=F32) / half)
    ang = (pos0 + jnp.arange(T)).astype(F32)[:, None] * inv[None, :]
    cos, sin = jnp.cos(ang), jnp.sin(ang)
    zero = jnp.zeros_like(sin)
    cos_t = jnp.tile(jnp.concatenate([cos, cos], axis=1), (1, RET_HEADS))
    sin_a = jnp.tile(jnp.concatenate([-sin, zero], axis=1), (1, RET_HEADS))
    sin_b = jnp.tile(jnp.concatenate([zero, sin], axis=1), (1, RET_HEADS))
    qk_w = RET_HEADS * RET_DK
    v_w = RET_HEADS * RET_DV
    row = lambda b, i: (b * nT + i, 0)
    tab = lambda b, i: (i, 0)
    fix3 = lambda b, i: (0, 0, 0)
    out, s_out = pl.pallas_call(
        functools.partial(_ret_body, prec=prec),
        grid=(B, nT),
        in_specs=[pl.BlockSpec((L, qk_w), row), pl.BlockSpec((L, qk_w), row),
                  pl.BlockSpec((L, v_w), row), pl.BlockSpec((L, v_w), row),
                  pl.BlockSpec((L, qk_w), tab), pl.BlockSpec((L, qk_w), tab), pl.BlockSpec((L, qk_w), tab),
                  pl.BlockSpec((RET_HEADS, L, L), fix3), pl.BlockSpec((RET_HEADS, L, 1), fix3),
                  pl.BlockSpec((RET_HEADS, L, 1), fix3), pl.BlockSpec((RET_HEADS, 1, 1), fix3),
                  pl.BlockSpec((1, RET_HEADS, RET_DK, RET_DV), lambda b, i: (b, 0, 0, 0)),
                  pl.BlockSpec((1, v_w), lambda b, i: (0, 0))],
        out_specs=[pl.BlockSpec((L, v_w), row),
                   pl.BlockSpec((1, RET_HEADS, RET_DK, RET_DV), lambda b, i: (b, 0, 0, 0))],
        out_shape=[jax.ShapeDtypeStruct((B * T, v_w), F32),
                   jax.ShapeDtypeStruct((B, RET_HEADS, RET_DK, RET_DV), F32)],
        scratch_shapes=[pltpu.VMEM((RET_HEADS, RET_DK, RET_DV), F32)],
        compiler_params=_cparams(("parallel", "arbitrary")),
        name="retention",
    )(hq, hk, hv, hg, cos_t, sin_a, sin_b, jnp.asarray(dmat, F32), jnp.asarray(qdec, F32),
      jnp.asarray(kdec, F32), jnp.asarray(sdec, F32), s0, gn_g.reshape(1, v_w))
    return out, s_out


def _rwkv_prep_body(h_ref, p_ref, mu_ref, w0_ref, w2_ref, a0_ref, a2_ref, g2_ref, kk_ref, ka_ref, rk_ref,
                    ones_ref, tri_ref, blk_ref, sel_ref,
                    kq_ref, rq_ref, kt_ref, bt_ref, ke_ref, be_ref, v_ref, g_ref, bon_ref, gt_ref, *scratch,
                    prec, shift_rows):
    hb = h_ref[...]
    if shift_rows:
        carry, = scratch
        tm = hb.shape[0]
        first = jnp.where(pl.program_id(1) == 0, p_ref[0], carry[...])
        row = lax.broadcasted_iota(jnp.int32, (tm, 1), 0)
        prev = jnp.where(row == 0, first, pltpu.roll(hb, 1, 0))
        carry[...] = hb[tm - 1:tm, :]
    else:
        prev = p_ref[...]
    xs = hb + (prev - hb) * mu_ref[...]
    W = RWKV_W
    r, k, v = xs[:, :W], xs[:, W:2 * W], xs[:, 2 * W:3 * W]
    w_lo = xs[:, 3 * W:3 * W + 64]
    a_lo = xs[:, 3 * W + 64:3 * W + 128]
    g_lo = xs[:, 3 * W + 128:]
    w_log = -_softplus(-(w0_ref[...] + _dot(jnp.tanh(w_lo), w2_ref[...], prec))) - 0.5
    lw = -jnp.exp(w_log)
    a = _sigmoid(a0_ref[...] + _dot(a_lo, a2_ref[...], prec))
    g = _dot(_sigmoid(g_lo), g2_ref[...], prec)
    ones = ones_ref[...]
    kk = k * kk_ref[...]
    nrm = jnp.sqrt(_exact_right_dot(kk * kk, ones))
    kk = kk / jnp.maximum(nrm, 1e-12)
    k2 = k * (1.0 + (a - 1.0) * ka_ref[...])
    beta = kk * a
    cl = _exact_left_dot(tri_ref[...], lw)
    tot = _exact_left_dot(blk_ref[...], lw)
    ginv = jnp.exp(-cl)
    gend = jnp.exp(tot - cl)
    kq_ref[...] = (kk * jnp.exp(cl - lw)).astype(kq_ref.dtype)
    rq_ref[...] = (r * jnp.exp(cl)).astype(rq_ref.dtype)
    kt_ref[...] = (k2 * ginv).astype(kt_ref.dtype)
    bt_ref[...] = (beta * ginv).astype(bt_ref.dtype)
    ke_ref[...] = (k2 * gend).astype(ke_ref.dtype)
    be_ref[...] = (beta * gend).astype(be_ref.dtype)
    v_ref[...] = v.astype(v_ref.dtype)
    gt_ref[...] = jnp.exp(_exact_left_dot(sel_ref[...], lw))
    g_ref[...] = g
    bon_ref[...] = _exact_right_dot(r * k2 * rk_ref[...], ones) * v


def _rwkv_prep(hb, prev, P, L, prec=False, seqs=None):
    M = hb.shape[0]
    tm = _tile(M if seqs is None else seqs[1], ROW_TILE)
    assert tm % L == 0
    W = RWKV_W
    idx = np.arange(tm)
    same = (idx[:, None] // L) == (idx[None, :] // L)
    tri = jnp.asarray(same & (idx[:, None] >= idx[None, :]), BF16)
    blk = jnp.asarray(same, BF16)
    sel = jnp.asarray((idx[None, :] // L) == np.arange(tm // L)[:, None], BF16)
    lane = np.arange(W)
    ones = jnp.asarray((lane[:, None] // RWKV_HD) == (lane[None, :] // RWKV_HD), BF16)
    if seqs is None:
        grid, nT = (1, M // tm), M // tm
        prev_spec = pl.BlockSpec((tm, RWKV_PROJ), lambda b, i: (i, 0))
        scratch = []
    else:
        grid, nT = (seqs[0], seqs[1] // tm), seqs[1] // tm
        prev_spec = pl.BlockSpec((1, 1, RWKV_PROJ), lambda b, i: (b, 0, 0))
        scratch = [pltpu.VMEM((1, RWKV_PROJ), F32)]
    row = lambda b, i: (b * nT + i, 0)
    vec = lambda a: a.reshape(1, -1)
    params = [vec(P['rwkv_mu']), vec(P['rwkv_w0']), P['rwkv_w2'], vec(P['rwkv_a0']),
              P['rwkv_a2'], P['rwkv_g2'], vec(P['rwkv_k_k']), vec(P['rwkv_k_a']),
              vec(P['rwkv_r_k']), ones, tri, blk, sel]
    out_dt = [F32 if prec else BF16] * 7 + [F32] * 2
    return pl.pallas_call(
        functools.partial(_rwkv_prep_body, prec=prec, shift_rows=seqs is not None),
        grid=grid,
        in_specs=[pl.BlockSpec((tm, RWKV_PROJ), row), prev_spec] + [_wspec(a) for a in params],
        out_specs=[pl.BlockSpec((tm, W), row) for _ in out_dt] + [pl.BlockSpec((tm // L, W), row)],
        out_shape=[jax.ShapeDtypeStruct((M, W), dt) for dt in out_dt] + [jax.ShapeDtypeStruct((M // L, W), F32)],
        scratch_shapes=scratch,
        compiler_params=_cparams(("parallel", "arbitrary")),
        name="rwkv_prep",
    )(hb, prev, *params)


def _rwkv_masks(L):
    ii = lax.broadcasted_iota(jnp.int32, (L, L), 0)
    jj = lax.broadcasted_iota(jnp.int32, (L, L), 1)
    return ii > jj, ii >= jj, (ii == jj).astype(F32)


def _rwkv_local(kq, rq, kt, bt, ke, be, v, masks, lmask_ref, nlev, prec):
    strict, incl, eye = masks
    L = kq.shape[1]
    x = jnp.concatenate([kq, rq], axis=1)
    sk = _dg(x, kt, _BNT, prec)
    sb = _dg(x, bt, _BNT, prec)
    n_m = jnp.where(strict, sk[:, :L], 0.0)
    a_k = jnp.where(incl, sk[:, L:], 0.0)
    m_m = jnp.where(strict, sb[:, :L], 0.0)
    a_b = jnp.where(incl, sb[:, L:], 0.0)
    t_m = eye - m_m * lmask_ref[0]
    for lv in range(1, nlev):
        c_m = m_m * lmask_ref[lv]
        t_m = t_m - _dg(_dg(t_m, c_m, _BNN, prec), t_m, _BNN, prec)
    kqp = _dg(t_m, kq, _BNN, prec)
    u0 = _dg(t_m, _dg(n_m, v, _BNN, prec), _BNN, prec)
    rqp = rq.astype(F32) - _dg(a_b, kqp, _BNN, prec)
    y0 = _dg(a_k, v, _BNN, prec) - _dg(a_b, u0, _BNN, prec)
    p_m = _dg(kqp, be, _BTN, prec)
    b_c = _dg(v, ke, _BTN, prec) - _dg(u0, be, _BTN, prec)
    return rqp, y0, p_m, b_c


def _rwkv_apply(s_old, rqp, y0, p_m, b_c, g_end, prec):
    y = _dg(rqp, s_old, _BNT, prec) + y0
    return y, s_old * g_end - _dg(s_old, p_m, _BNN, prec) + b_c


def _head_stack(tile, nblk):
    hd = RWKV_HD
    L = tile.shape[0] // nblk
    return jnp.concatenate([tile[:, hh * hd:(hh + 1) * hd].reshape(nblk, L, hd) for hh in range(2)], axis=0)


def _head_unstack(x, nblk):
    L, hd = x.shape[1], x.shape[2]
    return jnp.concatenate([x[hh * nblk:(hh + 1) * nblk].reshape(nblk * L, hd) for hh in range(2)], axis=1)


def _rwkv_local_body(kq_ref, rq_ref, kt_ref, bt_ref, ke_ref, be_ref, v_ref, lmask_ref,
                     rqp_ref, y0_ref, pm_ref, bc_ref, *, L, nchunk, nlev):
    ops = [_head_stack(r[...], nchunk) for r in (kq_ref, rq_ref, kt_ref, bt_ref, ke_ref, be_ref, v_ref)]
    res = _rwkv_local(*ops, _rwkv_masks(L), lmask_ref, nlev, False)
    for o_ref, a in zip((rqp_ref, y0_ref, pm_ref, bc_ref), res):
        o_ref[...] = _head_unstack(a, nchunk).astype(o_ref.dtype)


def _rwkv_apply_body(rqp_ref, y0_ref, pm_ref, bc_ref, gt_ref, g_ref, bon_ref, s0_ref, gng_ref, gnb_ref,
                     o_ref, sout_ref, s_scr, *, L, nchunk, B):
    i = pl.program_id(0)
    H, hd = RWKV_HEADS, RWKV_HD

    @pl.when(i == 0)
    def _():
        s_scr[...] = s0_ref[...].reshape(B * H, hd, hd)

    def heads(x):
        return jnp.stack([x[b][:, h * hd:(h + 1) * hd] for b in range(B) for h in range(H)], axis=0)

    def chunk(c, carry):
        rows = pl.ds(pl.multiple_of(c * L, L), L)
        ops = [heads(r[:, rows, :]) for r in (rqp_ref, y0_ref, pm_ref, bc_ref)]
        g_end = heads(gt_ref[:, pl.ds(c, 1), :])
        y, s_new = _rwkv_apply(s_scr[...], *ops, g_end, False)
        s_scr[...] = s_new
        yn = _lane_norm(y)
        for b in range(B):
            ynb = jnp.concatenate([yn[b * H + h] for h in range(H)], axis=1)
            o_ref[b, rows, :] = (ynb * gng_ref[...] + gnb_ref[...] + bon_ref[b, rows, :]) * g_ref[b, rows, :]
        return carry

    lax.fori_loop(0, nchunk, chunk, 0)

    @pl.when(i == pl.num_programs(0) - 1)
    def _():
        sout_ref[...] = s_scr[...].reshape(B, H, hd, hd)


def _rwkv_fused_body(kq_ref, rq_ref, kt_ref, bt_ref, ke_ref, be_ref, v_ref, g_ref, bon_ref, gt_ref,
                     s0_ref, gng_ref, gnb_ref, lmask_ref, o_ref, sout_ref, *, L, nseq, nlev, prec):
    hd = RWKV_HD
    ops = [_head_stack(r[...], nseq) for r in (kq_ref, rq_ref, kt_ref, bt_ref, ke_ref, be_ref, v_ref)]
    loc = _rwkv_local(*ops, _rwkv_masks(L), lmask_ref, nlev, prec)
    gt = gt_ref[...]
    g_end = jnp.concatenate([gt[:, :, hh * hd:(hh + 1) * hd] for hh in range(2)], axis=0)
    s_old = jnp.concatenate([s0_ref[:, hh] for hh in range(2)], axis=0)
    y, s_new = _rwkv_apply(s_old, *loc, g_end, prec)
    for hh in range(2):
        sout_ref[:, hh] = s_new[hh * nseq:(hh + 1) * nseq]
    yn = _head_unstack(_lane_norm(y), nseq)
    o_ref[...] = (yn * gng_ref[...] + gnb_ref[...] + bon_ref[...]) * g_ref[...]


def _rwkv_level_masks(L):
    nlev = int(math.log2(L))
    idx = np.arange(L)
    ii, jj = idx[:, None], idx[None, :]
    lmask = np.stack([((ii >> (lv + 1)) == (jj >> (lv + 1))) & ((ii & (1 << lv)) != 0) & ((jj & (1 << lv)) == 0)
                      for lv in range(nlev)]).astype(np.float32)
    return nlev, jnp.asarray(lmask)


def _rwkv_scan(pre, s0, gn_g, gn_b, B, T, L):
    W, H, hd = RWKV_W, RWKV_HEADS, RWKV_HD
    assert L == hd
    kq, rq, kt, bt, ke, be, v, g, bon, gt = pre
    nlev, lmask = _rwkv_level_masks(L)
    tl = _tile(T, RWKV_LOCAL_ROWS)
    nL = T // tl
    pw = 2 * hd
    row = lambda b, p, i: (b * nL + i, p)
    rqp, y0, p_m, b_c = pl.pallas_call(
        functools.partial(_rwkv_local_body, L=L, nchunk=tl // L, nlev=nlev),
        grid=(B, H // 2, nL),
        in_specs=[pl.BlockSpec((tl, pw), row) for _ in range(7)]
                 + [pl.BlockSpec((nlev, L, L), lambda b, p, i: (0, 0, 0))],
        out_specs=[pl.BlockSpec((tl, pw), row) for _ in range(4)],
        out_shape=[jax.ShapeDtypeStruct((B * T, W), dt) for dt in (BF16, F32, BF16, F32)],
        compiler_params=_cparams(("parallel", "parallel", "parallel")),
        name="rwkv_local",
    )(kq, rq, kt, bt, ke, be, v, lmask)
    tb = _tile(T, RWKV_ROWS)
    nT = T // tb
    nchunk = tb // L
    r3 = lambda a: a.reshape(B, T, W)
    blk = lambda i: (0, i, 0)
    fix2 = lambda i: (0, 0)
    fix4 = lambda i: (0, 0, 0, 0)
    out, s_out = pl.pallas_call(
        functools.partial(_rwkv_apply_body, L=L, nchunk=nchunk, B=B),
        grid=(nT,),
        in_specs=[pl.BlockSpec((B, tb, W), blk) for _ in range(4)]
                 + [pl.BlockSpec((B, nchunk, W), blk), pl.BlockSpec((B, tb, W), blk), pl.BlockSpec((B, tb, W), blk),
                    pl.BlockSpec((B, H, hd, hd), fix4), pl.BlockSpec((1, W), fix2), pl.BlockSpec((1, W), fix2)],
        out_specs=[pl.BlockSpec((B, tb, W), blk), pl.BlockSpec((B, H, hd, hd), fix4)],
        out_shape=[jax.ShapeDtypeStruct((B, T, W), F32), jax.ShapeDtypeStruct((B, H, hd, hd), F32)],
        scratch_shapes=[pltpu.VMEM((B * H, hd, hd), F32)],
        compiler_params=_cparams(("arbitrary",)),
        name="rwkv_apply",
    )(r3(rqp), r3(y0), r3(p_m), r3(b_c), gt.reshape(B, T // L, W), r3(g), r3(bon), s0,
      gn_g.reshape(1, W), gn_b.reshape(1, W))
    return out.reshape(B * T, W), s_out


def _rwkv_single_chunk(pre, s0, gn_g, gn_b, B, T, prec):
    W, H, hd = RWKV_W, RWKV_HEADS, RWKV_HD
    kq, rq, kt, bt, ke, be, v, g, bon, gt = pre
    nlev, lmask = _rwkv_level_masks(T)
    nseq = RWKV_SEQS if B % RWKV_SEQS == 0 else B
    pw = 2 * hd
    row = lambda i, p: (i, p)
    st = lambda i, p: (i, p, 0, 0)
    vec = lambda i, p: (0, p)
    out, s_out = pl.pallas_call(
        functools.partial(_rwkv_fused_body, L=T, nseq=nseq, nlev=nlev, prec=prec),
        grid=(B // nseq, H // 2),
        in_specs=[pl.BlockSpec((nseq * T, pw), row) for _ in range(9)]
                 + [pl.BlockSpec((nseq, 1, pw), lambda i, p: (i, 0, p)), pl.BlockSpec((nseq, 2, hd, hd), st),
                    pl.BlockSpec((1, pw), vec), pl.BlockSpec((1, pw), vec),
                    pl.BlockSpec((nlev, T, T), lambda i, p: (0, 0, 0))],
        out_specs=[pl.BlockSpec((nseq * T, pw), row), pl.BlockSpec((nseq, 2, hd, hd), st)],
        out_shape=[jax.ShapeDtypeStruct((B * T, W), F32), jax.ShapeDtypeStruct((B, H, hd, hd), F32)],
        compiler_params=_cparams(("parallel", "parallel")),
        name="rwkv_single_chunk",
    )(kq, rq, kt, bt, ke, be, v, g, bon, gt.reshape(B, 1, W), s0, gn_g.reshape(1, W), gn_b.reshape(1, W), lmask)
    return out, s_out


def _mlstm_body(qk_ref, v_ref, o_ref, gt_ref, cprev_ref, c0_ref, n0_ref, m0_ref, cw_ref, cb_ref, gb_ref,
                gn_ref, tri_ref, out_ref, cout_ref, nout_ref, mout_ref, convout_ref,
                xpad, c_scr, m_scr, *, L):
    i = pl.program_id(1)
    H, DK, DV = MLSTM_HEADS, MLSTM_DK, MLSTM_DV
    K = MLSTM_CONV - 1
    base = 8 - K

    @pl.when(i == 0)
    def _():
        xpad[base:8, :] = cprev_ref[0]
        for h in range(H):
            c_scr[h, :, 0:DV] = c0_ref[0, h]
            c_scr[h, :, DV:2 * DV] = jnp.broadcast_to(n0_ref[0, h], (DK, DV))
        m_scr[...] = m0_ref[0]

    xpad[8:8 + L, :] = qk_ref[...]
    conv = cb_ref[...] + xpad[pl.ds(base, L), :] * cw_ref[0:1, :]
    for j in range(1, MLSTM_CONV):
        conv = conv + xpad[pl.ds(base + j, L), :] * cw_ref[j:j + 1, :]
    tail = xpad[pl.ds(8 + L - K, K), :]
    xpad[base:8, :] = tail
    qk = _silu(conv)
    q = qk[:, :H * DK] * (DK ** -0.5)
    k = qk[:, H * DK:]
    v = v_ref[...]
    z = gt_ref[...] + gb_ref[...]
    lf = _log_sigmoid(z)
    bcum = _exact_left_dot(tri_ref[...], lf)
    z_t = z.T
    b_t = bcum.T
    ii = lax.broadcasted_iota(jnp.int32, (L, L), 0)
    jj = lax.broadcasted_iota(jnp.int32, (L, L), 1)
    causal = ii >= jj
    ones = jnp.ones((L, DV), F32)
    m_all = m_scr[...]
    lane = lax.broadcasted_iota(jnp.int32, (1, 128), 1)
    stack = lambda f: jnp.stack([f(h) for h in range(H)], axis=0)
    qh = stack(lambda h: q[:, h * DK:(h + 1) * DK])
    kh = stack(lambda h: k[:, h * DK:(h + 1) * DK])
    vh = stack(lambda h: jnp.concatenate([v[:, h * DV:(h + 1) * DV], ones], axis=1))
    b_col = stack(lambda h: bcum[:, H + h:H + h + 1])
    ig_col = stack(lambda h: z[:, h:h + 1])
    row_term = stack(lambda h: z_t[h:h + 1, :] - b_t[H + h:H + h + 1, :])
    m0 = stack(lambda h: m_all[:, h:h + 1])
    log_d = jnp.where(causal, b_col + row_term, NEG_INF)
    m_inter = b_col + m0
    m_t = jnp.maximum(m_inter, jnp.max(log_d, axis=-1, keepdims=True))
    w_d = jnp.exp(log_d - m_t)
    w_i = jnp.exp(m_inter - m_t)
    c_old = c_scr[...]
    s = _dg(qh, kh, _BNT, False) * w_d
    num = _dg(s, vh, _BNN, False) + _dg(qh, c_old, _BNN, False) * w_i
    den = num[:, :, DV:DV + 1]
    denom = jnp.maximum(jnp.abs(den), jnp.exp(-m_t))
    hn = _lane_norm(num[:, :, :DV] / denom)
    m_new = m_t[:, L - 1:L, :]
    b_last = b_col[:, L - 1:L, :]
    w_s = jnp.exp(b_last + m0 - m_new)
    w_k = jnp.exp(b_last - b_col + ig_col - m_new)
    c_scr[...] = w_s * c_old + _dg(kh * w_k, vh, _BTN, False)
    m_new_all = m_all
    for h in range(H):
        m_new_all = jnp.where(lane == h, m_new[h], m_new_all)
    m_scr[...] = m_new_all
    out_ref[...] = jnp.concatenate([hn[h] for h in range(H)], axis=1) * gn_ref[...] * _sigmoid(o_ref[...])

    @pl.when(i == pl.num_programs(1) - 1)
    def _():
        for h in range(H):
            cout_ref[0, h] = c_scr[h, :, 0:DV]
            nout_ref[0, h] = c_scr[h, :, DV:DV + 1]
        mout_ref[0] = m_scr[...]
        convout_ref[0] = tail


def _mlstm(hqk, hv, ho, hgate, conv_prev, c0, n0, m0, P, gate_bias, B, T, L):
    H, DK, DV = MLSTM_HEADS, MLSTM_DK, MLSTM_DV
    nT = T // L
    K = MLSTM_CONV - 1
    W = H * DV
    idx = np.arange(L)
    tri = jnp.asarray(idx[:, None] >= idx[None, :], BF16)
    m0p = jnp.zeros((B, 1, 128), F32).at[:, 0, :H].set(m0)
    row = lambda b, i: (b * nT + i, 0)
    fix = lambda b, i: (0, 0)
    perb3 = lambda b, i: (b, 0, 0)
    perb4 = lambda b, i: (b, 0, 0, 0)
    out, c_out, n_out, m_out, conv_out = pl.pallas_call(
        functools.partial(_mlstm_body, L=L),
        grid=(B, nT),
        in_specs=[pl.BlockSpec((L, W), row), pl.BlockSpec((L, W), row), pl.BlockSpec((L, W), row),
                  pl.BlockSpec((L, 128), row),
                  pl.BlockSpec((1, K, W), perb3),
                  pl.BlockSpec((1, H, DK, DV), perb4), pl.BlockSpec((1, H, DK, 1), perb4),
                  pl.BlockSpec((1, 1, 128), perb3),
                  pl.BlockSpec((MLSTM_CONV, W), fix), pl.BlockSpec((1, W), fix), pl.BlockSpec((1, 128), fix),
                  pl.BlockSpec((1, W), fix), pl.BlockSpec((L, L), fix)],
        out_specs=[pl.BlockSpec((L, W), row),
                   pl.BlockSpec((1, H, DK, DV), perb4), pl.BlockSpec((1, H, DK, 1), perb4),
                   pl.BlockSpec((1, 1, 128), perb3), pl.BlockSpec((1, K, W), perb3)],
        out_shape=[jax.ShapeDtypeStruct((B * T, W), F32),
                   jax.ShapeDtypeStruct((B, H, DK, DV), F32), jax.ShapeDtypeStruct((B, H, DK, 1), F32),
                   jax.ShapeDtypeStruct((B, 1, 128), F32), jax.ShapeDtypeStruct((B, K, W), F32)],
        scratch_shapes=[pltpu.VMEM((L + 8, W), F32), pltpu.VMEM((H, DK, 2 * DV), F32), pltpu.VMEM((1, 128), F32)],
        compiler_params=_cparams(("parallel", "arbitrary")),
        name="mlstm",
    )(hqk, hv, ho, hgate, conv_prev, c0, n0.reshape(B, H, DK, 1), m0p,
      P['mlstm_conv_w'], P['mlstm_conv_b'].reshape(1, W), gate_bias, P['mlstm_gn_g'].reshape(1, W), tri)
    return out, c_out, n_out.reshape(B, H, DK), m_out[:, 0, :H], conv_out


def _fox_prep_body(q_ref, k_ref, gt_ref, qg_ref, kg_ref, gb_ref, qn_ref, kn_ref, lf_ref):
    def rms(x, g):
        outs = []
        for h in range(FOX_HEADS):
            xh = x[:, h * FOX_HD:(h + 1) * FOX_HD]
            outs.append(xh * lax.rsqrt(jnp.mean(xh * xh, axis=-1, keepdims=True) + EPS) * g)
        return jnp.concatenate(outs, axis=1)

    qn_ref[...] = (rms(q_ref[...], qg_ref[...]) * (FOX_HD ** -0.5)).astype(BF16)
    kn_ref[...] = rms(k_ref[...], kg_ref[...])
    lf_ref[...] = _log_sigmoid(gt_ref[...] + gb_ref[...])


def _fox_prep(hq, hk, hgate, P, gate_bias):
    M, W = hq.shape
    tm = _tile(M, ROW_TILE)
    row = lambda i: (i, 0)
    fix = lambda i: (0, 0)
    return pl.pallas_call(
        _fox_prep_body,
        grid=(M // tm,),
        in_specs=[pl.BlockSpec((tm, W), row), pl.BlockSpec((tm, W), row), pl.BlockSpec((tm, 128), row),
                  pl.BlockSpec((1, FOX_HD), fix), pl.BlockSpec((1, FOX_HD), fix), pl.BlockSpec((1, 128), fix)],
        out_specs=[pl.BlockSpec((tm, W), row), pl.BlockSpec((tm, W), row), pl.BlockSpec((tm, 128), row)],
        out_shape=[jax.ShapeDtypeStruct((M, W), BF16), jax.ShapeDtypeStruct((M, W), F32),
                   jax.ShapeDtypeStruct((M, 128), F32)],
        compiler_params=_cparams(("parallel",)),
        name="fox_prep",
    )(hq, hk, hgate, P['fox_q_g'].reshape(1, FOX_HD), P['fox_k_g'].reshape(1, FOX_HD), gate_bias)


def _fox_prep_prompt_body(q_ref, k_ref, v_ref, gt_ref, qg_ref, kg_ref, gb_ref, tri_ref,
                          qa_ref, kn_ref, vc_ref, kt_ref, va_ref, lf_ref, carry):
    i = pl.program_id(1)
    tm = q_ref.shape[0]
    HD = FOX_HD

    @pl.when(i == 0)
    def _():
        carry[...] = jnp.zeros_like(carry)

    lf = _log_sigmoid(gt_ref[...] + gb_ref[...])
    lf_ref[...] = lf
    c = _exact_left_dot(tri_ref[...], lf) + carry[...]
    carry[...] = c[tm - 1:tm, :]
    c2t = (c * (-LOG2E)).T
    lane = lax.broadcasted_iota(jnp.int32, (tm, HD), 1)
    q_ones = jnp.where(lane < 3, 1.0, 0.0).astype(BF16)
    v_ones = jnp.where(lane < 1, 1.0, 0.0).astype(BF16)
    row16 = lax.broadcasted_iota(jnp.int32, (16, tm), 0)
    q, k, v = q_ref[...], k_ref[...], v_ref[...]
    for h in range(FOX_HEADS):
        hs = slice(h * HD, (h + 1) * HD)
        qh, kh = q[:, hs], k[:, hs]
        qh = qh * lax.rsqrt(jnp.mean(qh * qh, axis=-1, keepdims=True) + EPS) * qg_ref[...]
        kh = kh * lax.rsqrt(jnp.mean(kh * kh, axis=-1, keepdims=True) + EPS) * kg_ref[...]
        qa_ref[:, 2 * h * HD:(2 * h + 1) * HD] = (qh * (HD ** -0.5 * LOG2E)).astype(BF16)
        qa_ref[:, (2 * h + 1) * HD:(2 * h + 2) * HD] = q_ones
        kn_ref[pl.ds(h, tm, stride=FOX_HEADS), :] = kh
        vc_ref[pl.ds(h, tm, stride=FOX_HEADS), :] = v[:, hs]
        kt_ref[0, h, 0, 0:HD, :] = kh.T.astype(BF16)
        bias = c2t[2 * MLSTM_HEADS + h:2 * MLSTM_HEADS + h + 1, :]
        hi = bias.astype(BF16).astype(F32)
        mid = (bias - hi).astype(BF16).astype(F32)
        lo = bias - hi - mid
        blk = jnp.where(row16 == 0, hi, jnp.where(row16 == 1, mid, jnp.where(row16 == 2, lo, 0.0)))
        kt_ref[0, h, 0, HD:HD + 16, :] = blk.astype(BF16)
        kt_ref[0, h, 0, HD + 16:2 * HD, :] = jnp.zeros((HD - 16, tm), BF16)
        va_ref[:, 2 * h * HD:(2 * h + 1) * HD] = v[:, hs].astype(BF16)
        va_ref[:, (2 * h + 1) * HD:(2 * h + 2) * HD] = v_ones


def _fox_prep_prompt(hq, hk, hv, hgate, P, gate_bias, B, T):
    M, W = hq.shape
    H, HD = FOX_HEADS, FOX_HD
    tm = _tile(T, FOX_TQ)
    nT = T // tm
    idx = np.arange(tm)
    tri = jnp.asarray(idx[:, None] >= idx[None, :], BF16)
    row = lambda b, i: (b * nT + i, 0)
    fix = lambda b, i: (0, 0)
    return pl.pallas_call(
        _fox_prep_prompt_body,
        grid=(B, nT),
        in_specs=[pl.BlockSpec((tm, W), row), pl.BlockSpec((tm, W), row), pl.BlockSpec((tm, W), row),
                  pl.BlockSpec((tm, 128), row),
                  pl.BlockSpec((1, HD), fix), pl.BlockSpec((1, HD), fix), pl.BlockSpec((1, 128), fix),
                  pl.BlockSpec((tm, tm), fix)],
        out_specs=[pl.BlockSpec((tm, 2 * W), row), pl.BlockSpec((tm * H, HD), row), pl.BlockSpec((tm * H, HD), row),
                   pl.BlockSpec((1, H, 1, 2 * HD, tm), lambda b, i: (b, 0, i, 0, 0)),
                   pl.BlockSpec((tm, 2 * W), row), pl.BlockSpec((tm, 128), row)],
        out_shape=[jax.ShapeDtypeStruct((M, 2 * W), BF16), jax.ShapeDtypeStruct((M * H, HD), F32),
                   jax.ShapeDtypeStruct((M * H, HD), F32),
                   jax.ShapeDtypeStruct((B, H, nT, 2 * HD, tm), BF16),
                   jax.ShapeDtypeStruct((M, 2 * W), BF16), jax.ShapeDtypeStruct((M, 128), F32)],
        scratch_shapes=[pltpu.VMEM((1, 128), F32)],
        compiler_params=_cparams(("parallel", "arbitrary")),
        name="fox_prep_prompt",
    )(hq, hk, hv, hgate, P['fox_q_g'].reshape(1, HD), P['fox_k_g'].reshape(1, HD), gate_bias, tri)


def _fox_prompt_body(q_ref, kt_ref, va_ref, o_ref, sa_scr, sb_scr, m_scr, acc_scr, *, tq):
    qi = pl.program_id(2)
    m_scr[...] = jnp.full_like(m_scr, NEG_INF)
    acc_scr[...] = jnp.zeros_like(acc_scr)

    def scores(s_ref, kj):
        s_ref[...] = jnp.dot(q_ref[0], kt_ref[0, 0, kj], preferred_element_type=F32)

    def update(s_ref, kj, masked):
        vb = va_ref[0, pl.ds(pl.multiple_of(kj * tq, tq), tq), :]
        s = s_ref[...]
        if masked:
            ii = lax.broadcasted_iota(jnp.int32, (tq, tq), 0)
            jj = lax.broadcasted_iota(jnp.int32, (tq, tq), 1)
            s = jnp.where(jj <= ii, s, NEG_INF)
        m_old = m_scr[...]
        m_new = jnp.maximum(m_old, jnp.max(s, axis=-1, keepdims=True))
        p = jnp.exp2(s - m_new)
        acc_scr[...] = jnp.exp2(m_old - m_new) * acc_scr[...] + jnp.dot(p.astype(BF16), vb, preferred_element_type=F32)
        m_scr[...] = m_new

    scores(sa_scr, 0)

    def body(j, carry):
        scores(sb_scr, 2 * j + 1)
        update(sa_scr, 2 * j, False)
        scores(sa_scr, 2 * j + 2)
        update(sb_scr, 2 * j + 1, False)
        return carry

    lax.fori_loop(0, lax.shift_right_logical(qi, 1), body, 0)

    @pl.when((qi & 1) == 0)
    def _():
        update(sa_scr, qi, True)

    @pl.when((qi & 1) == 1)
    def _():
        scores(sb_scr, qi)
        update(sa_scr, qi - 1, False)
        update(sb_scr, qi, True)

    acc = acc_scr[...]
    o_ref[0] = acc[:, :FOX_HD] / acc[:, FOX_HD:FOX_HD + 1]


def _fox_prompt(qa, kt, va, B, T):
    H, HD = FOX_HEADS, FOX_HD
    W = H * HD
    tq = kt.shape[-1]
    nQ = T // tq
    out = pl.pallas_call(
        functools.partial(_fox_prompt_body, tq=tq),
        grid=(B, H, nQ),
        in_specs=[pl.BlockSpec((1, tq, 2 * HD), lambda b, h, i: (b, i, h)),
                  pl.BlockSpec((1, 1, nQ, 2 * HD, tq), lambda b, h, i: (b, h, 0, 0, 0),
                               pipeline_mode=pl.Buffered(1)),
                  pl.BlockSpec((1, T, 2 * HD), lambda b, h, i: (b, 0, h), pipeline_mode=pl.Buffered(1))],
        out_specs=pl.BlockSpec((1, tq, HD), lambda b, h, i: (b, i, h)),
        out_shape=jax.ShapeDtypeStruct((B, T, W), F32),
        scratch_shapes=[pltpu.VMEM((tq, tq), F32), pltpu.VMEM((tq, tq), F32),
                        pltpu.VMEM((tq, 1), F32), pltpu.VMEM((tq, 2 * HD), F32)],
        compiler_params=_cparams(("parallel", "parallel", "arbitrary")),
        name="fox_prompt",
    )(qa.reshape(B, T, 2 * W), kt, va.reshape(B, T, 2 * W))
    return out.reshape(B * T, W)


def _fox_sample_body(q_ref, kc_ref, vc_ref, kn_ref, vn_ref, cq_ref, ckc_ref, ckn_ref, o_ref, *, T, past):
    H, HD = FOX_HEADS, FOX_HD
    q = q_ref[0]
    kn, vn = kn_ref[0], vn_ref[0]
    ii = lax.broadcasted_iota(jnp.int32, (T, T), 0)
    jj = lax.broadcasted_iota(jnp.int32, (T, T), 1)
    outs = []
    for h in range(H):
        cs = slice(h * HD, (h + 1) * HD)
        kc = kc_ref[0, pl.ds(h, past, stride=H), :]
        vc = vc_ref[0, pl.ds(h, past, stride=H), :]
        cq = cq_ref[0, h]
        s1 = _dot_nt(q[:, cs], kc) + (cq - ckc_ref[0, h])
        s2 = _dot_nt(q[:, cs], kn[:, cs]) + (cq - ckn_ref[0, h])
        s2 = jnp.where(jj <= ii, s2, NEG_INF)
        m = jnp.maximum(jnp.max(s1, axis=-1, keepdims=True), jnp.max(s2, axis=-1, keepdims=True))
        p1 = jnp.exp(s1 - m)
        p2 = jnp.exp(s2 - m)
        den = jnp.sum(p1, axis=-1, keepdims=True) + jnp.sum(p2, axis=-1, keepdims=True)
        outs.append((_dot(p1, vc) + _dot(p2, vn[:, cs])) / den)
    o_ref[0] = jnp.concatenate(outs, axis=1)


def _fox_sample(qn, kn, vn, k_cache, v_cache, c_all, B, T):
    H, HD = FOX_HEADS, FOX_HD
    W = H * HD
    past = k_cache.shape[1]
    ct = jnp.transpose(c_all, (0, 2, 1))
    cq = ct[:, :, past:].reshape(B, H, T, 1)
    ckc = ct[:, :, :past].reshape(B, H, 1, past)
    ckn = ct[:, :, past:].reshape(B, H, 1, T)
    b3 = lambda b: (b, 0, 0)
    b4 = lambda b: (b, 0, 0, 0)
    out = pl.pallas_call(
        functools.partial(_fox_sample_body, T=T, past=past),
        grid=(B,),
        in_specs=[pl.BlockSpec((1, T, W), b3), pl.BlockSpec((1, past * H, HD), b3), pl.BlockSpec((1, past * H, HD), b3),
                  pl.BlockSpec((1, T, W), b3), pl.BlockSpec((1, T, W), b3),
                  pl.BlockSpec((1, H, T, 1), b4), pl.BlockSpec((1, H, 1, past), b4), pl.BlockSpec((1, H, 1, T), b4)],
        out_specs=pl.BlockSpec((1, T, W), b3),
        out_shape=jax.ShapeDtypeStruct((B, T, W), F32),
        compiler_params=_cparams(("parallel",)),
        name="fox_sample",
    )(qn.reshape(B, T, W), k_cache.reshape(B, past * H, HD), v_cache.reshape(B, past * H, HD),
      kn.reshape(B, T, W), vn.reshape(B, T, W), cq, ckc, ckn)
    return out.reshape(B * T, W)


def _mem_attn_body(x_ref, mk_ref, mv_ref, wq_ref, wo_ref, g_ref, b_ref, o_ref, *, prec):
    x = x_ref[...]
    q = _wdot(x, wq_ref, prec)
    outs = []
    for h in range(MEM_HEADS):
        cs = slice(h * MEM_HD, (h + 1) * MEM_HD)
        mkh, mvh = mk_ref[0, :, cs], mv_ref[0, :, cs]
        s = _dot_nt(q[:, cs], mkh, prec) * (MEM_HD ** -0.5)
        m = jnp.max(s, axis=-1, keepdims=True)
        p = jnp.exp(s - m)
        outs.append(_dot(p, mvh, prec) / jnp.sum(p, axis=-1, keepdims=True))
    o = jnp.concatenate(outs, axis=1)
    att = _wdot(o, wo_ref, prec)
    y = _layer_norm(ALPHA * x + att, g_ref[...], b_ref[...])
    o_ref[...] = y


def _mem_attn_ln(x, mk, mv, l, wq, wo, g, bias, B, T, prec=False):
    D = D_MODEL
    Mm = mk.shape[1]
    tm = _tile(T, ROW_TILE)
    nT = T // tm
    row = lambda b, i: (b * nT + i, 0)
    fix = lambda b, i: (0, 0)
    mem_spec = pl.BlockSpec((1,) + mk.shape[1:], lambda b, i: (l * B + b,) + (0,) * (mk.ndim - 1))
    return pl.pallas_call(
        functools.partial(_mem_attn_body, prec=prec),
        grid=(B, nT),
        in_specs=[pl.BlockSpec((tm, D), row), mem_spec, mem_spec,
                  _wspec(wq), _wspec(wo),
                  pl.BlockSpec((1, D), fix), pl.BlockSpec((1, D), fix)],
        out_specs=pl.BlockSpec((tm, D), row),
        out_shape=jax.ShapeDtypeStruct((B * T, D), F32),
        compiler_params=_cparams(("parallel", "parallel")),
        name="mem_attn_ln",
    )(x, mk, mv, wq, wo, g.reshape(1, D), bias.reshape(1, D))


def _router_body(x_ref, w_ref, b_ref, tri_ref, o_ref, cnt_ref, cnt_scr):
    x = x_ref[...]
    w = w_ref[...]
    x1, x2, x3 = _split3(x)
    w1, w2, w3 = _split3(w)
    nt = lambda a, c: lax.dot_general(a, c, _NT, preferred_element_type=F32)
    logits = (nt(w1, x1) + (nt(w1, x2) + nt(w2, x1)) + (nt(w1, x3) + nt(w2, x2) + nt(w3, x1))) + b_ref[...]
    m = jnp.max(logits, axis=0, keepdims=True)
    e = jnp.exp(logits - m)
    p = e / jnp.sum(e, axis=0, keepdims=True)
    rows = [p[j:j + 1, :] for j in range(N_EXPERTS)]
    best = None
    sel = None
    for g in range(N_GROUPS):
        a, b, c, d = rows[4 * g:4 * g + 4]
        top2 = jnp.maximum(jnp.maximum(jnp.maximum(a + b, a + c), jnp.maximum(a + d, b + c)),
                           jnp.maximum(b + d, c + d))
        if g == 0:
            best, sel = top2, jnp.zeros_like(top2, dtype=jnp.int32)
        else:
            upd = top2 > best
            sel = jnp.where(upd, g, sel)
            best = jnp.maximum(best, top2)
    pin = []
    for kk in range(EXPERTS_PER_GROUP):
        v = rows[kk]
        for g in range(1, N_GROUPS):
            v = jnp.where(sel == g, rows[4 * g + kk], v)
        pin.append(v)
    v1, i1 = pin[0], jnp.zeros_like(sel)
    for kk in range(1, EXPERTS_PER_GROUP):
        upd = pin[kk] > v1
        i1 = jnp.where(upd, kk, i1)
        v1 = jnp.maximum(v1, pin[kk])
    v2, i2 = None, None
    for kk in range(EXPERTS_PER_GROUP):
        cand = jnp.where(i1 == kk, -1.0, pin[kk])
        if v2 is None:
            v2, i2 = cand, jnp.zeros_like(sel)
        else:
            upd = cand > v2
            i2 = jnp.where(upd, kk, i2)
            v2 = jnp.maximum(v2, cand)
    tot = v1 + v2
    e1 = sel * EXPERTS_PER_GROUP + i1
    e2 = sel * EXPERTS_PER_GROUP + i2
    @pl.when(pl.program_id(0) == 0)
    def _():
        cnt_scr[...] = jnp.zeros_like(cnt_scr)

    eidx = lax.broadcasted_iota(jnp.int32, logits.shape, 0)
    oh1 = jnp.where(eidx == e1, 1.0, 0.0)
    oh2 = jnp.where(eidx == e2, 1.0, 0.0)
    oh = oh1 + oh2
    base = cnt_scr[...] + jnp.dot(oh.astype(BF16), tri_ref[...], preferred_element_type=F32)
    r1 = jnp.sum(oh1 * base, axis=0, keepdims=True)
    r2 = jnp.sum(oh2 * base, axis=0, keepdims=True)
    cnt = cnt_scr[...] + jnp.sum(oh, axis=1, keepdims=True)
    cnt_scr[...] = cnt
    cnt_ref[...] = jnp.broadcast_to(cnt, cnt_ref.shape)
    zero = jnp.zeros_like(v1)
    o_ref[...] = jnp.concatenate([e1.astype(F32), e2.astype(F32), v1 / tot, v2 / tot, r1, r2, zero, zero], axis=0)


def _router(x, w_router, b_router):
    M, D = x.shape
    tm = _tile(M, ROW_TILE)
    idx = np.arange(tm)
    tri = jnp.asarray(idx[:, None] < idx[None, :], BF16)
    r, cnt = pl.pallas_call(
        _router_body,
        grid=(M // tm,),
        in_specs=[pl.BlockSpec((tm, D), lambda i: (i, 0)),
                  pl.BlockSpec((N_EXPERTS, D), lambda i: (0, 0)),
                  pl.BlockSpec((N_EXPERTS, 1), lambda i: (0, 0)),
                  pl.BlockSpec((tm, tm), lambda i: (0, 0))],
        out_specs=[pl.BlockSpec((8, tm), lambda i: (0, i)), pl.BlockSpec((N_EXPERTS, 128), lambda i: (0, 0))],
        out_shape=[jax.ShapeDtypeStruct((8, M), F32), jax.ShapeDtypeStruct((N_EXPERTS, 128), F32)],
        scratch_shapes=[pltpu.VMEM((N_EXPERTS, 1), F32)],
        compiler_params=_cparams(("arbitrary",)),
        name="router",
    )(x, w_router.T, b_router.reshape(N_EXPERTS, 1), tri)
    return r, cnt[:, 0].astype(jnp.int32)


def _expert_body(be_ref, x_ref, wg_ref, wu_ref, wd_ref, o_ref, wg_s, wu_s, wd_s):
    i = pl.program_id(0)
    prev = be_ref[jnp.maximum(i - 1, 0)]

    @pl.when((i == 0) | (be_ref[i] != prev))
    def _():
        wg_s[...] = wg_ref[0, 0].astype(BF16)
        wu_s[...] = wu_ref[0, 0].astype(BF16)
        wd_s[...] = wd_ref[0, 0].astype(BF16)

    x = x_ref[...].astype(BF16)
    hg = jnp.dot(x, wg_s[...], preferred_element_type=F32)
    hu = jnp.dot(x, wu_s[...], preferred_element_type=F32)
    hb = (_silu(hg) * hu).astype(BF16)
    o_ref[...] = jnp.dot(hb, wd_s[...], preferred_element_type=F32)


def _experts(xg, blk_exp, wg, wu, wd, l):
    n_blocks = blk_exp.shape[0]
    D, DE = D_MODEL, D_EXPERT
    grid_spec = pltpu.PrefetchScalarGridSpec(
        num_scalar_prefetch=1,
        grid=(n_blocks,),
        in_specs=[pl.BlockSpec((MOE_BLOCK, D), lambda i, be: (i, 0)),
                  pl.BlockSpec((1, 1, D, DE), lambda i, be: (l, be[i], 0, 0)),
                  pl.BlockSpec((1, 1, D, DE), lambda i, be: (l, be[i], 0, 0)),
                  pl.BlockSpec((1, 1, DE, D), lambda i, be: (l, be[i], 0, 0))],
        out_specs=pl.BlockSpec((MOE_BLOCK, D), lambda i, be: (i, 0)),
        scratch_shapes=[pltpu.VMEM((D, DE), BF16), pltpu.VMEM((D, DE), BF16), pltpu.VMEM((DE, D), BF16)],
    )
    return pl.pallas_call(
        _expert_body,
        grid_spec=grid_spec,
        out_shape=jax.ShapeDtypeStruct((n_blocks * MOE_BLOCK, D), F32),
        compiler_params=_cparams(("arbitrary",)),
        name="experts",
    )(blk_exp, xg, wg, wu, wd)


def _combine_ln_body(x_ref, y0_ref, y1_ref, gt_ref, g_ref, b_ref, o_ref):
    gt = gt_ref[...]
    ffn = y0_ref[...] * gt[:, 0:1] + y1_ref[...] * gt[:, 1:2]
    o_ref[...] = _layer_norm(ALPHA * x_ref[...] + ffn, g_ref[...], b_ref[...])


def _combine_ln(x, y0, y1, gates, g, bias):
    M, D = x.shape
    tm = _tile(M, ROW_TILE)
    row = lambda i: (i, 0)
    fix = lambda i: (0, 0)
    return pl.pallas_call(
        _combine_ln_body,
        grid=(M // tm,),
        in_specs=[pl.BlockSpec((tm, D), row), pl.BlockSpec((tm, D), row), pl.BlockSpec((tm, D), row),
                  pl.BlockSpec((tm, 2), row), pl.BlockSpec((1, D), fix), pl.BlockSpec((1, D), fix)],
        out_specs=pl.BlockSpec((tm, D), row),
        out_shape=jax.ShapeDtypeStruct((M, D), F32),
        compiler_params=_cparams(("parallel",)),
        name="combine_ln",
    )(x, y0, y1, gates, g.reshape(1, D), bias.reshape(1, D))


def _moe_ln(x, P, l):
    M = x.shape[0]
    r, counts = _router(x, P['w_router'], P['b_router'])
    e = r[0:2].astype(jnp.int32).T.reshape(-1)
    gates = r[2:4].T
    rank = r[4:6].astype(jnp.int32).T.reshape(-1)
    padded = (counts + MOE_BLOCK - 1) // MOE_BLOCK * MOE_BLOCK
    p_ends = jnp.cumsum(padded)
    p_starts = p_ends - padded
    dest = p_starts[e] + rank
    n_blocks = -(-2 * M // MOE_BLOCK) + N_EXPERTS
    tok = jnp.arange(2 * M, dtype=jnp.int32) // 2
    slot_tok = jnp.zeros((n_blocks * MOE_BLOCK,), jnp.int32).at[dest].set(tok)
    blk_start = jnp.arange(n_blocks, dtype=jnp.int32) * MOE_BLOCK
    blk_exp = jnp.minimum(jnp.sum((p_ends[None, :] <= blk_start[:, None]).astype(jnp.int32), axis=1), N_EXPERTS - 1)
    xg = x[slot_tok]
    yb = _experts(xg, blk_exp, P['w_exp_gate'], P['w_exp_up'], P['w_exp_down'], l)
    d2 = dest.reshape(M, 2)
    return _combine_ln(x, yb[d2[:, 0]], yb[d2[:, 1]], gates, P['ln_g'][l, 2], P['ln_b'][l, 2])


def _trunk(x, pos0, mem_k, mem_v, states, P, Wc, Wf, is_prompt, prec0):
    B, T, D = x.shape
    ret_S, rwkv_S, shift, mC, mn, mm, conv, fk, fv, flf = states
    xf = x.reshape(B * T, D)
    W0 = Wf if prec0 else Wc
    hq, hk, hv, hg, hb = _project(xf, W0['w_in0'], (256, 256, 512, 512, RWKV_PROJ), prec0)
    L_ret = _tile(T, RET_CHUNK)
    out_a, ret_S = _retention(hq, hk, hv, hg, ret_S, P['ret_gn_g'], B, T, pos0, L_ret, prec0)
    hb3 = hb.reshape(B, T, RWKV_PROJ)
    new_shift = hb3[:, -1:]
    L_rwkv = min(RWKV_CHUNK, T)
    if T > L_rwkv:
        assert not prec0
        pre = _rwkv_prep(hb, shift, P, L_rwkv, False, seqs=(B, T))
        out_b, rwkv_S = _rwkv_scan(pre, rwkv_S, P['rwkv_gn_g'], P['rwkv_gn_b'], B, T, L_rwkv)
    else:
        prev = jnp.concatenate([shift, hb3[:, :-1]], axis=1).reshape(B * T, RWKV_PROJ)
        pre = _rwkv_prep(hb, prev, P, T, prec0)
        out_b, rwkv_S = _rwkv_single_chunk(pre, rwkv_S, P['rwkv_gn_g'], P['rwkv_gn_b'], B, T, prec0)
    xf = _out_proj_ln(xf, out_a, out_b, W0['w_out0a'], W0['w_out0b'], P['ln_g'][0, 0], P['ln_b'][0, 0], prec0)
    xf = _mem_attn_ln(xf, mem_k, mem_v, 0, W0['w_mem_q'][0], W0['w_mem_o'][0], P['ln_g'][0, 1], P['ln_b'][0, 1],
                      B, T, prec0)
    xf = _moe_ln(xf, P, 0)
    hqk, hv1, ho, fq, fkk, fvv, hgate = _project(xf, Wc['w_in1'], (512, 512, 512, 512, 512, 512, 128))
    out_c, mC, mn, mm, conv = _mlstm(hqk, hv1, ho, hgate, conv, mC, mn, mm, P, Wc['gate_bias'], B, T, L_ret)
    if is_prompt:
        qa, kn, vc, kt, va, lf = _fox_prep_prompt(fq, fkk, fvv, hgate, P, Wc['gate_bias'], B, T)
        out_d = _fox_prompt(qa, kt, va, B, T)
        logf = lf[:, 2 * MLSTM_HEADS:2 * MLSTM_HEADS + FOX_HEADS].reshape(B, T, FOX_HEADS)
    else:
        qn, kn, lf = _fox_prep(fq, fkk, hgate, P, Wc['gate_bias'])
        vc = fvv
        logf = lf[:, 2 * MLSTM_HEADS:2 * MLSTM_HEADS + FOX_HEADS].reshape(B, T, FOX_HEADS)
        c_all = jnp.cumsum(jnp.concatenate([flf, logf], axis=1), axis=1)
        out_d = _fox_sample(qn, kn, fvv, fk, fv, c_all, B, T)
    xf = _out_proj_ln(xf, out_c, out_d, Wc['w_out1a'], Wc['w_out1b'], P['ln_g'][1, 0], P['ln_b'][1, 0])
    xf = _mem_attn_ln(xf, mem_k, mem_v, 1, Wc['w_mem_q'][1], Wc['w_mem_o'][1], P['ln_g'][1, 1], P['ln_b'][1, 1],
                      B, T)
    xf = _moe_ln(xf, P, 1)
    fk_new = kn.reshape(B, T, FOX_HEADS, FOX_HD)
    fv_new = vc.reshape(B, T, FOX_HEADS, FOX_HD)
    return (xf.reshape(B, T, D), ret_S, rwkv_S, new_shift, mC, mn, mm, conv, fk_new, fv_new, logf)


def kernel(x_prompt, x_sample, mem_prompt, state_ret, state_rwkv, cache_rwkv_shift, state_mlstm_c, state_mlstm_n,
           state_mlstm_m, cache_mlstm_conv, cache_fox_k, cache_fox_v, cache_fox_logf, cache_mem_k, cache_mem_v,
           w_in0, ret_gn_g, rwkv_mu, rwkv_w0, rwkv_w2, rwkv_a0, rwkv_a2, rwkv_g2, rwkv_k_k, rwkv_k_a, rwkv_r_k,
           rwkv_gn_g, rwkv_gn_b, w_out0, w_in1, mlstm_conv_w, mlstm_conv_b, mlstm_b_i, mlstm_b_f, mlstm_gn_g,
           fox_q_g, fox_k_g, fox_b_f, w_out1, w_mem_q, w_mem_k, w_mem_v, w_mem_o, w_router, b_router,
           w_exp_gate, w_exp_up, w_exp_down, ln_g, ln_b):
    P = dict(ret_gn_g=ret_gn_g, rwkv_mu=rwkv_mu, rwkv_w0=rwkv_w0, rwkv_w2=rwkv_w2, rwkv_a0=rwkv_a0,
             rwkv_a2=rwkv_a2, rwkv_g2=rwkv_g2, rwkv_k_k=rwkv_k_k, rwkv_k_a=rwkv_k_a, rwkv_r_k=rwkv_r_k,
             rwkv_gn_g=rwkv_gn_g, rwkv_gn_b=rwkv_gn_b, mlstm_conv_w=mlstm_conv_w, mlstm_conv_b=mlstm_conv_b,
             mlstm_gn_g=mlstm_gn_g, fox_q_g=fox_q_g, fox_k_g=fox_k_g, w_router=w_router, b_router=b_router,
             w_exp_gate=w_exp_gate, w_exp_up=w_exp_up, w_exp_down=w_exp_down, ln_g=ln_g, ln_b=ln_b)
    B, M = mem_prompt.shape[0], mem_prompt.shape[1]
    D = D_MODEL
    H = MLSTM_HEADS
    ret_proj = 2 * RET_HEADS * RET_DK + 2 * RET_HEADS * RET_DV
    qk1, w1 = 2 * H * MLSTM_DK, H * MLSTM_DV
    off = qk1 + w1
    mlstm_proj = off + 2 * H + w1
    fw = FOX_HEADS * FOX_HD
    gate_cols = jnp.concatenate([w_in1[:, off:off + 2 * H], w_in1[:, mlstm_proj + 3 * fw:],
                                 jnp.zeros((D, 128 - 2 * H - FOX_HEADS), F32)], axis=1)
    w_in1_c = jnp.concatenate([w_in1[:, :off], w_in1[:, off + 2 * H:mlstm_proj],
                               w_in1[:, mlstm_proj:mlstm_proj + 3 * fw], gate_cols], axis=1)
    gate_bias = jnp.concatenate([mlstm_b_i, mlstm_b_f, fox_b_f, jnp.zeros((128 - 2 * H - FOX_HEADS,), F32)]).reshape(1, 128)
    wa0, wb0 = w_out0[:RET_HEADS * RET_DV], w_out0[RET_HEADS * RET_DV:]
    Wf = dict(w_in0=_hi_lo(w_in0), w_out0a=_hi_lo(wa0), w_out0b=_hi_lo(wb0),
              w_mem_q=[_hi_lo(w_mem_q[0])], w_mem_o=[_hi_lo(w_mem_o[0])])
    Wc = dict(w_in0=w_in0.astype(BF16), w_in1=w_in1_c.astype(BF16), gate_bias=gate_bias,
              w_out0a=wa0.astype(BF16), w_out0b=wb0.astype(BF16),
              w_out1a=w_out1[:w1].astype(BF16), w_out1b=w_out1[w1:].astype(BF16),
              w_mem_q=w_mem_q.astype(BF16), w_mem_o=w_mem_o.astype(BF16))
    w_kv = jnp.concatenate([w_mem_k, w_mem_v], axis=0)
    memkv = _mem_project(mem_prompt.reshape(B * M, D), jnp.swapaxes(_hi_lo(w_kv), 0, 1))
    p_mem_k = memkv[:DEPTH].reshape(DEPTH * B, M, D)
    p_mem_v = memkv[DEPTH:].reshape(DEPTH * B, M, D)
    zeros = lambda *s: jnp.zeros(s, F32)
    prompt_states = (zeros(B, RET_HEADS, RET_DK, RET_DV), zeros(B, RWKV_HEADS, RWKV_HD, RWKV_HD),
                     zeros(B, 1, RWKV_PROJ), zeros(B, H, MLSTM_DK, MLSTM_DV), zeros(B, H, MLSTM_DK), zeros(B, H),
                     zeros(B, MLSTM_CONV - 1, qk1), None, None, None)
    p_out = _trunk(x_prompt, 0, p_mem_k, p_mem_v, prompt_states, P, Wc, Wf, True, PROMPT_PREC0)
    DB = x_sample.shape[0]
    sample_states = (state_ret, state_rwkv, cache_rwkv_shift, state_mlstm_c, state_mlstm_n, state_mlstm_m,
                     cache_mlstm_conv, cache_fox_k, cache_fox_v, cache_fox_logf)
    s_out = _trunk(x_sample, cache_fox_k.shape[1], cache_mem_k.reshape(DEPTH * DB, M, D),
                   cache_mem_v.reshape(DEPTH * DB, M, D), sample_states, P, Wc, Wf, False, True)
    mem_shape = (DEPTH, B, M, MEM_HEADS, MEM_HD)
    return ((p_out[0], s_out[0]) + p_out[1:] + (p_mem_k.reshape(mem_shape), p_mem_v.reshape(mem_shape)) + s_out[1:])
```

```python
import functools
import math

import numpy as np
import jax
import jax.numpy as jnp
from jax import lax
from jax.experimental import pallas as pl
from jax.experimental.pallas import tpu as pltpu

F32 = jnp.float32
BF16 = jnp.bfloat16

D_MODEL = 1024
DEPTH = 2
RET_HEADS, RET_DK, RET_DV = 4, 64, 128
ROPE_BASE = 10000.0
RWKV_HEADS, RWKV_HD = 8, 64
RWKV_W = RWKV_HEADS * RWKV_HD
RWKV_PROJ = 3 * RWKV_W + 64 + 64 + 128
MLSTM_HEADS, MLSTM_DK, MLSTM_DV, MLSTM_CONV = 4, 64, 128, 4
FOX_HEADS, FOX_HD = 4, 128
MEM_HEADS, MEM_HD = 4, 256
N_EXPERTS, N_GROUPS, EXPERTS_PER_GROUP = 16, 4, 4
D_EXPERT = 512
ALPHA = (2 * DEPTH) ** 0.25
EPS = 1e-5
NEG_INF = -1e30
LOG2E = 1.4426950408889634

VMEM_LIMIT_BYTES = 56 * 1024 * 1024
ROW_TILE = 512
RET_CHUNK = 256
RWKV_CHUNK = 64
RWKV_ROWS = 512
RWKV_LOCAL_ROWS = 1024
RWKV_SEQS = 8
FOX_TQ = 1024
MOE_BLOCK = 512
PROMPT_PREC0 = False


def _cparams(sem):
    return pltpu.CompilerParams(dimension_semantics=sem, vmem_limit_bytes=VMEM_LIMIT_BYTES)


def _tile(n, pref):
    if n <= pref:
        return n
    t = pref
    while t >= 8:
        if n % t == 0:
            return t
        t -= 8
    return n


_NN = (((1,), (0,)), ((), ()))
_NT = (((1,), (1,)), ((), ()))
_TN = (((0,), (0,)), ((), ()))
_BNN = (((2,), (1,)), ((0,), (0,)))
_BNT = (((2,), (2,)), ((0,), (0,)))
_BTN = (((1,), (1,)), ((0,), (0,)))


def _split2(a):
    a = a.astype(F32)
    hi = a.astype(BF16)
    return hi, (a - hi.astype(F32)).astype(BF16)


def _dg(a, b, dims, prec):
    if not prec:
        return lax.dot_general(a.astype(BF16), b.astype(BF16), dims, preferred_element_type=F32)
    a1, a2 = _split2(a)
    b1, b2 = _split2(b)
    d = lambda p, q: lax.dot_general(p, q, dims, preferred_element_type=F32)
    return d(a1, b1) + (d(a1, b2) + d(a2, b1))


def _dot(a, b, prec=False):
    return _dg(a, b, _NN, prec)


def _wdot(x, w_ref, prec):
    if not prec:
        return jnp.dot(x.astype(BF16), w_ref[...], preferred_element_type=F32)
    x1, x2 = _split2(x)
    d = lambda p, q: jnp.dot(p, q, preferred_element_type=F32)
    return d(x1, w_ref[0]) + (d(x1, w_ref[1]) + d(x2, w_ref[0]))


def _wspec(w):
    return pl.BlockSpec(w.shape, lambda *_: (0,) * w.ndim)


def _hi_lo(w):
    bits = lax.bitcast_convert_type(w, jnp.uint32) & jnp.uint32(0xFFFF0000)
    hi = lax.bitcast_convert_type(bits, F32)
    return jnp.stack([hi.astype(BF16), (w - hi).astype(BF16)])


def _dot_nt(a, b, prec=False):
    return _dg(a, b, _NT, prec)


def _dot_tn(a, b, prec=False):
    return _dg(a, b, _TN, prec)


def _split3(a):
    a1 = a.astype(BF16)
    r1 = a - a1.astype(F32)
    a2 = r1.astype(BF16)
    a3 = (r1 - a2.astype(F32)).astype(BF16)
    return a1, a2, a3


def _exact_left_dot(e, a):
    a1, a2, a3 = _split3(a)
    d = lambda p: jnp.dot(e, p, preferred_element_type=F32)
    return d(a1) + d(a2) + d(a3)


def _exact_right_dot(a, e):
    a1, a2, a3 = _split3(a)
    d = lambda p: jnp.dot(p, e, preferred_element_type=F32)
    return d(a1) + d(a2) + d(a3)


def _sigmoid(x):
    return 1.0 / (1.0 + jnp.exp(-x))


def _silu(x):
    return x * _sigmoid(x)


def _softplus(x):
    return jnp.maximum(x, 0.0) + jnp.log1p(jnp.exp(-jnp.abs(x)))


def _log_sigmoid(x):
    return -_softplus(-x)


def _layer_norm(z, g, b):
    mu = jnp.mean(z, axis=-1, keepdims=True)
    d = z - mu
    var = jnp.mean(d * d, axis=-1, keepdims=True)
    return d * lax.rsqrt(var + EPS) * g + b


def _lane_norm(y):
    mu = jnp.mean(y, axis=-1, keepdims=True)
    d = y - mu
    var = jnp.mean(d * d, axis=-1, keepdims=True)
    return d * lax.rsqrt(var + EPS)


def _proj_body(x_ref, w_ref, *o_refs, widths, prec):
    h = _wdot(x_ref[...], w_ref, prec)
    off = 0
    for o_ref, wd in zip(o_refs, widths):
        o_ref[...] = h[:, off:off + wd]
        off += wd


def _project(x, w, widths, prec=False):
    M, K = x.shape
    tm = _tile(M, ROW_TILE)
    return pl.pallas_call(
        functools.partial(_proj_body, widths=widths, prec=prec),
        grid=(M // tm,),
        in_specs=[pl.BlockSpec((tm, K), lambda i: (i, 0)), _wspec(w)],
        out_specs=[pl.BlockSpec((tm, wd), lambda i: (i, 0)) for wd in widths],
        out_shape=[jax.ShapeDtypeStruct((M, wd), F32) for wd in widths],
        compiler_params=_cparams(("parallel",)),
        name="project",
    )(x, w)


def _mem_proj_body(x_ref, w_ref, o_ref):
    o_ref[0] = _wdot(x_ref[...], w_ref.at[0], True)


def _mem_project(x, w):
    M, K = x.shape
    J, _, _, N = w.shape
    return pl.pallas_call(
        _mem_proj_body,
        grid=(J,),
        in_specs=[pl.BlockSpec((M, K), lambda j: (0, 0)),
                  pl.BlockSpec((1, 2, K, N), lambda j: (j, 0, 0, 0))],
        out_specs=pl.BlockSpec((1, M, N), lambda j: (j, 0, 0)),
        out_shape=jax.ShapeDtypeStruct((J, M, N), F32),
        compiler_params=_cparams(("parallel",)),
        name="mem_project",
    )(x, w)


def _out_ln_body(x_ref, a_ref, b_ref, wa_ref, wb_ref, g_ref, bias_ref, o_ref, *, prec):
    mix = _wdot(a_ref[...], wa_ref, prec) + _wdot(b_ref[...], wb_ref, prec)
    o_ref[...] = _layer_norm(ALPHA * x_ref[...] + mix, g_ref[...], bias_ref[...])


def _out_proj_ln(x, a, b, wa, wb, g, bias, prec=False):
    M, D = x.shape
    Ka, Kb = a.shape[1], b.shape[1]
    tm = _tile(M, ROW_TILE)
    row = lambda i: (i, 0)
    fix = lambda i: (0, 0)
    return pl.pallas_call(
        functools.partial(_out_ln_body, prec=prec),
        grid=(M // tm,),
        in_specs=[pl.BlockSpec((tm, D), row), pl.BlockSpec((tm, Ka), row), pl.BlockSpec((tm, Kb), row),
                  _wspec(wa), _wspec(wb),
                  pl.BlockSpec((1, D), fix), pl.BlockSpec((1, D), fix)],
        out_specs=pl.BlockSpec((tm, D), row),
        out_shape=jax.ShapeDtypeStruct((M, D), F32),
        compiler_params=_cparams(("parallel",)),
        name="out_proj_ln",
    )(x, a, b, wa, wb, g.reshape(1, D), bias.reshape(1, D))


def _ret_body(q_ref, k_ref, v_ref, g_ref, cos_ref, sa_ref, sb_ref, dmat_ref, qdec_ref, kdec_ref, sdec_ref,
              s0_ref, gn_ref, o_ref, sout_ref, s_scr, *, prec):
    i = pl.program_id(1)

    @pl.when(i == 0)
    def _():
        s_scr[...] = s0_ref[0]

    cos, sin_a, sin_b = cos_ref[...], sa_ref[...], sb_ref[...]
    width = RET_HEADS * RET_DK
    half = RET_DK // 2

    def rope(x):
        return x * cos + pltpu.roll(x, width - half, 1) * sin_a + pltpu.roll(x, half, 1) * sin_b

    q = rope(q_ref[...])
    k = rope(k_ref[...]) * (RET_DK ** -0.5)
    v = v_ref[...]
    gate = g_ref[...]
    H = RET_HEADS
    qh = jnp.stack([q[:, h * RET_DK:(h + 1) * RET_DK] for h in range(H)], axis=0)
    kh = jnp.stack([k[:, h * RET_DK:(h + 1) * RET_DK] for h in range(H)], axis=0)
    vh = jnp.stack([v[:, h * RET_DV:(h + 1) * RET_DV] for h in range(H)], axis=0)
    s_old = s_scr[...]
    s = _dg(qh, kh, _BNT, prec) * dmat_ref[...]
    o = _dg(s, vh, _BNN, prec) + _dg(qh, s_old, _BNN, prec) * qdec_ref[...]
    s_scr[...] = sdec_ref[...] * s_old + _dg(kh * kdec_ref[...], vh, _BTN, prec)
    on = _lane_norm(o)
    hn = jnp.concatenate([on[h] for h in range(H)], axis=1) * gn_ref[...]
    o_ref[...] = _silu(gate) * hn

    @pl.when(i == pl.num_programs(1) - 1)
    def _():
        sout_ref[0] = s_scr[...]


def _retention(hq, hk, hv, hg, s0, gn_g, B, T, pos0, L, prec=False):
    nT = T // L
    lg = np.log(1.0 - 2.0 ** (-5.0 - np.arange(RET_HEADS)))
    idx = np.arange(L, dtype=np.float64)
    rel = idx[:, None] - idx[None, :]
    dmat = np.where(rel >= 0, np.exp(np.maximum(rel, 0.0)[None] * lg[:, None, None]), 0.0)
    qdec = np.exp((idx + 1.0)[None, :, None] * lg[:, None, None])
    kdec = np.exp((L - 1.0 - idx)[None, :, None] * lg[:, None, None])
    sdec = np.exp(L * lg)[:, None, None]
    half = RET_DK // 2
    inv = ROPE_BASE ** (-jnp.arange(half, dtype=F32) / half)
    ang = (pos0 + jnp.arange(T)).astype(F32)[:, None] * inv[None, :]
    cos, sin = jnp.cos(ang), jnp.sin(ang)
    zero = jnp.zeros_like(sin)
    cos_t = jnp.tile(jnp.concatenate([cos, cos], axis=1), (1, RET_HEADS))
    sin_a = jnp.tile(jnp.concatenate([-sin, zero], axis=1), (1, RET_HEADS))
    sin_b = jnp.tile(jnp.concatenate([zero, sin], axis=1), (1, RET_HEADS))
    qk_w = RET_HEADS * RET_DK
    v_w = RET_HEADS * RET_DV
    row = lambda b, i: (b * nT + i, 0)
    tab = lambda b, i: (i, 0)
    fix3 = lambda b, i: (0, 0, 0)
    out, s_out = pl.pallas_call(
        functools.partial(_ret_body, prec=prec),
        grid=(B, nT),
        in_specs=[pl.BlockSpec((L, qk_w), row), pl.BlockSpec((L, qk_w), row),
                  pl.BlockSpec((L, v_w), row), pl.BlockSpec((L, v_w), row),
                  pl.BlockSpec((L, qk_w), tab), pl.BlockSpec((L, qk_w), tab), pl.BlockSpec((L, qk_w), tab),
                  pl.BlockSpec((RET_HEADS, L, L), fix3), pl.BlockSpec((RET_HEADS, L, 1), fix3),
                  pl.BlockSpec((RET_HEADS, L, 1), fix3), pl.BlockSpec((RET_HEADS, 1, 1), fix3),
                  pl.BlockSpec((1, RET_HEADS, RET_DK, RET_DV), lambda b, i: (b, 0, 0, 0)),
                  pl.BlockSpec((1, v_w), lambda b, i: (0, 0))],
        out_specs=[pl.BlockSpec((L, v_w), row),
                   pl.BlockSpec((1, RET_HEADS, RET_DK, RET_DV), lambda b, i: (b, 0, 0, 0))],
        out_shape=[jax.ShapeDtypeStruct((B * T, v_w), F32),
                   jax.ShapeDtypeStruct((B, RET_HEADS, RET_DK, RET_DV), F32)],
        scratch_shapes=[pltpu.VMEM((RET_HEADS, RET_DK, RET_DV), F32)],
        compiler_params=_cparams(("parallel", "arbitrary")),
        name="retention",
    )(hq, hk, hv, hg, cos_t, sin_a, sin_b, jnp.asarray(dmat, F32), jnp.asarray(qdec, F32),
      jnp.asarray(kdec, F32), jnp.asarray(sdec, F32), s0, gn_g.reshape(1, v_w))
    return out, s_out


def _rwkv_prep_body(h_ref, p_ref, mu_ref, w0_ref, w2_ref, a0_ref, a2_ref, g2_ref, kk_ref, ka_ref, rk_ref,
                    ones_ref, tri_ref, blk_ref, sel_ref,
                    kq_ref, rq_ref, kt_ref, bt_ref, ke_ref, be_ref, v_ref, g_ref, bon_ref, gt_ref, *scratch,
                    prec, shift_rows):
    hb = h_ref[...]
    if shift_rows:
        carry, = scratch
        tm = hb.shape[0]
        first = jnp.where(pl.program_id(1) == 0, p_ref[0], carry[...])
        row = lax.broadcasted_iota(jnp.int32, (tm, 1), 0)
        prev = jnp.where(row == 0, first, pltpu.roll(hb, 1, 0))
        carry[...] = hb[tm - 1:tm, :]
    else:
        prev = p_ref[...]
    xs = hb + (prev - hb) * mu_ref[...]
    W = RWKV_W
    r, k, v = xs[:, :W], xs[:, W:2 * W], xs[:, 2 * W:3 * W]
    w_lo = xs[:, 3 * W:3 * W + 64]
    a_lo = xs[:, 3 * W + 64:3 * W + 128]
    g_lo = xs[:, 3 * W + 128:]
    w_log = -_softplus(-(w0_ref[...] + _dot(jnp.tanh(w_lo), w2_ref[...], prec))) - 0.5
    lw = -jnp.exp(w_log)
    a = _sigmoid(a0_ref[...] + _dot(a_lo, a2_ref[...], prec))
    g = _dot(_sigmoid(g_lo), g2_ref[...], prec)
    ones = ones_ref[...]
    kk = k * kk_ref[...]
    nrm = jnp.sqrt(_exact_right_dot(kk * kk, ones))
    kk = kk / jnp.maximum(nrm, 1e-12)
    k2 = k * (1.0 + (a - 1.0) * ka_ref[...])
    beta = kk * a
    cl = _exact_left_dot(tri_ref[...], lw)
    tot = _exact_left_dot(blk_ref[...], lw)
    ginv = jnp.exp(-cl)
    gend = jnp.exp(tot - cl)
    kq_ref[...] = (kk * jnp.exp(cl - lw)).astype(kq_ref.dtype)
    rq_ref[...] = (r * jnp.exp(cl)).astype(rq_ref.dtype)
    kt_ref[...] = (k2 * ginv).astype(kt_ref.dtype)
    bt_ref[...] = (beta * ginv).astype(bt_ref.dtype)
    ke_ref[...] = (k2 * gend).astype(ke_ref.dtype)
    be_ref[...] = (beta * gend).astype(be_ref.dtype)
    v_ref[...] = v.astype(v_ref.dtype)
    gt_ref[...] = jnp.exp(_exact_left_dot(sel_ref[...], lw))
    g_ref[...] = g
    bon_ref[...] = _exact_right_dot(r * k2 * rk_ref[...], ones) * v


def _rwkv_prep(hb, prev, P, L, prec=False, seqs=None):
    M = hb.shape[0]
    tm = _tile(M if seqs is None else seqs[1], ROW_TILE)
    assert tm % L == 0
    W = RWKV_W
    idx = np.arange(tm)
    same = (idx[:, None] // L) == (idx[None, :] // L)
    tri = jnp.asarray(same & (idx[:, None] >= idx[None, :]), BF16)
    blk = jnp.asarray(same, BF16)
    sel = jnp.asarray((idx[None, :] // L) == np.arange(tm // L)[:, None], BF16)
    lane = np.arange(W)
    ones = jnp.asarray((lane[:, None] // RWKV_HD) == (lane[None, :] // RWKV_HD), BF16)
    if seqs is None:
        grid, nT = (1, M // tm), M // tm
        prev_spec = pl.BlockSpec((tm, RWKV_PROJ), lambda b, i: (i, 0))
        scratch = []
    else:
        grid, nT = (seqs[0], seqs[1] // tm), seqs[1] // tm
        prev_spec = pl.BlockSpec((1, 1, RWKV_PROJ), lambda b, i: (b, 0, 0))
        scratch = [pltpu.VMEM((1, RWKV_PROJ), F32)]
    row = lambda b, i: (b * nT + i, 0)
    vec = lambda a: a.reshape(1, -1)
    params = [vec(P['rwkv_mu']), vec(P['rwkv_w0']), P['rwkv_w2'], vec(P['rwkv_a0']),
              P['rwkv_a2'], P['rwkv_g2'], vec(P['rwkv_k_k']), vec(P['rwkv_k_a']),
              vec(P['rwkv_r_k']), ones, tri, blk, sel]
    out_dt = [F32 if prec else BF16] * 7 + [F32] * 2
    return pl.pallas_call(
        functools.partial(_rwkv_prep_body, prec=prec, shift_rows=seqs is not None),
        grid=grid,
        in_specs=[pl.BlockSpec((tm, RWKV_PROJ), row), prev_spec] + [_wspec(a) for a in params],
        out_specs=[pl.BlockSpec((tm, W), row) for _ in out_dt] + [pl.BlockSpec((tm // L, W), row)],
        out_shape=[jax.ShapeDtypeStruct((M, W), dt) for dt in out_dt] + [jax.ShapeDtypeStruct((M // L, W), F32)],
        scratch_shapes=scratch,
        compiler_params=_cparams(("parallel", "arbitrary")),
        name="rwkv_prep",
    )(hb, prev, *params)


def _rwkv_masks(L):
    ii = lax.broadcasted_iota(jnp.int32, (L, L), 0)
    jj = lax.broadcasted_iota(jnp.int32, (L, L), 1)
    return ii > jj, ii >= jj, (ii == jj).astype(F32)


def _rwkv_local(kq, rq, kt, bt, ke, be, v, masks, lmask_ref, nlev, prec):
    strict, incl, eye = masks
    L = kq.shape[1]
    x = jnp.concatenate([kq, rq], axis=1)
    sk = _dg(x, kt, _BNT, prec)
    sb = _dg(x, bt, _BNT, prec)
    n_m = jnp.where(strict, sk[:, :L], 0.0)
    a_k = jnp.where(incl, sk[:, L:], 0.0)
    m_m = jnp.where(strict, sb[:, :L], 0.0)
    a_b = jnp.where(incl, sb[:, L:], 0.0)
    t_m = eye - m_m * lmask_ref[0]
    for lv in range(1, nlev):
        c_m = m_m * lmask_ref[lv]
        t_m = t_m - _dg(_dg(t_m, c_m, _BNN, prec), t_m, _BNN, prec)
    kqp = _dg(t_m, kq, _BNN, prec)
    u0 = _dg(t_m, _dg(n_m, v, _BNN, prec), _BNN, prec)
    rqp = rq.astype(F32) - _dg(a_b, kqp, _BNN, prec)
    y0 = _dg(a_k, v, _BNN, prec) - _dg(a_b, u0, _BNN, prec)
    p_m = _dg(kqp, be, _BTN, prec)
    b_c = _dg(v, ke, _BTN, prec) - _dg(u0, be, _BTN, prec)
    return rqp, y0, p_m, b_c


def _rwkv_apply(s_old, rqp, y0, p_m, b_c, g_end, prec):
    y = _dg(rqp, s_old, _BNT, prec) + y0
    return y, s_old * g_end - _dg(s_old, p_m, _BNN, prec) + b_c


def _head_stack(tile, nblk):
    hd = RWKV_HD
    L = tile.shape[0] // nblk
    return jnp.concatenate([tile[:, hh * hd:(hh + 1) * hd].reshape(nblk, L, hd) for hh in range(2)], axis=0)


def _head_unstack(x, nblk):
    L, hd = x.shape[1], x.shape[2]
    return jnp.concatenate([x[hh * nblk:(hh + 1) * nblk].reshape(nblk * L, hd) for hh in range(2)], axis=1)


def _rwkv_local_body(kq_ref, rq_ref, kt_ref, bt_ref, ke_ref, be_ref, v_ref, lmask_ref,
                     rqp_ref, y0_ref, pm_ref, bc_ref, *, L, nchunk, nlev):
    ops = [_head_stack(r[...], nchunk) for r in (kq_ref, rq_ref, kt_ref, bt_ref, ke_ref, be_ref, v_ref)]
    res = _rwkv_local(*ops, _rwkv_masks(L), lmask_ref, nlev, False)
    for o_ref, a in zip((rqp_ref, y0_ref, pm_ref, bc_ref), res):
        o_ref[...] = _head_unstack(a, nchunk).astype(o_ref.dtype)


def _rwkv_apply_body(rqp_ref, y0_ref, pm_ref, bc_ref, gt_ref, g_ref, bon_ref, s0_ref, gng_ref, gnb_ref,
                     o_ref, sout_ref, s_scr, *, L, nchunk, B):
    i = pl.program_id(0)
    H, hd = RWKV_HEADS, RWKV_HD

    @pl.when(i == 0)
    def _():
        s_scr[...] = s0_ref[...].reshape(B * H, hd, hd)

    def heads(x):
        return jnp.stack([x[b][:, h * hd:(h + 1) * hd] for b in range(B) for h in range(H)], axis=0)

    def chunk(c, carry):
        rows = pl.ds(pl.multiple_of(c * L, L), L)
        ops = [heads(r[:, rows, :]) for r in (rqp_ref, y0_ref, pm_ref, bc_ref)]
        g_end = heads(gt_ref[:, pl.ds(c, 1), :])
        y, s_new = _rwkv_apply(s_scr[...], *ops, g_end, False)
        s_scr[...] = s_new
        yn = _lane_norm(y)
        for b in range(B):
            ynb = jnp.concatenate([yn[b * H + h] for h in range(H)], axis=1)
            o_ref[b, rows, :] = (ynb * gng_ref[...] + gnb_ref[...] + bon_ref[b, rows, :]) * g_ref[b, rows, :]
        return carry

    lax.fori_loop(0, nchunk, chunk, 0)

    @pl.when(i == pl.num_programs(0) - 1)
    def _():
        sout_ref[...] = s_scr[...].reshape(B, H, hd, hd)


def _rwkv_fused_body(kq_ref, rq_ref, kt_ref, bt_ref, ke_ref, be_ref, v_ref, g_ref, bon_ref, gt_ref,
                     s0_ref, gng_ref, gnb_ref, lmask_ref, o_ref, sout_ref, *, L, nseq, nlev, prec):
    hd = RWKV_HD
    ops = [_head_stack(r[...], nseq) for r in (kq_ref, rq_ref, kt_ref, bt_ref, ke_ref, be_ref, v_ref)]
    loc = _rwkv_local(*ops, _rwkv_masks(L), lmask_ref, nlev, prec)
    gt = gt_ref[...]
    g_end = jnp.concatenate([gt[:, :, hh * hd:(hh + 1) * hd] for hh in range(2)], axis=0)
    s_old = jnp.concatenate([s0_ref[:, hh] for hh in range(2)], axis=0)
    y, s_new = _rwkv_apply(s_old, *loc, g_end, prec)
    for hh in range(2):
        sout_ref[:, hh] = s_new[hh * nseq:(hh + 1) * nseq]
    yn = _head_unstack(_lane_norm(y), nseq)
    o_ref[...] = (yn * gng_ref[...] + gnb_ref[...] + bon_ref[...]) * g_ref[...]


def _rwkv_level_masks(L):
    nlev = int(math.log2(L))
    idx = np.arange(L)
    ii, jj = idx[:, None], idx[None, :]
    lmask = np.stack([((ii >> (lv + 1)) == (jj >> (lv + 1))) & ((ii & (1 << lv)) != 0) & ((jj & (1 << lv)) == 0)
                      for lv in range(nlev)]).astype(np.float32)
    return nlev, jnp.asarray(lmask)


def _rwkv_scan(pre, s0, gn_g, gn_b, B, T, L):
    W, H, hd = RWKV_W, RWKV_HEADS, RWKV_HD
    assert L == hd
    kq, rq, kt, bt, ke, be, v, g, bon, gt = pre
    nlev, lmask = _rwkv_level_masks(L)
    tl = _tile(T, RWKV_LOCAL_ROWS)
    nL = T // tl
    pw = 2 * hd
    row = lambda b, p, i: (b * nL + i, p)
    rqp, y0, p_m, b_c = pl.pallas_call(
        functools.partial(_rwkv_local_body, L=L, nchunk=tl // L, nlev=nlev),
        grid=(B, H // 2, nL),
        in_specs=[pl.BlockSpec((tl, pw), row) for _ in range(7)]
                 + [pl.BlockSpec((nlev, L, L), lambda b, p, i: (0, 0, 0))],
        out_specs=[pl.BlockSpec((tl, pw), row) for _ in range(4)],
        out_shape=[jax.ShapeDtypeStruct((B * T, W), dt) for dt in (BF16, F32, BF16, F32)],
        compiler_params=_cparams(("parallel", "parallel", "parallel")),
        name="rwkv_local",
    )(kq, rq, kt, bt, ke, be, v, lmask)
    tb = _tile(T, RWKV_ROWS)
    nT = T // tb
    nchunk = tb // L
    r3 = lambda a: a.reshape(B, T, W)
    blk = lambda i: (0, i, 0)
    fix2 = lambda i: (0, 0)
    fix4 = lambda i: (0, 0, 0, 0)
    out, s_out = pl.pallas_call(
        functools.partial(_rwkv_apply_body, L=L, nchunk=nchunk, B=B),
        grid=(nT,),
        in_specs=[pl.BlockSpec((B, tb, W), blk) for _ in range(4)]
                 + [pl.BlockSpec((B, nchunk, W), blk), pl.BlockSpec((B, tb, W), blk), pl.BlockSpec((B, tb, W), blk),
                    pl.BlockSpec((B, H, hd, hd), fix4), pl.BlockSpec((1, W), fix2), pl.BlockSpec((1, W), fix2)],
        out_specs=[pl.BlockSpec((B, tb, W), blk), pl.BlockSpec((B, H, hd, hd), fix4)],
        out_shape=[jax.ShapeDtypeStruct((B, T, W), F32), jax.ShapeDtypeStruct((B, H, hd, hd), F32)],
        scratch_shapes=[pltpu.VMEM((B * H, hd, hd), F32)],
        compiler_params=_cparams(("arbitrary",)),
        name="rwkv_apply",
    )(r3(rqp), r3(y0), r3(p_m), r3(b_c), gt.reshape(B, T // L, W), r3(g), r3(bon), s0,
      gn_g.reshape(1, W), gn_b.reshape(1, W))
    return out.reshape(B * T, W), s_out


def _rwkv_single_chunk(pre, s0, gn_g, gn_b, B, T, prec):
    W, H, hd = RWKV_W, RWKV_HEADS, RWKV_HD
    kq, rq, kt, bt, ke, be, v, g, bon, gt = pre
    nlev, lmask = _rwkv_level_masks(T)
    nseq = RWKV_SEQS if B % RWKV_SEQS == 0 else B
    pw = 2 * hd
    row = lambda i, p: (i, p)
    st = lambda i, p: (i, p, 0, 0)
    vec = lambda i, p: (0, p)
    out, s_out = pl.pallas_call(
        functools.partial(_rwkv_fused_body, L=T, nseq=nseq, nlev=nlev, prec=prec),
        grid=(B // nseq, H // 2),
        in_specs=[pl.BlockSpec((nseq * T, pw), row) for _ in range(9)]
                 + [pl.BlockSpec((nseq, 1, pw), lambda i, p: (i, 0, p)), pl.BlockSpec((nseq, 2, hd, hd), st),
                    pl.BlockSpec((1, pw), vec), pl.BlockSpec((1, pw), vec),
                    pl.BlockSpec((nlev, T, T), lambda i, p: (0, 0, 0))],
        out_specs=[pl.BlockSpec((nseq * T, pw), row), pl.BlockSpec((nseq, 2, hd, hd), st)],
        out_shape=[jax.ShapeDtypeStruct((B * T, W), F32), jax.ShapeDtypeStruct((B, H, hd, hd), F32)],
        compiler_params=_cparams(("parallel", "parallel")),
        name="rwkv_single_chunk",
    )(kq, rq, kt, bt, ke, be, v, g, bon, gt.reshape(B, 1, W), s0, gn_g.reshape(1, W), gn_b.reshape(1, W), lmask)
    return out, s_out


def _mlstm_body(qk_ref, v_ref, o_ref, gt_ref, cprev_ref, c0_ref, n0_ref, m0_ref, cw_ref, cb_ref, gb_ref,
                gn_ref, tri_ref, out_ref, cout_ref, nout_ref, mout_ref, convout_ref,
                xpad, c_scr, m_scr, *, L):
    i = pl.program_id(1)
    H, DK, DV = MLSTM_HEADS, MLSTM_DK, MLSTM_DV
    K = MLSTM_CONV - 1
    base = 8 - K

    @pl.when(i == 0)
    def _():
        xpad[base:8, :] = cprev_ref[0]
        for h in range(H):
            c_scr[h, :, 0:DV] = c0_ref[0, h]
            c_scr[h, :, DV:2 * DV] = jnp.broadcast_to(n0_ref[0, h], (DK, DV))
        m_scr[...] = m0_ref[0]

    xpad[8:8 + L, :] = qk_ref[...]
    conv = cb_ref[...] + xpad[pl.ds(base, L), :] * cw_ref[0:1, :]
    for j in range(1, MLSTM_CONV):
        conv = conv + xpad[pl.ds(base + j, L), :] * cw_ref[j:j + 1, :]
    tail = xpad[pl.ds(8 + L - K, K), :]
    xpad[base:8, :] = tail
    qk = _silu(conv)
    q = qk[:, :H * DK] * (DK ** -0.5)
    k = qk[:, H * DK:]
    v = v_ref[...]
    z = gt_ref[...] + gb_ref[...]
    lf = _log_sigmoid(z)
    bcum = _exact_left_dot(tri_ref[...], lf)
    z_t = z.T
    b_t = bcum.T
    ii = lax.broadcasted_iota(jnp.int32, (L, L), 0)
    jj = lax.broadcasted_iota(jnp.int32, (L, L), 1)
    causal = ii >= jj
    ones = jnp.ones((L, DV), F32)
    m_all = m_scr[...]
    lane = lax.broadcasted_iota(jnp.int32, (1, 128), 1)
    stack = lambda f: jnp.stack([f(h) for h in range(H)], axis=0)
    qh = stack(lambda h: q[:, h * DK:(h + 1) * DK])
    kh = stack(lambda h: k[:, h * DK:(h + 1) * DK])
    vh = stack(lambda h: jnp.concatenate([v[:, h * DV:(h + 1) * DV], ones], axis=1))
    b_col = stack(lambda h: bcum[:, H + h:H + h + 1])
    ig_col = stack(lambda h: z[:, h:h + 1])
    row_term = stack(lambda h: z_t[h:h + 1, :] - b_t[H + h:H + h + 1, :])
    m0 = stack(lambda h: m_all[:, h:h + 1])
    log_d = jnp.where(causal, b_col + row_term, NEG_INF)
    m_inter = b_col + m0
    m_t = jnp.maximum(m_inter, jnp.max(log_d, axis=-1, keepdims=True))
    w_d = jnp.exp(log_d - m_t)
    w_i = jnp.exp(m_inter - m_t)
    c_old = c_scr[...]
    s = _dg(qh, kh, _BNT, False) * w_d
    num = _dg(s, vh, _BNN, False) + _dg(qh, c_old, _BNN, False) * w_i
    den = num[:, :, DV:DV + 1]
    denom = jnp.maximum(jnp.abs(den), jnp.exp(-m_t))
    hn = _lane_norm(num[:, :, :DV] / denom)
    m_new = m_t[:, L - 1:L, :]
    b_last = b_col[:, L - 1:L, :]
    w_s = jnp.exp(b_last + m0 - m_new)
    w_k = jnp.exp(b_last - b_col + ig_col - m_new)
    c_scr[...] = w_s * c_old + _dg(kh * w_k, vh, _BTN, False)
    m_new_all = m_all
    for h in range(H):
        m_new_all = jnp.where(lane == h, m_new[h], m_new_all)
    m_scr[...] = m_new_all
    out_ref[...] = jnp.concatenate([hn[h] for h in range(H)], axis=1) * gn_ref[...] * _sigmoid(o_ref[...])

    @pl.when(i == pl.num_programs(1) - 1)
    def _():
        for h in range(H):
            cout_ref[0, h] = c_scr[h, :, 0:DV]
            nout_ref[0, h] = c_scr[h, :, DV:DV + 1]
        mout_ref[0] = m_scr[...]
        convout_ref[0] = tail


def _mlstm(hqk, hv, ho, hgate, conv_prev, c0, n0, m0, P, gate_bias, B, T, L):
    H, DK, DV = MLSTM_HEADS, MLSTM_DK, MLSTM_DV
    nT = T // L
    K = MLSTM_CONV - 1
    W = H * DV
    idx = np.arange(L)
    tri = jnp.asarray(idx[:, None] >= idx[None, :], BF16)
    m0p = jnp.zeros((B, 1, 128), F32).at[:, 0, :H].set(m0)
    row = lambda b, i: (b * nT + i, 0)
    fix = lambda b, i: (0, 0)
    perb3 = lambda b, i: (b, 0, 0)
    perb4 = lambda b, i: (b, 0, 0, 0)
    out, c_out, n_out, m_out, conv_out = pl.pallas_call(
        functools.partial(_mlstm_body, L=L),
        grid=(B, nT),
        in_specs=[pl.BlockSpec((L, W), row), pl.BlockSpec((L, W), row), pl.BlockSpec((L, W), row),
                  pl.BlockSpec((L, 128), row),
                  pl.BlockSpec((1, K, W), perb3),
                  pl.BlockSpec((1, H, DK, DV), perb4), pl.BlockSpec((1, H, DK, 1), perb4),
                  pl.BlockSpec((1, 1, 128), perb3),
                  pl.BlockSpec((MLSTM_CONV, W), fix), pl.BlockSpec((1, W), fix), pl.BlockSpec((1, 128), fix),
                  pl.BlockSpec((1, W), fix), pl.BlockSpec((L, L), fix)],
        out_specs=[pl.BlockSpec((L, W), row),
                   pl.BlockSpec((1, H, DK, DV), perb4), pl.BlockSpec((1, H, DK, 1), perb4),
                   pl.BlockSpec((1, 1, 128), perb3), pl.BlockSpec((1, K, W), perb3)],
        out_shape=[jax.ShapeDtypeStruct((B * T, W), F32),
                   jax.ShapeDtypeStruct((B, H, DK, DV), F32), jax.ShapeDtypeStruct((B, H, DK, 1), F32),
                   jax.ShapeDtypeStruct((B, 1, 128), F32), jax.ShapeDtypeStruct((B, K, W), F32)],
        scratch_shapes=[pltpu.VMEM((L + 8, W), F32), pltpu.VMEM((H, DK, 2 * DV), F32), pltpu.VMEM((1, 128), F32)],
        compiler_params=_cparams(("parallel", "arbitrary")),
        name="mlstm",
    )(hqk, hv, ho, hgate, conv_prev, c0, n0.reshape(B, H, DK, 1), m0p,
      P['mlstm_conv_w'], P['mlstm_conv_b'].reshape(1, W), gate_bias, P['mlstm_gn_g'].reshape(1, W), tri)
    return out, c_out, n_out.reshape(B, H, DK), m_out[:, 0, :H], conv_out


def _fox_prep_body(q_ref, k_ref, gt_ref, qg_ref, kg_ref, gb_ref, qn_ref, kn_ref, lf_ref):
    def rms(x, g):
        outs = []
        for h in range(FOX_HEADS):
            xh = x[:, h * FOX_HD:(h + 1) * FOX_HD]
            outs.append(xh * lax.rsqrt(jnp.mean(xh * xh, axis=-1, keepdims=True) + EPS) * g)
        return jnp.concatenate(outs, axis=1)

    qn_ref[...] = (rms(q_ref[...], qg_ref[...]) * (FOX_HD ** -0.5)).astype(BF16)
    kn_ref[...] = rms(k_ref[...], kg_ref[...])
    lf_ref[...] = _log_sigmoid(gt_ref[...] + gb_ref[...])


def _fox_prep(hq, hk, hgate, P, gate_bias):
    M, W = hq.shape
    tm = _tile(M, ROW_TILE)
    row = lambda i: (i, 0)
    fix = lambda i: (0, 0)
    return pl.pallas_call(
        _fox_prep_body,
        grid=(M // tm,),
        in_specs=[pl.BlockSpec((tm, W), row), pl.BlockSpec((tm, W), row), pl.BlockSpec((tm, 128), row),
                  pl.BlockSpec((1, FOX_HD), fix), pl.BlockSpec((1, FOX_HD), fix), pl.BlockSpec((1, 128), fix)],
        out_specs=[pl.BlockSpec((tm, W), row), pl.BlockSpec((tm, W), row), pl.BlockSpec((tm, 128), row)],
        out_shape=[jax.ShapeDtypeStruct((M, W), BF16), jax.ShapeDtypeStruct((M, W), F32),
                   jax.ShapeDtypeStruct((M, 128), F32)],
        compiler_params=_cparams(("parallel",)),
        name="fox_prep",
    )(hq, hk, hgate, P['fox_q_g'].reshape(1, FOX_HD), P['fox_k_g'].reshape(1, FOX_HD), gate_bias)


def _fox_prep_prompt_body(q_ref, k_ref, v_ref, gt_ref, qg_ref, kg_ref, gb_ref, tri_ref,
                          qa_ref, kn_ref, vc_ref, kt_ref, va_ref, lf_ref, carry):
    i = pl.program_id(1)
    tm = q_ref.shape[0]
    HD = FOX_HD

    @pl.when(i == 0)
    def _():
        carry[...] = jnp.zeros_like(carry)

    lf = _log_sigmoid(gt_ref[...] + gb_ref[...])
    lf_ref[...] = lf
    c = _exact_left_dot(tri_ref[...], lf) + carry[...]
    carry[...] = c[tm - 1:tm, :]
    c2t = (c * (-LOG2E)).T
    lane = lax.broadcasted_iota(jnp.int32, (tm, HD), 1)
    q_ones = jnp.where(lane < 3, 1.0, 0.0).astype(BF16)
    v_ones = jnp.where(lane < 1, 1.0, 0.0).astype(BF16)
    row16 = lax.broadcasted_iota(jnp.int32, (16, tm), 0)
    q, k, v = q_ref[...], k_ref[...], v_ref[...]
    for h in range(FOX_HEADS):
        hs = slice(h * HD, (h + 1) * HD)
        qh, kh = q[:, hs], k[:, hs]
        qh = qh * lax.rsqrt(jnp.mean(qh * qh, axis=-1, keepdims=True) + EPS) * qg_ref[...]
        kh = kh * lax.rsqrt(jnp.mean(kh * kh, axis=-1, keepdims=True) + EPS) * kg_ref[...]
        qa_ref[:, 2 * h * HD:(2 * h + 1) * HD] = (qh * (HD ** -0.5 * LOG2E)).astype(BF16)
        qa_ref[:, (2 * h + 1) * HD:(2 * h + 2) * HD] = q_ones
        kn_ref[pl.ds(h, tm, stride=FOX_HEADS), :] = kh
        vc_ref[pl.ds(h, tm, stride=FOX_HEADS), :] = v[:, hs]
        kt_ref[0, h, 0, 0:HD, :] = kh.T.astype(BF16)
        bias = c2t[2 * MLSTM_HEADS + h:2 * MLSTM_HEADS + h + 1, :]
        hi = bias.astype(BF16).astype(F32)
        mid = (bias - hi).astype(BF16).astype(F32)
        lo = bias - hi - mid
        blk = jnp.where(row16 == 0, hi, jnp.where(row16 == 1, mid, jnp.where(row16 == 2, lo, 0.0)))
        kt_ref[0, h, 0, HD:HD + 16, :] = blk.astype(BF16)
        kt_ref[0, h, 0, HD + 16:2 * HD, :] = jnp.zeros((HD - 16, tm), BF16)
        va_ref[:, 2 * h * HD:(2 * h + 1) * HD] = v[:, hs].astype(BF16)
        va_ref[:, (2 * h + 1) * HD:(2 * h + 2) * HD] = v_ones


def _fox_prep_prompt(hq, hk, hv, hgate, P, gate_bias, B, T):
    M, W = hq.shape
    H, HD = FOX_HEADS, FOX_HD
    tm = _tile(T, FOX_TQ)
    nT = T // tm
    idx = np.arange(tm)
    tri = jnp.asarray(idx[:, None] >= idx[None, :], BF16)
    row = lambda b, i: (b * nT + i, 0)
    fix = lambda b, i: (0, 0)
    return pl.pallas_call(
        _fox_prep_prompt_body,
        grid=(B, nT),
        in_specs=[pl.BlockSpec((tm, W), row), pl.BlockSpec((tm, W), row), pl.BlockSpec((tm, W), row),
                  pl.BlockSpec((tm, 128), row),
                  pl.BlockSpec((1, HD), fix), pl.BlockSpec((1, HD), fix), pl.BlockSpec((1, 128), fix),
                  pl.BlockSpec((tm, tm), fix)],
        out_specs=[pl.BlockSpec((tm, 2 * W), row), pl.BlockSpec((tm * H, HD), row), pl.BlockSpec((tm * H, HD), row),
                   pl.BlockSpec((1, H, 1, 2 * HD, tm), lambda b, i: (b, 0, i, 0, 0)),
                   pl.BlockSpec((tm, 2 * W), row), pl.BlockSpec((tm, 128), row)],
        out_shape=[jax.ShapeDtypeStruct((M, 2 * W), BF16), jax.ShapeDtypeStruct((M * H, HD), F32),
                   jax.ShapeDtypeStruct((M * H, HD), F32),
                   jax.ShapeDtypeStruct((B, H, nT, 2 * HD, tm), BF16),
                   jax.ShapeDtypeStruct((M, 2 * W), BF16), jax.ShapeDtypeStruct((M, 128), F32)],
        scratch_shapes=[pltpu.VMEM((1, 128), F32)],
        compiler_params=_cparams(("parallel", "arbitrary")),
        name="fox_prep_prompt",
    )(hq, hk, hv, hgate, P['fox_q_g'].reshape(1, HD), P['fox_k_g'].reshape(1, HD), gate_bias, tri)


def _fox_prompt_body(q_ref, kt_ref, va_ref, o_ref, sa_scr, sb_scr, m_scr, acc_scr, *, tq):
    qi = pl.program_id(2)
    m_scr[...] = jnp.full_like(m_scr, NEG_INF)
    acc_scr[...] = jnp.zeros_like(acc_scr)

    def scores(s_ref, kj):
        s_ref[...] = jnp.dot(q_ref[0], kt_ref[0, 0, kj], preferred_element_type=F32)

    def update(s_ref, kj, masked):
        vb = va_ref[0, pl.ds(pl.multiple_of(kj * tq, tq), tq), :]
        s = s_ref[...]
        if masked:
            ii = lax.broadcasted_iota(jnp.int32, (tq, tq), 0)
            jj = lax.broadcasted_iota(jnp.int32, (tq, tq), 1)
            s = jnp.where(jj <= ii, s, NEG_INF)
        m_old = m_scr[...]
        m_new = jnp.maximum(m_old, jnp.max(s, axis=-1, keepdims=True))
        p = jnp.exp2(s - m_new)
        acc_scr[...] = jnp.exp2(m_old - m_new) * acc_scr[...] + jnp.dot(p.astype(BF16), vb, preferred_element_type=F32)
        m_scr[...] = m_new

    scores(sa_scr, 0)

    def body(j, carry):
        scores(sb_scr, 2 * j + 1)
        update(sa_scr, 2 * j, False)
        scores(sa_scr, 2 * j + 2)
        update(sb_scr, 2 * j + 1, False)
        return carry

    lax.fori_loop(0, lax.shift_right_logical(qi, 1), body, 0)

    @pl.when((qi & 1) == 0)
    def _():
        update(sa_scr, qi, True)

    @pl.when((qi & 1) == 1)
    def _():
        scores(sb_scr, qi)
        update(sa_scr, qi - 1, False)
        update(sb_scr, qi, True)

    acc = acc_scr[...]
    o_ref[0] = acc[:, :FOX_HD] / acc[:, FOX_HD:FOX_HD + 1]


def _fox_prompt(qa, kt, va, B, T):
    H, HD = FOX_HEADS, FOX_HD
    W = H * HD
    tq = kt.shape[-1]
    nQ = T // tq
    out = pl.pallas_call(
        functools.partial(_fox_prompt_body, tq=tq),
        grid=(B, H, nQ),
        in_specs=[pl.BlockSpec((1, tq, 2 * HD), lambda b, h, i: (b, i, h)),
                  pl.BlockSpec((1, 1, nQ, 2 * HD, tq), lambda b, h, i: (b, h, 0, 0, 0),
                               pipeline_mode=pl.Buffered(1)),
                  pl.BlockSpec((1, T, 2 * HD), lambda b, h, i: (b, 0, h), pipeline_mode=pl.Buffered(1))],
        out_specs=pl.BlockSpec((1, tq, HD), lambda b, h, i: (b, i, h)),
        out_shape=jax.ShapeDtypeStruct((B, T, W), F32),
        scratch_shapes=[pltpu.VMEM((tq, tq), F32), pltpu.VMEM((tq, tq), F32),
                        pltpu.VMEM((tq, 1), F32), pltpu.VMEM((tq, 2 * HD), F32)],
        compiler_params=_cparams(("parallel", "parallel", "arbitrary")),
        name="fox_prompt",
    )(qa.reshape(B, T, 2 * W), kt, va.reshape(B, T, 2 * W))
    return out.reshape(B * T, W)


def _fox_sample_body(q_ref, kc_ref, vc_ref, kn_ref, vn_ref, cq_ref, ckc_ref, ckn_ref, o_ref, *, T, past):
    H, HD = FOX_HEADS, FOX_HD
    q = q_ref[0]
    kn, vn = kn_ref[0], vn_ref[0]
    ii = lax.broadcasted_iota(jnp.int32, (T, T), 0)
    jj = lax.broadcasted_iota(jnp.int32, (T, T), 1)
    outs = []
    for h in range(H):
        cs = slice(h * HD, (h + 1) * HD)
        kc = kc_ref[0, pl.ds(h, past, stride=H), :]
        vc = vc_ref[0, pl.ds(h, past, stride=H), :]
        cq = cq_ref[0, h]
        s1 = _dot_nt(q[:, cs], kc) + (cq - ckc_ref[0, h])
        s2 = _dot_nt(q[:, cs], kn[:, cs]) + (cq - ckn_ref[0, h])
        s2 = jnp.where(jj <= ii, s2, NEG_INF)
        m = jnp.maximum(jnp.max(s1, axis=-1, keepdims=True), jnp.max(s2, axis=-1, keepdims=True))
        p1 = jnp.exp(s1 - m)
        p2 = jnp.exp(s2 - m)
        den = jnp.sum(p1, axis=-1, keepdims=True) + jnp.sum(p2, axis=-1, keepdims=True)
        outs.append((_dot(p1, vc) + _dot(p2, vn[:, cs])) / den)
    o_ref[0] = jnp.concatenate(outs, axis=1)


def _fox_sample(qn, kn, vn, k_cache, v_cache, c_all, B, T):
    H, HD = FOX_HEADS, FOX_HD
    W = H * HD
    past = k_cache.shape[1]
    ct = jnp.transpose(c_all, (0, 2, 1))
    cq = ct[:, :, past:].reshape(B, H, T, 1)
    ckc = ct[:, :, :past].reshape(B, H, 1, past)
    ckn = ct[:, :, past:].reshape(B, H, 1, T)
    b3 = lambda b: (b, 0, 0)
    b4 = lambda b: (b, 0, 0, 0)
    out = pl.pallas_call(
        functools.partial(_fox_sample_body, T=T, past=past),
        grid=(B,),
        in_specs=[pl.BlockSpec((1, T, W), b3), pl.BlockSpec((1, past * H, HD), b3), pl.BlockSpec((1, past * H, HD), b3),
                  pl.BlockSpec((1, T, W), b3), pl.BlockSpec((1, T, W), b3),
                  pl.BlockSpec((1, H, T, 1), b4), pl.BlockSpec((1, H, 1, past), b4), pl.BlockSpec((1, H, 1, T), b4)],
        out_specs=pl.BlockSpec((1, T, W), b3),
        out_shape=jax.ShapeDtypeStruct((B, T, W), F32),
        compiler_params=_cparams(("parallel",)),
        name="fox_sample",
    )(qn.reshape(B, T, W), k_cache.reshape(B, past * H, HD), v_cache.reshape(B, past * H, HD),
      kn.reshape(B, T, W), vn.reshape(B, T, W), cq, ckc, ckn)
    return out.reshape(B * T, W)


def _mem_attn_body(x_ref, mk_ref, mv_ref, wq_ref, wo_ref, g_ref, b_ref, o_ref, *, prec):
    x = x_ref[...]
    q = _wdot(x, wq_ref, prec)
    outs = []
    for h in range(MEM_HEADS):
        cs = slice(h * MEM_HD, (h + 1) * MEM_HD)
        mkh, mvh = mk_ref[0, :, cs], mv_ref[0, :, cs]
        s = _dot_nt(q[:, cs], mkh, prec) * (MEM_HD ** -0.5)
        m = jnp.max(s, axis=-1, keepdims=True)
        p = jnp.exp(s - m)
        outs.append(_dot(p, mvh, prec) / jnp.sum(p, axis=-1, keepdims=True))
    o = jnp.concatenate(outs, axis=1)
    att = _wdot(o, wo_ref, prec)
    y = _layer_norm(ALPHA * x + att, g_ref[...], b_ref[...])
    o_ref[...] = y


def _mem_attn_ln(x, mk, mv, l, wq, wo, g, bias, B, T, prec=False):
    D = D_MODEL
    Mm = mk.shape[1]
    tm = _tile(T, ROW_TILE)
    nT = T // tm
    row = lambda b, i: (b * nT + i, 0)
    fix = lambda b, i: (0, 0)
    mem_spec = pl.BlockSpec((1,) + mk.shape[1:], lambda b, i: (l * B + b,) + (0,) * (mk.ndim - 1))
    return pl.pallas_call(
        functools.partial(_mem_attn_body, prec=prec),
        grid=(B, nT),
        in_specs=[pl.BlockSpec((tm, D), row), mem_spec, mem_spec,
                  _wspec(wq), _wspec(wo),
                  pl.BlockSpec((1, D), fix), pl.BlockSpec((1, D), fix)],
        out_specs=pl.BlockSpec((tm, D), row),
        out_shape=jax.ShapeDtypeStruct((B * T, D), F32),
        compiler_params=_cparams(("parallel", "parallel")),
        name="mem_attn_ln",
    )(x, mk, mv, wq, wo, g.reshape(1, D), bias.reshape(1, D))


def _router_body(x_ref, w_ref, b_ref, tri_ref, o_ref, cnt_ref, cnt_scr):
    x = x_ref[...]
    w = w_ref[...]
    x1, x2, x3 = _split3(x)
    w1, w2, w3 = _split3(w)
    nt = lambda a, c: lax.dot_general(a, c, _NT, preferred_element_type=F32)
    logits = (nt(w1, x1) + (nt(w1, x2) + nt(w2, x1)) + (nt(w1, x3) + nt(w2, x2) + nt(w3, x1))) + b_ref[...]
    m = jnp.max(logits, axis=0, keepdims=True)
    e = jnp.exp(logits - m)
    p = e / jnp.sum(e, axis=0, keepdims=True)
    rows = [p[j:j + 1, :] for j in range(N_EXPERTS)]
    best = None
    sel = None
    for g in range(N_GROUPS):
        a, b, c, d = rows[4 * g:4 * g + 4]
        top2 = jnp.maximum(jnp.maximum(jnp.maximum(a + b, a + c), jnp.maximum(a + d, b + c)),
                           jnp.maximum(b + d, c + d))
        if g == 0:
            best, sel = top2, jnp.zeros_like(top2, dtype=jnp.int32)
        else:
            upd = top2 > best
            sel = jnp.where(upd, g, sel)
            best = jnp.maximum(best, top2)
    pin = []
    for kk in range(EXPERTS_PER_GROUP):
        v = rows[kk]
        for g in range(1, N_GROUPS):
            v = jnp.where(sel == g, rows[4 * g + kk], v)
        pin.append(v)
    v1, i1 = pin[0], jnp.zeros_like(sel)
    for kk in range(1, EXPERTS_PER_GROUP):
        upd = pin[kk] > v1
        i1 = jnp.where(upd, kk, i1)
        v1 = jnp.maximum(v1, pin[kk])
    v2, i2 = None, None
    for kk in range(EXPERTS_PER_GROUP):
        cand = jnp.where(i1 == kk, -1.0, pin[kk])
        if v2 is None:
            v2, i2 = cand, jnp.zeros_like(sel)
        else:
            upd = cand > v2
            i2 = jnp.where(upd, kk, i2)
            v2 = jnp.maximum(v2, cand)
    tot = v1 + v2
    e1 = sel * EXPERTS_PER_GROUP + i1
    e2 = sel * EXPERTS_PER_GROUP + i2
    @pl.when(pl.program_id(0) == 0)
    def _():
        cnt_scr[...] = jnp.zeros_like(cnt_scr)

    eidx = lax.broadcasted_iota(jnp.int32, logits.shape, 0)
    oh1 = jnp.where(eidx == e1, 1.0, 0.0)
    oh2 = jnp.where(eidx == e2, 1.0, 0.0)
    oh = oh1 + oh2
    base = cnt_scr[...] + jnp.dot(oh.astype(BF16), tri_ref[...], preferred_element_type=F32)
    r1 = jnp.sum(oh1 * base, axis=0, keepdims=True)
    r2 = jnp.sum(oh2 * base, axis=0, keepdims=True)
    cnt = cnt_scr[...] + jnp.sum(oh, axis=1, keepdims=True)
    cnt_scr[...] = cnt
    cnt_ref[...] = jnp.broadcast_to(cnt, cnt_ref.shape)
    zero = jnp.zeros_like(v1)
    o_ref[...] = jnp.concatenate([e1.astype(F32), e2.astype(F32), v1 / tot, v2 / tot, r1, r2, zero, zero], axis=0)


def _router(x, w_router, b_router):
    M, D = x.shape
    tm = _tile(M, ROW_TILE)
    idx = np.arange(tm)
    tri = jnp.asarray(idx[:, None] < idx[None, :], BF16)
    r, cnt = pl.pallas_call(
        _router_body,
        grid=(M // tm,),
        in_specs=[pl.BlockSpec((tm, D), lambda i: (i, 0)),
                  pl.BlockSpec((N_EXPERTS, D), lambda i: (0, 0)),
                  pl.BlockSpec((N_EXPERTS, 1), lambda i: (0, 0)),
                  pl.BlockSpec((tm, tm), lambda i: (0, 0))],
        out_specs=[pl.BlockSpec((8, tm), lambda i: (0, i)), pl.BlockSpec((N_EXPERTS, 128), lambda i: (0, 0))],
        out_shape=[jax.ShapeDtypeStruct((8, M), F32), jax.ShapeDtypeStruct((N_EXPERTS, 128), F32)],
        scratch_shapes=[pltpu.VMEM((N_EXPERTS, 1), F32)],
        compiler_params=_cparams(("arbitrary",)),
        name="router",
    )(x, w_router.T, b_router.reshape(N_EXPERTS, 1), tri)
    return r, cnt[:, 0].astype(jnp.int32)


def _expert_body(be_ref, x_ref, wg_ref, wu_ref, wd_ref, o_ref, wg_s, wu_s, wd_s):
    i = pl.program_id(0)
    prev = be_ref[jnp.maximum(i - 1, 0)]

    @pl.when((i == 0) | (be_ref[i] != prev))
    def _():
        wg_s[...] = wg_ref[0, 0].astype(BF16)
        wu_s[...] = wu_ref[0, 0].astype(BF16)
        wd_s[...] = wd_ref[0, 0].astype(BF16)

    x = x_ref[...].astype(BF16)
    hg = jnp.dot(x, wg_s[...], preferred_element_type=F32)
    hu = jnp.dot(x, wu_s[...], preferred_element_type=F32)
    hb = (_silu(hg) * hu).astype(BF16)
    o_ref[...] = jnp.dot(hb, wd_s[...], preferred_element_type=F32)


def _experts(xg, blk_exp, wg, wu, wd, l):
    n_blocks = blk_exp.shape[0]
    D, DE = D_MODEL, D_EXPERT
    grid_spec = pltpu.PrefetchScalarGridSpec(
        num_scalar_prefetch=1,
        grid=(n_blocks,),
        in_specs=[pl.BlockSpec((MOE_BLOCK, D), lambda i, be: (i, 0)),
                  pl.BlockSpec((1, 1, D, DE), lambda i, be: (l, be[i], 0, 0)),
                  pl.BlockSpec((1, 1, D, DE), lambda i, be: (l, be[i], 0, 0)),
                  pl.BlockSpec((1, 1, DE, D), lambda i, be: (l, be[i], 0, 0))],
        out_specs=pl.BlockSpec((MOE_BLOCK, D), lambda i, be: (i, 0)),
        scratch_shapes=[pltpu.VMEM((D, DE), BF16), pltpu.VMEM((D, DE), BF16), pltpu.VMEM((DE, D), BF16)],
    )
    return pl.pallas_call(
        _expert_body,
        grid_spec=grid_spec,
        out_shape=jax.ShapeDtypeStruct((n_blocks * MOE_BLOCK, D), F32),
        compiler_params=_cparams(("arbitrary",)),
        name="experts",
    )(blk_exp, xg, wg, wu, wd)


def _combine_ln_body(x_ref, y0_ref, y1_ref, gt_ref, g_ref, b_ref, o_ref):
    gt = gt_ref[...]
    ffn = y0_ref[...] * gt[:, 0:1] + y1_ref[...] * gt[:, 1:2]
    o_ref[...] = _layer_norm(ALPHA * x_ref[...] + ffn, g_ref[...], b_ref[...])


def _combine_ln(x, y0, y1, gates, g, bias):
    M, D = x.shape
    tm = _tile(M, ROW_TILE)
    row = lambda i: (i, 0)
    fix = lambda i: (0, 0)
    return pl.pallas_call(
        _combine_ln_body,
        grid=(M // tm,),
        in_specs=[pl.BlockSpec((tm, D), row), pl.BlockSpec((tm, D), row), pl.BlockSpec((tm, D), row),
                  pl.BlockSpec((tm, 2), row), pl.BlockSpec((1, D), fix), pl.BlockSpec((1, D), fix)],
        out_specs=pl.BlockSpec((tm, D), row),
        out_shape=jax.ShapeDtypeStruct((M, D), F32),
        compiler_params=_cparams(("parallel",)),
        name="combine_ln",
    )(x, y0, y1, gates, g.reshape(1, D), bias.reshape(1, D))


def _moe_ln(x, P, l):
    M = x.shape[0]
    r, counts = _router(x, P['w_router'], P['b_router'])
    e = r[0:2].astype(jnp.int32).T.reshape(-1)
    gates = r[2:4].T
    rank = r[4:6].astype(jnp.int32).T.reshape(-1)
    padded = (counts + MOE_BLOCK - 1) // MOE_BLOCK * MOE_BLOCK
    p_ends = jnp.cumsum(padded)
    p_starts = p_ends - padded
    dest = p_starts[e] + rank
    n_blocks = -(-2 * M // MOE_BLOCK) + N_EXPERTS
    tok = jnp.arange(2 * M, dtype=jnp.int32) // 2
    slot_tok = (jnp.arange(n_blocks * MOE_BLOCK, dtype=jnp.int32) % M).at[dest].set(tok, unique_indices=True)
    blk_start = jnp.arange(n_blocks, dtype=jnp.int32) * MOE_BLOCK
    blk_exp = jnp.minimum(jnp.sum((p_ends[None, :] <= blk_start[:, None]).astype(jnp.int32), axis=1), N_EXPERTS - 1)
    xg = x[slot_tok]
    yb = _experts(xg, blk_exp, P['w_exp_gate'], P['w_exp_up'], P['w_exp_down'], l)
    d2 = dest.reshape(M, 2)
    return _combine_ln(x, yb[d2[:, 0]], yb[d2[:, 1]], gates, P['ln_g'][l, 2], P['ln_b'][l, 2])


def _trunk(x, pos0, mem_k, mem_v, states, P, Wc, Wf, is_prompt, prec0):
    B, T, D = x.shape
    ret_S, rwkv_S, shift, mC, mn, mm, conv, fk, fv, flf = states
    xf = x.reshape(B * T, D)
    W0 = Wf if prec0 else Wc
    hq, hk, hv, hg, hb = _project(xf, W0['w_in0'], (256, 256, 512, 512, RWKV_PROJ), prec0)
    L_ret = _tile(T, RET_CHUNK)
    out_a, ret_S = _retention(hq, hk, hv, hg, ret_S, P['ret_gn_g'], B, T, pos0, L_ret, prec0)
    hb3 = hb.reshape(B, T, RWKV_PROJ)
    new_shift = hb3[:, -1:]
    L_rwkv = min(RWKV_CHUNK, T)
    if T > L_rwkv:
        assert not prec0
        pre = _rwkv_prep(hb, shift, P, L_rwkv, False, seqs=(B, T))
        out_b, rwkv_S = _rwkv_scan(pre, rwkv_S, P['rwkv_gn_g'], P['rwkv_gn_b'], B, T, L_rwkv)
    else:
        prev = jnp.concatenate([shift, hb3[:, :-1]], axis=1).reshape(B * T, RWKV_PROJ)
        pre = _rwkv_prep(hb, prev, P, T, prec0)
        out_b, rwkv_S = _rwkv_single_chunk(pre, rwkv_S, P['rwkv_gn_g'], P['rwkv_gn_b'], B, T, prec0)
    xf = _out_proj_ln(xf, out_a, out_b, W0['w_out0a'], W0['w_out0b'], P['ln_g'][0, 0], P['ln_b'][0, 0], prec0)
    xf = _mem_attn_ln(xf, mem_k, mem_v, 0, W0['w_mem_q'][0], W0['w_mem_o'][0], P['ln_g'][0, 1], P['ln_b'][0, 1],
                      B, T, prec0)
    xf = _moe_ln(xf, P, 0)
    hqk, hv1, ho, fq, fkk, fvv, hgate = _project(xf, Wc['w_in1'], (512, 512, 512, 512, 512, 512, 128))
    out_c, mC, mn, mm, conv = _mlstm(hqk, hv1, ho, hgate, conv, mC, mn, mm, P, Wc['gate_bias'], B, T, L_ret)
    if is_prompt:
        qa, kn, vc, kt, va, lf = _fox_prep_prompt(fq, fkk, fvv, hgate, P, Wc['gate_bias'], B, T)
        out_d = _fox_prompt(qa, kt, va, B, T)
        logf = lf[:, 2 * MLSTM_HEADS:2 * MLSTM_HEADS + FOX_HEADS].reshape(B, T, FOX_HEADS)
    else:
        qn, kn, lf = _fox_prep(fq, fkk, hgate, P, Wc['gate_bias'])
        vc = fvv
        logf = lf[:, 2 * MLSTM_HEADS:2 * MLSTM_HEADS + FOX_HEADS].reshape(B, T, FOX_HEADS)
        c_all = jnp.cumsum(jnp.concatenate([flf, logf], axis=1), axis=1)
        out_d = _fox_sample(qn, kn, fvv, fk, fv, c_all, B, T)
    xf = _out_proj_ln(xf, out_c, out_d, Wc['w_out1a'], Wc['w_out1b'], P['ln_g'][1, 0], P['ln_b'][1, 0])
    xf = _mem_attn_ln(xf, mem_k, mem_v, 1, Wc['w_mem_q'][1], Wc['w_mem_o'][1], P['ln_g'][1, 1], P['ln_b'][1, 1],
                      B, T)
    xf = _moe_ln(xf, P, 1)
    fk_new = kn.reshape(B, T, FOX_HEADS, FOX_HD)
    fv_new = vc.reshape(B, T, FOX_HEADS, FOX_HD)
    return (xf.reshape(B, T, D), ret_S, rwkv_S, new_shift, mC, mn, mm, conv, fk_new, fv_new, logf)


def kernel(x_prompt, x_sample, mem_prompt, state_ret, state_rwkv, cache_rwkv_shift, state_mlstm_c, state_mlstm_n,
           state_mlstm_m, cache_mlstm_conv, cache_fox_k, cache_fox_v, cache_fox_logf, cache_mem_k, cache_mem_v,
           w_in0, ret_gn_g, rwkv_mu, rwkv_w0, rwkv_w2, rwkv_a0, rwkv_a2, rwkv_g2, rwkv_k_k, rwkv_k_a, rwkv_r_k,
           rwkv_gn_g, rwkv_gn_b, w_out0, w_in1, mlstm_conv_w, mlstm_conv_b, mlstm_b_i, mlstm_b_f, mlstm_gn_g,
           fox_q_g, fox_k_g, fox_b_f, w_out1, w_mem_q, w_mem_k, w_mem_v, w_mem_o, w_router, b_router,
           w_exp_gate, w_exp_up, w_exp_down, ln_g, ln_b):
    P = dict(ret_gn_g=ret_gn_g, rwkv_mu=rwkv_mu, rwkv_w0=rwkv_w0, rwkv_w2=rwkv_w2, rwkv_a0=rwkv_a0,
             rwkv_a2=rwkv_a2, rwkv_g2=rwkv_g2, rwkv_k_k=rwkv_k_k, rwkv_k_a=rwkv_k_a, rwkv_r_k=rwkv_r_k,
             rwkv_gn_g=rwkv_gn_g, rwkv_gn_b=rwkv_gn_b, mlstm_conv_w=mlstm_conv_w, mlstm_conv_b=mlstm_conv_b,
             mlstm_gn_g=mlstm_gn_g, fox_q_g=fox_q_g, fox_k_g=fox_k_g, w_router=w_router, b_router=b_router,
             w_exp_gate=w_exp_gate, w_exp_up=w_exp_up, w_exp_down=w_exp_down, ln_g=ln_g, ln_b=ln_b)
    B, M = mem_prompt.shape[0], mem_prompt.shape[1]
    D = D_MODEL
    H = MLSTM_HEADS
    ret_proj = 2 * RET_HEADS * RET_DK + 2 * RET_HEADS * RET_DV
    qk1, w1 = 2 * H * MLSTM_DK, H * MLSTM_DV
    off = qk1 + w1
    mlstm_proj = off + 2 * H + w1
    fw = FOX_HEADS * FOX_HD
    gate_cols = jnp.concatenate([w_in1[:, off:off + 2 * H], w_in1[:, mlstm_proj + 3 * fw:],
                                 jnp.zeros((D, 128 - 2 * H - FOX_HEADS), F32)], axis=1)
    w_in1_c = jnp.concatenate([w_in1[:, :off], w_in1[:, off + 2 * H:mlstm_proj],
                               w_in1[:, mlstm_proj:mlstm_proj + 3 * fw], gate_cols], axis=1)
    gate_bias = jnp.concatenate([mlstm_b_i, mlstm_b_f, fox_b_f, jnp.zeros((128 - 2 * H - FOX_HEADS,), F32)]).reshape(1, 128)
    wa0, wb0 = w_out0[:RET_HEADS * RET_DV], w_out0[RET_HEADS * RET_DV:]
    Wf = dict(w_in0=_hi_lo(w_in0), w_out0a=_hi_lo(wa0), w_out0b=_hi_lo(wb0),
              w_mem_q=[_hi_lo(w_mem_q[0])], w_mem_o=[_hi_lo(w_mem_o[0])])
    Wc = dict(w_in0=w_in0.astype(BF16), w_in1=w_in1_c.astype(BF16), gate_bias=gate_bias,
              w_out0a=wa0.astype(BF16), w_out0b=wb0.astype(BF16),
              w_out1a=w_out1[:w1].astype(BF16), w_out1b=w_out1[w1:].astype(BF16),
              w_mem_q=w_mem_q.astype(BF16), w_mem_o=w_mem_o.astype(BF16))
    w_kv = jnp.concatenate([w_mem_k, w_mem_v], axis=0)
    memkv = _mem_project(mem_prompt.reshape(B * M, D), jnp.swapaxes(_hi_lo(w_kv), 0, 1))
    p_mem_k = memkv[:DEPTH].reshape(DEPTH * B, M, D)
    p_mem_v = memkv[DEPTH:].reshape(DEPTH * B, M, D)
    zeros = lambda *s: jnp.zeros(s, F32)
    prompt_states = (zeros(B, RET_HEADS, RET_DK, RET_DV), zeros(B, RWKV_HEADS, RWKV_HD, RWKV_HD),
                     zeros(B, 1, RWKV_PROJ), zeros(B, H, MLSTM_DK, MLSTM_DV), zeros(B, H, MLSTM_DK), zeros(B, H),
                     zeros(B, MLSTM_CONV - 1, qk1), None, None, None)
    p_out = _trunk(x_prompt, 0, p_mem_k, p_mem_v, prompt_states, P, Wc, Wf, True, PROMPT_PREC0)
    DB = x_sample.shape[0]
    sample_states = (state_ret, state_rwkv, cache_rwkv_shift, state_mlstm_c, state_mlstm_n, state_mlstm_m,
                     cache_mlstm_conv, cache_fox_k, cache_fox_v, cache_fox_logf)
    s_out = _trunk(x_sample, cache_fox_k.shape[1], cache_mem_k.reshape(DEPTH * DB, M, D),
                   cache_mem_v.reshape(DEPTH * DB, M, D), sample_states, P, Wc, Wf, False, True)
    mem_shape = (DEPTH, B, M, MEM_HEADS, MEM_HD)
    return ((p_out[0], s_out[0]) + p_out[1:] + (p_mem_k.reshape(mem_shape), p_mem_v.reshape(mem_shape)) + s_out[1:])
```

```python
import functools
import math

import numpy as np
import jax
import jax.numpy as jnp
from jax import lax
from jax.experimental import pallas as pl
from jax.experimental.pallas import tpu as pltpu

F32 = jnp.float32
BF16 = jnp.bfloat16

D_MODEL = 1024
DEPTH = 2
RET_HEADS, RET_DK, RET_DV = 4, 64, 128
ROPE_BASE = 10000.0
RWKV_HEADS, RWKV_HD = 8, 64
RWKV_W = RWKV_HEADS * RWKV_HD
RWKV_PROJ = 3 * RWKV_W + 64 + 64 + 128
MLSTM_HEADS, MLSTM_DK, MLSTM_DV, MLSTM_CONV = 4, 64, 128, 4
FOX_HEADS, FOX_HD = 4, 128
MEM_HEADS, MEM_HD = 4, 256
N_EXPERTS, N_GROUPS, EXPERTS_PER_GROUP = 16, 4, 4
D_EXPERT = 512
ALPHA = (2 * DEPTH) ** 0.25
EPS = 1e-5
NEG_INF = -1e30
LOG2E = 1.4426950408889634

VMEM_LIMIT_BYTES = 56 * 1024 * 1024
ROW_TILE = 512
RET_CHUNK = 256
RWKV_CHUNK = 64
RWKV_ROWS = 512
RWKV_LOCAL_ROWS = 1024
RWKV_SEQS = 8
FOX_TQ = 1024
MEM_SEQS = 4
MOE_BLOCK = 512
MOE_BLOCK_SMALL = 128
PROMPT_PREC0 = False


def _cparams(sem):
    return pltpu.CompilerParams(dimension_semantics=sem, vmem_limit_bytes=VMEM_LIMIT_BYTES)


def _tile(n, pref):
    if n <= pref:
        return n
    t = pref
    while t >= 8:
        if n % t == 0:
            return t
        t -= 8
    return n


_NN = (((1,), (0,)), ((), ()))
_NT = (((1,), (1,)), ((), ()))
_TN = (((0,), (0,)), ((), ()))
_BNN = (((2,), (1,)), ((0,), (0,)))
_BNT = (((2,), (2,)), ((0,), (0,)))
_BTN = (((1,), (1,)), ((0,), (0,)))


def _split2(a):
    a = a.astype(F32)
    hi = a.astype(BF16)
    return hi, (a - hi.astype(F32)).astype(BF16)


def _dg(a, b, dims, prec):
    if not prec:
        return lax.dot_general(a.astype(BF16), b.astype(BF16), dims, preferred_element_type=F32)
    a1, a2 = _split2(a)
    b1, b2 = _split2(b)
    d = lambda p, q: lax.dot_general(p, q, dims, preferred_element_type=F32)
    return d(a1, b1) + (d(a1, b2) + d(a2, b1))


def _dot(a, b, prec=False):
    return _dg(a, b, _NN, prec)


def _wdot(x, w_ref, prec):
    if not prec:
        return jnp.dot(x.astype(BF16), w_ref[...], preferred_element_type=F32)
    x1, x2 = _split2(x)
    d = lambda p, q: jnp.dot(p, q, preferred_element_type=F32)
    return d(x1, w_ref[0]) + (d(x1, w_ref[1]) + d(x2, w_ref[0]))


def _wspec(w):
    return pl.BlockSpec(w.shape, lambda *_: (0,) * w.ndim)


def _hi_lo(w):
    bits = lax.bitcast_convert_type(w, jnp.uint32) & jnp.uint32(0xFFFF0000)
    hi = lax.bitcast_convert_type(bits, F32)
    return jnp.stack([hi.astype(BF16), (w - hi).astype(BF16)])


def _dot_nt(a, b, prec=False):
    return _dg(a, b, _NT, prec)


def _dot_tn(a, b, prec=False):
    return _dg(a, b, _TN, prec)


def _split3(a):
    a1 = a.astype(BF16)
    r1 = a - a1.astype(F32)
    a2 = r1.astype(BF16)
    a3 = (r1 - a2.astype(F32)).astype(BF16)
    return a1, a2, a3


def _exact_left_dot(e, a):
    a1, a2, a3 = _split3(a)
    d = lambda p: jnp.dot(e, p, preferred_element_type=F32)
    return d(a1) + d(a2) + d(a3)


def _exact_right_dot(a, e):
    a1, a2, a3 = _split3(a)
    d = lambda p: jnp.dot(p, e, preferred_element_type=F32)
    return d(a1) + d(a2) + d(a3)


def _sigmoid(x):
    return 1.0 / (1.0 + jnp.exp(-x))


def _silu(x):
    return x * _sigmoid(x)


def _softplus(x):
    return jnp.maximum(x, 0.0) + jnp.log1p(jnp.exp(-jnp.abs(x)))


def _log_sigmoid(x):
    return -_softplus(-x)


def _layer_norm(z, g, b):
    mu = jnp.mean(z, axis=-1, keepdims=True)
    d = z - mu
    var = jnp.mean(d * d, axis=-1, keepdims=True)
    return d * lax.rsqrt(var + EPS) * g + b


def _lane_norm(y):
    mu = jnp.mean(y, axis=-1, keepdims=True)
    d = y - mu
    var = jnp.mean(d * d, axis=-1, keepdims=True)
    return d * lax.rsqrt(var + EPS)


def _proj_body(x_ref, w_ref, *o_refs, widths, prec):
    h = _wdot(x_ref[...], w_ref, prec)
    off = 0
    for o_ref, wd in zip(o_refs, widths):
        o_ref[...] = h[:, off:off + wd]
        off += wd


def _project(x, w, widths, prec=False):
    M, K = x.shape
    tm = _tile(M, ROW_TILE)
    return pl.pallas_call(
        functools.partial(_proj_body, widths=widths, prec=prec),
        grid=(M // tm,),
        in_specs=[pl.BlockSpec((tm, K), lambda i: (i, 0)), _wspec(w)],
        out_specs=[pl.BlockSpec((tm, wd), lambda i: (i, 0)) for wd in widths],
        out_shape=[jax.ShapeDtypeStruct((M, wd), F32) for wd in widths],
        compiler_params=_cparams(("parallel",)),
        name="project",
    )(x, w)


def _mem_proj_body(x_ref, w_ref, o_ref):
    o_ref[0] = _wdot(x_ref[...], w_ref.at[0], True)


def _mem_project(x, w):
    M, K = x.shape
    J, _, _, N = w.shape
    return pl.pallas_call(
        _mem_proj_body,
        grid=(J,),
        in_specs=[pl.BlockSpec((M, K), lambda j: (0, 0)),
                  pl.BlockSpec((1, 2, K, N), lambda j: (j, 0, 0, 0))],
        out_specs=pl.BlockSpec((1, M, N), lambda j: (j, 0, 0)),
        out_shape=jax.ShapeDtypeStruct((J, M, N), F32),
        compiler_params=_cparams(("parallel",)),
        name="mem_project",
    )(x, w)


def _out_ln_body(x_ref, a_ref, b_ref, wa_ref, wb_ref, g_ref, bias_ref, o_ref, *, prec):
    mix = _wdot(a_ref[...], wa_ref, prec) + _wdot(b_ref[...], wb_ref, prec)
    o_ref[...] = _layer_norm(ALPHA * x_ref[...] + mix, g_ref[...], bias_ref[...])


def _out_proj_ln(x, a, b, wa, wb, g, bias, prec=False):
    M, D = x.shape
    Ka, Kb = a.shape[1], b.shape[1]
    tm = _tile(M, ROW_TILE)
    row = lambda i: (i, 0)
    fix = lambda i: (0, 0)
    return pl.pallas_call(
        functools.partial(_out_ln_body, prec=prec),
        grid=(M // tm,),
        in_specs=[pl.BlockSpec((tm, D), row), pl.BlockSpec((tm, Ka), row), pl.BlockSpec((tm, Kb), row),
                  _wspec(wa), _wspec(wb),
                  pl.BlockSpec((1, D), fix), pl.BlockSpec((1, D), fix)],
        out_specs=pl.BlockSpec((tm, D), row),
        out_shape=jax.ShapeDtypeStruct((M, D), F32),
        compiler_params=_cparams(("parallel",)),
        name="out_proj_ln",
    )(x, a, b, wa, wb, g.reshape(1, D), bias.reshape(1, D))


def _ret_body(q_ref, k_ref, v_ref, g_ref, cos_ref, sa_ref, sb_ref, dmat_ref, qdec_ref, kdec_ref, sdec_ref,
              s0_ref, gn_ref, o_ref, sout_ref, s_scr, *, prec):
    i = pl.program_id(1)

    @pl.when(i == 0)
    def _():
        s_scr[...] = s0_ref[0]

    cos, sin_a, sin_b = cos_ref[...], sa_ref[...], sb_ref[...]
    width = RET_HEADS * RET_DK
    half = RET_DK // 2

    def rope(x):
        return x * cos + pltpu.roll(x, width - half, 1) * sin_a + pltpu.roll(x, half, 1) * sin_b

    q = rope(q_ref[...])
    k = rope(k_ref[...]) * (RET_DK ** -0.5)
    v = v_ref[...]
    gate = g_ref[...]
    H = RET_HEADS
    qh = jnp.stack([q[:, h * RET_DK:(h + 1) * RET_DK] for h in range(H)], axis=0)
    kh = jnp.stack([k[:, h * RET_DK:(h + 1) * RET_DK] for h in range(H)], axis=0)
    vh = jnp.stack([v[:, h * RET_DV:(h + 1) * RET_DV] for h in range(H)], axis=0)
    s_old = s_scr[...]
    s = _dg(qh, kh, _BNT, prec) * dmat_ref[...]
    o = _dg(s, vh, _BNN, prec) + _dg(qh, s_old, _BNN, prec) * qdec_ref[...]
    s_scr[...] = sdec_ref[...] * s_old + _dg(kh * kdec_ref[...], vh, _BTN, prec)
    on = _lane_norm(o)
    hn = jnp.concatenate([on[h] for h in range(H)], axis=1) * gn_ref[...]
    o_ref[...] = _silu(gate) * hn

    @pl.when(i == pl.num_programs(1) - 1)
    def _():
        sout_ref[0] = s_scr[...]


def _retention(hq, hk, hv, hg, s0, gn_g, B, T, pos0, L, prec=False):
    nT = T // L
    lg = np.log(1.0 - 2.0 ** (-5.0 - np.arange(RET_HEADS)))
    idx = np.arange(L, dtype=np.float64)
    rel = idx[:, None] - idx[None, :]
    dmat = np.where(rel >= 0, np.exp(np.maximum(rel, 0.0)[None] * lg[:, None, None]), 0.0)
    qdec = np.exp((idx + 1.0)[None, :, None] * lg[:, None, None])
    kdec = np.exp((L - 1.0 - idx)[None, :, None] * lg[:, None, None])
    sdec = np.exp(L * lg)[:, None, None]
    half = RET_DK // 2
    inv = ROPE_BASE ** (-jnp.arange(half, dtype=F32) / half)
    ang = (pos0 + jnp.arange(T)).astype(F32)[:, None] * inv[None, :]
    cos, sin = jnp.cos(ang), jnp.sin(ang)
    zero = jnp.zeros_like(sin)
    cos_t = jnp.tile(jnp.concatenate([cos, cos], axis=1), (1, RET_HEADS))
    sin_a = jnp.tile(jnp.concatenate([-sin, zero], axis=1), (1, RET_HEADS))
    sin_b = jnp.tile(jnp.concatenate([zero, sin], axis=1), (1, RET_HEADS))
    qk_w = RET_HEADS * RET_DK
    v_w = RET_HEADS * RET_DV
    row = lambda b, i: (b * nT + i, 0)
    tab = lambda b, i: (i, 0)
    fix3 = lambda b, i: (0, 0, 0)
    out, s_out = pl.pallas_call(
        functools.partial(_ret_body, prec=prec),
        grid=(B, nT),
        in_specs=[pl.BlockSpec((L, qk_w), row), pl.BlockSpec((L, qk_w), row),
                  pl.BlockSpec((L, v_w), row), pl.BlockSpec((L, v_w), row),
                  pl.BlockSpec((L, qk_w), tab), pl.BlockSpec((L, qk_w), tab), pl.BlockSpec((L, qk_w), tab),
                  pl.BlockSpec((RET_HEADS, L, L), fix3), pl.BlockSpec((RET_HEADS, L, 1), fix3),
                  pl.BlockSpec((RET_HEADS, L, 1), fix3), pl.BlockSpec((RET_HEADS, 1, 1), fix3),
                  pl.BlockSpec((1, RET_HEADS, RET_DK, RET_DV), lambda b, i: (b, 0, 0, 0)),
                  pl.BlockSpec((1, v_w), lambda b, i: (0, 0))],
        out_specs=[pl.BlockSpec((L, v_w), row),
                   pl.BlockSpec((1, RET_HEADS, RET_DK, RET_DV), lambda b, i: (b, 0, 0, 0))],
        out_shape=[jax.ShapeDtypeStruct((B * T, v_w), F32),
                   jax.ShapeDtypeStruct((B, RET_HEADS, RET_DK, RET_DV), F32)],
        scratch_shapes=[pltpu.VMEM((RET_HEADS, RET_DK, RET_DV), F32)],
        compiler_params=_cparams(("parallel", "arbitrary")),
        name="retention",
    )(hq, hk, hv, hg, cos_t, sin_a, sin_b, jnp.asarray(dmat, F32), jnp.asarray(qdec, F32),
      jnp.asarray(kdec, F32), jnp.asarray(sdec, F32), s0, gn_g.reshape(1, v_w))
    return out, s_out


def _rwkv_prep_body(h_ref, p_ref, mu_ref, w0_ref, w2_ref, a0_ref, a2_ref, g2_ref, kk_ref, ka_ref, rk_ref,
                    ones_ref, tri_ref, blk_ref, sel_ref,
                    kq_ref, rq_ref, kt_ref, bt_ref, ke_ref, be_ref, v_ref, g_ref, bon_ref, gt_ref, *scratch,
                    prec, shift_rows):
    hb = h_ref[...]
    if shift_rows:
        carry, = scratch
        tm = hb.shape[0]
        first = jnp.where(pl.program_id(1) == 0, p_ref[0], carry[...])
        row = lax.broadcasted_iota(jnp.int32, (tm, 1), 0)
        prev = jnp.where(row == 0, first, pltpu.roll(hb, 1, 0))
        carry[...] = hb[tm - 1:tm, :]
    else:
        prev = p_ref[...]
    xs = hb + (prev - hb) * mu_ref[...]
    W = RWKV_W
    r, k, v = xs[:, :W], xs[:, W:2 * W], xs[:, 2 * W:3 * W]
    w_lo = xs[:, 3 * W:3 * W + 64]
    a_lo = xs[:, 3 * W + 64:3 * W + 128]
    g_lo = xs[:, 3 * W + 128:]
    w_log = -_softplus(-(w0_ref[...] + _dot(jnp.tanh(w_lo), w2_ref[...], prec))) - 0.5
    lw = -jnp.exp(w_log)
    a = _sigmoid(a0_ref[...] + _dot(a_lo, a2_ref[...], prec))
    g = _dot(_sigmoid(g_lo), g2_ref[...], prec)
    ones = ones_ref[...]
    kk = k * kk_ref[...]
    nrm = jnp.sqrt(_exact_right_dot(kk * kk, ones))
    kk = kk / jnp.maximum(nrm, 1e-12)
    k2 = k * (1.0 + (a - 1.0) * ka_ref[...])
    beta = kk * a
    cl = _exact_left_dot(tri_ref[...], lw)
    tot = _exact_left_dot(blk_ref[...], lw)
    ginv = jnp.exp(-cl)
    gend = jnp.exp(tot - cl)
    kq_ref[...] = (kk * jnp.exp(cl - lw)).astype(kq_ref.dtype)
    rq_ref[...] = (r * jnp.exp(cl)).astype(rq_ref.dtype)
    kt_ref[...] = (k2 * ginv).astype(kt_ref.dtype)
    bt_ref[...] = (beta * ginv).astype(bt_ref.dtype)
    ke_ref[...] = (k2 * gend).astype(ke_ref.dtype)
    be_ref[...] = (beta * gend).astype(be_ref.dtype)
    v_ref[...] = v.astype(v_ref.dtype)
    gt_ref[...] = jnp.exp(_exact_left_dot(sel_ref[...], lw))
    g_ref[...] = g
    bon_ref[...] = _exact_right_dot(r * k2 * rk_ref[...], ones) * v


def _rwkv_prep(hb, prev, P, L, prec=False, seqs=None):
    M = hb.shape[0]
    tm = _tile(M if seqs is None else seqs[1], ROW_TILE)
    assert tm % L == 0
    W = RWKV_W
    idx = np.arange(tm)
    same = (idx[:, None] // L) == (idx[None, :] // L)
    tri = jnp.asarray(same & (idx[:, None] >= idx[None, :]), BF16)
    blk = jnp.asarray(same, BF16)
    sel = jnp.asarray((idx[None, :] // L) == np.arange(tm // L)[:, None], BF16)
    lane = np.arange(W)
    ones = jnp.asarray((lane[:, None] // RWKV_HD) == (lane[None, :] // RWKV_HD), BF16)
    if seqs is None:
        grid, nT = (1, M // tm), M // tm
        prev_spec = pl.BlockSpec((tm, RWKV_PROJ), lambda b, i: (i, 0))
        scratch = []
    else:
        grid, nT = (seqs[0], seqs[1] // tm), seqs[1] // tm
        prev_spec = pl.BlockSpec((1, 1, RWKV_PROJ), lambda b, i: (b, 0, 0))
        scratch = [pltpu.VMEM((1, RWKV_PROJ), F32)]
    row = lambda b, i: (b * nT + i, 0)
    vec = lambda a: a.reshape(1, -1)
    params = [vec(P['rwkv_mu']), vec(P['rwkv_w0']), P['rwkv_w2'], vec(P['rwkv_a0']),
              P['rwkv_a2'], P['rwkv_g2'], vec(P['rwkv_k_k']), vec(P['rwkv_k_a']),
              vec(P['rwkv_r_k']), ones, tri, blk, sel]
    out_dt = [F32 if prec else BF16] * 7 + [F32] * 2
    return pl.pallas_call(
        functools.partial(_rwkv_prep_body, prec=prec, shift_rows=seqs is not None),
        grid=grid,
        in_specs=[pl.BlockSpec((tm, RWKV_PROJ), row), prev_spec] + [_wspec(a) for a in params],
        out_specs=[pl.BlockSpec((tm, W), row) for _ in out_dt] + [pl.BlockSpec((tm // L, W), row)],
        out_shape=[jax.ShapeDtypeStruct((M, W), dt) for dt in out_dt] + [jax.ShapeDtypeStruct((M // L, W), F32)],
        scratch_shapes=scratch,
        compiler_params=_cparams(("parallel", "arbitrary")),
        name="rwkv_prep",
    )(hb, prev, *params)


def _rwkv_masks(L):
    ii = lax.broadcasted_iota(jnp.int32, (L, L), 0)
    jj = lax.broadcasted_iota(jnp.int32, (L, L), 1)
    return ii > jj, ii >= jj, (ii == jj).astype(F32)


def _rwkv_local(kq, rq, kt, bt, ke, be, v, masks, lmask_ref, nlev, prec):
    strict, incl, eye = masks
    L = kq.shape[1]
    x = jnp.concatenate([kq, rq], axis=1)
    sk = _dg(x, kt, _BNT, prec)
    sb = _dg(x, bt, _BNT, prec)
    n_m = jnp.where(strict, sk[:, :L], 0.0)
    a_k = jnp.where(incl, sk[:, L:], 0.0)
    m_m = jnp.where(strict, sb[:, :L], 0.0)
    a_b = jnp.where(incl, sb[:, L:], 0.0)
    t_m = eye - m_m * lmask_ref[0]
    for lv in range(1, nlev):
        c_m = m_m * lmask_ref[lv]
        t_m = t_m - _dg(_dg(t_m, c_m, _BNN, prec), t_m, _BNN, prec)
    kqp = _dg(t_m, kq, _BNN, prec)
    u0 = _dg(t_m, _dg(n_m, v, _BNN, prec), _BNN, prec)
    rqp = rq.astype(F32) - _dg(a_b, kqp, _BNN, prec)
    y0 = _dg(a_k, v, _BNN, prec) - _dg(a_b, u0, _BNN, prec)
    p_m = _dg(kqp, be, _BTN, prec)
    b_c = _dg(v, ke, _BTN, prec) - _dg(u0, be, _BTN, prec)
    return rqp, y0, p_m, b_c


def _rwkv_apply(s_old, rqp, y0, p_m, b_c, g_end, prec):
    y = _dg(rqp, s_old, _BNT, prec) + y0
    return y, s_old * g_end - _dg(s_old, p_m, _BNN, prec) + b_c


def _head_stack(tile, nblk):
    hd = RWKV_HD
    L = tile.shape[0] // nblk
    return jnp.concatenate([tile[:, hh * hd:(hh + 1) * hd].reshape(nblk, L, hd) for hh in range(2)], axis=0)


def _head_unstack(x, nblk):
    L, hd = x.shape[1], x.shape[2]
    return jnp.concatenate([x[hh * nblk:(hh + 1) * nblk].reshape(nblk * L, hd) for hh in range(2)], axis=1)


def _rwkv_local_body(kq_ref, rq_ref, kt_ref, bt_ref, ke_ref, be_ref, v_ref, lmask_ref,
                     rqp_ref, y0_ref, pm_ref, bc_ref, *, L, nchunk, nlev):
    ops = [_head_stack(r[...], nchunk) for r in (kq_ref, rq_ref, kt_ref, bt_ref, ke_ref, be_ref, v_ref)]
    res = _rwkv_local(*ops, _rwkv_masks(L), lmask_ref, nlev, False)
    for o_ref, a in zip((rqp_ref, y0_ref, pm_ref, bc_ref), res):
        o_ref[...] = _head_unstack(a, nchunk).astype(o_ref.dtype)


def _rwkv_apply_body(rqp_ref, y0_ref, pm_ref, bc_ref, gt_ref, g_ref, bon_ref, s0_ref, gng_ref, gnb_ref,
                     o_ref, sout_ref, s_scr, *, L, nchunk, B):
    i = pl.program_id(0)
    H, hd = RWKV_HEADS, RWKV_HD

    @pl.when(i == 0)
    def _():
        s_scr[...] = s0_ref[...].reshape(B * H, hd, hd)

    def heads(x):
        return jnp.stack([x[b][:, h * hd:(h + 1) * hd] for b in range(B) for h in range(H)], axis=0)

    def chunk(c, carry):
        rows = pl.ds(pl.multiple_of(c * L, L), L)
        ops = [heads(r[:, rows, :]) for r in (rqp_ref, y0_ref, pm_ref, bc_ref)]
        g_end = heads(gt_ref[:, pl.ds(c, 1), :])
        y, s_new = _rwkv_apply(s_scr[...], *ops, g_end, False)
        s_scr[...] = s_new
        yn = _lane_norm(y)
        for b in range(B):
            ynb = jnp.concatenate([yn[b * H + h] for h in range(H)], axis=1)
            o_ref[b, rows, :] = (ynb * gng_ref[...] + gnb_ref[...] + bon_ref[b, rows, :]) * g_ref[b, rows, :]
        return carry

    lax.fori_loop(0, nchunk, chunk, 0)

    @pl.when(i == pl.num_programs(0) - 1)
    def _():
        sout_ref[...] = s_scr[...].reshape(B, H, hd, hd)


def _rwkv_fused_body(kq_ref, rq_ref, kt_ref, bt_ref, ke_ref, be_ref, v_ref, g_ref, bon_ref, gt_ref,
                     s0_ref, gng_ref, gnb_ref, lmask_ref, o_ref, sout_ref, *, L, nseq, nlev, prec):
    hd = RWKV_HD
    ops = [_head_stack(r[...], nseq) for r in (kq_ref, rq_ref, kt_ref, bt_ref, ke_ref, be_ref, v_ref)]
    loc = _rwkv_local(*ops, _rwkv_masks(L), lmask_ref, nlev, prec)
    gt = gt_ref[...]
    g_end = jnp.concatenate([gt[:, :, hh * hd:(hh + 1) * hd] for hh in range(2)], axis=0)
    s_old = jnp.concatenate([s0_ref[:, hh] for hh in range(2)], axis=0)
    y, s_new = _rwkv_apply(s_old, *loc, g_end, prec)
    for hh in range(2):
        sout_ref[:, hh] = s_new[hh * nseq:(hh + 1) * nseq]
    yn = _head_unstack(_lane_norm(y), nseq)
    o_ref[...] = (yn * gng_ref[...] + gnb_ref[...] + bon_ref[...]) * g_ref[...]


def _rwkv_level_masks(L):
    nlev = int(math.log2(L))
    idx = np.arange(L)
    ii, jj = idx[:, None], idx[None, :]
    lmask = np.stack([((ii >> (lv + 1)) == (jj >> (lv + 1))) & ((ii & (1 << lv)) != 0) & ((jj & (1 << lv)) == 0)
                      for lv in range(nlev)]).astype(np.float32)
    return nlev, jnp.asarray(lmask)


def _rwkv_scan(pre, s0, gn_g, gn_b, B, T, L):
    W, H, hd = RWKV_W, RWKV_HEADS, RWKV_HD
    assert L == hd
    kq, rq, kt, bt, ke, be, v, g, bon, gt = pre
    nlev, lmask = _rwkv_level_masks(L)
    tl = _tile(T, RWKV_LOCAL_ROWS)
    nL = T // tl
    pw = 2 * hd
    row = lambda b, p, i: (b * nL + i, p)
    rqp, y0, p_m, b_c = pl.pallas_call(
        functools.partial(_rwkv_local_body, L=L, nchunk=tl // L, nlev=nlev),
        grid=(B, H // 2, nL),
        in_specs=[pl.BlockSpec((tl, pw), row) for _ in range(7)]
                 + [pl.BlockSpec((nlev, L, L), lambda b, p, i: (0, 0, 0))],
        out_specs=[pl.BlockSpec((tl, pw), row) for _ in range(4)],
        out_shape=[jax.ShapeDtypeStruct((B * T, W), dt) for dt in (BF16, F32, BF16, F32)],
        compiler_params=_cparams(("parallel", "parallel", "parallel")),
        name="rwkv_local",
    )(kq, rq, kt, bt, ke, be, v, lmask)
    tb = _tile(T, RWKV_ROWS)
    nT = T // tb
    nchunk = tb // L
    r3 = lambda a: a.reshape(B, T, W)
    blk = lambda i: (0, i, 0)
    fix2 = lambda i: (0, 0)
    fix4 = lambda i: (0, 0, 0, 0)
    out, s_out = pl.pallas_call(
        functools.partial(_rwkv_apply_body, L=L, nchunk=nchunk, B=B),
        grid=(nT,),
        in_specs=[pl.BlockSpec((B, tb, W), blk) for _ in range(4)]
                 + [pl.BlockSpec((B, nchunk, W), blk), pl.BlockSpec((B, tb, W), blk), pl.BlockSpec((B, tb, W), blk),
                    pl.BlockSpec((B, H, hd, hd), fix4), pl.BlockSpec((1, W), fix2), pl.BlockSpec((1, W), fix2)],
        out_specs=[pl.BlockSpec((B, tb, W), blk), pl.BlockSpec((B, H, hd, hd), fix4)],
        out_shape=[jax.ShapeDtypeStruct((B, T, W), F32), jax.ShapeDtypeStruct((B, H, hd, hd), F32)],
        scratch_shapes=[pltpu.VMEM((B * H, hd, hd), F32)],
        compiler_params=_cparams(("arbitrary",)),
        name="rwkv_apply",
    )(r3(rqp), r3(y0), r3(p_m), r3(b_c), gt.reshape(B, T // L, W), r3(g), r3(bon), s0,
      gn_g.reshape(1, W), gn_b.reshape(1, W))
    return out.reshape(B * T, W), s_out


def _rwkv_single_chunk(pre, s0, gn_g, gn_b, B, T, prec):
    W, H, hd = RWKV_W, RWKV_HEADS, RWKV_HD
    kq, rq, kt, bt, ke, be, v, g, bon, gt = pre
    nlev, lmask = _rwkv_level_masks(T)
    nseq = RWKV_SEQS if B % RWKV_SEQS == 0 else B
    pw = 2 * hd
    row = lambda i, p: (i, p)
    st = lambda i, p: (i, p, 0, 0)
    vec = lambda i, p: (0, p)
    out, s_out = pl.pallas_call(
        functools.partial(_rwkv_fused_body, L=T, nseq=nseq, nlev=nlev, prec=prec),
        grid=(B // nseq, H // 2),
        in_specs=[pl.BlockSpec((nseq * T, pw), row) for _ in range(9)]
                 + [pl.BlockSpec((nseq, 1, pw), lambda i, p: (i, 0, p)), pl.BlockSpec((nseq, 2, hd, hd), st),
                    pl.BlockSpec((1, pw), vec), pl.BlockSpec((1, pw), vec),
                    pl.BlockSpec((nlev, T, T), lambda i, p: (0, 0, 0))],
        out_specs=[pl.BlockSpec((nseq * T, pw), row), pl.BlockSpec((nseq, 2, hd, hd), st)],
        out_shape=[jax.ShapeDtypeStruct((B * T, W), F32), jax.ShapeDtypeStruct((B, H, hd, hd), F32)],
        compiler_params=_cparams(("parallel", "parallel")),
        name="rwkv_single_chunk",
    )(kq, rq, kt, bt, ke, be, v, g, bon, gt.reshape(B, 1, W), s0, gn_g.reshape(1, W), gn_b.reshape(1, W), lmask)
    return out, s_out


def _mlstm_body(qk_ref, v_ref, o_ref, gt_ref, cprev_ref, c0_ref, n0_ref, m0_ref, cw_ref, cb_ref, gb_ref,
                gn_ref, tri_ref, out_ref, cout_ref, nout_ref, mout_ref, convout_ref,
                xpad, c_scr, m_scr, *, L):
    i = pl.program_id(1)
    H, DK, DV = MLSTM_HEADS, MLSTM_DK, MLSTM_DV
    K = MLSTM_CONV - 1
    base = 8 - K

    @pl.when(i == 0)
    def _():
        xpad[base:8, :] = cprev_ref[0]
        for h in range(H):
            c_scr[h, :, 0:DV] = c0_ref[0, h]
            c_scr[h, :, DV:2 * DV] = jnp.broadcast_to(n0_ref[0, h], (DK, DV))
        m_scr[...] = m0_ref[0]

    xpad[8:8 + L, :] = qk_ref[...]
    conv = cb_ref[...] + xpad[pl.ds(base, L), :] * cw_ref[0:1, :]
    for j in range(1, MLSTM_CONV):
        conv = conv + xpad[pl.ds(base + j, L), :] * cw_ref[j:j + 1, :]
    tail = xpad[pl.ds(8 + L - K, K), :]
    xpad[base:8, :] = tail
    qk = _silu(conv)
    q = qk[:, :H * DK] * (DK ** -0.5)
    k = qk[:, H * DK:]
    v = v_ref[...]
    z = gt_ref[...] + gb_ref[...]
    lf = _log_sigmoid(z)
    bcum = _exact_left_dot(tri_ref[...], lf)
    z_t = z.T
    b_t = bcum.T
    ii = lax.broadcasted_iota(jnp.int32, (L, L), 0)
    jj = lax.broadcasted_iota(jnp.int32, (L, L), 1)
    causal = ii >= jj
    ones = jnp.ones((L, DV), F32)
    m_all = m_scr[...]
    lane = lax.broadcasted_iota(jnp.int32, (1, 128), 1)
    stack = lambda f: jnp.stack([f(h) for h in range(H)], axis=0)
    qh = stack(lambda h: q[:, h * DK:(h + 1) * DK])
    kh = stack(lambda h: k[:, h * DK:(h + 1) * DK])
    vh = stack(lambda h: jnp.concatenate([v[:, h * DV:(h + 1) * DV], ones], axis=1))
    b_col = stack(lambda h: bcum[:, H + h:H + h + 1])
    ig_col = stack(lambda h: z[:, h:h + 1])
    row_term = stack(lambda h: z_t[h:h + 1, :] - b_t[H + h:H + h + 1, :])
    m0 = stack(lambda h: m_all[:, h:h + 1])
    log_d = jnp.where(causal, b_col + row_term, NEG_INF)
    m_inter = b_col + m0
    m_t = jnp.maximum(m_inter, jnp.max(log_d, axis=-1, keepdims=True))
    w_d = jnp.exp(log_d - m_t)
    w_i = jnp.exp(m_inter - m_t)
    c_old = c_scr[...]
    s = _dg(qh, kh, _BNT, False) * w_d
    num = _dg(s, vh, _BNN, False) + _dg(qh, c_old, _BNN, False) * w_i
    den = num[:, :, DV:DV + 1]
    denom = jnp.maximum(jnp.abs(den), jnp.exp(-m_t))
    hn = _lane_norm(num[:, :, :DV] / denom)
    m_new = m_t[:, L - 1:L, :]
    b_last = b_col[:, L - 1:L, :]
    w_s = jnp.exp(b_last + m0 - m_new)
    w_k = jnp.exp(b_last - b_col + ig_col - m_new)
    c_scr[...] = w_s * c_old + _dg(kh * w_k, vh, _BTN, False)
    m_new_all = m_all
    for h in range(H):
        m_new_all = jnp.where(lane == h, m_new[h], m_new_all)
    m_scr[...] = m_new_all
    out_ref[...] = jnp.concatenate([hn[h] for h in range(H)], axis=1) * gn_ref[...] * _sigmoid(o_ref[...])

    @pl.when(i == pl.num_programs(1) - 1)
    def _():
        for h in range(H):
            cout_ref[0, h] = c_scr[h, :, 0:DV]
            nout_ref[0, h] = c_scr[h, :, DV:DV + 1]
        mout_ref[0] = m_scr[...]
        convout_ref[0] = tail


def _mlstm(hqk, hv, ho, hgate, conv_prev, c0, n0, m0, P, gate_bias, B, T, L):
    H, DK, DV = MLSTM_HEADS, MLSTM_DK, MLSTM_DV
    nT = T // L
    K = MLSTM_CONV - 1
    W = H * DV
    idx = np.arange(L)
    tri = jnp.asarray(idx[:, None] >= idx[None, :], BF16)
    m0p = jnp.zeros((B, 1, 128), F32).at[:, 0, :H].set(m0)
    row = lambda b, i: (b * nT + i, 0)
    fix = lambda b, i: (0, 0)
    perb3 = lambda b, i: (b, 0, 0)
    perb4 = lambda b, i: (b, 0, 0, 0)
    out, c_out, n_out, m_out, conv_out = pl.pallas_call(
        functools.partial(_mlstm_body, L=L),
        grid=(B, nT),
        in_specs=[pl.BlockSpec((L, W), row), pl.BlockSpec((L, W), row), pl.BlockSpec((L, W), row),
                  pl.BlockSpec((L, 128), row),
                  pl.BlockSpec((1, K, W), perb3),
                  pl.BlockSpec((1, H, DK, DV), perb4), pl.BlockSpec((1, H, DK, 1), perb4),
                  pl.BlockSpec((1, 1, 128), perb3),
                  pl.BlockSpec((MLSTM_CONV, W), fix), pl.BlockSpec((1, W), fix), pl.BlockSpec((1, 128), fix),
                  pl.BlockSpec((1, W), fix), pl.BlockSpec((L, L), fix)],
        out_specs=[pl.BlockSpec((L, W), row),
                   pl.BlockSpec((1, H, DK, DV), perb4), pl.BlockSpec((1, H, DK, 1), perb4),
                   pl.BlockSpec((1, 1, 128), perb3), pl.BlockSpec((1, K, W), perb3)],
        out_shape=[jax.ShapeDtypeStruct((B * T, W), F32),
                   jax.ShapeDtypeStruct((B, H, DK, DV), F32), jax.ShapeDtypeStruct((B, H, DK, 1), F32),
                   jax.ShapeDtypeStruct((B, 1, 128), F32), jax.ShapeDtypeStruct((B, K, W), F32)],
        scratch_shapes=[pltpu.VMEM((L + 8, W), F32), pltpu.VMEM((H, DK, 2 * DV), F32), pltpu.VMEM((1, 128), F32)],
        compiler_params=_cparams(("parallel", "arbitrary")),
        name="mlstm",
    )(hqk, hv, ho, hgate, conv_prev, c0, n0.reshape(B, H, DK, 1), m0p,
      P['mlstm_conv_w'], P['mlstm_conv_b'].reshape(1, W), gate_bias, P['mlstm_gn_g'].reshape(1, W), tri)
    return out, c_out, n_out.reshape(B, H, DK), m_out[:, 0, :H], conv_out


def _fox_prep_body(q_ref, k_ref, gt_ref, qg_ref, kg_ref, gb_ref, qn_ref, kn_ref, lf_ref):
    def rms(x, g):
        outs = []
        for h in range(FOX_HEADS):
            xh = x[:, h * FOX_HD:(h + 1) * FOX_HD]
            outs.append(xh * lax.rsqrt(jnp.mean(xh * xh, axis=-1, keepdims=True) + EPS) * g)
        return jnp.concatenate(outs, axis=1)

    qn_ref[...] = (rms(q_ref[...], qg_ref[...]) * (FOX_HD ** -0.5)).astype(BF16)
    kn_ref[...] = rms(k_ref[...], kg_ref[...])
    lf_ref[...] = _log_sigmoid(gt_ref[...] + gb_ref[...])


def _fox_prep(hq, hk, hgate, P, gate_bias):
    M, W = hq.shape
    tm = _tile(M, ROW_TILE)
    row = lambda i: (i, 0)
    fix = lambda i: (0, 0)
    return pl.pallas_call(
        _fox_prep_body,
        grid=(M // tm,),
        in_specs=[pl.BlockSpec((tm, W), row), pl.BlockSpec((tm, W), row), pl.BlockSpec((tm, 128), row),
                  pl.BlockSpec((1, FOX_HD), fix), pl.BlockSpec((1, FOX_HD), fix), pl.BlockSpec((1, 128), fix)],
        out_specs=[pl.BlockSpec((tm, W), row), pl.BlockSpec((tm, W), row), pl.BlockSpec((tm, 128), row)],
        out_shape=[jax.ShapeDtypeStruct((M, W), BF16), jax.ShapeDtypeStruct((M, W), F32),
                   jax.ShapeDtypeStruct((M, 128), F32)],
        compiler_params=_cparams(("parallel",)),
        name="fox_prep",
    )(hq, hk, hgate, P['fox_q_g'].reshape(1, FOX_HD), P['fox_k_g'].reshape(1, FOX_HD), gate_bias)


def _fox_prep_prompt_body(q_ref, k_ref, v_ref, gt_ref, qg_ref, kg_ref, gb_ref, tri_ref,
                          qa_ref, kn_ref, vc_ref, kt_ref, va_ref, lf_ref, carry):
    i = pl.program_id(1)
    tm = q_ref.shape[0]
    HD = FOX_HD

    @pl.when(i == 0)
    def _():
        carry[...] = jnp.zeros_like(carry)

    lf = _log_sigmoid(gt_ref[...] + gb_ref[...])
    lf_ref[...] = lf
    c = _exact_left_dot(tri_ref[...], lf) + carry[...]
    carry[...] = c[tm - 1:tm, :]
    c2t = (c * (-LOG2E)).T
    lane = lax.broadcasted_iota(jnp.int32, (tm, HD), 1)
    q_ones = jnp.where(lane < 3, 1.0, 0.0).astype(BF16)
    v_ones = jnp.where(lane < 1, 1.0, 0.0).astype(BF16)
    row16 = lax.broadcasted_iota(jnp.int32, (16, tm), 0)
    q, k, v = q_ref[...], k_ref[...], v_ref[...]
    for h in range(FOX_HEADS):
        hs = slice(h * HD, (h + 1) * HD)
        qh, kh = q[:, hs], k[:, hs]
        qh = qh * lax.rsqrt(jnp.mean(qh * qh, axis=-1, keepdims=True) + EPS) * qg_ref[...]
        kh = kh * lax.rsqrt(jnp.mean(kh * kh, axis=-1, keepdims=True) + EPS) * kg_ref[...]
        qa_ref[:, 2 * h * HD:(2 * h + 1) * HD] = (qh * (HD ** -0.5 * LOG2E)).astype(BF16)
        qa_ref[:, (2 * h + 1) * HD:(2 * h + 2) * HD] = q_ones
        kn_ref[pl.ds(h, tm, stride=FOX_HEADS), :] = kh
        vc_ref[pl.ds(h, tm, stride=FOX_HEADS), :] = v[:, hs]
        kt_ref[0, h, 0, 0:HD, :] = kh.T.astype(BF16)
        bias = c2t[2 * MLSTM_HEADS + h:2 * MLSTM_HEADS + h + 1, :]
        hi = bias.astype(BF16).astype(F32)
        mid = (bias - hi).astype(BF16).astype(F32)
        lo = bias - hi - mid
        blk = jnp.where(row16 == 0, hi, jnp.where(row16 == 1, mid, jnp.where(row16 == 2, lo, 0.0)))
        kt_ref[0, h, 0, HD:HD + 16, :] = blk.astype(BF16)
        kt_ref[0, h, 0, HD + 16:2 * HD, :] = jnp.zeros((HD - 16, tm), BF16)
        va_ref[:, 2 * h * HD:(2 * h + 1) * HD] = v[:, hs].astype(BF16)
        va_ref[:, (2 * h + 1) * HD:(2 * h + 2) * HD] = v_ones


def _fox_prep_prompt(hq, hk, hv, hgate, P, gate_bias, B, T):
    M, W = hq.shape
    H, HD = FOX_HEADS, FOX_HD
    tm = _tile(T, FOX_TQ)
    nT = T // tm
    idx = np.arange(tm)
    tri = jnp.asarray(idx[:, None] >= idx[None, :], BF16)
    row = lambda b, i: (b * nT + i, 0)
    fix = lambda b, i: (0, 0)
    return pl.pallas_call(
        _fox_prep_prompt_body,
        grid=(B, nT),
        in_specs=[pl.BlockSpec((tm, W), row), pl.BlockSpec((tm, W), row), pl.BlockSpec((tm, W), row),
                  pl.BlockSpec((tm, 128), row),
                  pl.BlockSpec((1, HD), fix), pl.BlockSpec((1, HD), fix), pl.BlockSpec((1, 128), fix),
                  pl.BlockSpec((tm, tm), fix)],
        out_specs=[pl.BlockSpec((tm, 2 * W), row), pl.BlockSpec((tm * H, HD), row), pl.BlockSpec((tm * H, HD), row),
                   pl.BlockSpec((1, H, 1, 2 * HD, tm), lambda b, i: (b, 0, i, 0, 0)),
                   pl.BlockSpec((tm, 2 * W), row), pl.BlockSpec((tm, 128), row)],
        out_shape=[jax.ShapeDtypeStruct((M, 2 * W), BF16), jax.ShapeDtypeStruct((M * H, HD), F32),
                   jax.ShapeDtypeStruct((M * H, HD), F32),
                   jax.ShapeDtypeStruct((B, H, nT, 2 * HD, tm), BF16),
                   jax.ShapeDtypeStruct((M, 2 * W), BF16), jax.ShapeDtypeStruct((M, 128), F32)],
        scratch_shapes=[pltpu.VMEM((1, 128), F32)],
        compiler_params=_cparams(("parallel", "arbitrary")),
        name="fox_prep_prompt",
    )(hq, hk, hv, hgate, P['fox_q_g'].reshape(1, HD), P['fox_k_g'].reshape(1, HD), gate_bias, tri)


def _fox_prompt_body(q_ref, kt_ref, va_ref, o_ref, sa_scr, sb_scr, m_scr, acc_scr, *, tq):
    qi = pl.program_id(2)
    m_scr[...] = jnp.full_like(m_scr, NEG_INF)
    acc_scr[...] = jnp.zeros_like(acc_scr)

    def scores(s_ref, kj):
        s_ref[...] = jnp.dot(q_ref[0], kt_ref[0, 0, kj], preferred_element_type=F32)

    def update(s_ref, kj, masked):
        vb = va_ref[0, pl.ds(pl.multiple_of(kj * tq, tq), tq), :]
        s = s_ref[...]
        if masked:
            ii = lax.broadcasted_iota(jnp.int32, (tq, tq), 0)
            jj = lax.broadcasted_iota(jnp.int32, (tq, tq), 1)
            s = jnp.where(jj <= ii, s, NEG_INF)
        m_old = m_scr[...]
        m_new = jnp.maximum(m_old, jnp.max(s, axis=-1, keepdims=True))
        p = jnp.exp2(s - m_new)
        acc_scr[...] = jnp.exp2(m_old - m_new) * acc_scr[...] + jnp.dot(p.astype(BF16), vb, preferred_element_type=F32)
        m_scr[...] = m_new

    scores(sa_scr, 0)

    def body(j, carry):
        scores(sb_scr, 2 * j + 1)
        update(sa_scr, 2 * j, False)
        scores(sa_scr, 2 * j + 2)
        update(sb_scr, 2 * j + 1, False)
        return carry

    lax.fori_loop(0, lax.shift_right_logical(qi, 1), body, 0)

    @pl.when((qi & 1) == 0)
    def _():
        update(sa_scr, qi, True)

    @pl.when((qi & 1) == 1)
    def _():
        scores(sb_scr, qi)
        update(sa_scr, qi - 1, False)
        update(sb_scr, qi, True)

    acc = acc_scr[...]
    o_ref[0] = acc[:, :FOX_HD] / acc[:, FOX_HD:FOX_HD + 1]


def _fox_prompt(qa, kt, va, B, T):
    H, HD = FOX_HEADS, FOX_HD
    W = H * HD
    tq = kt.shape[-1]
    nQ = T // tq
    out = pl.pallas_call(
        functools.partial(_fox_prompt_body, tq=tq),
        grid=(B, H, nQ),
        in_specs=[pl.BlockSpec((1, tq, 2 * HD), lambda b, h, i: (b, i, h)),
                  pl.BlockSpec((1, 1, nQ, 2 * HD, tq), lambda b, h, i: (b, h, 0, 0, 0),
                               pipeline_mode=pl.Buffered(1)),
                  pl.BlockSpec((1, T, 2 * HD), lambda b, h, i: (b, 0, h), pipeline_mode=pl.Buffered(1))],
        out_specs=pl.BlockSpec((1, tq, HD), lambda b, h, i: (b, i, h)),
        out_shape=jax.ShapeDtypeStruct((B, T, W), F32),
        scratch_shapes=[pltpu.VMEM((tq, tq), F32), pltpu.VMEM((tq, tq), F32),
                        pltpu.VMEM((tq, 1), F32), pltpu.VMEM((tq, 2 * HD), F32)],
        compiler_params=_cparams(("parallel", "parallel", "arbitrary")),
        name="fox_prompt",
    )(qa.reshape(B, T, 2 * W), kt, va.reshape(B, T, 2 * W))
    return out.reshape(B * T, W)


def _fox_sample_body(q_ref, kc_ref, vc_ref, kn_ref, vn_ref, cq_ref, ckc_ref, ckn_ref, o_ref, *, T, past):
    H, HD = FOX_HEADS, FOX_HD
    q = q_ref[0]
    kn, vn = kn_ref[0], vn_ref[0]
    ii = lax.broadcasted_iota(jnp.int32, (T, T), 0)
    jj = lax.broadcasted_iota(jnp.int32, (T, T), 1)
    outs = []
    for h in range(H):
        cs = slice(h * HD, (h + 1) * HD)
        kc = kc_ref[0, pl.ds(h, past, stride=H), :]
        vc = vc_ref[0, pl.ds(h, past, stride=H), :]
        cq = cq_ref[0, h]
        s1 = _dot_nt(q[:, cs], kc) + (cq - ckc_ref[0, h])
        s2 = _dot_nt(q[:, cs], kn[:, cs]) + (cq - ckn_ref[0, h])
        s2 = jnp.where(jj <= ii, s2, NEG_INF)
        m = jnp.maximum(jnp.max(s1, axis=-1, keepdims=True), jnp.max(s2, axis=-1, keepdims=True))
        p1 = jnp.exp(s1 - m)
        p2 = jnp.exp(s2 - m)
        den = jnp.sum(p1, axis=-1, keepdims=True) + jnp.sum(p2, axis=-1, keepdims=True)
        outs.append((_dot(p1, vc) + _dot(p2, vn[:, cs])) / den)
    o_ref[0] = jnp.concatenate(outs, axis=1)


def _fox_sample(qn, kn, vn, k_cache, v_cache, c_all, B, T):
    H, HD = FOX_HEADS, FOX_HD
    W = H * HD
    past = k_cache.shape[1]
    ct = jnp.transpose(c_all, (0, 2, 1))
    cq = ct[:, :, past:].reshape(B, H, T, 1)
    ckc = ct[:, :, :past].reshape(B, H, 1, past)
    ckn = ct[:, :, past:].reshape(B, H, 1, T)
    b3 = lambda b: (b, 0, 0)
    b4 = lambda b: (b, 0, 0, 0)
    out = pl.pallas_call(
        functools.partial(_fox_sample_body, T=T, past=past),
        grid=(B,),
        in_specs=[pl.BlockSpec((1, T, W), b3), pl.BlockSpec((1, past * H, HD), b3), pl.BlockSpec((1, past * H, HD), b3),
                  pl.BlockSpec((1, T, W), b3), pl.BlockSpec((1, T, W), b3),
                  pl.BlockSpec((1, H, T, 1), b4), pl.BlockSpec((1, H, 1, past), b4), pl.BlockSpec((1, H, 1, T), b4)],
        out_specs=pl.BlockSpec((1, T, W), b3),
        out_shape=jax.ShapeDtypeStruct((B, T, W), F32),
        compiler_params=_cparams(("parallel",)),
        name="fox_sample",
    )(qn.reshape(B, T, W), k_cache.reshape(B, past * H, HD), v_cache.reshape(B, past * H, HD),
      kn.reshape(B, T, W), vn.reshape(B, T, W), cq, ckc, ckn)
    return out.reshape(B * T, W)


def _mem_attn_body(x_ref, mk_ref, mv_ref, wq_ref, wo_ref, g_ref, b_ref, o_ref, *, prec):
    x = x_ref[...]
    q = _wdot(x, wq_ref, prec)
    nseq = mk_ref.shape[0]
    rows = x.shape[0] // nseq
    per_seq = []
    for j in range(nseq):
        qj = q[j * rows:(j + 1) * rows]
        outs = []
        for h in range(MEM_HEADS):
            cs = slice(h * MEM_HD, (h + 1) * MEM_HD)
            mkh, mvh = mk_ref[j, :, cs], mv_ref[j, :, cs]
            s = _dot_nt(qj[:, cs], mkh, prec) * (MEM_HD ** -0.5)
            m = jnp.max(s, axis=-1, keepdims=True)
            p = jnp.exp(s - m)
            outs.append(_dot(p, mvh, prec) / jnp.sum(p, axis=-1, keepdims=True))
        per_seq.append(jnp.concatenate(outs, axis=1))
    o = per_seq[0] if nseq == 1 else jnp.concatenate(per_seq, axis=0)
    att = _wdot(o, wo_ref, prec)
    y = _layer_norm(ALPHA * x + att, g_ref[...], b_ref[...])
    o_ref[...] = y


def _mem_attn_ln(x, mk, mv, l, wq, wo, g, bias, B, T, prec=False):
    D = D_MODEL
    Mm = mk.shape[1]
    tm = _tile(T, ROW_TILE)
    nT = T // tm
    nseq = MEM_SEQS if (nT == 1 and B % MEM_SEQS == 0) else 1
    row = lambda b, i: (b * nT + i, 0)
    fix = lambda b, i: (0, 0)
    mem_spec = pl.BlockSpec((nseq, Mm, D), lambda b, i: (l * (B // nseq) + b, 0, 0))
    return pl.pallas_call(
        functools.partial(_mem_attn_body, prec=prec),
        grid=(B // nseq, nT),
        in_specs=[pl.BlockSpec((nseq * tm, D), row), mem_spec, mem_spec,
                  _wspec(wq), _wspec(wo),
                  pl.BlockSpec((1, D), fix), pl.BlockSpec((1, D), fix)],
        out_specs=pl.BlockSpec((nseq * tm, D), row),
        out_shape=jax.ShapeDtypeStruct((B * T, D), F32),
        compiler_params=_cparams(("parallel", "parallel")),
        name="mem_attn_ln",
    )(x, mk, mv, wq, wo, g.reshape(1, D), bias.reshape(1, D))


def _router_body(x_ref, w_ref, b_ref, tri_ref, o_ref, cnt_ref, cnt_scr):
    x = x_ref[...]
    w = w_ref[...]
    x1, x2, x3 = _split3(x)
    w1, w2, w3 = _split3(w)
    nt = lambda a, c: lax.dot_general(a, c, _NT, preferred_element_type=F32)
    logits = (nt(w1, x1) + (nt(w1, x2) + nt(w2, x1)) + (nt(w1, x3) + nt(w2, x2) + nt(w3, x1))) + b_ref[...]
    m = jnp.max(logits, axis=0, keepdims=True)
    e = jnp.exp(logits - m)
    p = e / jnp.sum(e, axis=0, keepdims=True)
    rows = [p[j:j + 1, :] for j in range(N_EXPERTS)]
    best = None
    sel = None
    for g in range(N_GROUPS):
        a, b, c, d = rows[4 * g:4 * g + 4]
        top2 = jnp.maximum(jnp.maximum(jnp.maximum(a + b, a + c), jnp.maximum(a + d, b + c)),
                           jnp.maximum(b + d, c + d))
        if g == 0:
            best, sel = top2, jnp.zeros_like(top2, dtype=jnp.int32)
        else:
            upd = top2 > best
            sel = jnp.where(upd, g, sel)
            best = jnp.maximum(best, top2)
    pin = []
    for kk in range(EXPERTS_PER_GROUP):
        v = rows[kk]
        for g in range(1, N_GROUPS):
            v = jnp.where(sel == g, rows[4 * g + kk], v)
        pin.append(v)
    v1, i1 = pin[0], jnp.zeros_like(sel)
    for kk in range(1, EXPERTS_PER_GROUP):
        upd = pin[kk] > v1
        i1 = jnp.where(upd, kk, i1)
        v1 = jnp.maximum(v1, pin[kk])
    v2, i2 = None, None
    for kk in range(EXPERTS_PER_GROUP):
        cand = jnp.where(i1 == kk, -1.0, pin[kk])
        if v2 is None:
            v2, i2 = cand, jnp.zeros_like(sel)
        else:
            upd = cand > v2
            i2 = jnp.where(upd, kk, i2)
            v2 = jnp.maximum(v2, cand)
    tot = v1 + v2
    e1 = sel * EXPERTS_PER_GROUP + i1
    e2 = sel * EXPERTS_PER_GROUP + i2
    @pl.when(pl.program_id(0) == 0)
    def _():
        cnt_scr[...] = jnp.zeros_like(cnt_scr)

    eidx = lax.broadcasted_iota(jnp.int32, logits.shape, 0)
    oh1 = jnp.where(eidx == e1, 1.0, 0.0)
    oh2 = jnp.where(eidx == e2, 1.0, 0.0)
    oh = oh1 + oh2
    base = cnt_scr[...] + jnp.dot(oh.astype(BF16), tri_ref[...], preferred_element_type=F32)
    r1 = jnp.sum(oh1 * base, axis=0, keepdims=True)
    r2 = jnp.sum(oh2 * base, axis=0, keepdims=True)
    cnt = cnt_scr[...] + jnp.sum(oh, axis=1, keepdims=True)
    cnt_scr[...] = cnt
    cnt_ref[...] = jnp.broadcast_to(cnt, cnt_ref.shape)
    zero = jnp.zeros_like(v1)
    o_ref[...] = jnp.concatenate([e1.astype(F32), e2.astype(F32), v1 / tot, v2 / tot, r1, r2, zero, zero], axis=0)


def _router(x, w_router, b_router):
    M, D = x.shape
    tm = _tile(M, ROW_TILE)
    idx = np.arange(tm)
    tri = jnp.asarray(idx[:, None] < idx[None, :], BF16)
    r, cnt = pl.pallas_call(
        _router_body,
        grid=(M // tm,),
        in_specs=[pl.BlockSpec((tm, D), lambda i: (i, 0)),
                  pl.BlockSpec((N_EXPERTS, D), lambda i: (0, 0)),
                  pl.BlockSpec((N_EXPERTS, 1), lambda i: (0, 0)),
                  pl.BlockSpec((tm, tm), lambda i: (0, 0))],
        out_specs=[pl.BlockSpec((8, tm), lambda i: (0, i)), pl.BlockSpec((N_EXPERTS, 128), lambda i: (0, 0))],
        out_shape=[jax.ShapeDtypeStruct((8, M), F32), jax.ShapeDtypeStruct((N_EXPERTS, 128), F32)],
        scratch_shapes=[pltpu.VMEM((N_EXPERTS, 1), F32)],
        compiler_params=_cparams(("arbitrary",)),
        name="router",
    )(x, w_router.T, b_router.reshape(N_EXPERTS, 1), tri)
    return r, cnt[:, 0].astype(jnp.int32)


def _expert_body(be_ref, x_ref, wg_ref, wu_ref, wd_ref, o_ref, wg_s, wu_s, wd_s):
    i = pl.program_id(0)
    prev = be_ref[jnp.maximum(i - 1, 0)]

    @pl.when((i == 0) | (be_ref[i] != prev))
    def _():
        wg_s[...] = wg_ref[0, 0].astype(BF16)
        wu_s[...] = wu_ref[0, 0].astype(BF16)
        wd_s[...] = wd_ref[0, 0].astype(BF16)

    x = x_ref[...].astype(BF16)
    hg = jnp.dot(x, wg_s[...], preferred_element_type=F32)
    hu = jnp.dot(x, wu_s[...], preferred_element_type=F32)
    hb = (_silu(hg) * hu).astype(BF16)
    o_ref[...] = jnp.dot(hb, wd_s[...], preferred_element_type=F32)


def _experts(xg, blk_exp, wg, wu, wd, l):
    n_blocks = blk_exp.shape[0]
    blk = xg.shape[0] // n_blocks
    D, DE = D_MODEL, D_EXPERT
    grid_spec = pltpu.PrefetchScalarGridSpec(
        num_scalar_prefetch=1,
        grid=(n_blocks,),
        in_specs=[pl.BlockSpec((blk, D), lambda i, be: (i, 0)),
                  pl.BlockSpec((1, 1, D, DE), lambda i, be: (l, be[i], 0, 0)),
                  pl.BlockSpec((1, 1, D, DE), lambda i, be: (l, be[i], 0, 0)),
                  pl.BlockSpec((1, 1, DE, D), lambda i, be: (l, be[i], 0, 0))],
        out_specs=pl.BlockSpec((blk, D), lambda i, be: (i, 0)),
        scratch_shapes=[pltpu.VMEM((D, DE), BF16), pltpu.VMEM((D, DE), BF16), pltpu.VMEM((DE, D), BF16)],
    )
    return pl.pallas_call(
        _expert_body,
        grid_spec=grid_spec,
        out_shape=jax.ShapeDtypeStruct((n_blocks * blk, D), F32),
        compiler_params=_cparams(("arbitrary",)),
        name="experts",
    )(blk_exp, xg, wg, wu, wd)


def _combine_ln_body(x_ref, y0_ref, y1_ref, gt_ref, g_ref, b_ref, o_ref):
    gt = gt_ref[...]
    ffn = y0_ref[...] * gt[:, 0:1] + y1_ref[...] * gt[:, 1:2]
    o_ref[...] = _layer_norm(ALPHA * x_ref[...] + ffn, g_ref[...], b_ref[...])


def _combine_ln(x, y0, y1, gates, g, bias):
    M, D = x.shape
    tm = _tile(M, ROW_TILE)
    row = lambda i: (i, 0)
    fix = lambda i: (0, 0)
    return pl.pallas_call(
        _combine_ln_body,
        grid=(M // tm,),
        in_specs=[pl.BlockSpec((tm, D), row), pl.BlockSpec((tm, D), row), pl.BlockSpec((tm, D), row),
                  pl.BlockSpec((tm, 2), row), pl.BlockSpec((1, D), fix), pl.BlockSpec((1, D), fix)],
        out_specs=pl.BlockSpec((tm, D), row),
        out_shape=jax.ShapeDtypeStruct((M, D), F32),
        compiler_params=_cparams(("parallel",)),
        name="combine_ln",
    )(x, y0, y1, gates, g.reshape(1, D), bias.reshape(1, D))


def _moe_ln(x, P, l):
    M = x.shape[0]
    r, counts = _router(x, P['w_router'], P['b_router'])
    e = r[0:2].astype(jnp.int32).T.reshape(-1)
    gates = r[2:4].T
    rank = r[4:6].astype(jnp.int32).T.reshape(-1)
    blk = MOE_BLOCK if 2 * M >= N_EXPERTS * MOE_BLOCK else MOE_BLOCK_SMALL
    padded = (counts + blk - 1) // blk * blk
    p_ends = jnp.cumsum(padded)
    p_starts = p_ends - padded
    dest = p_starts[e] + rank
    n_blocks = -(-2 * M // blk) + N_EXPERTS
    blk_start = jnp.arange(n_blocks, dtype=jnp.int32) * blk
    blk_exp = jnp.minimum(jnp.sum((p_ends[None, :] <= blk_start[:, None]).astype(jnp.int32), axis=1), N_EXPERTS - 1)
    order = jnp.argsort(e, stable=True).astype(jnp.int32)
    starts = jnp.cumsum(counts) - counts
    slot = jnp.arange(n_blocks * blk, dtype=jnp.int32)
    slot_e = jnp.repeat(blk_exp, blk)
    r_in_e = slot - p_starts[slot_e]
    src = order[jnp.clip(starts[slot_e] + r_in_e, 0, 2 * M - 1)] // 2
    slot_tok = jnp.where(r_in_e < counts[slot_e], src, slot % M)
    xg = x[slot_tok]
    yb = _experts(xg, blk_exp, P['w_exp_gate'], P['w_exp_up'], P['w_exp_down'], l)
    d2 = dest.reshape(M, 2)
    return _combine_ln(x, yb[d2[:, 0]], yb[d2[:, 1]], gates, P['ln_g'][l, 2], P['ln_b'][l, 2])


def _trunk(x, pos0, mem_k, mem_v, states, P, Wc, Wf, is_prompt, prec0):
    B, T, D = x.shape
    ret_S, rwkv_S, shift, mC, mn, mm, conv, fk, fv, flf = states
    xf = x.reshape(B * T, D)
    W0 = Wf if prec0 else Wc
    hq, hk, hv, hg, hb = _project(xf, W0['w_in0'], (256, 256, 512, 512, RWKV_PROJ), prec0)
    L_ret = _tile(T, RET_CHUNK)
    out_a, ret_S = _retention(hq, hk, hv, hg, ret_S, P['ret_gn_g'], B, T, pos0, L_ret, prec0)
    hb3 = hb.reshape(B, T, RWKV_PROJ)
    new_shift = hb3[:, -1:]
    L_rwkv = min(RWKV_CHUNK, T)
    if T > L_rwkv:
        assert not prec0
        pre = _rwkv_prep(hb, shift, P, L_rwkv, False, seqs=(B, T))
        out_b, rwkv_S = _rwkv_scan(pre, rwkv_S, P['rwkv_gn_g'], P['rwkv_gn_b'], B, T, L_rwkv)
    else:
        prev = jnp.concatenate([shift, hb3[:, :-1]], axis=1).reshape(B * T, RWKV_PROJ)
        pre = _rwkv_prep(hb, prev, P, T, prec0)
        out_b, rwkv_S = _rwkv_single_chunk(pre, rwkv_S, P['rwkv_gn_g'], P['rwkv_gn_b'], B, T, prec0)
    xf = _out_proj_ln(xf, out_a, out_b, W0['w_out0a'], W0['w_out0b'], P['ln_g'][0, 0], P['ln_b'][0, 0], prec0)
    xf = _mem_attn_ln(xf, mem_k, mem_v, 0, W0['w_mem_q'][0], W0['w_mem_o'][0], P['ln_g'][0, 1], P['ln_b'][0, 1],
                      B, T, prec0)
    xf = _moe_ln(xf, P, 0)
    hqk, hv1, ho, fq, fkk, fvv, hgate = _project(xf, Wc['w_in1'], (512, 512, 512, 512, 512, 512, 128))
    out_c, mC, mn, mm, conv = _mlstm(hqk, hv1, ho, hgate, conv, mC, mn, mm, P, Wc['gate_bias'], B, T, L_ret)
    if is_prompt:
        qa, kn, vc, kt, va, lf = _fox_prep_prompt(fq, fkk, fvv, hgate, P, Wc['gate_bias'], B, T)
        out_d = _fox_prompt(qa, kt, va, B, T)
        logf = lf[:, 2 * MLSTM_HEADS:2 * MLSTM_HEADS + FOX_HEADS].reshape(B, T, FOX_HEADS)
    else:
        qn, kn, lf = _fox_prep(fq, fkk, hgate, P, Wc['gate_bias'])
        vc = fvv
        logf = lf[:, 2 * MLSTM_HEADS:2 * MLSTM_HEADS + FOX_HEADS].reshape(B, T, FOX_HEADS)
        c_all = jnp.cumsum(jnp.concatenate([flf, logf], axis=1), axis=1)
        out_d = _fox_sample(qn, kn, fvv, fk, fv, c_all, B, T)
    xf = _out_proj_ln(xf, out_c, out_d, Wc['w_out1a'], Wc['w_out1b'], P['ln_g'][1, 0], P['ln_b'][1, 0])
    xf = _mem_attn_ln(xf, mem_k, mem_v, 1, Wc['w_mem_q'][1], Wc['w_mem_o'][1], P['ln_g'][1, 1], P['ln_b'][1, 1],
                      B, T)
    xf = _moe_ln(xf, P, 1)
    fk_new = kn.reshape(B, T, FOX_HEADS, FOX_HD)
    fv_new = vc.reshape(B, T, FOX_HEADS, FOX_HD)
    return (xf.reshape(B, T, D), ret_S, rwkv_S, new_shift, mC, mn, mm, conv, fk_new, fv_new, logf)


def kernel(x_prompt, x_sample, mem_prompt, state_ret, state_rwkv, cache_rwkv_shift, state_mlstm_c, state_mlstm_n,
           state_mlstm_m, cache_mlstm_conv, cache_fox_k, cache_fox_v, cache_fox_logf, cache_mem_k, cache_mem_v,
           w_in0, ret_gn_g, rwkv_mu, rwkv_w0, rwkv_w2, rwkv_a0, rwkv_a2, rwkv_g2, rwkv_k_k, rwkv_k_a, rwkv_r_k,
           rwkv_gn_g, rwkv_gn_b, w_out0, w_in1, mlstm_conv_w, mlstm_conv_b, mlstm_b_i, mlstm_b_f, mlstm_gn_g,
           fox_q_g, fox_k_g, fox_b_f, w_out1, w_mem_q, w_mem_k, w_mem_v, w_mem_o, w_router, b_router,
           w_exp_gate, w_exp_up, w_exp_down, ln_g, ln_b):
    P = dict(ret_gn_g=ret_gn_g, rwkv_mu=rwkv_mu, rwkv_w0=rwkv_w0, rwkv_w2=rwkv_w2, rwkv_a0=rwkv_a0,
             rwkv_a2=rwkv_a2, rwkv_g2=rwkv_g2, rwkv_k_k=rwkv_k_k, rwkv_k_a=rwkv_k_a, rwkv_r_k=rwkv_r_k,
             rwkv_gn_g=rwkv_gn_g, rwkv_gn_b=rwkv_gn_b, mlstm_conv_w=mlstm_conv_w, mlstm_conv_b=mlstm_conv_b,
             mlstm_gn_g=mlstm_gn_g, fox_q_g=fox_q_g, fox_k_g=fox_k_g, w_router=w_router, b_router=b_router,
             w_exp_gate=w_exp_gate, w_exp_up=w_exp_up, w_exp_down=w_exp_down, ln_g=ln_g, ln_b=ln_b)
    B, M = mem_prompt.shape[0], mem_prompt.shape[1]
    D = D_MODEL
    H = MLSTM_HEADS
    ret_proj = 2 * RET_HEADS * RET_DK + 2 * RET_HEADS * RET_DV
    qk1, w1 = 2 * H * MLSTM_DK, H * MLSTM_DV
    off = qk1 + w1
    mlstm_proj = off + 2 * H + w1
    fw = FOX_HEADS * FOX_HD
    gate_cols = jnp.concatenate([w_in1[:, off:off + 2 * H], w_in1[:, mlstm_proj + 3 * fw:],
                                 jnp.zeros((D, 128 - 2 * H - FOX_HEADS), F32)], axis=1)
    w_in1_c = jnp.concatenate([w_in1[:, :off], w_in1[:, off + 2 * H:mlstm_proj],
                               w_in1[:, mlstm_proj:mlstm_proj + 3 * fw], gate_cols], axis=1)
    gate_bias = jnp.concatenate([mlstm_b_i, mlstm_b_f, fox_b_f, jnp.zeros((128 - 2 * H - FOX_HEADS,), F32)]).reshape(1, 128)
    wa0, wb0 = w_out0[:RET_HEADS * RET_DV], w_out0[RET_HEADS * RET_DV:]
    Wf = dict(w_in0=_hi_lo(w_in0), w_out0a=_hi_lo(wa0), w_out0b=_hi_lo(wb0),
              w_mem_q=[_hi_lo(w_mem_q[0])], w_mem_o=[_hi_lo(w_mem_o[0])])
    Wc = dict(w_in0=w_in0.astype(BF16), w_in1=w_in1_c.astype(BF16), gate_bias=gate_bias,
              w_out0a=wa0.astype(BF16), w_out0b=wb0.astype(BF16),
              w_out1a=w_out1[:w1].astype(BF16), w_out1b=w_out1[w1:].astype(BF16),
              w_mem_q=w_mem_q.astype(BF16), w_mem_o=w_mem_o.astype(BF16))
    w_kv = jnp.concatenate([w_mem_k, w_mem_v], axis=0)
    memkv = _mem_project(mem_prompt.reshape(B * M, D), jnp.swapaxes(_hi_lo(w_kv), 0, 1))
    p_mem_k = memkv[:DEPTH].reshape(DEPTH * B, M, D)
    p_mem_v = memkv[DEPTH:].reshape(DEPTH * B, M, D)
    zeros = lambda *s: jnp.zeros(s, F32)
    prompt_states = (zeros(B, RET_HEADS, RET_DK, RET_DV), zeros(B, RWKV_HEADS, RWKV_HD, RWKV_HD),
                     zeros(B, 1, RWKV_PROJ), zeros(B, H, MLSTM_DK, MLSTM_DV), zeros(B, H, MLSTM_DK), zeros(B, H),
                     zeros(B, MLSTM_CONV - 1, qk1), None, None, None)
    p_out = _trunk(x_prompt, 0, p_mem_k, p_mem_v, prompt_states, P, Wc, Wf, True, PROMPT_PREC0)
    DB = x_sample.shape[0]
    sample_states = (state_ret, state_rwkv, cache_rwkv_shift, state_mlstm_c, state_mlstm_n, state_mlstm_m,
                     cache_mlstm_conv, cache_fox_k, cache_fox_v, cache_fox_logf)
    s_out = _trunk(x_sample, cache_fox_k.shape[1], cache_mem_k.reshape(DEPTH * DB, M, D),
                   cache_mem_v.reshape(DEPTH * DB, M, D), sample_states, P, Wc, Wf, False, True)
    mem_shape = (DEPTH, B, M, MEM_HEADS, MEM_HD)
    return ((p_out[0], s_out[0]) + p_out[1:] + (p_mem_k.reshape(mem_shape), p_mem_v.reshape(mem_shape)) + s_out[1:])
```

```python
import functools
import math

import numpy as np
import jax
import jax.numpy as jnp
from jax import lax
from jax.experimental import pallas as pl
from jax.experimental.pallas import tpu as pltpu

F32 = jnp.float32
BF16 = jnp.bfloat16

D_MODEL = 1024
DEPTH = 2
RET_HEADS, RET_DK, RET_DV = 4, 64, 128
ROPE_BASE = 10000.0
RWKV_HEADS, RWKV_HD = 8, 64
RWKV_W = RWKV_HEADS * RWKV_HD
RWKV_PROJ = 3 * RWKV_W + 64 + 64 + 128
MLSTM_HEADS, MLSTM_DK, MLSTM_DV, MLSTM_CONV = 4, 64, 128, 4
FOX_HEADS, FOX_HD = 4, 128
MEM_HEADS, MEM_HD = 4, 256
N_EXPERTS, N_GROUPS, EXPERTS_PER_GROUP = 16, 4, 4
D_EXPERT = 512
ALPHA = (2 * DEPTH) ** 0.25
EPS = 1e-5
NEG_INF = -1e30
LOG2E = 1.4426950408889634

VMEM_LIMIT_BYTES = 56 * 1024 * 1024
ROW_TILE = 512
RET_CHUNK = 256
RWKV_CHUNK = 64
RWKV_ROWS = 512
RWKV_LOCAL_ROWS = 1024
RWKV_SEQS = 8
FOX_TQ = 1024
MEM_SEQS = 4
MOE_BLOCK = 512
MOE_BLOCK_SMALL = 128
PROMPT_PREC0 = False


def _cparams(sem):
    return pltpu.CompilerParams(dimension_semantics=sem, vmem_limit_bytes=VMEM_LIMIT_BYTES)


def _tile(n, pref):
    if n <= pref:
        return n
    t = pref
    while t >= 8:
        if n % t == 0:
            return t
        t -= 8
    return n


_NN = (((1,), (0,)), ((), ()))
_NT = (((1,), (1,)), ((), ()))
_TN = (((0,), (0,)), ((), ()))
_BNN = (((2,), (1,)), ((0,), (0,)))
_BNT = (((2,), (2,)), ((0,), (0,)))
_BTN = (((1,), (1,)), ((0,), (0,)))


def _split2(a):
    a = a.astype(F32)
    hi = a.astype(BF16)
    return hi, (a - hi.astype(F32)).astype(BF16)


def _dg(a, b, dims, prec):
    if not prec:
        return lax.dot_general(a.astype(BF16), b.astype(BF16), dims, preferred_element_type=F32)
    a1, a2 = _split2(a)
    b1, b2 = _split2(b)
    d = lambda p, q: lax.dot_general(p, q, dims, preferred_element_type=F32)
    return d(a1, b1) + (d(a1, b2) + d(a2, b1))


def _dot(a, b, prec=False):
    return _dg(a, b, _NN, prec)


def _wdot(x, w_ref, prec):
    if not prec:
        return jnp.dot(x.astype(BF16), w_ref[...], preferred_element_type=F32)
    x1, x2 = _split2(x)
    d = lambda p, q: jnp.dot(p, q, preferred_element_type=F32)
    return d(x1, w_ref[0]) + (d(x1, w_ref[1]) + d(x2, w_ref[0]))


def _wspec(w):
    return pl.BlockSpec(w.shape, lambda *_: (0,) * w.ndim)


def _hi_lo(w):
    bits = lax.bitcast_convert_type(w, jnp.uint32) & jnp.uint32(0xFFFF0000)
    hi = lax.bitcast_convert_type(bits, F32)
    return jnp.stack([hi.astype(BF16), (w - hi).astype(BF16)])


def _dot_nt(a, b, prec=False):
    return _dg(a, b, _NT, prec)


def _dot_tn(a, b, prec=False):
    return _dg(a, b, _TN, prec)


def _split3(a):
    a1 = a.astype(BF16)
    r1 = a - a1.astype(F32)
    a2 = r1.astype(BF16)
    a3 = (r1 - a2.astype(F32)).astype(BF16)
    return a1, a2, a3


def _exact_left_dot(e, a):
    a1, a2, a3 = _split3(a)
    d = lambda p: jnp.dot(e, p, preferred_element_type=F32)
    return d(a1) + d(a2) + d(a3)


def _exact_right_dot(a, e, parts=3):
    d = lambda p: jnp.dot(p, e, preferred_element_type=F32)
    return sum(d(p) for p in _split3(a)[:parts])


def _sigmoid(x):
    return 1.0 / (1.0 + jnp.exp(-x))


def _silu(x):
    return x * _sigmoid(x)


def _softplus(x):
    return jnp.maximum(x, 0.0) + jnp.log1p(jnp.exp(-jnp.abs(x)))


def _log_sigmoid(x):
    return -_softplus(-x)


def _layer_norm(z, g, b):
    mu = jnp.mean(z, axis=-1, keepdims=True)
    d = z - mu
    var = jnp.mean(d * d, axis=-1, keepdims=True)
    return d * lax.rsqrt(var + EPS) * g + b


def _lane_norm(y):
    mu = jnp.mean(y, axis=-1, keepdims=True)
    d = y - mu
    var = jnp.mean(d * d, axis=-1, keepdims=True)
    return d * lax.rsqrt(var + EPS)


def _proj_body(x_ref, w_ref, *o_refs, widths, prec):
    h = _wdot(x_ref[...], w_ref, prec)
    off = 0
    for o_ref, wd in zip(o_refs, widths):
        o_ref[...] = h[:, off:off + wd]
        off += wd


def _project(x, w, widths, prec=False):
    M, K = x.shape
    tm = _tile(M, ROW_TILE)
    return pl.pallas_call(
        functools.partial(_proj_body, widths=widths, prec=prec),
        grid=(M // tm,),
        in_specs=[pl.BlockSpec((tm, K), lambda i: (i, 0)), _wspec(w)],
        out_specs=[pl.BlockSpec((tm, wd), lambda i: (i, 0)) for wd in widths],
        out_shape=[jax.ShapeDtypeStruct((M, wd), F32) for wd in widths],
        compiler_params=_cparams(("parallel",)),
        name="project",
    )(x, w)


def _mem_proj_body(x_ref, w_ref, o_ref):
    o_ref[0] = _wdot(x_ref[...], w_ref.at[0], True)


def _mem_project(x, w):
    M, K = x.shape
    J, _, _, N = w.shape
    return pl.pallas_call(
        _mem_proj_body,
        grid=(J,),
        in_specs=[pl.BlockSpec((M, K), lambda j: (0, 0)),
                  pl.BlockSpec((1, 2, K, N), lambda j: (j, 0, 0, 0))],
        out_specs=pl.BlockSpec((1, M, N), lambda j: (j, 0, 0)),
        out_shape=jax.ShapeDtypeStruct((J, M, N), F32),
        compiler_params=_cparams(("parallel",)),
        name="mem_project",
    )(x, w)


def _out_ln_body(x_ref, a_ref, b_ref, wa_ref, wb_ref, g_ref, bias_ref, o_ref, *, prec):
    mix = _wdot(a_ref[...], wa_ref, prec) + _wdot(b_ref[...], wb_ref, prec)
    o_ref[...] = _layer_norm(ALPHA * x_ref[...] + mix, g_ref[...], bias_ref[...])


def _out_proj_ln(x, a, b, wa, wb, g, bias, prec=False):
    M, D = x.shape
    Ka, Kb = a.shape[1], b.shape[1]
    tm = _tile(M, ROW_TILE)
    row = lambda i: (i, 0)
    fix = lambda i: (0, 0)
    return pl.pallas_call(
        functools.partial(_out_ln_body, prec=prec),
        grid=(M // tm,),
        in_specs=[pl.BlockSpec((tm, D), row), pl.BlockSpec((tm, Ka), row), pl.BlockSpec((tm, Kb), row),
                  _wspec(wa), _wspec(wb),
                  pl.BlockSpec((1, D), fix), pl.BlockSpec((1, D), fix)],
        out_specs=pl.BlockSpec((tm, D), row),
        out_shape=jax.ShapeDtypeStruct((M, D), F32),
        compiler_params=_cparams(("parallel",)),
        name="out_proj_ln",
    )(x, a, b, wa, wb, g.reshape(1, D), bias.reshape(1, D))


def _ret_body(q_ref, k_ref, v_ref, g_ref, cos_ref, sa_ref, sb_ref, dmat_ref, qdec_ref, kdec_ref, sdec_ref,
              s0_ref, gn_ref, o_ref, sout_ref, s_scr, *, prec):
    i = pl.program_id(1)

    @pl.when(i == 0)
    def _():
        s_scr[...] = s0_ref[0]

    cos, sin_a, sin_b = cos_ref[...], sa_ref[...], sb_ref[...]
    width = RET_HEADS * RET_DK
    half = RET_DK // 2

    def rope(x):
        return x * cos + pltpu.roll(x, width - half, 1) * sin_a + pltpu.roll(x, half, 1) * sin_b

    q = rope(q_ref[...])
    k = rope(k_ref[...]) * (RET_DK ** -0.5)
    v = v_ref[...]
    gate = g_ref[...]
    H = RET_HEADS
    qh = jnp.stack([q[:, h * RET_DK:(h + 1) * RET_DK] for h in range(H)], axis=0)
    kh = jnp.stack([k[:, h * RET_DK:(h + 1) * RET_DK] for h in range(H)], axis=0)
    vh = jnp.stack([v[:, h * RET_DV:(h + 1) * RET_DV] for h in range(H)], axis=0)
    s_old = s_scr[...]
    s = _dg(qh, kh, _BNT, prec) * dmat_ref[...]
    o = _dg(s, vh, _BNN, prec) + _dg(qh, s_old, _BNN, prec) * qdec_ref[...]
    s_scr[...] = sdec_ref[...] * s_old + _dg(kh * kdec_ref[...], vh, _BTN, prec)
    on = _lane_norm(o)
    hn = jnp.concatenate([on[h] for h in range(H)], axis=1) * gn_ref[...]
    o_ref[...] = _silu(gate) * hn

    @pl.when(i == pl.num_programs(1) - 1)
    def _():
        sout_ref[0] = s_scr[...]


def _retention(hq, hk, hv, hg, s0, gn_g, B, T, pos0, L, prec=False):
    nT = T // L
    lg = np.log(1.0 - 2.0 ** (-5.0 - np.arange(RET_HEADS)))
    idx = np.arange(L, dtype=np.float64)
    rel = idx[:, None] - idx[None, :]
    dmat = np.where(rel >= 0, np.exp(np.maximum(rel, 0.0)[None] * lg[:, None, None]), 0.0)
    qdec = np.exp((idx + 1.0)[None, :, None] * lg[:, None, None])
    kdec = np.exp((L - 1.0 - idx)[None, :, None] * lg[:, None, None])
    sdec = np.exp(L * lg)[:, None, None]
    half = RET_DK // 2
    inv = ROPE_BASE ** (-jnp.arange(half, dtype=F32) / half)
    ang = (pos0 + jnp.arange(T)).astype(F32)[:, None] * inv[None, :]
    cos, sin = jnp.cos(ang), jnp.sin(ang)
    zero = jnp.zeros_like(sin)
    cos_t = jnp.tile(jnp.concatenate([cos, cos], axis=1), (1, RET_HEADS))
    sin_a = jnp.tile(jnp.concatenate([-sin, zero], axis=1), (1, RET_HEADS))
    sin_b = jnp.tile(jnp.concatenate([zero, sin], axis=1), (1, RET_HEADS))
    qk_w = RET_HEADS * RET_DK
    v_w = RET_HEADS * RET_DV
    row = lambda b, i: (b * nT + i, 0)
    tab = lambda b, i: (i, 0)
    fix3 = lambda b, i: (0, 0, 0)
    out, s_out = pl.pallas_call(
        functools.partial(_ret_body, prec=prec),
        grid=(B, nT),
        in_specs=[pl.BlockSpec((L, qk_w), row), pl.BlockSpec((L, qk_w), row),
                  pl.BlockSpec((L, v_w), row), pl.BlockSpec((L, v_w), row),
                  pl.BlockSpec((L, qk_w), tab), pl.BlockSpec((L, qk_w), tab), pl.BlockSpec((L, qk_w), tab),
                  pl.BlockSpec((RET_HEADS, L, L), fix3), pl.BlockSpec((RET_HEADS, L, 1), fix3),
                  pl.BlockSpec((RET_HEADS, L, 1), fix3), pl.BlockSpec((RET_HEADS, 1, 1), fix3),
                  pl.BlockSpec((1, RET_HEADS, RET_DK, RET_DV), lambda b, i: (b, 0, 0, 0)),
                  pl.BlockSpec((1, v_w), lambda b, i: (0, 0))],
        out_specs=[pl.BlockSpec((L, v_w), row),
                   pl.BlockSpec((1, RET_HEADS, RET_DK, RET_DV), lambda b, i: (b, 0, 0, 0))],
        out_shape=[jax.ShapeDtypeStruct((B * T, v_w), F32),
                   jax.ShapeDtypeStruct((B, RET_HEADS, RET_DK, RET_DV), F32)],
        scratch_shapes=[pltpu.VMEM((RET_HEADS, RET_DK, RET_DV), F32)],
        compiler_params=_cparams(("parallel", "arbitrary")),
        name="retention",
    )(hq, hk, hv, hg, cos_t, sin_a, sin_b, jnp.asarray(dmat, F32), jnp.asarray(qdec, F32),
      jnp.asarray(kdec, F32), jnp.asarray(sdec, F32), s0, gn_g.reshape(1, v_w))
    return out, s_out


def _rwkv_prep_body(h_ref, p_ref, mu_ref, w0_ref, w2_ref, a0_ref, a2_ref, g2_ref, kk_ref, ka_ref, rk_ref,
                    ones_ref, tri_ref, blk_ref, sel_ref,
                    kq_ref, rq_ref, kt_ref, bt_ref, ke_ref, be_ref, v_ref, g_ref, bon_ref, gt_ref, *scratch,
                    prec, shift_rows):
    hb = h_ref[...]
    if shift_rows:
        carry, = scratch
        tm = hb.shape[0]
        first = jnp.where(pl.program_id(1) == 0, p_ref[0], carry[...])
        row = lax.broadcasted_iota(jnp.int32, (tm, 1), 0)
        prev = jnp.where(row == 0, first, pltpu.roll(hb, 1, 0))
        carry[...] = hb[tm - 1:tm, :]
    else:
        prev = p_ref[...]
    xs = hb + (prev - hb) * mu_ref[...]
    W = RWKV_W
    r, k, v = xs[:, :W], xs[:, W:2 * W], xs[:, 2 * W:3 * W]
    w_lo = xs[:, 3 * W:3 * W + 64]
    a_lo = xs[:, 3 * W + 64:3 * W + 128]
    g_lo = xs[:, 3 * W + 128:]
    w_log = -_softplus(-(w0_ref[...] + _dot(jnp.tanh(w_lo), w2_ref[...], prec))) - 0.5
    lw = -jnp.exp(w_log)
    a = _sigmoid(a0_ref[...] + _dot(a_lo, a2_ref[...], prec))
    g = _dot(_sigmoid(g_lo), g2_ref[...], prec)
    ones = ones_ref[...]
    kk = k * kk_ref[...]
    nrm = jnp.sqrt(_exact_right_dot(kk * kk, ones, 2 if not prec else 3))
    kk = kk / jnp.maximum(nrm, 1e-12)
    k2 = k * (1.0 + (a - 1.0) * ka_ref[...])
    beta = kk * a
    cl = _exact_left_dot(tri_ref[...], lw)
    tot = _exact_left_dot(blk_ref[...], lw)
    ginv = jnp.exp(-cl)
    gend = jnp.exp(tot - cl)
    kq_ref[...] = (kk * jnp.exp(cl - lw)).astype(kq_ref.dtype)
    rq_ref[...] = (r * jnp.exp(cl)).astype(rq_ref.dtype)
    kt_ref[...] = (k2 * ginv).astype(kt_ref.dtype)
    bt_ref[...] = (beta * ginv).astype(bt_ref.dtype)
    ke_ref[...] = (k2 * gend).astype(ke_ref.dtype)
    be_ref[...] = (beta * gend).astype(be_ref.dtype)
    v_ref[...] = v.astype(v_ref.dtype)
    gt_ref[...] = jnp.exp(_exact_left_dot(sel_ref[...], lw))
    g_ref[...] = g
    bon_ref[...] = _exact_right_dot(r * k2 * rk_ref[...], ones, 2 if not prec else 3) * v


def _rwkv_prep(hb, prev, P, L, prec=False, seqs=None):
    M = hb.shape[0]
    tm = _tile(M if seqs is None else seqs[1], ROW_TILE)
    assert tm % L == 0
    W = RWKV_W
    idx = np.arange(tm)
    same = (idx[:, None] // L) == (idx[None, :] // L)
    tri = jnp.asarray(same & (idx[:, None] >= idx[None, :]), BF16)
    blk = jnp.asarray(same, BF16)
    sel = jnp.asarray((idx[None, :] // L) == np.arange(tm // L)[:, None], BF16)
    lane = np.arange(W)
    ones = jnp.asarray((lane[:, None] // RWKV_HD) == (lane[None, :] // RWKV_HD), BF16)
    if seqs is None:
        grid, nT = (1, M // tm), M // tm
        prev_spec = pl.BlockSpec((tm, RWKV_PROJ), lambda b, i: (i, 0))
        scratch = []
    else:
        grid, nT = (seqs[0], seqs[1] // tm), seqs[1] // tm
        prev_spec = pl.BlockSpec((1, 1, RWKV_PROJ), lambda b, i: (b, 0, 0))
        scratch = [pltpu.VMEM((1, RWKV_PROJ), F32)]
    row = lambda b, i: (b * nT + i, 0)
    vec = lambda a: a.reshape(1, -1)
    params = [vec(P['rwkv_mu']), vec(P['rwkv_w0']), P['rwkv_w2'], vec(P['rwkv_a0']),
              P['rwkv_a2'], P['rwkv_g2'], vec(P['rwkv_k_k']), vec(P['rwkv_k_a']),
              vec(P['rwkv_r_k']), ones, tri, blk, sel]
    out_dt = [F32 if prec else BF16] * 7 + [F32] * 2
    return pl.pallas_call(
        functools.partial(_rwkv_prep_body, prec=prec, shift_rows=seqs is not None),
        grid=grid,
        in_specs=[pl.BlockSpec((tm, RWKV_PROJ), row), prev_spec] + [_wspec(a) for a in params],
        out_specs=[pl.BlockSpec((tm, W), row) for _ in out_dt] + [pl.BlockSpec((tm // L, W), row)],
        out_shape=[jax.ShapeDtypeStruct((M, W), dt) for dt in out_dt] + [jax.ShapeDtypeStruct((M // L, W), F32)],
        scratch_shapes=scratch,
        compiler_params=_cparams(("parallel", "arbitrary")),
        name="rwkv_prep",
    )(hb, prev, *params)


def _rwkv_masks(L):
    ii = lax.broadcasted_iota(jnp.int32, (L, L), 0)
    jj = lax.broadcasted_iota(jnp.int32, (L, L), 1)
    return ii > jj, ii >= jj, (ii == jj).astype(F32)


def _rwkv_local(kq, rq, kt, bt, ke, be, v, masks, lmask_ref, nlev, prec):
    strict, incl, eye = masks
    L = kq.shape[1]
    x = jnp.concatenate([kq, rq], axis=1)
    sk = _dg(x, kt, _BNT, prec)
    sb = _dg(x, bt, _BNT, prec)
    n_m = jnp.where(strict, sk[:, :L], 0.0)
    a_k = jnp.where(incl, sk[:, L:], 0.0)
    m_m = jnp.where(strict, sb[:, :L], 0.0)
    a_b = jnp.where(incl, sb[:, L:], 0.0)
    t_m = eye - m_m * lmask_ref[0]
    for lv in range(1, nlev):
        c_m = m_m * lmask_ref[lv]
        t_m = t_m - _dg(_dg(t_m, c_m, _BNN, prec), t_m, _BNN, prec)
    kqp = _dg(t_m, kq, _BNN, prec)
    u0 = _dg(t_m, _dg(n_m, v, _BNN, prec), _BNN, prec)
    rqp = rq.astype(F32) - _dg(a_b, kqp, _BNN, prec)
    y0 = _dg(a_k, v, _BNN, prec) - _dg(a_b, u0, _BNN, prec)
    p_m = _dg(kqp, be, _BTN, prec)
    b_c = _dg(v, ke, _BTN, prec) - _dg(u0, be, _BTN, prec)
    return rqp, y0, p_m, b_c


def _rwkv_apply(s_old, rqp, y0, p_m, b_c, g_end, prec):
    y = _dg(rqp, s_old, _BNT, prec) + y0
    return y, s_old * g_end - _dg(s_old, p_m, _BNN, prec) + b_c


def _head_stack(tile, nblk):
    hd = RWKV_HD
    L = tile.shape[0] // nblk
    return jnp.concatenate([tile[:, hh * hd:(hh + 1) * hd].reshape(nblk, L, hd) for hh in range(2)], axis=0)


def _head_unstack(x, nblk):
    L, hd = x.shape[1], x.shape[2]
    return jnp.concatenate([x[hh * nblk:(hh + 1) * nblk].reshape(nblk * L, hd) for hh in range(2)], axis=1)


def _rwkv_local_body(kq_ref, rq_ref, kt_ref, bt_ref, ke_ref, be_ref, v_ref, lmask_ref,
                     rqp_ref, y0_ref, pm_ref, bc_ref, *, L, nchunk, nlev):
    ops = [_head_stack(r[...], nchunk) for r in (kq_ref, rq_ref, kt_ref, bt_ref, ke_ref, be_ref, v_ref)]
    res = _rwkv_local(*ops, _rwkv_masks(L), lmask_ref, nlev, False)
    for o_ref, a in zip((rqp_ref, y0_ref, pm_ref, bc_ref), res):
        o_ref[...] = _head_unstack(a, nchunk).astype(o_ref.dtype)


def _rwkv_apply_body(rqp_ref, y0_ref, pm_ref, bc_ref, gt_ref, g_ref, bon_ref, s0_ref, gng_ref, gnb_ref,
                     o_ref, sout_ref, s_scr, *, L, nchunk, B):
    i = pl.program_id(0)
    H, hd = RWKV_HEADS, RWKV_HD

    @pl.when(i == 0)
    def _():
        s_scr[...] = s0_ref[...].reshape(B * H, hd, hd)

    def heads(x):
        return jnp.stack([x[b][:, h * hd:(h + 1) * hd] for b in range(B) for h in range(H)], axis=0)

    def chunk(c, carry):
        rows = pl.ds(pl.multiple_of(c * L, L), L)
        ops = [heads(r[:, rows, :]) for r in (rqp_ref, y0_ref, pm_ref, bc_ref)]
        g_end = heads(gt_ref[:, pl.ds(c, 1), :])
        y, s_new = _rwkv_apply(s_scr[...], *ops, g_end, False)
        s_scr[...] = s_new
        yn = _lane_norm(y)
        for b in range(B):
            ynb = jnp.concatenate([yn[b * H + h] for h in range(H)], axis=1)
            o_ref[b, rows, :] = (ynb * gng_ref[...] + gnb_ref[...] + bon_ref[b, rows, :]) * g_ref[b, rows, :]
        return carry

    lax.fori_loop(0, nchunk, chunk, 0)

    @pl.when(i == pl.num_programs(0) - 1)
    def _():
        sout_ref[...] = s_scr[...].reshape(B, H, hd, hd)


def _rwkv_fused_body(kq_ref, rq_ref, kt_ref, bt_ref, ke_ref, be_ref, v_ref, g_ref, bon_ref, gt_ref,
                     s0_ref, gng_ref, gnb_ref, lmask_ref, o_ref, sout_ref, *, L, nseq, nlev, prec):
    hd = RWKV_HD
    ops = [_head_stack(r[...], nseq) for r in (kq_ref, rq_ref, kt_ref, bt_ref, ke_ref, be_ref, v_ref)]
    loc = _rwkv_local(*ops, _rwkv_masks(L), lmask_ref, nlev, prec)
    gt = gt_ref[...]
    g_end = jnp.concatenate([gt[:, :, hh * hd:(hh + 1) * hd] for hh in range(2)], axis=0)
    s_old = jnp.concatenate([s0_ref[:, hh] for hh in range(2)], axis=0)
    y, s_new = _rwkv_apply(s_old, *loc, g_end, prec)
    for hh in range(2):
        sout_ref[:, hh] = s_new[hh * nseq:(hh + 1) * nseq]
    yn = _head_unstack(_lane_norm(y), nseq)
    o_ref[...] = (yn * gng_ref[...] + gnb_ref[...] + bon_ref[...]) * g_ref[...]


def _rwkv_level_masks(L):
    nlev = int(math.log2(L))
    idx = np.arange(L)
    ii, jj = idx[:, None], idx[None, :]
    lmask = np.stack([((ii >> (lv + 1)) == (jj >> (lv + 1))) & ((ii & (1 << lv)) != 0) & ((jj & (1 << lv)) == 0)
                      for lv in range(nlev)]).astype(np.float32)
    return nlev, jnp.asarray(lmask)


def _rwkv_scan(pre, s0, gn_g, gn_b, B, T, L):
    W, H, hd = RWKV_W, RWKV_HEADS, RWKV_HD
    assert L == hd
    kq, rq, kt, bt, ke, be, v, g, bon, gt = pre
    nlev, lmask = _rwkv_level_masks(L)
    tl = _tile(T, RWKV_LOCAL_ROWS)
    nL = T // tl
    pw = 2 * hd
    row = lambda b, p, i: (b * nL + i, p)
    rqp, y0, p_m, b_c = pl.pallas_call(
        functools.partial(_rwkv_local_body, L=L, nchunk=tl // L, nlev=nlev),
        grid=(B, H // 2, nL),
        in_specs=[pl.BlockSpec((tl, pw), row) for _ in range(7)]
                 + [pl.BlockSpec((nlev, L, L), lambda b, p, i: (0, 0, 0))],
        out_specs=[pl.BlockSpec((tl, pw), row) for _ in range(4)],
        out_shape=[jax.ShapeDtypeStruct((B * T, W), dt) for dt in (BF16, F32, BF16, F32)],
        compiler_params=_cparams(("parallel", "parallel", "parallel")),
        name="rwkv_local",
    )(kq, rq, kt, bt, ke, be, v, lmask)
    tb = _tile(T, RWKV_ROWS)
    nT = T // tb
    nchunk = tb // L
    r3 = lambda a: a.reshape(B, T, W)
    blk = lambda i: (0, i, 0)
    fix2 = lambda i: (0, 0)
    fix4 = lambda i: (0, 0, 0, 0)
    out, s_out = pl.pallas_call(
        functools.partial(_rwkv_apply_body, L=L, nchunk=nchunk, B=B),
        grid=(nT,),
        in_specs=[pl.BlockSpec((B, tb, W), blk) for _ in range(4)]
                 + [pl.BlockSpec((B, nchunk, W), blk), pl.BlockSpec((B, tb, W), blk), pl.BlockSpec((B, tb, W), blk),
                    pl.BlockSpec((B, H, hd, hd), fix4), pl.BlockSpec((1, W), fix2), pl.BlockSpec((1, W), fix2)],
        out_specs=[pl.BlockSpec((B, tb, W), blk), pl.BlockSpec((B, H, hd, hd), fix4)],
        out_shape=[jax.ShapeDtypeStruct((B, T, W), F32), jax.ShapeDtypeStruct((B, H, hd, hd), F32)],
        scratch_shapes=[pltpu.VMEM((B * H, hd, hd), F32)],
        compiler_params=_cparams(("arbitrary",)),
        name="rwkv_apply",
    )(r3(rqp), r3(y0), r3(p_m), r3(b_c), gt.reshape(B, T // L, W), r3(g), r3(bon), s0,
      gn_g.reshape(1, W), gn_b.reshape(1, W))
    return out.reshape(B * T, W), s_out


def _rwkv_single_chunk(pre, s0, gn_g, gn_b, B, T, prec):
    W, H, hd = RWKV_W, RWKV_HEADS, RWKV_HD
    kq, rq, kt, bt, ke, be, v, g, bon, gt = pre
    nlev, lmask = _rwkv_level_masks(T)
    nseq = RWKV_SEQS if B % RWKV_SEQS == 0 else B
    pw = 2 * hd
    row = lambda i, p: (i, p)
    st = lambda i, p: (i, p, 0, 0)
    vec = lambda i, p: (0, p)
    out, s_out = pl.pallas_call(
        functools.partial(_rwkv_fused_body, L=T, nseq=nseq, nlev=nlev, prec=prec),
        grid=(B // nseq, H // 2),
        in_specs=[pl.BlockSpec((nseq * T, pw), row) for _ in range(9)]
                 + [pl.BlockSpec((nseq, 1, pw), lambda i, p: (i, 0, p)), pl.BlockSpec((nseq, 2, hd, hd), st),
                    pl.BlockSpec((1, pw), vec), pl.BlockSpec((1, pw), vec),
                    pl.BlockSpec((nlev, T, T), lambda i, p: (0, 0, 0))],
        out_specs=[pl.BlockSpec((nseq * T, pw), row), pl.BlockSpec((nseq, 2, hd, hd), st)],
        out_shape=[jax.ShapeDtypeStruct((B * T, W), F32), jax.ShapeDtypeStruct((B, H, hd, hd), F32)],
        compiler_params=_cparams(("parallel", "parallel")),
        name="rwkv_single_chunk",
    )(kq, rq, kt, bt, ke, be, v, g, bon, gt.reshape(B, 1, W), s0, gn_g.reshape(1, W), gn_b.reshape(1, W), lmask)
    return out, s_out


def _mlstm_body(qk_ref, v_ref, o_ref, gt_ref, cprev_ref, c0_ref, n0_ref, m0_ref, cw_ref, cb_ref, gb_ref,
                gn_ref, tri_ref, out_ref, cout_ref, nout_ref, mout_ref, convout_ref,
                xpad, c_scr, m_scr, *, L):
    i = pl.program_id(1)
    H, DK, DV = MLSTM_HEADS, MLSTM_DK, MLSTM_DV
    K = MLSTM_CONV - 1
    base = 8 - K

    @pl.when(i == 0)
    def _():
        xpad[base:8, :] = cprev_ref[0]
        for h in range(H):
            c_scr[h, :, 0:DV] = c0_ref[0, h]
            c_scr[h, :, DV:2 * DV] = jnp.broadcast_to(n0_ref[0, h], (DK, DV))
        m_scr[...] = m0_ref[0]

    xpad[8:8 + L, :] = qk_ref[...]
    conv = cb_ref[...] + xpad[pl.ds(base, L), :] * cw_ref[0:1, :]
    for j in range(1, MLSTM_CONV):
        conv = conv + xpad[pl.ds(base + j, L), :] * cw_ref[j:j + 1, :]
    tail = xpad[pl.ds(8 + L - K, K), :]
    xpad[base:8, :] = tail
    qk = _silu(conv)
    q = qk[:, :H * DK] * (DK ** -0.5)
    k = qk[:, H * DK:]
    v = v_ref[...]
    z = gt_ref[...] + gb_ref[...]
    lf = _log_sigmoid(z)
    bcum = _exact_left_dot(tri_ref[...], lf)
    z_t = z.T
    b_t = bcum.T
    ii = lax.broadcasted_iota(jnp.int32, (L, L), 0)
    jj = lax.broadcasted_iota(jnp.int32, (L, L), 1)
    causal = ii >= jj
    ones = jnp.ones((L, DV), F32)
    m_all = m_scr[...]
    lane = lax.broadcasted_iota(jnp.int32, (1, 128), 1)
    stack = lambda f: jnp.stack([f(h) for h in range(H)], axis=0)
    qh = stack(lambda h: q[:, h * DK:(h + 1) * DK])
    kh = stack(lambda h: k[:, h * DK:(h + 1) * DK])
    vh = stack(lambda h: jnp.concatenate([v[:, h * DV:(h + 1) * DV], ones], axis=1))
    b_col = stack(lambda h: bcum[:, H + h:H + h + 1])
    ig_col = stack(lambda h: z[:, h:h + 1])
    row_term = stack(lambda h: z_t[h:h + 1, :] - b_t[H + h:H + h + 1, :])
    m0 = stack(lambda h: m_all[:, h:h + 1])
    log_d = jnp.where(causal, b_col + row_term, NEG_INF)
    m_inter = b_col + m0
    m_t = jnp.maximum(m_inter, jnp.max(log_d, axis=-1, keepdims=True))
    w_d = jnp.exp(log_d - m_t)
    w_i = jnp.exp(m_inter - m_t)
    c_old = c_scr[...]
    s = _dg(qh, kh, _BNT, False) * w_d
    num = _dg(s, vh, _BNN, False) + _dg(qh, c_old, _BNN, False) * w_i
    den = num[:, :, DV:DV + 1]
    denom = jnp.maximum(jnp.abs(den), jnp.exp(-m_t))
    hn = _lane_norm(num[:, :, :DV] / denom)
    m_new = m_t[:, L - 1:L, :]
    b_last = b_col[:, L - 1:L, :]
    w_s = jnp.exp(b_last + m0 - m_new)
    w_k = jnp.exp(b_last - b_col + ig_col - m_new)
    c_scr[...] = w_s * c_old + _dg(kh * w_k, vh, _BTN, False)
    m_new_all = m_all
    for h in range(H):
        m_new_all = jnp.where(lane == h, m_new[h], m_new_all)
    m_scr[...] = m_new_all
    out_ref[...] = jnp.concatenate([hn[h] for h in range(H)], axis=1) * gn_ref[...] * _sigmoid(o_ref[...])

    @pl.when(i == pl.num_programs(1) - 1)
    def _():
        for h in range(H):
            cout_ref[0, h] = c_scr[h, :, 0:DV]
            nout_ref[0, h] = c_scr[h, :, DV:DV + 1]
        mout_ref[0] = m_scr[...]
        convout_ref[0] = tail


def _mlstm(hqk, hv, ho, hgate, conv_prev, c0, n0, m0, P, gate_bias, B, T, L):
    H, DK, DV = MLSTM_HEADS, MLSTM_DK, MLSTM_DV
    nT = T // L
    K = MLSTM_CONV - 1
    W = H * DV
    idx = np.arange(L)
    tri = jnp.asarray(idx[:, None] >= idx[None, :], BF16)
    m0p = jnp.zeros((B, 1, 128), F32).at[:, 0, :H].set(m0)
    row = lambda b, i: (b * nT + i, 0)
    fix = lambda b, i: (0, 0)
    perb3 = lambda b, i: (b, 0, 0)
    perb4 = lambda b, i: (b, 0, 0, 0)
    out, c_out, n_out, m_out, conv_out = pl.pallas_call(
        functools.partial(_mlstm_body, L=L),
        grid=(B, nT),
        in_specs=[pl.BlockSpec((L, W), row), pl.BlockSpec((L, W), row), pl.BlockSpec((L, W), row),
                  pl.BlockSpec((L, 128), row),
                  pl.BlockSpec((1, K, W), perb3),
                  pl.BlockSpec((1, H, DK, DV), perb4), pl.BlockSpec((1, H, DK, 1), perb4),
                  pl.BlockSpec((1, 1, 128), perb3),
                  pl.BlockSpec((MLSTM_CONV, W), fix), pl.BlockSpec((1, W), fix), pl.BlockSpec((1, 128), fix),
                  pl.BlockSpec((1, W), fix), pl.BlockSpec((L, L), fix)],
        out_specs=[pl.BlockSpec((L, W), row),
                   pl.BlockSpec((1, H, DK, DV), perb4), pl.BlockSpec((1, H, DK, 1), perb4),
                   pl.BlockSpec((1, 1, 128), perb3), pl.BlockSpec((1, K, W), perb3)],
        out_shape=[jax.ShapeDtypeStruct((B * T, W), F32),
                   jax.ShapeDtypeStruct((B, H, DK, DV), F32), jax.ShapeDtypeStruct((B, H, DK, 1), F32),
                   jax.ShapeDtypeStruct((B, 1, 128), F32), jax.ShapeDtypeStruct((B, K, W), F32)],
        scratch_shapes=[pltpu.VMEM((L + 8, W), F32), pltpu.VMEM((H, DK, 2 * DV), F32), pltpu.VMEM((1, 128), F32)],
        compiler_params=_cparams(("parallel", "arbitrary")),
        name="mlstm",
    )(hqk, hv, ho, hgate, conv_prev, c0, n0.reshape(B, H, DK, 1), m0p,
      P['mlstm_conv_w'], P['mlstm_conv_b'].reshape(1, W), gate_bias, P['mlstm_gn_g'].reshape(1, W), tri)
    return out, c_out, n_out.reshape(B, H, DK), m_out[:, 0, :H], conv_out


def _fox_prep_body(q_ref, k_ref, gt_ref, qg_ref, kg_ref, gb_ref, qn_ref, kn_ref, lf_ref):
    def rms(x, g):
        outs = []
        for h in range(FOX_HEADS):
            xh = x[:, h * FOX_HD:(h + 1) * FOX_HD]
            outs.append(xh * lax.rsqrt(jnp.mean(xh * xh, axis=-1, keepdims=True) + EPS) * g)
        return jnp.concatenate(outs, axis=1)

    qn_ref[...] = (rms(q_ref[...], qg_ref[...]) * (FOX_HD ** -0.5)).astype(BF16)
    kn_ref[...] = rms(k_ref[...], kg_ref[...])
    lf_ref[...] = _log_sigmoid(gt_ref[...] + gb_ref[...])


def _fox_prep(hq, hk, hgate, P, gate_bias):
    M, W = hq.shape
    tm = _tile(M, ROW_TILE)
    row = lambda i: (i, 0)
    fix = lambda i: (0, 0)
    return pl.pallas_call(
        _fox_prep_body,
        grid=(M // tm,),
        in_specs=[pl.BlockSpec((tm, W), row), pl.BlockSpec((tm, W), row), pl.BlockSpec((tm, 128), row),
                  pl.BlockSpec((1, FOX_HD), fix), pl.BlockSpec((1, FOX_HD), fix), pl.BlockSpec((1, 128), fix)],
        out_specs=[pl.BlockSpec((tm, W), row), pl.BlockSpec((tm, W), row), pl.BlockSpec((tm, 128), row)],
        out_shape=[jax.ShapeDtypeStruct((M, W), BF16), jax.ShapeDtypeStruct((M, W), F32),
                   jax.ShapeDtypeStruct((M, 128), F32)],
        compiler_params=_cparams(("parallel",)),
        name="fox_prep",
    )(hq, hk, hgate, P['fox_q_g'].reshape(1, FOX_HD), P['fox_k_g'].reshape(1, FOX_HD), gate_bias)


def _fox_prep_prompt_body(q_ref, k_ref, v_ref, gt_ref, qg_ref, kg_ref, gb_ref, tri_ref,
                          qa_ref, kn_ref, vc_ref, kt_ref, va_ref, lf_ref, carry):
    i = pl.program_id(1)
    tm = q_ref.shape[0]
    HD = FOX_HD

    @pl.when(i == 0)
    def _():
        carry[...] = jnp.zeros_like(carry)

    lf = _log_sigmoid(gt_ref[...] + gb_ref[...])
    lf_ref[...] = lf
    c = _exact_left_dot(tri_ref[...], lf) + carry[...]
    carry[...] = c[tm - 1:tm, :]
    c2t = (c * (-LOG2E)).T
    lane = lax.broadcasted_iota(jnp.int32, (tm, HD), 1)
    q_ones = jnp.where(lane < 3, 1.0, 0.0).astype(BF16)
    v_ones = jnp.where(lane < 1, 1.0, 0.0).astype(BF16)
    row16 = lax.broadcasted_iota(jnp.int32, (16, tm), 0)
    q, k, v = q_ref[...], k_ref[...], v_ref[...]
    for h in range(FOX_HEADS):
        hs = slice(h * HD, (h + 1) * HD)
        qh, kh = q[:, hs], k[:, hs]
        qh = qh * lax.rsqrt(jnp.mean(qh * qh, axis=-1, keepdims=True) + EPS) * qg_ref[...]
        kh = kh * lax.rsqrt(jnp.mean(kh * kh, axis=-1, keepdims=True) + EPS) * kg_ref[...]
        qa_ref[:, 2 * h * HD:(2 * h + 1) * HD] = (qh * (HD ** -0.5 * LOG2E)).astype(BF16)
        qa_ref[:, (2 * h + 1) * HD:(2 * h + 2) * HD] = q_ones
        kn_ref[pl.ds(h, tm, stride=FOX_HEADS), :] = kh
        vc_ref[pl.ds(h, tm, stride=FOX_HEADS), :] = v[:, hs]
        kt_ref[0, h, 0, 0:HD, :] = kh.T.astype(BF16)
        bias = c2t[2 * MLSTM_HEADS + h:2 * MLSTM_HEADS + h + 1, :]
        hi = bias.astype(BF16).astype(F32)
        mid = (bias - hi).astype(BF16).astype(F32)
        lo = bias - hi - mid
        blk = jnp.where(row16 == 0, hi, jnp.where(row16 == 1, mid, jnp.where(row16 == 2, lo, 0.0)))
        kt_ref[0, h, 0, HD:HD + 16, :] = blk.astype(BF16)
        kt_ref[0, h, 0, HD + 16:2 * HD, :] = jnp.zeros((HD - 16, tm), BF16)
        va_ref[:, 2 * h * HD:(2 * h + 1) * HD] = v[:, hs].astype(BF16)
        va_ref[:, (2 * h + 1) * HD:(2 * h + 2) * HD] = v_ones


def _fox_prep_prompt(hq, hk, hv, hgate, P, gate_bias, B, T):
    M, W = hq.shape
    H, HD = FOX_HEADS, FOX_HD
    tm = _tile(T, FOX_TQ)
    nT = T // tm
    idx = np.arange(tm)
    tri = jnp.asarray(idx[:, None] >= idx[None, :], BF16)
    row = lambda b, i: (b * nT + i, 0)
    fix = lambda b, i: (0, 0)
    return pl.pallas_call(
        _fox_prep_prompt_body,
        grid=(B, nT),
        in_specs=[pl.BlockSpec((tm, W), row), pl.BlockSpec((tm, W), row), pl.BlockSpec((tm, W), row),
                  pl.BlockSpec((tm, 128), row),
                  pl.BlockSpec((1, HD), fix), pl.BlockSpec((1, HD), fix), pl.BlockSpec((1, 128), fix),
                  pl.BlockSpec((tm, tm), fix)],
        out_specs=[pl.BlockSpec((tm, 2 * W), row), pl.BlockSpec((tm * H, HD), row), pl.BlockSpec((tm * H, HD), row),
                   pl.BlockSpec((1, H, 1, 2 * HD, tm), lambda b, i: (b, 0, i, 0, 0)),
                   pl.BlockSpec((tm, 2 * W), row), pl.BlockSpec((tm, 128), row)],
        out_shape=[jax.ShapeDtypeStruct((M, 2 * W), BF16), jax.ShapeDtypeStruct((M * H, HD), F32),
                   jax.ShapeDtypeStruct((M * H, HD), F32),
                   jax.ShapeDtypeStruct((B, H, nT, 2 * HD, tm), BF16),
                   jax.ShapeDtypeStruct((M, 2 * W), BF16), jax.ShapeDtypeStruct((M, 128), F32)],
        scratch_shapes=[pltpu.VMEM((1, 128), F32)],
        compiler_params=_cparams(("parallel", "arbitrary")),
        name="fox_prep_prompt",
    )(hq, hk, hv, hgate, P['fox_q_g'].reshape(1, HD), P['fox_k_g'].reshape(1, HD), gate_bias, tri)


def _fox_prompt_body(q_ref, kt_ref, va_ref, o_ref, sa_scr, sb_scr, m_scr, acc_scr, *, tq):
    qi = pl.program_id(2)
    nq = pl.num_programs(2)
    m_scr[...] = jnp.full_like(m_scr, NEG_INF)
    acc_scr[...] = jnp.zeros_like(acc_scr)

    def scores(s_ref, kj, qt=None):
        q0 = pl.multiple_of((qi if qt is None else qt) * tq, tq)
        s_ref[...] = jnp.dot(q_ref[0, pl.ds(q0, tq), :], kt_ref[0, 0, kj], preferred_element_type=F32)

    def update(s_ref, kj, masked):
        vb = va_ref[0, pl.ds(pl.multiple_of(kj * tq, tq), tq), :]
        s = s_ref[...]
        if masked:
            ii = lax.broadcasted_iota(jnp.int32, (tq, tq), 0)
            jj = lax.broadcasted_iota(jnp.int32, (tq, tq), 1)
            s = jnp.where(jj <= ii, s, NEG_INF)
        m_old = m_scr[...]
        m_new = jnp.maximum(m_old, jnp.max(s, axis=-1, keepdims=True))
        p = jnp.exp2(s - m_new)
        acc_scr[...] = jnp.exp2(m_old - m_new) * acc_scr[...] + jnp.dot(p.astype(BF16), vb, preferred_element_type=F32)
        m_scr[...] = m_new

    @pl.when(qi == 0)
    def _():
        scores(sa_scr, 0)

    nxt = jnp.minimum(qi + 1, nq - 1)

    def run(first, second):
        def body(j, carry):
            scores(second, 2 * j + 1)
            update(first, 2 * j, False)
            scores(first, 2 * j + 2)
            update(second, 2 * j + 1, False)
            return carry

        lax.fori_loop(0, lax.shift_right_logical(qi, 1), body, 0)

        @pl.when((qi & 1) == 0)
        def _():
            scores(second, 0, nxt)
            update(first, qi, True)

        @pl.when((qi & 1) == 1)
        def _():
            scores(second, qi)
            update(first, qi - 1, False)
            scores(first, 0, nxt)
            update(second, qi, True)

    starts_in_a = (lax.shift_right_logical(qi + 1, 1) & 1) == 0

    @pl.when(starts_in_a)
    def _():
        run(sa_scr, sb_scr)

    @pl.when(jnp.logical_not(starts_in_a))
    def _():
        run(sb_scr, sa_scr)

    acc = acc_scr[...]
    o_ref[0] = acc[:, :FOX_HD] / acc[:, FOX_HD:FOX_HD + 1]


def _fox_prompt(qa, kt, va, B, T):
    H, HD = FOX_HEADS, FOX_HD
    W = H * HD
    tq = kt.shape[-1]
    nQ = T // tq
    out = pl.pallas_call(
        functools.partial(_fox_prompt_body, tq=tq),
        grid=(B, H, nQ),
        in_specs=[pl.BlockSpec((1, T, 2 * HD), lambda b, h, i: (b, 0, h), pipeline_mode=pl.Buffered(1)),
                  pl.BlockSpec((1, 1, nQ, 2 * HD, tq), lambda b, h, i: (b, h, 0, 0, 0),
                               pipeline_mode=pl.Buffered(1)),
                  pl.BlockSpec((1, T, 2 * HD), lambda b, h, i: (b, 0, h), pipeline_mode=pl.Buffered(1))],
        out_specs=pl.BlockSpec((1, tq, HD), lambda b, h, i: (b, i, h)),
        out_shape=jax.ShapeDtypeStruct((B, T, W), F32),
        scratch_shapes=[pltpu.VMEM((tq, tq), F32), pltpu.VMEM((tq, tq), F32),
                        pltpu.VMEM((tq, 1), F32), pltpu.VMEM((tq, 2 * HD), F32)],
        compiler_params=_cparams(("parallel", "parallel", "arbitrary")),
        name="fox_prompt",
    )(qa.reshape(B, T, 2 * W), kt, va.reshape(B, T, 2 * W))
    return out.reshape(B * T, W)


def _fox_sample_body(q_ref, kc_ref, vc_ref, kn_ref, vn_ref, cq_ref, ckc_ref, ckn_ref, o_ref, *, T, past):
    H, HD = FOX_HEADS, FOX_HD
    q = q_ref[0]
    kn, vn = kn_ref[0], vn_ref[0]
    ii = lax.broadcasted_iota(jnp.int32, (T, T), 0)
    jj = lax.broadcasted_iota(jnp.int32, (T, T), 1)
    outs = []
    for h in range(H):
        cs = slice(h * HD, (h + 1) * HD)
        kc = kc_ref[0, pl.ds(h, past, stride=H), :]
        vc = vc_ref[0, pl.ds(h, past, stride=H), :]
        cq = cq_ref[0, h]
        s1 = _dot_nt(q[:, cs], kc) + (cq - ckc_ref[0, h])
        s2 = _dot_nt(q[:, cs], kn[:, cs]) + (cq - ckn_ref[0, h])
        s2 = jnp.where(jj <= ii, s2, NEG_INF)
        m = jnp.maximum(jnp.max(s1, axis=-1, keepdims=True), jnp.max(s2, axis=-1, keepdims=True))
        p1 = jnp.exp(s1 - m)
        p2 = jnp.exp(s2 - m)
        den = jnp.sum(p1, axis=-1, keepdims=True) + jnp.sum(p2, axis=-1, keepdims=True)
        outs.append((_dot(p1, vc) + _dot(p2, vn[:, cs])) / den)
    o_ref[0] = jnp.concatenate(outs, axis=1)


def _fox_sample(qn, kn, vn, k_cache, v_cache, c_all, B, T):
    H, HD = FOX_HEADS, FOX_HD
    W = H * HD
    past = k_cache.shape[1]
    ct = jnp.transpose(c_all, (0, 2, 1))
    cq = ct[:, :, past:].reshape(B, H, T, 1)
    ckc = ct[:, :, :past].reshape(B, H, 1, past)
    ckn = ct[:, :, past:].reshape(B, H, 1, T)
    b3 = lambda b: (b, 0, 0)
    b4 = lambda b: (b, 0, 0, 0)
    out = pl.pallas_call(
        functools.partial(_fox_sample_body, T=T, past=past),
        grid=(B,),
        in_specs=[pl.BlockSpec((1, T, W), b3), pl.BlockSpec((1, past * H, HD), b3), pl.BlockSpec((1, past * H, HD), b3),
                  pl.BlockSpec((1, T, W), b3), pl.BlockSpec((1, T, W), b3),
                  pl.BlockSpec((1, H, T, 1), b4), pl.BlockSpec((1, H, 1, past), b4), pl.BlockSpec((1, H, 1, T), b4)],
        out_specs=pl.BlockSpec((1, T, W), b3),
        out_shape=jax.ShapeDtypeStruct((B, T, W), F32),
        compiler_params=_cparams(("parallel",)),
        name="fox_sample",
    )(qn.reshape(B, T, W), k_cache.reshape(B, past * H, HD), v_cache.reshape(B, past * H, HD),
      kn.reshape(B, T, W), vn.reshape(B, T, W), cq, ckc, ckn)
    return out.reshape(B * T, W)


def _mem_attn_body(x_ref, mk_ref, mv_ref, wq_ref, wo_ref, g_ref, b_ref, o_ref, *, prec):
    x = x_ref[...]
    q = _wdot(x, wq_ref, prec)
    nseq = mk_ref.shape[0]
    rows = x.shape[0] // nseq
    per_seq = []
    for j in range(nseq):
        qj = q[j * rows:(j + 1) * rows]
        outs = []
        for h in range(MEM_HEADS):
            cs = slice(h * MEM_HD, (h + 1) * MEM_HD)
            mkh, mvh = mk_ref[j, :, cs], mv_ref[j, :, cs]
            s = _dot_nt(qj[:, cs], mkh, prec) * (MEM_HD ** -0.5)
            m = jnp.max(s, axis=-1, keepdims=True)
            p = jnp.exp(s - m)
            outs.append(_dot(p, mvh, prec) / jnp.sum(p, axis=-1, keepdims=True))
        per_seq.append(jnp.concatenate(outs, axis=1))
    o = per_seq[0] if nseq == 1 else jnp.concatenate(per_seq, axis=0)
    att = _wdot(o, wo_ref, prec)
    y = _layer_norm(ALPHA * x + att, g_ref[...], b_ref[...])
    o_ref[...] = y


def _mem_attn_ln(x, mk, mv, l, wq, wo, g, bias, B, T, prec=False):
    D = D_MODEL
    Mm = mk.shape[1]
    tm = _tile(T, ROW_TILE)
    nT = T // tm
    nseq = MEM_SEQS if (nT == 1 and B % MEM_SEQS == 0) else 1
    row = lambda b, i: (b * nT + i, 0)
    fix = lambda b, i: (0, 0)
    mem_spec = pl.BlockSpec((nseq, Mm, D), lambda b, i: (l * (B // nseq) + b, 0, 0))
    return pl.pallas_call(
        functools.partial(_mem_attn_body, prec=prec),
        grid=(B // nseq, nT),
        in_specs=[pl.BlockSpec((nseq * tm, D), row), mem_spec, mem_spec,
                  _wspec(wq), _wspec(wo),
                  pl.BlockSpec((1, D), fix), pl.BlockSpec((1, D), fix)],
        out_specs=pl.BlockSpec((nseq * tm, D), row),
        out_shape=jax.ShapeDtypeStruct((B * T, D), F32),
        compiler_params=_cparams(("parallel", "parallel")),
        name="mem_attn_ln",
    )(x, mk, mv, wq, wo, g.reshape(1, D), bias.reshape(1, D))


def _router_body(x_ref, w_ref, b_ref, tri_ref, o_ref, cnt_ref, cnt_scr):
    x = x_ref[...]
    w = w_ref[...]
    x1, x2, x3 = _split3(x)
    w1, w2, w3 = _split3(w)
    nt = lambda a, c: lax.dot_general(a, c, _NT, preferred_element_type=F32)
    logits = (nt(w1, x1) + (nt(w1, x2) + nt(w2, x1)) + (nt(w1, x3) + nt(w2, x2) + nt(w3, x1))) + b_ref[...]
    m = jnp.max(logits, axis=0, keepdims=True)
    e = jnp.exp(logits - m)
    p = e / jnp.sum(e, axis=0, keepdims=True)
    rows = [p[j:j + 1, :] for j in range(N_EXPERTS)]
    best = None
    sel = None
    for g in range(N_GROUPS):
        a, b, c, d = rows[4 * g:4 * g + 4]
        top2 = jnp.maximum(jnp.maximum(jnp.maximum(a + b, a + c), jnp.maximum(a + d, b + c)),
                           jnp.maximum(b + d, c + d))
        if g == 0:
            best, sel = top2, jnp.zeros_like(top2, dtype=jnp.int32)
        else:
            upd = top2 > best
            sel = jnp.where(upd, g, sel)
            best = jnp.maximum(best, top2)
    pin = []
    for kk in range(EXPERTS_PER_GROUP):
        v = rows[kk]
        for g in range(1, N_GROUPS):
            v = jnp.where(sel == g, rows[4 * g + kk], v)
        pin.append(v)
    v1, i1 = pin[0], jnp.zeros_like(sel)
    for kk in range(1, EXPERTS_PER_GROUP):
        upd = pin[kk] > v1
        i1 = jnp.where(upd, kk, i1)
        v1 = jnp.maximum(v1, pin[kk])
    v2, i2 = None, None
    for kk in range(EXPERTS_PER_GROUP):
        cand = jnp.where(i1 == kk, -1.0, pin[kk])
        if v2 is None:
            v2, i2 = cand, jnp.zeros_like(sel)
        else:
            upd = cand > v2
            i2 = jnp.where(upd, kk, i2)
            v2 = jnp.maximum(v2, cand)
    tot = v1 + v2
    e1 = sel * EXPERTS_PER_GROUP + i1
    e2 = sel * EXPERTS_PER_GROUP + i2
    @pl.when(pl.program_id(0) == 0)
    def _():
        cnt_scr[...] = jnp.zeros_like(cnt_scr)

    eidx = lax.broadcasted_iota(jnp.int32, logits.shape, 0)
    oh1 = jnp.where(eidx == e1, 1.0, 0.0)
    oh2 = jnp.where(eidx == e2, 1.0, 0.0)
    oh = oh1 + oh2
    base = cnt_scr[...] + jnp.dot(oh.astype(BF16), tri_ref[...], preferred_element_type=F32)
    r1 = jnp.sum(oh1 * base, axis=0, keepdims=True)
    r2 = jnp.sum(oh2 * base, axis=0, keepdims=True)
    cnt = cnt_scr[...] + jnp.sum(oh, axis=1, keepdims=True)
    cnt_scr[...] = cnt
    cnt_ref[...] = jnp.broadcast_to(cnt, cnt_ref.shape)
    zero = jnp.zeros_like(v1)
    o_ref[...] = jnp.concatenate([e1.astype(F32), e2.astype(F32), v1 / tot, v2 / tot, r1, r2, zero, zero], axis=0)


def _router(x, w_router, b_router):
    M, D = x.shape
    tm = _tile(M, ROW_TILE)
    idx = np.arange(tm)
    tri = jnp.asarray(idx[:, None] < idx[None, :], BF16)
    r, cnt = pl.pallas_call(
        _router_body,
        grid=(M // tm,),
        in_specs=[pl.BlockSpec((tm, D), lambda i: (i, 0)),
                  pl.BlockSpec((N_EXPERTS, D), lambda i: (0, 0)),
                  pl.BlockSpec((N_EXPERTS, 1), lambda i: (0, 0)),
                  pl.BlockSpec((tm, tm), lambda i: (0, 0))],
        out_specs=[pl.BlockSpec((8, tm), lambda i: (0, i)), pl.BlockSpec((N_EXPERTS, 128), lambda i: (0, 0))],
        out_shape=[jax.ShapeDtypeStruct((8, M), F32), jax.ShapeDtypeStruct((N_EXPERTS, 128), F32)],
        scratch_shapes=[pltpu.VMEM((N_EXPERTS, 1), F32)],
        compiler_params=_cparams(("arbitrary",)),
        name="router",
    )(x, w_router.T, b_router.reshape(N_EXPERTS, 1), tri)
    return r, cnt[:, 0].astype(jnp.int32)


def _expert_body(be_ref, x_ref, wg_ref, wu_ref, wd_ref, o_ref, wg_s, wu_s, wd_s):
    i = pl.program_id(0)
    prev = be_ref[jnp.maximum(i - 1, 0)]

    @pl.when((i == 0) | (be_ref[i] != prev))
    def _():
        wg_s[...] = wg_ref[0, 0].astype(BF16)
        wu_s[...] = wu_ref[0, 0].astype(BF16)
        wd_s[...] = wd_ref[0, 0].astype(BF16)

    x = x_ref[...].astype(BF16)
    hg = jnp.dot(x, wg_s[...], preferred_element_type=F32)
    hu = jnp.dot(x, wu_s[...], preferred_element_type=F32)
    hb = (_silu(hg) * hu).astype(BF16)
    o_ref[...] = jnp.dot(hb, wd_s[...], preferred_element_type=F32)


def _experts(xg, blk_exp, wg, wu, wd, l):
    n_blocks = blk_exp.shape[0]
    blk = xg.shape[0] // n_blocks
    D, DE = D_MODEL, D_EXPERT
    grid_spec = pltpu.PrefetchScalarGridSpec(
        num_scalar_prefetch=1,
        grid=(n_blocks,),
        in_specs=[pl.BlockSpec((blk, D), lambda i, be: (i, 0)),
                  pl.BlockSpec((1, 1, D, DE), lambda i, be: (l, be[i], 0, 0)),
                  pl.BlockSpec((1, 1, D, DE), lambda i, be: (l, be[i], 0, 0)),
                  pl.BlockSpec((1, 1, DE, D), lambda i, be: (l, be[i], 0, 0))],
        out_specs=pl.BlockSpec((blk, D), lambda i, be: (i, 0)),
        scratch_shapes=[pltpu.VMEM((D, DE), BF16), pltpu.VMEM((D, DE), BF16), pltpu.VMEM((DE, D), BF16)],
    )
    return pl.pallas_call(
        _expert_body,
        grid_spec=grid_spec,
        out_shape=jax.ShapeDtypeStruct((n_blocks * blk, D), F32),
        compiler_params=_cparams(("arbitrary",)),
        name="experts",
    )(blk_exp, xg, wg, wu, wd)


def _combine_ln_body(x_ref, y0_ref, y1_ref, gt_ref, g_ref, b_ref, o_ref):
    gt = gt_ref[...]
    ffn = y0_ref[...] * gt[:, 0:1] + y1_ref[...] * gt[:, 1:2]
    o_ref[...] = _layer_norm(ALPHA * x_ref[...] + ffn, g_ref[...], b_ref[...])


def _combine_ln(x, y0, y1, gates, g, bias):
    M, D = x.shape
    tm = _tile(M, ROW_TILE)
    row = lambda i: (i, 0)
    fix = lambda i: (0, 0)
    return pl.pallas_call(
        _combine_ln_body,
        grid=(M // tm,),
        in_specs=[pl.BlockSpec((tm, D), row), pl.BlockSpec((tm, D), row), pl.BlockSpec((tm, D), row),
                  pl.BlockSpec((tm, 2), row), pl.BlockSpec((1, D), fix), pl.BlockSpec((1, D), fix)],
        out_specs=pl.BlockSpec((tm, D), row),
        out_shape=jax.ShapeDtypeStruct((M, D), F32),
        compiler_params=_cparams(("parallel",)),
        name="combine_ln",
    )(x, y0, y1, gates, g.reshape(1, D), bias.reshape(1, D))


def _moe_ln(x, P, l):
    M = x.shape[0]
    r, counts = _router(x, P['w_router'], P['b_router'])
    e = r[0:2].astype(jnp.int32).T.reshape(-1)
    gates = r[2:4].T
    rank = r[4:6].astype(jnp.int32).T.reshape(-1)
    blk = MOE_BLOCK if 2 * M >= N_EXPERTS * MOE_BLOCK else MOE_BLOCK_SMALL
    padded = (counts + blk - 1) // blk * blk
    p_ends = jnp.cumsum(padded)
    p_starts = p_ends - padded
    dest = p_starts[e] + rank
    n_blocks = -(-2 * M // blk) + N_EXPERTS
    blk_start = jnp.arange(n_blocks, dtype=jnp.int32) * blk
    blk_exp = jnp.minimum(jnp.sum((p_ends[None, :] <= blk_start[:, None]).astype(jnp.int32), axis=1), N_EXPERTS - 1)
    order = jnp.argsort(e, stable=True).astype(jnp.int32)
    starts = jnp.cumsum(counts) - counts
    slot = jnp.arange(n_blocks * blk, dtype=jnp.int32)
    slot_e = jnp.repeat(blk_exp, blk)
    r_in_e = slot - p_starts[slot_e]
    src = order[jnp.clip(starts[slot_e] + r_in_e, 0, 2 * M - 1)] // 2
    slot_tok = jnp.where(r_in_e < counts[slot_e], src, slot % M)
    xg = x[slot_tok]
    yb = _experts(xg, blk_exp, P['w_exp_gate'], P['w_exp_up'], P['w_exp_down'], l)
    d2 = dest.reshape(M, 2)
    return _combine_ln(x, yb[d2[:, 0]], yb[d2[:, 1]], gates, P['ln_g'][l, 2], P['ln_b'][l, 2])


def _trunk(x, pos0, mem_k, mem_v, states, P, Wc, Wf, is_prompt, prec0):
    B, T, D = x.shape
    ret_S, rwkv_S, shift, mC, mn, mm, conv, fk, fv, flf = states
    xf = x.reshape(B * T, D)
    W0 = Wf if prec0 else Wc
    hq, hk, hv, hg, hb = _project(xf, W0['w_in0'], (256, 256, 512, 512, RWKV_PROJ), prec0)
    L_ret = _tile(T, RET_CHUNK)
    out_a, ret_S = _retention(hq, hk, hv, hg, ret_S, P['ret_gn_g'], B, T, pos0, L_ret, prec0)
    hb3 = hb.reshape(B, T, RWKV_PROJ)
    new_shift = hb3[:, -1:]
    L_rwkv = min(RWKV_CHUNK, T)
    if T > L_rwkv:
        assert not prec0
        pre = _rwkv_prep(hb, shift, P, L_rwkv, False, seqs=(B, T))
        out_b, rwkv_S = _rwkv_scan(pre, rwkv_S, P['rwkv_gn_g'], P['rwkv_gn_b'], B, T, L_rwkv)
    else:
        prev = jnp.concatenate([shift, hb3[:, :-1]], axis=1).reshape(B * T, RWKV_PROJ)
        pre = _rwkv_prep(hb, prev, P, T, prec0)
        out_b, rwkv_S = _rwkv_single_chunk(pre, rwkv_S, P['rwkv_gn_g'], P['rwkv_gn_b'], B, T, prec0)
    xf = _out_proj_ln(xf, out_a, out_b, W0['w_out0a'], W0['w_out0b'], P['ln_g'][0, 0], P['ln_b'][0, 0], prec0)
    xf = _mem_attn_ln(xf, mem_k, mem_v, 0, W0['w_mem_q'][0], W0['w_mem_o'][0], P['ln_g'][0, 1], P['ln_b'][0, 1],
                      B, T, prec0)
    xf = _moe_ln(xf, P, 0)
    hqk, hv1, ho, fq, fkk, fvv, hgate = _project(xf, Wc['w_in1'], (512, 512, 512, 512, 512, 512, 128))
    out_c, mC, mn, mm, conv = _mlstm(hqk, hv1, ho, hgate, conv, mC, mn, mm, P, Wc['gate_bias'], B, T, L_ret)
    if is_prompt:
        qa, kn, vc, kt, va, lf = _fox_prep_prompt(fq, fkk, fvv, hgate, P, Wc['gate_bias'], B, T)
        out_d = _fox_prompt(qa, kt, va, B, T)
        logf = lf[:, 2 * MLSTM_HEADS:2 * MLSTM_HEADS + FOX_HEADS].reshape(B, T, FOX_HEADS)
    else:
        qn, kn, lf = _fox_prep(fq, fkk, hgate, P, Wc['gate_bias'])
        vc = fvv
        logf = lf[:, 2 * MLSTM_HEADS:2 * MLSTM_HEADS + FOX_HEADS].reshape(B, T, FOX_HEADS)
        c_all = jnp.cumsum(jnp.concatenate([flf, logf], axis=1), axis=1)
        out_d = _fox_sample(qn, kn, fvv, fk, fv, c_all, B, T)
    xf = _out_proj_ln(xf, out_c, out_d, Wc['w_out1a'], Wc['w_out1b'], P['ln_g'][1, 0], P['ln_b'][1, 0])
    xf = _mem_attn_ln(xf, mem_k, mem_v, 1, Wc['w_mem_q'][1], Wc['w_mem_o'][1], P['ln_g'][1, 1], P['ln_b'][1, 1],
                      B, T)
    xf = _moe_ln(xf, P, 1)
    fk_new = kn.reshape(B, T, FOX_HEADS, FOX_HD)
    fv_new = vc.reshape(B, T, FOX_HEADS, FOX_HD)
    return (xf.reshape(B, T, D), ret_S, rwkv_S, new_shift, mC, mn, mm, conv, fk_new, fv_new, logf)


def kernel(x_prompt, x_sample, mem_prompt, state_ret, state_rwkv, cache_rwkv_shift, state_mlstm_c, state_mlstm_n,
           state_mlstm_m, cache_mlstm_conv, cache_fox_k, cache_fox_v, cache_fox_logf, cache_mem_k, cache_mem_v,
           w_in0, ret_gn_g, rwkv_mu, rwkv_w0, rwkv_w2, rwkv_a0, rwkv_a2, rwkv_g2, rwkv_k_k, rwkv_k_a, rwkv_r_k,
           rwkv_gn_g, rwkv_gn_b, w_out0, w_in1, mlstm_conv_w, mlstm_conv_b, mlstm_b_i, mlstm_b_f, mlstm_gn_g,
           fox_q_g, fox_k_g, fox_b_f, w_out1, w_mem_q, w_mem_k, w_mem_v, w_mem_o, w_router, b_router,
           w_exp_gate, w_exp_up, w_exp_down, ln_g, ln_b):
    P = dict(ret_gn_g=ret_gn_g, rwkv_mu=rwkv_mu, rwkv_w0=rwkv_w0, rwkv_w2=rwkv_w2, rwkv_a0=rwkv_a0,
             rwkv_a2=rwkv_a2, rwkv_g2=rwkv_g2, rwkv_k_k=rwkv_k_k, rwkv_k_a=rwkv_k_a, rwkv_r_k=rwkv_r_k,
             rwkv_gn_g=rwkv_gn_g, rwkv_gn_b=rwkv_gn_b, mlstm_conv_w=mlstm_conv_w, mlstm_conv_b=mlstm_conv_b,
             mlstm_gn_g=mlstm_gn_g, fox_q_g=fox_q_g, fox_k_g=fox_k_g, w_router=w_router, b_router=b_router,
             w_exp_gate=w_exp_gate, w_exp_up=w_exp_up, w_exp_down=w_exp_down, ln_g=ln_g, ln_b=ln_b)
    B, M = mem_prompt.shape[0], mem_prompt.shape[1]
    D = D_MODEL
    H = MLSTM_HEADS
    ret_proj = 2 * RET_HEADS * RET_DK + 2 * RET_HEADS * RET_DV
    qk1, w1 = 2 * H * MLSTM_DK, H * MLSTM_DV
    off = qk1 + w1
    mlstm_proj = off + 2 * H + w1
    fw = FOX_HEADS * FOX_HD
    gate_cols = jnp.concatenate([w_in1[:, off:off + 2 * H], w_in1[:, mlstm_proj + 3 * fw:],
                                 jnp.zeros((D, 128 - 2 * H - FOX_HEADS), F32)], axis=1)
    w_in1_c = jnp.concatenate([w_in1[:, :off], w_in1[:, off + 2 * H:mlstm_proj],
                               w_in1[:, mlstm_proj:mlstm_proj + 3 * fw], gate_cols], axis=1)
    gate_bias = jnp.concatenate([mlstm_b_i, mlstm_b_f, fox_b_f, jnp.zeros((128 - 2 * H - FOX_HEADS,), F32)]).reshape(1, 128)
    wa0, wb0 = w_out0[:RET_HEADS * RET_DV], w_out0[RET_HEADS * RET_DV:]
    Wf = dict(w_in0=_hi_lo(w_in0), w_out0a=_hi_lo(wa0), w_out0b=_hi_lo(wb0),
              w_mem_q=[_hi_lo(w_mem_q[0])], w_mem_o=[_hi_lo(w_mem_o[0])])
    Wc = dict(w_in0=w_in0.astype(BF16), w_in1=w_in1_c.astype(BF16), gate_bias=gate_bias,
              w_out0a=wa0.astype(BF16), w_out0b=wb0.astype(BF16),
              w_out1a=w_out1[:w1].astype(BF16), w_out1b=w_out1[w1:].astype(BF16),
              w_mem_q=w_mem_q.astype(BF16), w_mem_o=w_mem_o.astype(BF16))
    w_kv = jnp.concatenate([w_mem_k, w_mem_v], axis=0)
    memkv = _mem_project(mem_prompt.reshape(B * M, D), jnp.swapaxes(_hi_lo(w_kv), 0, 1))
    p_mem_k = memkv[:DEPTH].reshape(DEPTH * B, M, D)
    p_mem_v = memkv[DEPTH:].reshape(DEPTH * B, M, D)
    zeros = lambda *s: jnp.zeros(s, F32)
    prompt_states = (zeros(B, RET_HEADS, RET_DK, RET_DV), zeros(B, RWKV_HEADS, RWKV_HD, RWKV_HD),
                     zeros(B, 1, RWKV_PROJ), zeros(B, H, MLSTM_DK, MLSTM_DV), zeros(B, H, MLSTM_DK), zeros(B, H),
                     zeros(B, MLSTM_CONV - 1, qk1), None, None, None)
    p_out = _trunk(x_prompt, 0, p_mem_k, p_mem_v, prompt_states, P, Wc, Wf, True, PROMPT_PREC0)
    DB = x_sample.shape[0]
    sample_states = (state_ret, state_rwkv, cache_rwkv_shift, state_mlstm_c, state_mlstm_n, state_mlstm_m,
                     cache_mlstm_conv, cache_fox_k, cache_fox_v, cache_fox_logf)
    s_out = _trunk(x_sample, cache_fox_k.shape[1], cache_mem_k.reshape(DEPTH * DB, M, D),
                   cache_mem_v.reshape(DEPTH * DB, M, D), sample_states, P, Wc, Wf, False, True)
    mem_shape = (DEPTH, B, M, MEM_HEADS, MEM_HD)
    return ((p_out[0], s_out[0]) + p_out[1:] + (p_mem_k.reshape(mem_shape), p_mem_v.reshape(mem_shape)) + s_out[1:])
```

```python
import functools
import math

import numpy as np
import jax
import jax.numpy as jnp
from jax import lax
from jax.experimental import pallas as pl
from jax.experimental.pallas import tpu as pltpu

F32 = jnp.float32
BF16 = jnp.bfloat16

D_MODEL = 1024
DEPTH = 2
RET_HEADS, RET_DK, RET_DV = 4, 64, 128
ROPE_BASE = 10000.0
RWKV_HEADS, RWKV_HD = 8, 64
RWKV_W = RWKV_HEADS * RWKV_HD
RWKV_PROJ = 3 * RWKV_W + 64 + 64 + 128
MLSTM_HEADS, MLSTM_DK, MLSTM_DV, MLSTM_CONV = 4, 64, 128, 4
FOX_HEADS, FOX_HD = 4, 128
MEM_HEADS, MEM_HD = 4, 256
N_EXPERTS, N_GROUPS, EXPERTS_PER_GROUP = 16, 4, 4
D_EXPERT = 512
ALPHA = (2 * DEPTH) ** 0.25
EPS = 1e-5
NEG_INF = -1e30
LOG2E = 1.4426950408889634

VMEM_LIMIT_BYTES = 56 * 1024 * 1024
ROW_TILE = 512
RET_CHUNK = 256
RWKV_CHUNK = 64
RWKV_ROWS = 512
RWKV_LOCAL_ROWS = 1024
RWKV_SEQS = 8
FOX_TQ = 1024
MEM_SEQS = 4
MOE_BLOCK = 512
MOE_BLOCK_SMALL = 128


def _cparams(sem):
    return pltpu.CompilerParams(dimension_semantics=sem, vmem_limit_bytes=VMEM_LIMIT_BYTES)


def _tile(n, pref):
    if n <= pref:
        return n
    t = pref
    while t >= 8:
        if n % t == 0:
            return t
        t -= 8
    return n


_NN = (((1,), (0,)), ((), ()))
_NT = (((1,), (1,)), ((), ()))
_TN = (((0,), (0,)), ((), ()))
_BNN = (((2,), (1,)), ((0,), (0,)))
_BNT = (((2,), (2,)), ((0,), (0,)))
_BTN = (((1,), (1,)), ((0,), (0,)))


def _split2(a):
    a = a.astype(F32)
    hi = a.astype(BF16)
    return hi, (a - hi.astype(F32)).astype(BF16)


def _dg(a, b, dims, prec):
    if not prec:
        return lax.dot_general(a.astype(BF16), b.astype(BF16), dims, preferred_element_type=F32)
    a1, a2 = _split2(a)
    b1, b2 = _split2(b)
    d = lambda p, q: lax.dot_general(p, q, dims, preferred_element_type=F32)
    return d(a1, b1) + (d(a1, b2) + d(a2, b1))


def _dot(a, b, prec=False):
    return _dg(a, b, _NN, prec)


def _wdot(x, w_ref, prec):
    if not prec:
        return jnp.dot(x.astype(BF16), w_ref[...], preferred_element_type=F32)
    x1, x2 = _split2(x)
    d = lambda p, q: jnp.dot(p, q, preferred_element_type=F32)
    return d(x1, w_ref[0]) + (d(x1, w_ref[1]) + d(x2, w_ref[0]))


def _wspec(w):
    return pl.BlockSpec(w.shape, lambda *_: (0,) * w.ndim)


def _hi_lo(w):
    bits = lax.bitcast_convert_type(w, jnp.uint32) & jnp.uint32(0xFFFF0000)
    hi = lax.bitcast_convert_type(bits, F32)
    return jnp.stack([hi.astype(BF16), (w - hi).astype(BF16)])


def _dot_nt(a, b, prec=False):
    return _dg(a, b, _NT, prec)


def _dot_tn(a, b, prec=False):
    return _dg(a, b, _TN, prec)


def _split3(a):
    a1 = a.astype(BF16)
    r1 = a - a1.astype(F32)
    a2 = r1.astype(BF16)
    a3 = (r1 - a2.astype(F32)).astype(BF16)
    return a1, a2, a3


def _exact_left_dot(e, a):
    a1, a2, a3 = _split3(a)
    d = lambda p: jnp.dot(e, p, preferred_element_type=F32)
    return d(a1) + d(a2) + d(a3)


def _exact_right_dot(a, e, parts=3):
    d = lambda p: jnp.dot(p, e, preferred_element_type=F32)
    return sum(d(p) for p in _split3(a)[:parts])


def _sigmoid(x):
    return 1.0 / (1.0 + jnp.exp(-x))


def _silu(x):
    return x * _sigmoid(x)


def _softplus(x):
    return jnp.maximum(x, 0.0) + jnp.log1p(jnp.exp(-jnp.abs(x)))


def _log_sigmoid(x):
    return -_softplus(-x)


def _layer_norm(z, g, b):
    mu = jnp.mean(z, axis=-1, keepdims=True)
    d = z - mu
    var = jnp.mean(d * d, axis=-1, keepdims=True)
    return d * lax.rsqrt(var + EPS) * g + b


def _lane_norm(y):
    mu = jnp.mean(y, axis=-1, keepdims=True)
    d = y - mu
    var = jnp.mean(d * d, axis=-1, keepdims=True)
    return d * lax.rsqrt(var + EPS)


def _proj_body(x_ref, w_ref, *o_refs, widths, prec):
    h = _wdot(x_ref[...], w_ref, prec)
    off = 0
    for o_ref, wd in zip(o_refs, widths):
        o_ref[...] = h[:, off:off + wd]
        off += wd


def _project(x, w, widths, prec=False):
    M, K = x.shape
    tm = _tile(M, ROW_TILE)
    return pl.pallas_call(
        functools.partial(_proj_body, widths=widths, prec=prec),
        grid=(M // tm,),
        in_specs=[pl.BlockSpec((tm, K), lambda i: (i, 0)), _wspec(w)],
        out_specs=[pl.BlockSpec((tm, wd), lambda i: (i, 0)) for wd in widths],
        out_shape=[jax.ShapeDtypeStruct((M, wd), F32) for wd in widths],
        compiler_params=_cparams(("parallel",)),
        name="project",
    )(x, w)


def _mem_proj_body(x_ref, w_ref, o_ref):
    o_ref[0] = _wdot(x_ref[...], w_ref.at[0], True)


def _mem_project(x, w):
    M, K = x.shape
    J, _, _, N = w.shape
    return pl.pallas_call(
        _mem_proj_body,
        grid=(J,),
        in_specs=[pl.BlockSpec((M, K), lambda j: (0, 0)),
                  pl.BlockSpec((1, 2, K, N), lambda j: (j, 0, 0, 0))],
        out_specs=pl.BlockSpec((1, M, N), lambda j: (j, 0, 0)),
        out_shape=jax.ShapeDtypeStruct((J, M, N), F32),
        compiler_params=_cparams(("parallel",)),
        name="mem_project",
    )(x, w)


def _out_ln_body(x_ref, a_ref, b_ref, wa_ref, wb_ref, g_ref, bias_ref, o_ref, *, prec):
    mix = _wdot(a_ref[...], wa_ref, prec) + _wdot(b_ref[...], wb_ref, prec)
    o_ref[...] = _layer_norm(ALPHA * x_ref[...] + mix, g_ref[...], bias_ref[...])


def _out_proj_ln(x, a, b, wa, wb, g, bias, prec=False):
    M, D = x.shape
    Ka, Kb = a.shape[1], b.shape[1]
    tm = _tile(M, ROW_TILE)
    row = lambda i: (i, 0)
    fix = lambda i: (0, 0)
    return pl.pallas_call(
        functools.partial(_out_ln_body, prec=prec),
        grid=(M // tm,),
        in_specs=[pl.BlockSpec((tm, D), row), pl.BlockSpec((tm, Ka), row), pl.BlockSpec((tm, Kb), row),
                  _wspec(wa), _wspec(wb),
                  pl.BlockSpec((1, D), fix), pl.BlockSpec((1, D), fix)],
        out_specs=pl.BlockSpec((tm, D), row),
        out_shape=jax.ShapeDtypeStruct((M, D), F32),
        compiler_params=_cparams(("parallel",)),
        name="out_proj_ln",
    )(x, a, b, wa, wb, g.reshape(1, D), bias.reshape(1, D))


def _ret_body(q_ref, k_ref, v_ref, g_ref, cos_ref, sa_ref, sb_ref, dmat_ref, qdec_ref, kdec_ref, sdec_ref,
              s0_ref, gn_ref, o_ref, sout_ref, s_scr, *, prec):
    i = pl.program_id(1)

    @pl.when(i == 0)
    def _():
        s_scr[...] = s0_ref[0]

    cos, sin_a, sin_b = cos_ref[...], sa_ref[...], sb_ref[...]
    width = RET_HEADS * RET_DK
    half = RET_DK // 2

    def rope(x):
        return x * cos + pltpu.roll(x, width - half, 1) * sin_a + pltpu.roll(x, half, 1) * sin_b

    q = rope(q_ref[...])
    k = rope(k_ref[...]) * (RET_DK ** -0.5)
    v = v_ref[...]
    gate = g_ref[...]
    H = RET_HEADS
    qh = jnp.stack([q[:, h * RET_DK:(h + 1) * RET_DK] for h in range(H)], axis=0)
    kh = jnp.stack([k[:, h * RET_DK:(h + 1) * RET_DK] for h in range(H)], axis=0)
    vh = jnp.stack([v[:, h * RET_DV:(h + 1) * RET_DV] for h in range(H)], axis=0)
    s_old = s_scr[...]
    s = _dg(qh, kh, _BNT, prec) * dmat_ref[...]
    o = _dg(s, vh, _BNN, prec) + _dg(qh, s_old, _BNN, prec) * qdec_ref[...]
    s_scr[...] = sdec_ref[...] * s_old + _dg(kh * kdec_ref[...], vh, _BTN, prec)
    on = _lane_norm(o)
    hn = jnp.concatenate([on[h] for h in range(H)], axis=1) * gn_ref[...]
    o_ref[...] = _silu(gate) * hn

    @pl.when(i == pl.num_programs(1) - 1)
    def _():
        sout_ref[0] = s_scr[...]


def _retention(hq, hk, hv, hg, s0, gn_g, B, T, pos0, L, prec=False):
    nT = T // L
    lg = np.log(1.0 - 2.0 ** (-5.0 - np.arange(RET_HEADS)))
    idx = np.arange(L, dtype=np.float64)
    rel = idx[:, None] - idx[None, :]
    dmat = np.where(rel >= 0, np.exp(np.maximum(rel, 0.0)[None] * lg[:, None, None]), 0.0)
    qdec = np.exp((idx + 1.0)[None, :, None] * lg[:, None, None])
    kdec = np.exp((L - 1.0 - idx)[None, :, None] * lg[:, None, None])
    sdec = np.exp(L * lg)[:, None, None]
    half = RET_DK // 2
    inv = ROPE_BASE ** (-jnp.arange(half, dtype=F32) / half)
    ang = (pos0 + jnp.arange(T)).astype(F32)[:, None] * inv[None, :]
    cos, sin = jnp.cos(ang), jnp.sin(ang)
    zero = jnp.zeros_like(sin)
    cos_t = jnp.tile(jnp.concatenate([cos, cos], axis=1), (1, RET_HEADS))
    sin_a = jnp.tile(jnp.concatenate([-sin, zero], axis=1), (1, RET_HEADS))
    sin_b = jnp.tile(jnp.concatenate([zero, sin], axis=1), (1, RET_HEADS))
    qk_w = RET_HEADS * RET_DK
    v_w = RET_HEADS * RET_DV
    row = lambda b, i: (b * nT + i, 0)
    tab = lambda b, i: (i, 0)
    fix3 = lambda b, i: (0, 0, 0)
    out, s_out = pl.pallas_call(
        functools.partial(_ret_body, prec=prec),
        grid=(B, nT),
        in_specs=[pl.BlockSpec((L, qk_w), row), pl.BlockSpec((L, qk_w), row),
                  pl.BlockSpec((L, v_w), row), pl.BlockSpec((L, v_w), row),
                  pl.BlockSpec((L, qk_w), tab), pl.BlockSpec((L, qk_w), tab), pl.BlockSpec((L, qk_w), tab),
                  pl.BlockSpec((RET_HEADS, L, L), fix3), pl.BlockSpec((RET_HEADS, L, 1), fix3),
                  pl.BlockSpec((RET_HEADS, L, 1), fix3), pl.BlockSpec((RET_HEADS, 1, 1), fix3),
                  pl.BlockSpec((1, RET_HEADS, RET_DK, RET_DV), lambda b, i: (b, 0, 0, 0)),
                  pl.BlockSpec((1, v_w), lambda b, i: (0, 0))],
        out_specs=[pl.BlockSpec((L, v_w), row),
                   pl.BlockSpec((1, RET_HEADS, RET_DK, RET_DV), lambda b, i: (b, 0, 0, 0))],
        out_shape=[jax.ShapeDtypeStruct((B * T, v_w), F32),
                   jax.ShapeDtypeStruct((B, RET_HEADS, RET_DK, RET_DV), F32)],
        scratch_shapes=[pltpu.VMEM((RET_HEADS, RET_DK, RET_DV), F32)],
        compiler_params=_cparams(("parallel", "arbitrary")),
        name="retention",
    )(hq, hk, hv, hg, cos_t, sin_a, sin_b, jnp.asarray(dmat, F32), jnp.asarray(qdec, F32),
      jnp.asarray(kdec, F32), jnp.asarray(sdec, F32), s0, gn_g.reshape(1, v_w))
    return out, s_out


def _rwkv_prep_body(h_ref, p_ref, mu_ref, w0_ref, w2_ref, a0_ref, a2_ref, g2_ref, kk_ref, ka_ref, rk_ref,
                    ones_ref, tri_ref, blk_ref, sel_ref,
                    kq_ref, rq_ref, kt_ref, bt_ref, ke_ref, be_ref, v_ref, g_ref, bon_ref, gt_ref, *scratch,
                    prec, shift_rows):
    hb = h_ref[...]
    if shift_rows:
        carry, = scratch
        tm = hb.shape[0]
        first = jnp.where(pl.program_id(1) == 0, p_ref[0], carry[...])
        row = lax.broadcasted_iota(jnp.int32, (tm, 1), 0)
        prev = jnp.where(row == 0, first, pltpu.roll(hb, 1, 0))
        carry[...] = hb[tm - 1:tm, :]
    else:
        prev = p_ref[...]
    xs = hb + (prev - hb) * mu_ref[...]
    W = RWKV_W
    r, k, v = xs[:, :W], xs[:, W:2 * W], xs[:, 2 * W:3 * W]
    w_lo = xs[:, 3 * W:3 * W + 64]
    a_lo = xs[:, 3 * W + 64:3 * W + 128]
    g_lo = xs[:, 3 * W + 128:]
    w_log = -_softplus(-(w0_ref[...] + _dot(jnp.tanh(w_lo), w2_ref[...], prec))) - 0.5
    lw = -jnp.exp(w_log)
    a = _sigmoid(a0_ref[...] + _dot(a_lo, a2_ref[...], prec))
    g = _dot(_sigmoid(g_lo), g2_ref[...], prec)
    ones = ones_ref[...]
    kk = k * kk_ref[...]
    nrm = jnp.sqrt(_exact_right_dot(kk * kk, ones, 2 if not prec else 3))
    kk = kk / jnp.maximum(nrm, 1e-12)
    k2 = k * (1.0 + (a - 1.0) * ka_ref[...])
    beta = kk * a
    cl = _exact_left_dot(tri_ref[...], lw)
    tot = _exact_left_dot(blk_ref[...], lw)
    ginv = jnp.exp(-cl)
    gend = jnp.exp(tot - cl)
    kq_ref[...] = (kk * jnp.exp(cl - lw)).astype(kq_ref.dtype)
    rq_ref[...] = (r * jnp.exp(cl)).astype(rq_ref.dtype)
    kt_ref[...] = (k2 * ginv).astype(kt_ref.dtype)
    bt_ref[...] = (beta * ginv).astype(bt_ref.dtype)
    ke_ref[...] = (k2 * gend).astype(ke_ref.dtype)
    be_ref[...] = (beta * gend).astype(be_ref.dtype)
    v_ref[...] = v.astype(v_ref.dtype)
    gt_ref[...] = jnp.exp(_exact_left_dot(sel_ref[...], lw))
    g_ref[...] = g
    bon_ref[...] = _exact_right_dot(r * k2 * rk_ref[...], ones, 2 if not prec else 3) * v


def _rwkv_prep(hb, prev, P, L, prec=False, seqs=None):
    M = hb.shape[0]
    tm = _tile(M if seqs is None else seqs[1], ROW_TILE)
    assert tm % L == 0
    W = RWKV_W
    idx = np.arange(tm)
    same = (idx[:, None] // L) == (idx[None, :] // L)
    tri = jnp.asarray(same & (idx[:, None] >= idx[None, :]), BF16)
    blk = jnp.asarray(same, BF16)
    sel = jnp.asarray((idx[None, :] // L) == np.arange(tm // L)[:, None], BF16)
    lane = np.arange(W)
    ones = jnp.asarray((lane[:, None] // RWKV_HD) == (lane[None, :] // RWKV_HD), BF16)
    if seqs is None:
        grid, nT = (1, M // tm), M // tm
        prev_spec = pl.BlockSpec((tm, RWKV_PROJ), lambda b, i: (i, 0))
        scratch = []
    else:
        grid, nT = (seqs[0], seqs[1] // tm), seqs[1] // tm
        prev_spec = pl.BlockSpec((1, 1, RWKV_PROJ), lambda b, i: (b, 0, 0))
        scratch = [pltpu.VMEM((1, RWKV_PROJ), F32)]
    row = lambda b, i: (b * nT + i, 0)
    vec = lambda a: a.reshape(1, -1)
    params = [vec(P['rwkv_mu']), vec(P['rwkv_w0']), P['rwkv_w2'], vec(P['rwkv_a0']),
              P['rwkv_a2'], P['rwkv_g2'], vec(P['rwkv_k_k']), vec(P['rwkv_k_a']),
              vec(P['rwkv_r_k']), ones, tri, blk, sel]
    out_dt = [F32 if prec else BF16] * 7 + [F32] * 2
    return pl.pallas_call(
        functools.partial(_rwkv_prep_body, prec=prec, shift_rows=seqs is not None),
        grid=grid,
        in_specs=[pl.BlockSpec((tm, RWKV_PROJ), row), prev_spec] + [_wspec(a) for a in params],
        out_specs=[pl.BlockSpec((tm, W), row) for _ in out_dt] + [pl.BlockSpec((tm // L, W), row)],
        out_shape=[jax.ShapeDtypeStruct((M, W), dt) for dt in out_dt] + [jax.ShapeDtypeStruct((M // L, W), F32)],
        scratch_shapes=scratch,
        compiler_params=_cparams(("parallel", "arbitrary")),
        name="rwkv_prep",
    )(hb, prev, *params)


def _rwkv_masks(L):
    ii = lax.broadcasted_iota(jnp.int32, (L, L), 0)
    jj = lax.broadcasted_iota(jnp.int32, (L, L), 1)
    return ii > jj, ii >= jj, (ii == jj).astype(F32)


def _rwkv_local(kq, rq, kt, bt, ke, be, v, masks, lmask_ref, nlev, prec):
    strict, incl, eye = masks
    L = kq.shape[1]
    x = jnp.concatenate([kq, rq], axis=1)
    sk = _dg(x, kt, _BNT, prec)
    sb = _dg(x, bt, _BNT, prec)
    n_m = jnp.where(strict, sk[:, :L], 0.0)
    a_k = jnp.where(incl, sk[:, L:], 0.0)
    m_m = jnp.where(strict, sb[:, :L], 0.0)
    a_b = jnp.where(incl, sb[:, L:], 0.0)
    t_m = eye - m_m * lmask_ref[0]
    for lv in range(1, nlev):
        c_m = m_m * lmask_ref[lv]
        t_m = t_m - _dg(_dg(t_m, c_m, _BNN, prec), t_m, _BNN, prec)
    kqp = _dg(t_m, kq, _BNN, prec)
    u0 = _dg(t_m, _dg(n_m, v, _BNN, prec), _BNN, prec)
    rqp = rq.astype(F32) - _dg(a_b, kqp, _BNN, prec)
    y0 = _dg(a_k, v, _BNN, prec) - _dg(a_b, u0, _BNN, prec)
    p_m = _dg(kqp, be, _BTN, prec)
    b_c = _dg(v, ke, _BTN, prec) - _dg(u0, be, _BTN, prec)
    return rqp, y0, p_m, b_c


def _rwkv_apply(s_old, rqp, y0, p_m, b_c, g_end, prec):
    y = _dg(rqp, s_old, _BNT, prec) + y0
    return y, s_old * g_end - _dg(s_old, p_m, _BNN, prec) + b_c


def _head_stack(tile, nblk):
    hd = RWKV_HD
    L = tile.shape[0] // nblk
    return jnp.concatenate([tile[:, hh * hd:(hh + 1) * hd].reshape(nblk, L, hd) for hh in range(2)], axis=0)


def _head_unstack(x, nblk):
    L, hd = x.shape[1], x.shape[2]
    return jnp.concatenate([x[hh * nblk:(hh + 1) * nblk].reshape(nblk * L, hd) for hh in range(2)], axis=1)


def _rwkv_local_body(kq_ref, rq_ref, kt_ref, bt_ref, ke_ref, be_ref, v_ref, lmask_ref,
                     rqp_ref, y0_ref, pm_ref, bc_ref, *, L, nchunk, nlev):
    ops = [_head_stack(r[...], nchunk) for r in (kq_ref, rq_ref, kt_ref, bt_ref, ke_ref, be_ref, v_ref)]
    res = _rwkv_local(*ops, _rwkv_masks(L), lmask_ref, nlev, False)
    for o_ref, a in zip((rqp_ref, y0_ref, pm_ref, bc_ref), res):
        o_ref[...] = _head_unstack(a, nchunk).astype(o_ref.dtype)


def _rwkv_apply_body(rqp_ref, y0_ref, pm_ref, bc_ref, gt_ref, g_ref, bon_ref, s0_ref, gng_ref, gnb_ref,
                     o_ref, sout_ref, s_scr, *, L, nchunk, B):
    i = pl.program_id(0)
    H, hd = RWKV_HEADS, RWKV_HD

    @pl.when(i == 0)
    def _():
        s_scr[...] = s0_ref[...].reshape(B * H, hd, hd)

    def heads(x):
        return jnp.stack([x[b][:, h * hd:(h + 1) * hd] for b in range(B) for h in range(H)], axis=0)

    def chunk(c, carry):
        rows = pl.ds(pl.multiple_of(c * L, L), L)
        ops = [heads(r[:, rows, :]) for r in (rqp_ref, y0_ref, pm_ref, bc_ref)]
        g_end = heads(gt_ref[:, pl.ds(c, 1), :])
        y, s_new = _rwkv_apply(s_scr[...], *ops, g_end, False)
        s_scr[...] = s_new
        yn = _lane_norm(y)
        for b in range(B):
            ynb = jnp.concatenate([yn[b * H + h] for h in range(H)], axis=1)
            o_ref[b, rows, :] = (ynb * gng_ref[...] + gnb_ref[...] + bon_ref[b, rows, :]) * g_ref[b, rows, :]
        return carry

    lax.fori_loop(0, nchunk, chunk, 0)

    @pl.when(i == pl.num_programs(0) - 1)
    def _():
        sout_ref[...] = s_scr[...].reshape(B, H, hd, hd)


def _rwkv_fused_body(kq_ref, rq_ref, kt_ref, bt_ref, ke_ref, be_ref, v_ref, g_ref, bon_ref, gt_ref,
                     s0_ref, gng_ref, gnb_ref, lmask_ref, o_ref, sout_ref, *, L, nseq, nlev, prec):
    hd = RWKV_HD
    ops = [_head_stack(r[...], nseq) for r in (kq_ref, rq_ref, kt_ref, bt_ref, ke_ref, be_ref, v_ref)]
    loc = _rwkv_local(*ops, _rwkv_masks(L), lmask_ref, nlev, prec)
    gt = gt_ref[...]
    g_end = jnp.concatenate([gt[:, :, hh * hd:(hh + 1) * hd] for hh in range(2)], axis=0)
    s_old = jnp.concatenate([s0_ref[:, hh] for hh in range(2)], axis=0)
    y, s_new = _rwkv_apply(s_old, *loc, g_end, prec)
    for hh in range(2):
        sout_ref[:, hh] = s_new[hh * nseq:(hh + 1) * nseq]
    yn = _head_unstack(_lane_norm(y), nseq)
    o_ref[...] = (yn * gng_ref[...] + gnb_ref[...] + bon_ref[...]) * g_ref[...]


def _rwkv_level_masks(L):
    nlev = int(math.log2(L))
    idx = np.arange(L)
    ii, jj = idx[:, None], idx[None, :]
    lmask = np.stack([((ii >> (lv + 1)) == (jj >> (lv + 1))) & ((ii & (1 << lv)) != 0) & ((jj & (1 << lv)) == 0)
                      for lv in range(nlev)]).astype(np.float32)
    return nlev, jnp.asarray(lmask)


def _rwkv_scan(pre, s0, gn_g, gn_b, B, T, L):
    W, H, hd = RWKV_W, RWKV_HEADS, RWKV_HD
    assert L == hd
    kq, rq, kt, bt, ke, be, v, g, bon, gt = pre
    nlev, lmask = _rwkv_level_masks(L)
    tl = _tile(T, RWKV_LOCAL_ROWS)
    nL = T // tl
    pw = 2 * hd
    row = lambda b, p, i: (b * nL + i, p)
    rqp, y0, p_m, b_c = pl.pallas_call(
        functools.partial(_rwkv_local_body, L=L, nchunk=tl // L, nlev=nlev),
        grid=(B, H // 2, nL),
        in_specs=[pl.BlockSpec((tl, pw), row) for _ in range(7)]
                 + [pl.BlockSpec((nlev, L, L), lambda b, p, i: (0, 0, 0))],
        out_specs=[pl.BlockSpec((tl, pw), row) for _ in range(4)],
        out_shape=[jax.ShapeDtypeStruct((B * T, W), dt) for dt in (BF16, F32, BF16, F32)],
        compiler_params=_cparams(("parallel", "parallel", "parallel")),
        name="rwkv_local",
    )(kq, rq, kt, bt, ke, be, v, lmask)
    tb = _tile(T, RWKV_ROWS)
    nT = T // tb
    nchunk = tb // L
    r3 = lambda a: a.reshape(B, T, W)
    blk = lambda i: (0, i, 0)
    fix2 = lambda i: (0, 0)
    fix4 = lambda i: (0, 0, 0, 0)
    out, s_out = pl.pallas_call(
        functools.partial(_rwkv_apply_body, L=L, nchunk=nchunk, B=B),
        grid=(nT,),
        in_specs=[pl.BlockSpec((B, tb, W), blk) for _ in range(4)]
                 + [pl.BlockSpec((B, nchunk, W), blk), pl.BlockSpec((B, tb, W), blk), pl.BlockSpec((B, tb, W), blk),
                    pl.BlockSpec((B, H, hd, hd), fix4), pl.BlockSpec((1, W), fix2), pl.BlockSpec((1, W), fix2)],
        out_specs=[pl.BlockSpec((B, tb, W), blk), pl.BlockSpec((B, H, hd, hd), fix4)],
        out_shape=[jax.ShapeDtypeStruct((B, T, W), F32), jax.ShapeDtypeStruct((B, H, hd, hd), F32)],
        scratch_shapes=[pltpu.VMEM((B * H, hd, hd), F32)],
        compiler_params=_cparams(("arbitrary",)),
        name="rwkv_apply",
    )(r3(rqp), r3(y0), r3(p_m), r3(b_c), gt.reshape(B, T // L, W), r3(g), r3(bon), s0,
      gn_g.reshape(1, W), gn_b.reshape(1, W))
    return out.reshape(B * T, W), s_out


def _rwkv_single_chunk(pre, s0, gn_g, gn_b, B, T, prec):
    W, H, hd = RWKV_W, RWKV_HEADS, RWKV_HD
    kq, rq, kt, bt, ke, be, v, g, bon, gt = pre
    nlev, lmask = _rwkv_level_masks(T)
    nseq = RWKV_SEQS if B % RWKV_SEQS == 0 else B
    pw = 2 * hd
    row = lambda i, p: (i, p)
    st = lambda i, p: (i, p, 0, 0)
    vec = lambda i, p: (0, p)
    out, s_out = pl.pallas_call(
        functools.partial(_rwkv_fused_body, L=T, nseq=nseq, nlev=nlev, prec=prec),
        grid=(B // nseq, H // 2),
        in_specs=[pl.BlockSpec((nseq * T, pw), row) for _ in range(9)]
                 + [pl.BlockSpec((nseq, 1, pw), lambda i, p: (i, 0, p)), pl.BlockSpec((nseq, 2, hd, hd), st),
                    pl.BlockSpec((1, pw), vec), pl.BlockSpec((1, pw), vec),
                    pl.BlockSpec((nlev, T, T), lambda i, p: (0, 0, 0))],
        out_specs=[pl.BlockSpec((nseq * T, pw), row), pl.BlockSpec((nseq, 2, hd, hd), st)],
        out_shape=[jax.ShapeDtypeStruct((B * T, W), F32), jax.ShapeDtypeStruct((B, H, hd, hd), F32)],
        compiler_params=_cparams(("parallel", "parallel")),
        name="rwkv_single_chunk",
    )(kq, rq, kt, bt, ke, be, v, g, bon, gt.reshape(B, 1, W), s0, gn_g.reshape(1, W), gn_b.reshape(1, W), lmask)
    return out, s_out


def _mlstm_body(qk_ref, v_ref, o_ref, gt_ref, cprev_ref, c0_ref, n0_ref, m0_ref, cw_ref, cb_ref, gb_ref,
                gn_ref, tri_ref, out_ref, cout_ref, nout_ref, mout_ref, convout_ref,
                xpad, c_scr, m_scr, *, L):
    i = pl.program_id(1)
    H, DK, DV = MLSTM_HEADS, MLSTM_DK, MLSTM_DV
    K = MLSTM_CONV - 1
    base = 8 - K

    @pl.when(i == 0)
    def _():
        xpad[base:8, :] = cprev_ref[0]
        for h in range(H):
            c_scr[h, :, 0:DV] = c0_ref[0, h]
            c_scr[h, :, DV:2 * DV] = jnp.broadcast_to(n0_ref[0, h], (DK, DV))
        m_scr[...] = m0_ref[0]

    xpad[8:8 + L, :] = qk_ref[...]
    conv = cb_ref[...] + xpad[pl.ds(base, L), :] * cw_ref[0:1, :]
    for j in range(1, MLSTM_CONV):
        conv = conv + xpad[pl.ds(base + j, L), :] * cw_ref[j:j + 1, :]
    tail = xpad[pl.ds(8 + L - K, K), :]
    xpad[base:8, :] = tail
    qk = _silu(conv)
    q = qk[:, :H * DK] * (DK ** -0.5)
    k = qk[:, H * DK:]
    v = v_ref[...]
    z = gt_ref[...] + gb_ref[...]
    lf = _log_sigmoid(z)
    bcum = _exact_left_dot(tri_ref[...], lf)
    z_t = z.T
    b_t = bcum.T
    ii = lax.broadcasted_iota(jnp.int32, (L, L), 0)
    jj = lax.broadcasted_iota(jnp.int32, (L, L), 1)
    causal = ii >= jj
    ones = jnp.ones((L, DV), F32)
    m_all = m_scr[...]
    lane = lax.broadcasted_iota(jnp.int32, (1, 128), 1)
    stack = lambda f: jnp.stack([f(h) for h in range(H)], axis=0)
    qh = stack(lambda h: q[:, h * DK:(h + 1) * DK])
    kh = stack(lambda h: k[:, h * DK:(h + 1) * DK])
    vh = stack(lambda h: jnp.concatenate([v[:, h * DV:(h + 1) * DV], ones], axis=1))
    b_col = stack(lambda h: bcum[:, H + h:H + h + 1])
    ig_col = stack(lambda h: z[:, h:h + 1])
    row_term = stack(lambda h: z_t[h:h + 1, :] - b_t[H + h:H + h + 1, :])
    m0 = stack(lambda h: m_all[:, h:h + 1])
    log_d = jnp.where(causal, b_col + row_term, NEG_INF)
    m_inter = b_col + m0
    m_t = jnp.maximum(m_inter, jnp.max(log_d, axis=-1, keepdims=True))
    w_d = jnp.exp(log_d - m_t)
    w_i = jnp.exp(m_inter - m_t)
    c_old = c_scr[...]
    s = _dg(qh, kh, _BNT, False) * w_d
    num = _dg(s, vh, _BNN, False) + _dg(qh, c_old, _BNN, False) * w_i
    den = num[:, :, DV:DV + 1]
    denom = jnp.maximum(jnp.abs(den), jnp.exp(-m_t))
    hn = _lane_norm(num[:, :, :DV] / denom)
    m_new = m_t[:, L - 1:L, :]
    b_last = b_col[:, L - 1:L, :]
    w_s = jnp.exp(b_last + m0 - m_new)
    w_k = jnp.exp(b_last - b_col + ig_col - m_new)
    c_scr[...] = w_s * c_old + _dg(kh * w_k, vh, _BTN, False)
    m_new_all = m_all
    for h in range(H):
        m_new_all = jnp.where(lane == h, m_new[h], m_new_all)
    m_scr[...] = m_new_all
    out_ref[...] = jnp.concatenate([hn[h] for h in range(H)], axis=1) * gn_ref[...] * _sigmoid(o_ref[...])

    @pl.when(i == pl.num_programs(1) - 1)
    def _():
        for h in range(H):
            cout_ref[0, h] = c_scr[h, :, 0:DV]
            nout_ref[0, h] = c_scr[h, :, DV:DV + 1]
        mout_ref[0] = m_scr[...]
        convout_ref[0] = tail


def _mlstm(hqk, hv, ho, hgate, conv_prev, c0, n0, m0, P, gate_bias, B, T, L):
    H, DK, DV = MLSTM_HEADS, MLSTM_DK, MLSTM_DV
    nT = T // L
    K = MLSTM_CONV - 1
    W = H * DV
    idx = np.arange(L)
    tri = jnp.asarray(idx[:, None] >= idx[None, :], BF16)
    m0p = jnp.zeros((B, 1, 128), F32).at[:, 0, :H].set(m0)
    row = lambda b, i: (b * nT + i, 0)
    fix = lambda b, i: (0, 0)
    perb3 = lambda b, i: (b, 0, 0)
    perb4 = lambda b, i: (b, 0, 0, 0)
    out, c_out, n_out, m_out, conv_out = pl.pallas_call(
        functools.partial(_mlstm_body, L=L),
        grid=(B, nT),
        in_specs=[pl.BlockSpec((L, W), row), pl.BlockSpec((L, W), row), pl.BlockSpec((L, W), row),
                  pl.BlockSpec((L, 128), row),
                  pl.BlockSpec((1, K, W), perb3),
                  pl.BlockSpec((1, H, DK, DV), perb4), pl.BlockSpec((1, H, DK, 1), perb4),
                  pl.BlockSpec((1, 1, 128), perb3),
                  pl.BlockSpec((MLSTM_CONV, W), fix), pl.BlockSpec((1, W), fix), pl.BlockSpec((1, 128), fix),
                  pl.BlockSpec((1, W), fix), pl.BlockSpec((L, L), fix)],
        out_specs=[pl.BlockSpec((L, W), row),
                   pl.BlockSpec((1, H, DK, DV), perb4), pl.BlockSpec((1, H, DK, 1), perb4),
                   pl.BlockSpec((1, 1, 128), perb3), pl.BlockSpec((1, K, W), perb3)],
        out_shape=[jax.ShapeDtypeStruct((B * T, W), F32),
                   jax.ShapeDtypeStruct((B, H, DK, DV), F32), jax.ShapeDtypeStruct((B, H, DK, 1), F32),
                   jax.ShapeDtypeStruct((B, 1, 128), F32), jax.ShapeDtypeStruct((B, K, W), F32)],
        scratch_shapes=[pltpu.VMEM((L + 8, W), F32), pltpu.VMEM((H, DK, 2 * DV), F32), pltpu.VMEM((1, 128), F32)],
        compiler_params=_cparams(("parallel", "arbitrary")),
        name="mlstm",
    )(hqk, hv, ho, hgate, conv_prev, c0, n0.reshape(B, H, DK, 1), m0p,
      P['mlstm_conv_w'], P['mlstm_conv_b'].reshape(1, W), gate_bias, P['mlstm_gn_g'].reshape(1, W), tri)
    return out, c_out, n_out.reshape(B, H, DK), m_out[:, 0, :H], conv_out


def _fox_prep_body(q_ref, k_ref, gt_ref, qg_ref, kg_ref, gb_ref, qn_ref, kn_ref, lf_ref):
    def rms(x, g):
        outs = []
        for h in range(FOX_HEADS):
            xh = x[:, h * FOX_HD:(h + 1) * FOX_HD]
            outs.append(xh * lax.rsqrt(jnp.mean(xh * xh, axis=-1, keepdims=True) + EPS) * g)
        return jnp.concatenate(outs, axis=1)

    qn_ref[...] = (rms(q_ref[...], qg_ref[...]) * (FOX_HD ** -0.5)).astype(BF16)
    kn_ref[...] = rms(k_ref[...], kg_ref[...])
    lf_ref[...] = _log_sigmoid(gt_ref[...] + gb_ref[...])


def _fox_prep(hq, hk, hgate, P, gate_bias):
    M, W = hq.shape
    tm = _tile(M, ROW_TILE)
    row = lambda i: (i, 0)
    fix = lambda i: (0, 0)
    return pl.pallas_call(
        _fox_prep_body,
        grid=(M // tm,),
        in_specs=[pl.BlockSpec((tm, W), row), pl.BlockSpec((tm, W), row), pl.BlockSpec((tm, 128), row),
                  pl.BlockSpec((1, FOX_HD), fix), pl.BlockSpec((1, FOX_HD), fix), pl.BlockSpec((1, 128), fix)],
        out_specs=[pl.BlockSpec((tm, W), row), pl.BlockSpec((tm, W), row), pl.BlockSpec((tm, 128), row)],
        out_shape=[jax.ShapeDtypeStruct((M, W), BF16), jax.ShapeDtypeStruct((M, W), F32),
                   jax.ShapeDtypeStruct((M, 128), F32)],
        compiler_params=_cparams(("parallel",)),
        name="fox_prep",
    )(hq, hk, hgate, P['fox_q_g'].reshape(1, FOX_HD), P['fox_k_g'].reshape(1, FOX_HD), gate_bias)


def _fox_prep_prompt_body(q_ref, k_ref, v_ref, gt_ref, qg_ref, kg_ref, gb_ref, tri_ref,
                          qa_ref, kn_ref, vc_ref, kt_ref, va_ref, lf_ref, carry):
    i = pl.program_id(1)
    tm = q_ref.shape[0]
    HD = FOX_HD

    @pl.when(i == 0)
    def _():
        carry[...] = jnp.zeros_like(carry)

    lf = _log_sigmoid(gt_ref[...] + gb_ref[...])
    lf_ref[...] = lf
    c = _exact_left_dot(tri_ref[...], lf) + carry[...]
    carry[...] = c[tm - 1:tm, :]
    c2t = (c * (-LOG2E)).T
    lane = lax.broadcasted_iota(jnp.int32, (tm, HD), 1)
    q_ones = jnp.where(lane < 3, 1.0, 0.0).astype(BF16)
    v_ones = jnp.where(lane < 1, 1.0, 0.0).astype(BF16)
    row16 = lax.broadcasted_iota(jnp.int32, (16, tm), 0)
    q, k, v = q_ref[...], k_ref[...], v_ref[...]
    for h in range(FOX_HEADS):
        hs = slice(h * HD, (h + 1) * HD)
        qh, kh = q[:, hs], k[:, hs]
        qh = qh * lax.rsqrt(jnp.mean(qh * qh, axis=-1, keepdims=True) + EPS) * qg_ref[...]
        kh = kh * lax.rsqrt(jnp.mean(kh * kh, axis=-1, keepdims=True) + EPS) * kg_ref[...]
        qa_ref[:, 2 * h * HD:(2 * h + 1) * HD] = (qh * (HD ** -0.5 * LOG2E)).astype(BF16)
        qa_ref[:, (2 * h + 1) * HD:(2 * h + 2) * HD] = q_ones
        kn_ref[pl.ds(h, tm, stride=FOX_HEADS), :] = kh
        vc_ref[pl.ds(h, tm, stride=FOX_HEADS), :] = v[:, hs]
        kt_ref[0, h, 0, 0:HD, :] = kh.T.astype(BF16)
        bias = c2t[2 * MLSTM_HEADS + h:2 * MLSTM_HEADS + h + 1, :]
        hi = bias.astype(BF16).astype(F32)
        mid = (bias - hi).astype(BF16).astype(F32)
        lo = bias - hi - mid
        blk = jnp.where(row16 == 0, hi, jnp.where(row16 == 1, mid, jnp.where(row16 == 2, lo, 0.0)))
        kt_ref[0, h, 0, HD:HD + 16, :] = blk.astype(BF16)
        kt_ref[0, h, 0, HD + 16:2 * HD, :] = jnp.zeros((HD - 16, tm), BF16)
        va_ref[:, 2 * h * HD:(2 * h + 1) * HD] = v[:, hs].astype(BF16)
        va_ref[:, (2 * h + 1) * HD:(2 * h + 2) * HD] = v_ones


def _fox_prep_prompt(hq, hk, hv, hgate, P, gate_bias, B, T):
    M, W = hq.shape
    H, HD = FOX_HEADS, FOX_HD
    tm = _tile(T, FOX_TQ)
    nT = T // tm
    idx = np.arange(tm)
    tri = jnp.asarray(idx[:, None] >= idx[None, :], BF16)
    row = lambda b, i: (b * nT + i, 0)
    fix = lambda b, i: (0, 0)
    return pl.pallas_call(
        _fox_prep_prompt_body,
        grid=(B, nT),
        in_specs=[pl.BlockSpec((tm, W), row), pl.BlockSpec((tm, W), row), pl.BlockSpec((tm, W), row),
                  pl.BlockSpec((tm, 128), row),
                  pl.BlockSpec((1, HD), fix), pl.BlockSpec((1, HD), fix), pl.BlockSpec((1, 128), fix),
                  pl.BlockSpec((tm, tm), fix)],
        out_specs=[pl.BlockSpec((tm, 2 * W), row), pl.BlockSpec((tm * H, HD), row), pl.BlockSpec((tm * H, HD), row),
                   pl.BlockSpec((1, H, 1, 2 * HD, tm), lambda b, i: (b, 0, i, 0, 0)),
                   pl.BlockSpec((tm, 2 * W), row), pl.BlockSpec((tm, 128), row)],
        out_shape=[jax.ShapeDtypeStruct((M, 2 * W), BF16), jax.ShapeDtypeStruct((M * H, HD), F32),
                   jax.ShapeDtypeStruct((M * H, HD), F32),
                   jax.ShapeDtypeStruct((B, H, nT, 2 * HD, tm), BF16),
                   jax.ShapeDtypeStruct((M, 2 * W), BF16), jax.ShapeDtypeStruct((M, 128), F32)],
        scratch_shapes=[pltpu.VMEM((1, 128), F32)],
        compiler_params=_cparams(("parallel", "arbitrary")),
        name="fox_prep_prompt",
    )(hq, hk, hv, hgate, P['fox_q_g'].reshape(1, HD), P['fox_k_g'].reshape(1, HD), gate_bias, tri)


def _fox_prompt_body(q_ref, kt_ref, va_ref, o_ref, sa_scr, sb_scr, m_scr, acc_scr, *, tq):
    qi = pl.program_id(2)
    nq = pl.num_programs(2)
    m_scr[...] = jnp.full_like(m_scr, NEG_INF)
    acc_scr[...] = jnp.zeros_like(acc_scr)

    def scores(s_ref, kj, qt=None):
        q0 = pl.multiple_of((qi if qt is None else qt) * tq, tq)
        s_ref[...] = jnp.dot(q_ref[0, pl.ds(q0, tq), :], kt_ref[0, 0, kj], preferred_element_type=F32)

    def update(s_ref, kj, masked):
        vb = va_ref[0, pl.ds(pl.multiple_of(kj * tq, tq), tq), :]
        s = s_ref[...]
        if masked:
            ii = lax.broadcasted_iota(jnp.int32, (tq, tq), 0)
            jj = lax.broadcasted_iota(jnp.int32, (tq, tq), 1)
            s = jnp.where(jj <= ii, s, NEG_INF)
        m_old = m_scr[...]
        m_new = jnp.maximum(m_old, jnp.max(s, axis=-1, keepdims=True))
        p = jnp.exp2(s - m_new)
        acc_scr[...] = jnp.exp2(m_old - m_new) * acc_scr[...] + jnp.dot(p.astype(BF16), vb, preferred_element_type=F32)
        m_scr[...] = m_new

    @pl.when(qi == 0)
    def _():
        scores(sa_scr, 0)

    nxt = jnp.minimum(qi + 1, nq - 1)

    def run(first, second):
        def body(j, carry):
            scores(second, 2 * j + 1)
            update(first, 2 * j, False)
            scores(first, 2 * j + 2)
            update(second, 2 * j + 1, False)
            return carry

        lax.fori_loop(0, lax.shift_right_logical(qi, 1), body, 0)

        @pl.when((qi & 1) == 0)
        def _():
            scores(second, 0, nxt)
            update(first, qi, True)

        @pl.when((qi & 1) == 1)
        def _():
            scores(second, qi)
            update(first, qi - 1, False)
            scores(first, 0, nxt)
            update(second, qi, True)

    starts_in_a = (lax.shift_right_logical(qi + 1, 1) & 1) == 0

    @pl.when(starts_in_a)
    def _():
        run(sa_scr, sb_scr)

    @pl.when(jnp.logical_not(starts_in_a))
    def _():
        run(sb_scr, sa_scr)

    acc = acc_scr[...]
    o_ref[0] = acc[:, :FOX_HD] / acc[:, FOX_HD:FOX_HD + 1]


def _fox_prompt(qa, kt, va, B, T):
    H, HD = FOX_HEADS, FOX_HD
    W = H * HD
    tq = kt.shape[-1]
    nQ = T // tq
    out = pl.pallas_call(
        functools.partial(_fox_prompt_body, tq=tq),
        grid=(B, H, nQ),
        in_specs=[pl.BlockSpec((1, T, 2 * HD), lambda b, h, i: (b, 0, h), pipeline_mode=pl.Buffered(1)),
                  pl.BlockSpec((1, 1, nQ, 2 * HD, tq), lambda b, h, i: (b, h, 0, 0, 0),
                               pipeline_mode=pl.Buffered(1)),
                  pl.BlockSpec((1, T, 2 * HD), lambda b, h, i: (b, 0, h), pipeline_mode=pl.Buffered(1))],
        out_specs=pl.BlockSpec((1, tq, HD), lambda b, h, i: (b, i, h)),
        out_shape=jax.ShapeDtypeStruct((B, T, W), F32),
        scratch_shapes=[pltpu.VMEM((tq, tq), F32), pltpu.VMEM((tq, tq), F32),
                        pltpu.VMEM((tq, 1), F32), pltpu.VMEM((tq, 2 * HD), F32)],
        compiler_params=_cparams(("parallel", "parallel", "arbitrary")),
        name="fox_prompt",
    )(qa.reshape(B, T, 2 * W), kt, va.reshape(B, T, 2 * W))
    return out.reshape(B * T, W)


def _fox_sample_body(q_ref, kc_ref, vc_ref, kn_ref, vn_ref, cq_ref, ckc_ref, ckn_ref, o_ref, *, T, past):
    H, HD = FOX_HEADS, FOX_HD
    q = q_ref[0]
    kn, vn = kn_ref[0], vn_ref[0]
    ii = lax.broadcasted_iota(jnp.int32, (T, T), 0)
    jj = lax.broadcasted_iota(jnp.int32, (T, T), 1)
    outs = []
    for h in range(H):
        cs = slice(h * HD, (h + 1) * HD)
        kc = kc_ref[0, pl.ds(h, past, stride=H), :]
        vc = vc_ref[0, pl.ds(h, past, stride=H), :]
        cq = cq_ref[0, h]
        s1 = _dot_nt(q[:, cs], kc) + (cq - ckc_ref[0, h])
        s2 = _dot_nt(q[:, cs], kn[:, cs]) + (cq - ckn_ref[0, h])
        s2 = jnp.where(jj <= ii, s2, NEG_INF)
        m = jnp.maximum(jnp.max(s1, axis=-1, keepdims=True), jnp.max(s2, axis=-1, keepdims=True))
        p1 = jnp.exp(s1 - m)
        p2 = jnp.exp(s2 - m)
        den = jnp.sum(p1, axis=-1, keepdims=True) + jnp.sum(p2, axis=-1, keepdims=True)
        outs.append((_dot(p1, vc) + _dot(p2, vn[:, cs])) / den)
    o_ref[0] = jnp.concatenate(outs, axis=1)


def _fox_sample(qn, kn, vn, k_cache, v_cache, c_all, B, T):
    H, HD = FOX_HEADS, FOX_HD
    W = H * HD
    past = k_cache.shape[1]
    ct = jnp.transpose(c_all, (0, 2, 1))
    cq = ct[:, :, past:].reshape(B, H, T, 1)
    ckc = ct[:, :, :past].reshape(B, H, 1, past)
    ckn = ct[:, :, past:].reshape(B, H, 1, T)
    b3 = lambda b: (b, 0, 0)
    b4 = lambda b: (b, 0, 0, 0)
    out = pl.pallas_call(
        functools.partial(_fox_sample_body, T=T, past=past),
        grid=(B,),
        in_specs=[pl.BlockSpec((1, T, W), b3), pl.BlockSpec((1, past * H, HD), b3), pl.BlockSpec((1, past * H, HD), b3),
                  pl.BlockSpec((1, T, W), b3), pl.BlockSpec((1, T, W), b3),
                  pl.BlockSpec((1, H, T, 1), b4), pl.BlockSpec((1, H, 1, past), b4), pl.BlockSpec((1, H, 1, T), b4)],
        out_specs=pl.BlockSpec((1, T, W), b3),
        out_shape=jax.ShapeDtypeStruct((B, T, W), F32),
        compiler_params=_cparams(("parallel",)),
        name="fox_sample",
    )(qn.reshape(B, T, W), k_cache.reshape(B, past * H, HD), v_cache.reshape(B, past * H, HD),
      kn.reshape(B, T, W), vn.reshape(B, T, W), cq, ckc, ckn)
    return out.reshape(B * T, W)


def _mem_attn_body(x_ref, mk_ref, mv_ref, wq_ref, wo_ref, g_ref, b_ref, o_ref, *, prec):
    x = x_ref[...]
    q = _wdot(x, wq_ref, prec)
    nseq = mk_ref.shape[0]
    rows = x.shape[0] // nseq
    per_seq = []
    for j in range(nseq):
        qj = q[j * rows:(j + 1) * rows]
        outs = []
        for h in range(MEM_HEADS):
            cs = slice(h * MEM_HD, (h + 1) * MEM_HD)
            mkh, mvh = mk_ref[j, :, cs], mv_ref[j, :, cs]
            s = _dot_nt(qj[:, cs], mkh, prec) * (MEM_HD ** -0.5)
            m = jnp.max(s, axis=-1, keepdims=True)
            p = jnp.exp(s - m)
            outs.append(_dot(p, mvh, prec) / jnp.sum(p, axis=-1, keepdims=True))
        per_seq.append(jnp.concatenate(outs, axis=1))
    o = per_seq[0] if nseq == 1 else jnp.concatenate(per_seq, axis=0)
    att = _wdot(o, wo_ref, prec)
    y = _layer_norm(ALPHA * x + att, g_ref[...], b_ref[...])
    o_ref[...] = y


def _mem_attn_ln(x, mk, mv, l, wq, wo, g, bias, B, T, prec=False):
    D = D_MODEL
    Mm = mk.shape[1]
    tm = _tile(T, ROW_TILE)
    nT = T // tm
    nseq = MEM_SEQS if (nT == 1 and B % MEM_SEQS == 0) else 1
    row = lambda b, i: (b * nT + i, 0)
    fix = lambda b, i: (0, 0)
    mem_spec = pl.BlockSpec((nseq, Mm, D), lambda b, i: (l * (B // nseq) + b, 0, 0))
    return pl.pallas_call(
        functools.partial(_mem_attn_body, prec=prec),
        grid=(B // nseq, nT),
        in_specs=[pl.BlockSpec((nseq * tm, D), row), mem_spec, mem_spec,
                  _wspec(wq), _wspec(wo),
                  pl.BlockSpec((1, D), fix), pl.BlockSpec((1, D), fix)],
        out_specs=pl.BlockSpec((nseq * tm, D), row),
        out_shape=jax.ShapeDtypeStruct((B * T, D), F32),
        compiler_params=_cparams(("parallel", "parallel")),
        name="mem_attn_ln",
    )(x, mk, mv, wq, wo, g.reshape(1, D), bias.reshape(1, D))


def _router_body(x_ref, w_ref, b_ref, tri_ref, o_ref, cnt_ref, cnt_scr):
    x = x_ref[...]
    w = w_ref[...]
    x1, x2, x3 = _split3(x)
    w1, w2, w3 = _split3(w)
    nt = lambda a, c: lax.dot_general(a, c, _NT, preferred_element_type=F32)
    logits = (nt(w1, x1) + (nt(w1, x2) + nt(w2, x1)) + (nt(w1, x3) + nt(w2, x2) + nt(w3, x1))) + b_ref[...]
    m = jnp.max(logits, axis=0, keepdims=True)
    e = jnp.exp(logits - m)
    p = e / jnp.sum(e, axis=0, keepdims=True)
    rows = [p[j:j + 1, :] for j in range(N_EXPERTS)]
    best = None
    sel = None
    for g in range(N_GROUPS):
        a, b, c, d = rows[4 * g:4 * g + 4]
        top2 = jnp.maximum(jnp.maximum(jnp.maximum(a + b, a + c), jnp.maximum(a + d, b + c)),
                           jnp.maximum(b + d, c + d))
        if g == 0:
            best, sel = top2, jnp.zeros_like(top2, dtype=jnp.int32)
        else:
            upd = top2 > best
            sel = jnp.where(upd, g, sel)
            best = jnp.maximum(best, top2)
    pin = []
    for kk in range(EXPERTS_PER_GROUP):
        v = rows[kk]
        for g in range(1, N_GROUPS):
            v = jnp.where(sel == g, rows[4 * g + kk], v)
        pin.append(v)
    v1, i1 = pin[0], jnp.zeros_like(sel)
    for kk in range(1, EXPERTS_PER_GROUP):
        upd = pin[kk] > v1
        i1 = jnp.where(upd, kk, i1)
        v1 = jnp.maximum(v1, pin[kk])
    v2, i2 = None, None
    for kk in range(EXPERTS_PER_GROUP):
        cand = jnp.where(i1 == kk, -1.0, pin[kk])
        if v2 is None:
            v2, i2 = cand, jnp.zeros_like(sel)
        else:
            upd = cand > v2
            i2 = jnp.where(upd, kk, i2)
            v2 = jnp.maximum(v2, cand)
    tot = v1 + v2
    e1 = sel * EXPERTS_PER_GROUP + i1
    e2 = sel * EXPERTS_PER_GROUP + i2
    @pl.when(pl.program_id(0) == 0)
    def _():
        cnt_scr[...] = jnp.zeros_like(cnt_scr)

    eidx = lax.broadcasted_iota(jnp.int32, logits.shape, 0)
    oh1 = jnp.where(eidx == e1, 1.0, 0.0)
    oh2 = jnp.where(eidx == e2, 1.0, 0.0)
    oh = oh1 + oh2
    base = cnt_scr[...] + jnp.dot(oh.astype(BF16), tri_ref[...], preferred_element_type=F32)
    r1 = jnp.sum(oh1 * base, axis=0, keepdims=True)
    r2 = jnp.sum(oh2 * base, axis=0, keepdims=True)
    cnt = cnt_scr[...] + jnp.sum(oh, axis=1, keepdims=True)
    cnt_scr[...] = cnt
    cnt_ref[...] = jnp.broadcast_to(cnt, cnt_ref.shape)
    zero = jnp.zeros_like(v1)
    o_ref[...] = jnp.concatenate([e1.astype(F32), e2.astype(F32), v1 / tot, v2 / tot, r1, r2, zero, zero], axis=0)


def _router(x, w_router, b_router):
    M, D = x.shape
    tm = _tile(M, ROW_TILE)
    idx = np.arange(tm)
    tri = jnp.asarray(idx[:, None] < idx[None, :], BF16)
    r, cnt = pl.pallas_call(
        _router_body,
        grid=(M // tm,),
        in_specs=[pl.BlockSpec((tm, D), lambda i: (i, 0)),
                  pl.BlockSpec((N_EXPERTS, D), lambda i: (0, 0)),
                  pl.BlockSpec((N_EXPERTS, 1), lambda i: (0, 0)),
                  pl.BlockSpec((tm, tm), lambda i: (0, 0))],
        out_specs=[pl.BlockSpec((8, tm), lambda i: (0, i)), pl.BlockSpec((N_EXPERTS, 128), lambda i: (0, 0))],
        out_shape=[jax.ShapeDtypeStruct((8, M), F32), jax.ShapeDtypeStruct((N_EXPERTS, 128), F32)],
        scratch_shapes=[pltpu.VMEM((N_EXPERTS, 1), F32)],
        compiler_params=_cparams(("arbitrary",)),
        name="router",
    )(x, w_router.T, b_router.reshape(N_EXPERTS, 1), tri)
    return r, cnt[:, 0].astype(jnp.int32)


def _expert_body(be_ref, x_ref, wg_ref, wu_ref, wd_ref, o_ref, wg_s, wu_s, wd_s):
    i = pl.program_id(0)
    prev = be_ref[jnp.maximum(i - 1, 0)]

    @pl.when((i == 0) | (be_ref[i] != prev))
    def _():
        wg_s[...] = wg_ref[0, 0].astype(BF16)
        wu_s[...] = wu_ref[0, 0].astype(BF16)
        wd_s[...] = wd_ref[0, 0].astype(BF16)

    x = x_ref[...].astype(BF16)
    hg = jnp.dot(x, wg_s[...], preferred_element_type=F32)
    hu = jnp.dot(x, wu_s[...], preferred_element_type=F32)
    hb = (_silu(hg) * hu).astype(BF16)
    o_ref[...] = jnp.dot(hb, wd_s[...], preferred_element_type=F32)


def _experts(xg, blk_exp, wg, wu, wd, l):
    n_blocks = blk_exp.shape[0]
    blk = xg.shape[0] // n_blocks
    D, DE = D_MODEL, D_EXPERT
    grid_spec = pltpu.PrefetchScalarGridSpec(
        num_scalar_prefetch=1,
        grid=(n_blocks,),
        in_specs=[pl.BlockSpec((blk, D), lambda i, be: (i, 0)),
                  pl.BlockSpec((1, 1, D, DE), lambda i, be: (l, be[i], 0, 0)),
                  pl.BlockSpec((1, 1, D, DE), lambda i, be: (l, be[i], 0, 0)),
                  pl.BlockSpec((1, 1, DE, D), lambda i, be: (l, be[i], 0, 0))],
        out_specs=pl.BlockSpec((blk, D), lambda i, be: (i, 0)),
        scratch_shapes=[pltpu.VMEM((D, DE), BF16), pltpu.VMEM((D, DE), BF16), pltpu.VMEM((DE, D), BF16)],
    )
    return pl.pallas_call(
        _expert_body,
        grid_spec=grid_spec,
        out_shape=jax.ShapeDtypeStruct((n_blocks * blk, D), F32),
        compiler_params=_cparams(("arbitrary",)),
        name="experts",
    )(blk_exp, xg, wg, wu, wd)


def _combine_ln_body(x_ref, y0_ref, y1_ref, gt_ref, g_ref, b_ref, o_ref):
    gt = gt_ref[...]
    ffn = y0_ref[...] * gt[:, 0:1] + y1_ref[...] * gt[:, 1:2]
    o_ref[...] = _layer_norm(ALPHA * x_ref[...] + ffn, g_ref[...], b_ref[...])


def _combine_ln(x, y0, y1, gates, g, bias):
    M, D = x.shape
    tm = _tile(M, ROW_TILE)
    row = lambda i: (i, 0)
    fix = lambda i: (0, 0)
    return pl.pallas_call(
        _combine_ln_body,
        grid=(M // tm,),
        in_specs=[pl.BlockSpec((tm, D), row), pl.BlockSpec((tm, D), row), pl.BlockSpec((tm, D), row),
                  pl.BlockSpec((tm, 2), row), pl.BlockSpec((1, D), fix), pl.BlockSpec((1, D), fix)],
        out_specs=pl.BlockSpec((tm, D), row),
        out_shape=jax.ShapeDtypeStruct((M, D), F32),
        compiler_params=_cparams(("parallel",)),
        name="combine_ln",
    )(x, y0, y1, gates, g.reshape(1, D), bias.reshape(1, D))


def _moe_ln(x, P, l):
    M = x.shape[0]
    r, counts = _router(x, P['w_router'], P['b_router'])
    e = r[0:2].astype(jnp.int32).T.reshape(-1)
    gates = r[2:4].T
    rank = r[4:6].astype(jnp.int32).T.reshape(-1)
    blk = MOE_BLOCK if 2 * M >= N_EXPERTS * MOE_BLOCK else MOE_BLOCK_SMALL
    padded = (counts + blk - 1) // blk * blk
    p_ends = jnp.cumsum(padded)
    p_starts = p_ends - padded
    dest = p_starts[e] + rank
    n_blocks = -(-2 * M // blk) + N_EXPERTS
    blk_start = jnp.arange(n_blocks, dtype=jnp.int32) * blk
    blk_exp = jnp.minimum(jnp.sum((p_ends[None, :] <= blk_start[:, None]).astype(jnp.int32), axis=1), N_EXPERTS - 1)
    order = jnp.argsort(e, stable=True).astype(jnp.int32)
    starts = jnp.cumsum(counts) - counts
    slot = jnp.arange(n_blocks * blk, dtype=jnp.int32)
    slot_e = jnp.repeat(blk_exp, blk)
    r_in_e = slot - p_starts[slot_e]
    src = order[jnp.clip(starts[slot_e] + r_in_e, 0, 2 * M - 1)] // 2
    slot_tok = jnp.where(r_in_e < counts[slot_e], src, slot % M)
    xg = x[slot_tok]
    yb = _experts(xg, blk_exp, P['w_exp_gate'], P['w_exp_up'], P['w_exp_down'], l)
    d2 = dest.reshape(M, 2)
    return _combine_ln(x, yb[d2[:, 0]], yb[d2[:, 1]], gates, P['ln_g'][l, 2], P['ln_b'][l, 2])


def _trunk(x, pos0, mem_k, mem_v, states, P, Wc, Wf, is_prompt, prec0):
    B, T, D = x.shape
    ret_S, rwkv_S, shift, mC, mn, mm, conv, fk, fv, flf = states
    xf = x.reshape(B * T, D)
    W0 = Wf if prec0 else Wc
    hq, hk, hv, hg, hb = _project(xf, W0['w_in0'], (256, 256, 512, 512, RWKV_PROJ), prec0)
    L_ret = _tile(T, RET_CHUNK)
    out_a, ret_S = _retention(hq, hk, hv, hg, ret_S, P['ret_gn_g'], B, T, pos0, L_ret, prec0)
    hb3 = hb.reshape(B, T, RWKV_PROJ)
    new_shift = hb3[:, -1:]
    L_rwkv = min(RWKV_CHUNK, T)
    if T > L_rwkv:
        assert not prec0
        pre = _rwkv_prep(hb, shift, P, L_rwkv, False, seqs=(B, T))
        out_b, rwkv_S = _rwkv_scan(pre, rwkv_S, P['rwkv_gn_g'], P['rwkv_gn_b'], B, T, L_rwkv)
    else:
        prev = jnp.concatenate([shift, hb3[:, :-1]], axis=1).reshape(B * T, RWKV_PROJ)
        pre = _rwkv_prep(hb, prev, P, T, prec0)
        out_b, rwkv_S = _rwkv_single_chunk(pre, rwkv_S, P['rwkv_gn_g'], P['rwkv_gn_b'], B, T, prec0)
    xf = _out_proj_ln(xf, out_a, out_b, W0['w_out0a'], W0['w_out0b'], P['ln_g'][0, 0], P['ln_b'][0, 0], prec0)
    xf = _mem_attn_ln(xf, mem_k, mem_v, 0, W0['w_mem_q'][0], W0['w_mem_o'][0], P['ln_g'][0, 1], P['ln_b'][0, 1],
                      B, T, prec0)
    xf = _moe_ln(xf, P, 0)
    hqk, hv1, ho, fq, fkk, fvv, hgate = _project(xf, Wc['w_in1'], (512, 512, 512, 512, 512, 512, 128))
    out_c, mC, mn, mm, conv = _mlstm(hqk, hv1, ho, hgate, conv, mC, mn, mm, P, Wc['gate_bias'], B, T, L_ret)
    if is_prompt:
        qa, kn, vc, kt, va, lf = _fox_prep_prompt(fq, fkk, fvv, hgate, P, Wc['gate_bias'], B, T)
        out_d = _fox_prompt(qa, kt, va, B, T)
        logf = lf[:, 2 * MLSTM_HEADS:2 * MLSTM_HEADS + FOX_HEADS].reshape(B, T, FOX_HEADS)
    else:
        qn, kn, lf = _fox_prep(fq, fkk, hgate, P, Wc['gate_bias'])
        vc = fvv
        logf = lf[:, 2 * MLSTM_HEADS:2 * MLSTM_HEADS + FOX_HEADS].reshape(B, T, FOX_HEADS)
        c_all = jnp.cumsum(jnp.concatenate([flf, logf], axis=1), axis=1)
        out_d = _fox_sample(qn, kn, fvv, fk, fv, c_all, B, T)
    xf = _out_proj_ln(xf, out_c, out_d, Wc['w_out1a'], Wc['w_out1b'], P['ln_g'][1, 0], P['ln_b'][1, 0])
    xf = _mem_attn_ln(xf, mem_k, mem_v, 1, Wc['w_mem_q'][1], Wc['w_mem_o'][1], P['ln_g'][1, 1], P['ln_b'][1, 1],
                      B, T)
    xf = _moe_ln(xf, P, 1)
    fk_new = kn.reshape(B, T, FOX_HEADS, FOX_HD)
    fv_new = vc.reshape(B, T, FOX_HEADS, FOX_HD)
    return (xf.reshape(B, T, D), ret_S, rwkv_S, new_shift, mC, mn, mm, conv, fk_new, fv_new, logf)


def kernel(x_prompt, x_sample, mem_prompt, state_ret, state_rwkv, cache_rwkv_shift, state_mlstm_c, state_mlstm_n,
           state_mlstm_m, cache_mlstm_conv, cache_fox_k, cache_fox_v, cache_fox_logf, cache_mem_k, cache_mem_v,
           w_in0, ret_gn_g, rwkv_mu, rwkv_w0, rwkv_w2, rwkv_a0, rwkv_a2, rwkv_g2, rwkv_k_k, rwkv_k_a, rwkv_r_k,
           rwkv_gn_g, rwkv_gn_b, w_out0, w_in1, mlstm_conv_w, mlstm_conv_b, mlstm_b_i, mlstm_b_f, mlstm_gn_g,
           fox_q_g, fox_k_g, fox_b_f, w_out1, w_mem_q, w_mem_k, w_mem_v, w_mem_o, w_router, b_router,
           w_exp_gate, w_exp_up, w_exp_down, ln_g, ln_b):
    P = dict(ret_gn_g=ret_gn_g, rwkv_mu=rwkv_mu, rwkv_w0=rwkv_w0, rwkv_w2=rwkv_w2, rwkv_a0=rwkv_a0,
             rwkv_a2=rwkv_a2, rwkv_g2=rwkv_g2, rwkv_k_k=rwkv_k_k, rwkv_k_a=rwkv_k_a, rwkv_r_k=rwkv_r_k,
             rwkv_gn_g=rwkv_gn_g, rwkv_gn_b=rwkv_gn_b, mlstm_conv_w=mlstm_conv_w, mlstm_conv_b=mlstm_conv_b,
             mlstm_gn_g=mlstm_gn_g, fox_q_g=fox_q_g, fox_k_g=fox_k_g, w_router=w_router, b_router=b_router,
             w_exp_gate=w_exp_gate, w_exp_up=w_exp_up, w_exp_down=w_exp_down, ln_g=ln_g, ln_b=ln_b)
    B, M = mem_prompt.shape[0], mem_prompt.shape[1]
    D = D_MODEL
    H = MLSTM_HEADS
    ret_proj = 2 * RET_HEADS * RET_DK + 2 * RET_HEADS * RET_DV
    qk1, w1 = 2 * H * MLSTM_DK, H * MLSTM_DV
    off = qk1 + w1
    mlstm_proj = off + 2 * H + w1
    fw = FOX_HEADS * FOX_HD
    gate_cols = jnp.concatenate([w_in1[:, off:off + 2 * H], w_in1[:, mlstm_proj + 3 * fw:],
                                 jnp.zeros((D, 128 - 2 * H - FOX_HEADS), F32)], axis=1)
    w_in1_c = jnp.concatenate([w_in1[:, :off], w_in1[:, off + 2 * H:mlstm_proj],
                               w_in1[:, mlstm_proj:mlstm_proj + 3 * fw], gate_cols], axis=1)
    gate_bias = jnp.concatenate([mlstm_b_i, mlstm_b_f, fox_b_f, jnp.zeros((128 - 2 * H - FOX_HEADS,), F32)]).reshape(1, 128)
    wa0, wb0 = w_out0[:RET_HEADS * RET_DV], w_out0[RET_HEADS * RET_DV:]
    Wf = dict(w_in0=_hi_lo(w_in0), w_out0a=_hi_lo(wa0), w_out0b=_hi_lo(wb0),
              w_mem_q=[_hi_lo(w_mem_q[0])], w_mem_o=[_hi_lo(w_mem_o[0])])
    Wc = dict(w_in0=w_in0.astype(BF16), w_in1=w_in1_c.astype(BF16), gate_bias=gate_bias,
              w_out0a=wa0.astype(BF16), w_out0b=wb0.astype(BF16),
              w_out1a=w_out1[:w1].astype(BF16), w_out1b=w_out1[w1:].astype(BF16),
              w_mem_q=w_mem_q.astype(BF16), w_mem_o=w_mem_o.astype(BF16))
    w_kv = jnp.concatenate([w_mem_k, w_mem_v], axis=0)
    memkv = _mem_project(mem_prompt.reshape(B * M, D), jnp.swapaxes(_hi_lo(w_kv), 0, 1))
    p_mem_k = memkv[:DEPTH].reshape(DEPTH * B, M, D)
    p_mem_v = memkv[DEPTH:].reshape(DEPTH * B, M, D)
    zeros = lambda *s: jnp.zeros(s, F32)
    prompt_states = (zeros(B, RET_HEADS, RET_DK, RET_DV), zeros(B, RWKV_HEADS, RWKV_HD, RWKV_HD),
                     zeros(B, 1, RWKV_PROJ), zeros(B, H, MLSTM_DK, MLSTM_DV), zeros(B, H, MLSTM_DK), zeros(B, H),
                     zeros(B, MLSTM_CONV - 1, qk1), None, None, None)
    p_out = _trunk(x_prompt, 0, p_mem_k, p_mem_v, prompt_states, P, Wc, Wf, True, False)
    DB = x_sample.shape[0]
    sample_states = (state_ret, state_rwkv, cache_rwkv_shift, state_mlstm_c, state_mlstm_n, state_mlstm_m,
                     cache_mlstm_conv, cache_fox_k, cache_fox_v, cache_fox_logf)
    s_out = _trunk(x_sample, cache_fox_k.shape[1], cache_mem_k.reshape(DEPTH * DB, M, D),
                   cache_mem_v.reshape(DEPTH * DB, M, D), sample_states, P, Wc, Wf, False, True)
    mem_shape = (DEPTH, B, M, MEM_HEADS, MEM_HD)
    return ((p_out[0], s_out[0]) + p_out[1:] + (p_mem_k.reshape(mem_shape), p_mem_v.reshape(mem_shape)) + s_out[1:])
```

```python
import functools
import math

import numpy as np
import jax
import jax.numpy as jnp
from jax import lax
from jax.experimental import pallas as pl
from jax.experimental.pallas import tpu as pltpu

F32 = jnp.float32
BF16 = jnp.bfloat16

D_MODEL = 1024
DEPTH = 2
RET_HEADS, RET_DK, RET_DV = 4, 64, 128
ROPE_BASE = 10000.0
RWKV_HEADS, RWKV_HD = 8, 64
RWKV_W = RWKV_HEADS * RWKV_HD
RWKV_PROJ = 3 * RWKV_W + 64 + 64 + 128
MLSTM_HEADS, MLSTM_DK, MLSTM_DV, MLSTM_CONV = 4, 64, 128, 4
FOX_HEADS, FOX_HD = 4, 128
MEM_HEADS, MEM_HD = 4, 256
N_EXPERTS, N_GROUPS, EXPERTS_PER_GROUP = 16, 4, 4
D_EXPERT = 512
ALPHA = (2 * DEPTH) ** 0.25
EPS = 1e-5
NEG_INF = -1e30
LOG2E = 1.4426950408889634

VMEM_LIMIT_BYTES = 56 * 1024 * 1024
ROW_TILE = 512
ROW_TILE_WIDE = 1024
RET_CHUNK = 256
RWKV_CHUNK = 64
RWKV_ROWS = 512
RWKV_LOCAL_ROWS = 1024
RWKV_SEQS = 8
FOX_TQ = 1024
MEM_SEQS = 4
MOE_BLOCK = 512
MOE_BLOCK_SMALL = 128


def _cparams(sem):
    return pltpu.CompilerParams(dimension_semantics=sem, vmem_limit_bytes=VMEM_LIMIT_BYTES)


def _tile(n, pref):
    if n <= pref:
        return n
    t = pref
    while t >= 8:
        if n % t == 0:
            return t
        t -= 8
    return n


_NN = (((1,), (0,)), ((), ()))
_NT = (((1,), (1,)), ((), ()))
_TN = (((0,), (0,)), ((), ()))
_BNN = (((2,), (1,)), ((0,), (0,)))
_BNT = (((2,), (2,)), ((0,), (0,)))
_BTN = (((1,), (1,)), ((0,), (0,)))


def _split2(a):
    a = a.astype(F32)
    hi = a.astype(BF16)
    return hi, (a - hi.astype(F32)).astype(BF16)


def _dg(a, b, dims, prec):
    if not prec:
        return lax.dot_general(a.astype(BF16), b.astype(BF16), dims, preferred_element_type=F32)
    a1, a2 = _split2(a)
    b1, b2 = _split2(b)
    d = lambda p, q: lax.dot_general(p, q, dims, preferred_element_type=F32)
    return d(a1, b1) + (d(a1, b2) + d(a2, b1))


def _dot(a, b, prec=False):
    return _dg(a, b, _NN, prec)


def _wdot(x, w_ref, prec):
    if not prec:
        return jnp.dot(x.astype(BF16), w_ref[...], preferred_element_type=F32)
    x1, x2 = _split2(x)
    d = lambda p, q: jnp.dot(p, q, preferred_element_type=F32)
    return d(x1, w_ref[0]) + (d(x1, w_ref[1]) + d(x2, w_ref[0]))


def _wspec(w):
    return pl.BlockSpec(w.shape, lambda *_: (0,) * w.ndim)


def _hi_lo(w):
    bits = lax.bitcast_convert_type(w, jnp.uint32) & jnp.uint32(0xFFFF0000)
    hi = lax.bitcast_convert_type(bits, F32)
    return jnp.stack([hi.astype(BF16), (w - hi).astype(BF16)])


def _dot_nt(a, b, prec=False):
    return _dg(a, b, _NT, prec)


def _dot_tn(a, b, prec=False):
    return _dg(a, b, _TN, prec)


def _split3(a):
    a1 = a.astype(BF16)
    r1 = a - a1.astype(F32)
    a2 = r1.astype(BF16)
    a3 = (r1 - a2.astype(F32)).astype(BF16)
    return a1, a2, a3


def _exact_left_dot(e, a):
    a1, a2, a3 = _split3(a)
    d = lambda p: jnp.dot(e, p, preferred_element_type=F32)
    return d(a1) + d(a2) + d(a3)


def _exact_right_dot(a, e, parts=3):
    d = lambda p: jnp.dot(p, e, preferred_element_type=F32)
    return sum(d(p) for p in _split3(a)[:parts])


def _sigmoid(x):
    return 1.0 / (1.0 + jnp.exp(-x))


def _silu(x):
    return x * _sigmoid(x)


def _softplus(x):
    return jnp.maximum(x, 0.0) + jnp.log1p(jnp.exp(-jnp.abs(x)))


def _log_sigmoid(x):
    return -_softplus(-x)


def _layer_norm(z, g, b):
    mu = jnp.mean(z, axis=-1, keepdims=True)
    d = z - mu
    var = jnp.mean(d * d, axis=-1, keepdims=True)
    return d * lax.rsqrt(var + EPS) * g + b


def _lane_norm(y):
    mu = jnp.mean(y, axis=-1, keepdims=True)
    d = y - mu
    var = jnp.mean(d * d, axis=-1, keepdims=True)
    return d * lax.rsqrt(var + EPS)


def _proj_body(x_ref, w_ref, *o_refs, widths, prec):
    h = _wdot(x_ref[...], w_ref, prec)
    off = 0
    for o_ref, wd in zip(o_refs, widths):
        o_ref[...] = h[:, off:off + wd]
        off += wd


def _project(x, w, widths, prec=False):
    M, K = x.shape
    tm = _tile(M, ROW_TILE)
    return pl.pallas_call(
        functools.partial(_proj_body, widths=widths, prec=prec),
        grid=(M // tm,),
        in_specs=[pl.BlockSpec((tm, K), lambda i: (i, 0)), _wspec(w)],
        out_specs=[pl.BlockSpec((tm, wd), lambda i: (i, 0)) for wd in widths],
        out_shape=[jax.ShapeDtypeStruct((M, wd), F32) for wd in widths],
        compiler_params=_cparams(("parallel",)),
        name="project",
    )(x, w)


def _mem_proj_body(x_ref, w_ref, o_ref):
    o_ref[0] = _wdot(x_ref[...], w_ref.at[0], True)


def _mem_project(x, w):
    M, K = x.shape
    J, _, _, N = w.shape
    return pl.pallas_call(
        _mem_proj_body,
        grid=(J,),
        in_specs=[pl.BlockSpec((M, K), lambda j: (0, 0)),
                  pl.BlockSpec((1, 2, K, N), lambda j: (j, 0, 0, 0))],
        out_specs=pl.BlockSpec((1, M, N), lambda j: (j, 0, 0)),
        out_shape=jax.ShapeDtypeStruct((J, M, N), F32),
        compiler_params=_cparams(("parallel",)),
        name="mem_project",
    )(x, w)


def _out_ln_body(x_ref, a_ref, b_ref, wa_ref, wb_ref, g_ref, bias_ref, o_ref, *, prec):
    mix = _wdot(a_ref[...], wa_ref, prec) + _wdot(b_ref[...], wb_ref, prec)
    o_ref[...] = _layer_norm(ALPHA * x_ref[...] + mix, g_ref[...], bias_ref[...])


def _out_proj_ln(x, a, b, wa, wb, g, bias, prec=False):
    M, D = x.shape
    Ka, Kb = a.shape[1], b.shape[1]
    tm = _tile(M, ROW_TILE_WIDE)
    row = lambda i: (i, 0)
    fix = lambda i: (0, 0)
    return pl.pallas_call(
        functools.partial(_out_ln_body, prec=prec),
        grid=(M // tm,),
        in_specs=[pl.BlockSpec((tm, D), row), pl.BlockSpec((tm, Ka), row), pl.BlockSpec((tm, Kb), row),
                  _wspec(wa), _wspec(wb),
                  pl.BlockSpec((1, D), fix), pl.BlockSpec((1, D), fix)],
        out_specs=pl.BlockSpec((tm, D), row),
        out_shape=jax.ShapeDtypeStruct((M, D), F32),
        compiler_params=_cparams(("parallel",)),
        name="out_proj_ln",
    )(x, a, b, wa, wb, g.reshape(1, D), bias.reshape(1, D))


def _ret_body(q_ref, k_ref, v_ref, g_ref, cos_ref, sa_ref, sb_ref, dmat_ref, qdec_ref, kdec_ref, sdec_ref,
              s0_ref, gn_ref, o_ref, sout_ref, s_scr, *, prec):
    i = pl.program_id(1)

    @pl.when(i == 0)
    def _():
        s_scr[...] = s0_ref[0]

    cos, sin_a, sin_b = cos_ref[...], sa_ref[...], sb_ref[...]
    width = RET_HEADS * RET_DK
    half = RET_DK // 2

    def rope(x):
        return x * cos + pltpu.roll(x, width - half, 1) * sin_a + pltpu.roll(x, half, 1) * sin_b

    q = rope(q_ref[...])
    k = rope(k_ref[...]) * (RET_DK ** -0.5)
    v = v_ref[...]
    gate = g_ref[...]
    H = RET_HEADS
    qh = jnp.stack([q[:, h * RET_DK:(h + 1) * RET_DK] for h in range(H)], axis=0)
    kh = jnp.stack([k[:, h * RET_DK:(h + 1) * RET_DK] for h in range(H)], axis=0)
    vh = jnp.stack([v[:, h * RET_DV:(h + 1) * RET_DV] for h in range(H)], axis=0)
    s_old = s_scr[...]
    s = _dg(qh, kh, _BNT, prec) * dmat_ref[...]
    o = _dg(s, vh, _BNN, prec) + _dg(qh, s_old, _BNN, prec) * qdec_ref[...]
    s_scr[...] = sdec_ref[...] * s_old + _dg(kh * kdec_ref[...], vh, _BTN, prec)
    on = _lane_norm(o)
    hn = jnp.concatenate([on[h] for h in range(H)], axis=1) * gn_ref[...]
    o_ref[...] = _silu(gate) * hn

    @pl.when(i == pl.num_programs(1) - 1)
    def _():
        sout_ref[0] = s_scr[...]


def _retention(hq, hk, hv, hg, s0, gn_g, B, T, pos0, L, prec=False):
    nT = T // L
    lg = np.log(1.0 - 2.0 ** (-5.0 - np.arange(RET_HEADS)))
    idx = np.arange(L, dtype=np.float64)
    rel = idx[:, None] - idx[None, :]
    dmat = np.where(rel >= 0, np.exp(np.maximum(rel, 0.0)[None] * lg[:, None, None]), 0.0)
    qdec = np.exp((idx + 1.0)[None, :, None] * lg[:, None, None])
    kdec = np.exp((L - 1.0 - idx)[None, :, None] * lg[:, None, None])
    sdec = np.exp(L * lg)[:, None, None]
    half = RET_DK // 2
    inv = ROPE_BASE ** (-jnp.arange(half, dtype=F32) / half)
    ang = (pos0 + jnp.arange(T)).astype(F32)[:, None] * inv[None, :]
    cos, sin = jnp.cos(ang), jnp.sin(ang)
    zero = jnp.zeros_like(sin)
    cos_t = jnp.tile(jnp.concatenate([cos, cos], axis=1), (1, RET_HEADS))
    sin_a = jnp.tile(jnp.concatenate([-sin, zero], axis=1), (1, RET_HEADS))
    sin_b = jnp.tile(jnp.concatenate([zero, sin], axis=1), (1, RET_HEADS))
    qk_w = RET_HEADS * RET_DK
    v_w = RET_HEADS * RET_DV
    row = lambda b, i: (b * nT + i, 0)
    tab = lambda b, i: (i, 0)
    fix3 = lambda b, i: (0, 0, 0)
    out, s_out = pl.pallas_call(
        functools.partial(_ret_body, prec=prec),
        grid=(B, nT),
        in_specs=[pl.BlockSpec((L, qk_w), row), pl.BlockSpec((L, qk_w), row),
                  pl.BlockSpec((L, v_w), row), pl.BlockSpec((L, v_w), row),
                  pl.BlockSpec((L, qk_w), tab), pl.BlockSpec((L, qk_w), tab), pl.BlockSpec((L, qk_w), tab),
                  pl.BlockSpec((RET_HEADS, L, L), fix3), pl.BlockSpec((RET_HEADS, L, 1), fix3),
                  pl.BlockSpec((RET_HEADS, L, 1), fix3), pl.BlockSpec((RET_HEADS, 1, 1), fix3),
                  pl.BlockSpec((1, RET_HEADS, RET_DK, RET_DV), lambda b, i: (b, 0, 0, 0)),
                  pl.BlockSpec((1, v_w), lambda b, i: (0, 0))],
        out_specs=[pl.BlockSpec((L, v_w), row),
                   pl.BlockSpec((1, RET_HEADS, RET_DK, RET_DV), lambda b, i: (b, 0, 0, 0))],
        out_shape=[jax.ShapeDtypeStruct((B * T, v_w), F32),
                   jax.ShapeDtypeStruct((B, RET_HEADS, RET_DK, RET_DV), F32)],
        scratch_shapes=[pltpu.VMEM((RET_HEADS, RET_DK, RET_DV), F32)],
        compiler_params=_cparams(("parallel", "arbitrary")),
        name="retention",
    )(hq, hk, hv, hg, cos_t, sin_a, sin_b, jnp.asarray(dmat, F32), jnp.asarray(qdec, F32),
      jnp.asarray(kdec, F32), jnp.asarray(sdec, F32), s0, gn_g.reshape(1, v_w))
    return out, s_out


def _rwkv_prep_body(h_ref, p_ref, mu_ref, w0_ref, w2_ref, a0_ref, a2_ref, g2_ref, kk_ref, ka_ref, rk_ref,
                    ones_ref, tri_ref, blk_ref, sel_ref,
                    kq_ref, rq_ref, kt_ref, bt_ref, ke_ref, be_ref, v_ref, g_ref, bon_ref, gt_ref, *scratch,
                    prec, shift_rows):
    hb = h_ref[...]
    if shift_rows:
        carry, = scratch
        tm = hb.shape[0]
        first = jnp.where(pl.program_id(1) == 0, p_ref[0], carry[...])
        row = lax.broadcasted_iota(jnp.int32, (tm, 1), 0)
        prev = jnp.where(row == 0, first, pltpu.roll(hb, 1, 0))
        carry[...] = hb[tm - 1:tm, :]
    else:
        prev = p_ref[...]
    xs = hb + (prev - hb) * mu_ref[...]
    W = RWKV_W
    r, k, v = xs[:, :W], xs[:, W:2 * W], xs[:, 2 * W:3 * W]
    w_lo = xs[:, 3 * W:3 * W + 64]
    a_lo = xs[:, 3 * W + 64:3 * W + 128]
    g_lo = xs[:, 3 * W + 128:]
    w_log = -_softplus(-(w0_ref[...] + _dot(jnp.tanh(w_lo), w2_ref[...], prec))) - 0.5
    lw = -jnp.exp(w_log)
    a = _sigmoid(a0_ref[...] + _dot(a_lo, a2_ref[...], prec))
    g = _dot(_sigmoid(g_lo), g2_ref[...], prec)
    ones = ones_ref[...]
    kk = k * kk_ref[...]
    nrm = jnp.sqrt(_exact_right_dot(kk * kk, ones, 2 if not prec else 3))
    kk = kk / jnp.maximum(nrm, 1e-12)
    k2 = k * (1.0 + (a - 1.0) * ka_ref[...])
    beta = kk * a
    cl = _exact_left_dot(tri_ref[...], lw)
    tot = _exact_left_dot(blk_ref[...], lw)
    ginv = jnp.exp(-cl)
    gend = jnp.exp(tot - cl)
    kq_ref[...] = (kk * jnp.exp(cl - lw)).astype(kq_ref.dtype)
    rq_ref[...] = (r * jnp.exp(cl)).astype(rq_ref.dtype)
    kt_ref[...] = (k2 * ginv).astype(kt_ref.dtype)
    bt_ref[...] = (beta * ginv).astype(bt_ref.dtype)
    ke_ref[...] = (k2 * gend).astype(ke_ref.dtype)
    be_ref[...] = (beta * gend).astype(be_ref.dtype)
    v_ref[...] = v.astype(v_ref.dtype)
    gt_ref[...] = jnp.exp(_exact_left_dot(sel_ref[...], lw))
    g_ref[...] = g
    bon_ref[...] = _exact_right_dot(r * k2 * rk_ref[...], ones, 2 if not prec else 3) * v


def _rwkv_prep(hb, prev, P, L, prec=False, seqs=None):
    M = hb.shape[0]
    tm = _tile(M if seqs is None else seqs[1], ROW_TILE)
    assert tm % L == 0
    W = RWKV_W
    idx = np.arange(tm)
    same = (idx[:, None] // L) == (idx[None, :] // L)
    tri = jnp.asarray(same & (idx[:, None] >= idx[None, :]), BF16)
    blk = jnp.asarray(same, BF16)
    sel = jnp.asarray((idx[None, :] // L) == np.arange(tm // L)[:, None], BF16)
    lane = np.arange(W)
    ones = jnp.asarray((lane[:, None] // RWKV_HD) == (lane[None, :] // RWKV_HD), BF16)
    if seqs is None:
        grid, nT = (1, M // tm), M // tm
        prev_spec = pl.BlockSpec((tm, RWKV_PROJ), lambda b, i: (i, 0))
        scratch = []
    else:
        grid, nT = (seqs[0], seqs[1] // tm), seqs[1] // tm
        prev_spec = pl.BlockSpec((1, 1, RWKV_PROJ), lambda b, i: (b, 0, 0))
        scratch = [pltpu.VMEM((1, RWKV_PROJ), F32)]
    row = lambda b, i: (b * nT + i, 0)
    vec = lambda a: a.reshape(1, -1)
    params = [vec(P['rwkv_mu']), vec(P['rwkv_w0']), P['rwkv_w2'], vec(P['rwkv_a0']),
              P['rwkv_a2'], P['rwkv_g2'], vec(P['rwkv_k_k']), vec(P['rwkv_k_a']),
              vec(P['rwkv_r_k']), ones, tri, blk, sel]
    out_dt = [F32 if prec else BF16] * 7 + [F32] * 2
    return pl.pallas_call(
        functools.partial(_rwkv_prep_body, prec=prec, shift_rows=seqs is not None),
        grid=grid,
        in_specs=[pl.BlockSpec((tm, RWKV_PROJ), row), prev_spec] + [_wspec(a) for a in params],
        out_specs=[pl.BlockSpec((tm, W), row) for _ in out_dt] + [pl.BlockSpec((tm // L, W), row)],
        out_shape=[jax.ShapeDtypeStruct((M, W), dt) for dt in out_dt] + [jax.ShapeDtypeStruct((M // L, W), F32)],
        scratch_shapes=scratch,
        compiler_params=_cparams(("parallel", "arbitrary")),
        name="rwkv_prep",
    )(hb, prev, *params)


def _rwkv_masks(L):
    ii = lax.broadcasted_iota(jnp.int32, (L, L), 0)
    jj = lax.broadcasted_iota(jnp.int32, (L, L), 1)
    return ii > jj, ii >= jj, (ii == jj).astype(F32)


def _rwkv_local(kq, rq, kt, bt, ke, be, v, masks, lmask_ref, nlev, prec):
    strict, incl, eye = masks
    L = kq.shape[1]
    x = jnp.concatenate([kq, rq], axis=1)
    sk = _dg(x, kt, _BNT, prec)
    sb = _dg(x, bt, _BNT, prec)
    n_m = jnp.where(strict, sk[:, :L], 0.0)
    a_k = jnp.where(incl, sk[:, L:], 0.0)
    m_m = jnp.where(strict, sb[:, :L], 0.0)
    a_b = jnp.where(incl, sb[:, L:], 0.0)
    t_m = eye - m_m * lmask_ref[0]
    for lv in range(1, nlev):
        c_m = m_m * lmask_ref[lv]
        t_m = t_m - _dg(_dg(t_m, c_m, _BNN, prec), t_m, _BNN, prec)
    kqp = _dg(t_m, kq, _BNN, prec)
    u0 = _dg(t_m, _dg(n_m, v, _BNN, prec), _BNN, prec)
    rqp = rq.astype(F32) - _dg(a_b, kqp, _BNN, prec)
    y0 = _dg(a_k, v, _BNN, prec) - _dg(a_b, u0, _BNN, prec)
    p_m = _dg(kqp, be, _BTN, prec)
    b_c = _dg(v, ke, _BTN, prec) - _dg(u0, be, _BTN, prec)
    return rqp, y0, p_m, b_c


def _rwkv_apply(s_old, rqp, y0, p_m, b_c, g_end, prec):
    y = _dg(rqp, s_old, _BNT, prec) + y0
    return y, s_old * g_end - _dg(s_old, p_m, _BNN, prec) + b_c


def _head_stack(tile, nblk):
    hd = RWKV_HD
    L = tile.shape[0] // nblk
    return jnp.concatenate([tile[:, hh * hd:(hh + 1) * hd].reshape(nblk, L, hd) for hh in range(2)], axis=0)


def _head_unstack(x, nblk):
    L, hd = x.shape[1], x.shape[2]
    return jnp.concatenate([x[hh * nblk:(hh + 1) * nblk].reshape(nblk * L, hd) for hh in range(2)], axis=1)


def _rwkv_local_body(kq_ref, rq_ref, kt_ref, bt_ref, ke_ref, be_ref, v_ref, lmask_ref,
                     rqp_ref, y0_ref, pm_ref, bc_ref, *, L, nchunk, nlev):
    ops = [_head_stack(r[...], nchunk) for r in (kq_ref, rq_ref, kt_ref, bt_ref, ke_ref, be_ref, v_ref)]
    res = _rwkv_local(*ops, _rwkv_masks(L), lmask_ref, nlev, False)
    for o_ref, a in zip((rqp_ref, y0_ref, pm_ref, bc_ref), res):
        o_ref[...] = _head_unstack(a, nchunk).astype(o_ref.dtype)


def _rwkv_apply_body(rqp_ref, y0_ref, pm_ref, bc_ref, gt_ref, g_ref, bon_ref, s0_ref, gng_ref, gnb_ref,
                     o_ref, sout_ref, s_scr, *, L, nchunk, B):
    i = pl.program_id(0)
    H, hd = RWKV_HEADS, RWKV_HD

    @pl.when(i == 0)
    def _():
        s_scr[...] = s0_ref[...].reshape(B * H, hd, hd)

    def heads(x):
        return jnp.stack([x[b][:, h * hd:(h + 1) * hd] for b in range(B) for h in range(H)], axis=0)

    def chunk(c, carry):
        rows = pl.ds(pl.multiple_of(c * L, L), L)
        ops = [heads(r[:, rows, :]) for r in (rqp_ref, y0_ref, pm_ref, bc_ref)]
        g_end = heads(gt_ref[:, pl.ds(c, 1), :])
        y, s_new = _rwkv_apply(s_scr[...], *ops, g_end, False)
        s_scr[...] = s_new
        yn = _lane_norm(y)
        for b in range(B):
            ynb = jnp.concatenate([yn[b * H + h] for h in range(H)], axis=1)
            o_ref[b, rows, :] = (ynb * gng_ref[...] + gnb_ref[...] + bon_ref[b, rows, :]) * g_ref[b, rows, :]
        return carry

    lax.fori_loop(0, nchunk, chunk, 0)

    @pl.when(i == pl.num_programs(0) - 1)
    def _():
        sout_ref[...] = s_scr[...].reshape(B, H, hd, hd)


def _rwkv_fused_body(kq_ref, rq_ref, kt_ref, bt_ref, ke_ref, be_ref, v_ref, g_ref, bon_ref, gt_ref,
                     s0_ref, gng_ref, gnb_ref, lmask_ref, o_ref, sout_ref, *, L, nseq, nlev, prec):
    hd = RWKV_HD
    ops = [_head_stack(r[...], nseq) for r in (kq_ref, rq_ref, kt_ref, bt_ref, ke_ref, be_ref, v_ref)]
    loc = _rwkv_local(*ops, _rwkv_masks(L), lmask_ref, nlev, prec)
    gt = gt_ref[...]
    g_end = jnp.concatenate([gt[:, :, hh * hd:(hh + 1) * hd] for hh in range(2)], axis=0)
    s_old = jnp.concatenate([s0_ref[:, hh] for hh in range(2)], axis=0)
    y, s_new = _rwkv_apply(s_old, *loc, g_end, prec)
    for hh in range(2):
        sout_ref[:, hh] = s_new[hh * nseq:(hh + 1) * nseq]
    yn = _head_unstack(_lane_norm(y), nseq)
    o_ref[...] = (yn * gng_ref[...] + gnb_ref[...] + bon_ref[...]) * g_ref[...]


def _rwkv_level_masks(L):
    nlev = int(math.log2(L))
    idx = np.arange(L)
    ii, jj = idx[:, None], idx[None, :]
    lmask = np.stack([((ii >> (lv + 1)) == (jj >> (lv + 1))) & ((ii & (1 << lv)) != 0) & ((jj & (1 << lv)) == 0)
                      for lv in range(nlev)]).astype(np.float32)
    return nlev, jnp.asarray(lmask)


def _rwkv_scan(pre, s0, gn_g, gn_b, B, T, L):
    W, H, hd = RWKV_W, RWKV_HEADS, RWKV_HD
    assert L == hd
    kq, rq, kt, bt, ke, be, v, g, bon, gt = pre
    nlev, lmask = _rwkv_level_masks(L)
    tl = _tile(T, RWKV_LOCAL_ROWS)
    nL = T // tl
    pw = 2 * hd
    row = lambda b, p, i: (b * nL + i, p)
    rqp, y0, p_m, b_c = pl.pallas_call(
        functools.partial(_rwkv_local_body, L=L, nchunk=tl // L, nlev=nlev),
        grid=(B, H // 2, nL),
        in_specs=[pl.BlockSpec((tl, pw), row) for _ in range(7)]
                 + [pl.BlockSpec((nlev, L, L), lambda b, p, i: (0, 0, 0))],
        out_specs=[pl.BlockSpec((tl, pw), row) for _ in range(4)],
        out_shape=[jax.ShapeDtypeStruct((B * T, W), dt) for dt in (BF16, F32, BF16, F32)],
        compiler_params=_cparams(("parallel", "parallel", "parallel")),
        name="rwkv_local",
    )(kq, rq, kt, bt, ke, be, v, lmask)
    tb = _tile(T, RWKV_ROWS)
    nT = T // tb
    nchunk = tb // L
    r3 = lambda a: a.reshape(B, T, W)
    blk = lambda i: (0, i, 0)
    fix2 = lambda i: (0, 0)
    fix4 = lambda i: (0, 0, 0, 0)
    out, s_out = pl.pallas_call(
        functools.partial(_rwkv_apply_body, L=L, nchunk=nchunk, B=B),
        grid=(nT,),
        in_specs=[pl.BlockSpec((B, tb, W), blk) for _ in range(4)]
                 + [pl.BlockSpec((B, nchunk, W), blk), pl.BlockSpec((B, tb, W), blk), pl.BlockSpec((B, tb, W), blk),
                    pl.BlockSpec((B, H, hd, hd), fix4), pl.BlockSpec((1, W), fix2), pl.BlockSpec((1, W), fix2)],
        out_specs=[pl.BlockSpec((B, tb, W), blk), pl.BlockSpec((B, H, hd, hd), fix4)],
        out_shape=[jax.ShapeDtypeStruct((B, T, W), F32), jax.ShapeDtypeStruct((B, H, hd, hd), F32)],
        scratch_shapes=[pltpu.VMEM((B * H, hd, hd), F32)],
        compiler_params=_cparams(("arbitrary",)),
        name="rwkv_apply",
    )(r3(rqp), r3(y0), r3(p_m), r3(b_c), gt.reshape(B, T // L, W), r3(g), r3(bon), s0,
      gn_g.reshape(1, W), gn_b.reshape(1, W))
    return out.reshape(B * T, W), s_out


def _rwkv_single_chunk(pre, s0, gn_g, gn_b, B, T, prec):
    W, H, hd = RWKV_W, RWKV_HEADS, RWKV_HD
    kq, rq, kt, bt, ke, be, v, g, bon, gt = pre
    nlev, lmask = _rwkv_level_masks(T)
    nseq = RWKV_SEQS if B % RWKV_SEQS == 0 else B
    pw = 2 * hd
    row = lambda i, p: (i, p)
    st = lambda i, p: (i, p, 0, 0)
    vec = lambda i, p: (0, p)
    out, s_out = pl.pallas_call(
        functools.partial(_rwkv_fused_body, L=T, nseq=nseq, nlev=nlev, prec=prec),
        grid=(B // nseq, H // 2),
        in_specs=[pl.BlockSpec((nseq * T, pw), row) for _ in range(9)]
                 + [pl.BlockSpec((nseq, 1, pw), lambda i, p: (i, 0, p)), pl.BlockSpec((nseq, 2, hd, hd), st),
                    pl.BlockSpec((1, pw), vec), pl.BlockSpec((1, pw), vec),
                    pl.BlockSpec((nlev, T, T), lambda i, p: (0, 0, 0))],
        out_specs=[pl.BlockSpec((nseq * T, pw), row), pl.BlockSpec((nseq, 2, hd, hd), st)],
        out_shape=[jax.ShapeDtypeStruct((B * T, W), F32), jax.ShapeDtypeStruct((B, H, hd, hd), F32)],
        compiler_params=_cparams(("parallel", "parallel")),
        name="rwkv_single_chunk",
    )(kq, rq, kt, bt, ke, be, v, g, bon, gt.reshape(B, 1, W), s0, gn_g.reshape(1, W), gn_b.reshape(1, W), lmask)
    return out, s_out


def _mlstm_body(qk_ref, v_ref, o_ref, gt_ref, cprev_ref, c0_ref, n0_ref, m0_ref, cw_ref, cb_ref, gb_ref,
                gn_ref, tri_ref, out_ref, cout_ref, nout_ref, mout_ref, convout_ref,
                xpad, c_scr, m_scr, *, L):
    i = pl.program_id(1)
    H, DK, DV = MLSTM_HEADS, MLSTM_DK, MLSTM_DV
    K = MLSTM_CONV - 1
    base = 8 - K

    @pl.when(i == 0)
    def _():
        xpad[base:8, :] = cprev_ref[0]
        for h in range(H):
            c_scr[h, :, 0:DV] = c0_ref[0, h]
            c_scr[h, :, DV:2 * DV] = jnp.broadcast_to(n0_ref[0, h], (DK, DV))
        m_scr[...] = m0_ref[0]

    xpad[8:8 + L, :] = qk_ref[...]
    conv = cb_ref[...] + xpad[pl.ds(base, L), :] * cw_ref[0:1, :]
    for j in range(1, MLSTM_CONV):
        conv = conv + xpad[pl.ds(base + j, L), :] * cw_ref[j:j + 1, :]
    tail = xpad[pl.ds(8 + L - K, K), :]
    xpad[base:8, :] = tail
    qk = _silu(conv)
    q = qk[:, :H * DK] * (DK ** -0.5)
    k = qk[:, H * DK:]
    v = v_ref[...]
    z = gt_ref[...] + gb_ref[...]
    lf = _log_sigmoid(z)
    bcum = _exact_left_dot(tri_ref[...], lf)
    z_t = z.T
    b_t = bcum.T
    ii = lax.broadcasted_iota(jnp.int32, (L, L), 0)
    jj = lax.broadcasted_iota(jnp.int32, (L, L), 1)
    causal = ii >= jj
    ones = jnp.ones((L, DV), F32)
    m_all = m_scr[...]
    lane = lax.broadcasted_iota(jnp.int32, (1, 128), 1)
    stack = lambda f: jnp.stack([f(h) for h in range(H)], axis=0)
    qh = stack(lambda h: q[:, h * DK:(h + 1) * DK])
    kh = stack(lambda h: k[:, h * DK:(h + 1) * DK])
    vh = stack(lambda h: jnp.concatenate([v[:, h * DV:(h + 1) * DV], ones], axis=1))
    b_col = stack(lambda h: bcum[:, H + h:H + h + 1])
    ig_col = stack(lambda h: z[:, h:h + 1])
    row_term = stack(lambda h: z_t[h:h + 1, :] - b_t[H + h:H + h + 1, :])
    m0 = stack(lambda h: m_all[:, h:h + 1])
    log_d = jnp.where(causal, b_col + row_term, NEG_INF)
    m_inter = b_col + m0
    m_t = jnp.maximum(m_inter, jnp.max(log_d, axis=-1, keepdims=True))
    w_d = jnp.exp(log_d - m_t)
    w_i = jnp.exp(m_inter - m_t)
    c_old = c_scr[...]
    s = _dg(qh, kh, _BNT, False) * w_d
    num = _dg(s, vh, _BNN, False) + _dg(qh, c_old, _BNN, False) * w_i
    den = num[:, :, DV:DV + 1]
    denom = jnp.maximum(jnp.abs(den), jnp.exp(-m_t))
    hn = _lane_norm(num[:, :, :DV] / denom)
    m_new = m_t[:, L - 1:L, :]
    b_last = b_col[:, L - 1:L, :]
    w_s = jnp.exp(b_last + m0 - m_new)
    w_k = jnp.exp(b_last - b_col + ig_col - m_new)
    c_scr[...] = w_s * c_old + _dg(kh * w_k, vh, _BTN, False)
    m_new_all = m_all
    for h in range(H):
        m_new_all = jnp.where(lane == h, m_new[h], m_new_all)
    m_scr[...] = m_new_all
    out_ref[...] = jnp.concatenate([hn[h] for h in range(H)], axis=1) * gn_ref[...] * _sigmoid(o_ref[...])

    @pl.when(i == pl.num_programs(1) - 1)
    def _():
        for h in range(H):
            cout_ref[0, h] = c_scr[h, :, 0:DV]
            nout_ref[0, h] = c_scr[h, :, DV:DV + 1]
        mout_ref[0] = m_scr[...]
        convout_ref[0] = tail


def _mlstm(hqk, hv, ho, hgate, conv_prev, c0, n0, m0, P, gate_bias, B, T, L):
    H, DK, DV = MLSTM_HEADS, MLSTM_DK, MLSTM_DV
    nT = T // L
    K = MLSTM_CONV - 1
    W = H * DV
    idx = np.arange(L)
    tri = jnp.asarray(idx[:, None] >= idx[None, :], BF16)
    m0p = jnp.zeros((B, 1, 128), F32).at[:, 0, :H].set(m0)
    row = lambda b, i: (b * nT + i, 0)
    fix = lambda b, i: (0, 0)
    perb3 = lambda b, i: (b, 0, 0)
    perb4 = lambda b, i: (b, 0, 0, 0)
    out, c_out, n_out, m_out, conv_out = pl.pallas_call(
        functools.partial(_mlstm_body, L=L),
        grid=(B, nT),
        in_specs=[pl.BlockSpec((L, W), row), pl.BlockSpec((L, W), row), pl.BlockSpec((L, W), row),
                  pl.BlockSpec((L, 128), row),
                  pl.BlockSpec((1, K, W), perb3),
                  pl.BlockSpec((1, H, DK, DV), perb4), pl.BlockSpec((1, H, DK, 1), perb4),
                  pl.BlockSpec((1, 1, 128), perb3),
                  pl.BlockSpec((MLSTM_CONV, W), fix), pl.BlockSpec((1, W), fix), pl.BlockSpec((1, 128), fix),
                  pl.BlockSpec((1, W), fix), pl.BlockSpec((L, L), fix)],
        out_specs=[pl.BlockSpec((L, W), row),
                   pl.BlockSpec((1, H, DK, DV), perb4), pl.BlockSpec((1, H, DK, 1), perb4),
                   pl.BlockSpec((1, 1, 128), perb3), pl.BlockSpec((1, K, W), perb3)],
        out_shape=[jax.ShapeDtypeStruct((B * T, W), F32),
                   jax.ShapeDtypeStruct((B, H, DK, DV), F32), jax.ShapeDtypeStruct((B, H, DK, 1), F32),
                   jax.ShapeDtypeStruct((B, 1, 128), F32), jax.ShapeDtypeStruct((B, K, W), F32)],
        scratch_shapes=[pltpu.VMEM((L + 8, W), F32), pltpu.VMEM((H, DK, 2 * DV), F32), pltpu.VMEM((1, 128), F32)],
        compiler_params=_cparams(("parallel", "arbitrary")),
        name="mlstm",
    )(hqk, hv, ho, hgate, conv_prev, c0, n0.reshape(B, H, DK, 1), m0p,
      P['mlstm_conv_w'], P['mlstm_conv_b'].reshape(1, W), gate_bias, P['mlstm_gn_g'].reshape(1, W), tri)
    return out, c_out, n_out.reshape(B, H, DK), m_out[:, 0, :H], conv_out


def _fox_prep_body(q_ref, k_ref, gt_ref, qg_ref, kg_ref, gb_ref, qn_ref, kn_ref, lf_ref):
    def rms(x, g):
        outs = []
        for h in range(FOX_HEADS):
            xh = x[:, h * FOX_HD:(h + 1) * FOX_HD]
            outs.append(xh * lax.rsqrt(jnp.mean(xh * xh, axis=-1, keepdims=True) + EPS) * g)
        return jnp.concatenate(outs, axis=1)

    qn_ref[...] = (rms(q_ref[...], qg_ref[...]) * (FOX_HD ** -0.5)).astype(BF16)
    kn_ref[...] = rms(k_ref[...], kg_ref[...])
    lf_ref[...] = _log_sigmoid(gt_ref[...] + gb_ref[...])


def _fox_prep(hq, hk, hgate, P, gate_bias):
    M, W = hq.shape
    tm = _tile(M, ROW_TILE)
    row = lambda i: (i, 0)
    fix = lambda i: (0, 0)
    return pl.pallas_call(
        _fox_prep_body,
        grid=(M // tm,),
        in_specs=[pl.BlockSpec((tm, W), row), pl.BlockSpec((tm, W), row), pl.BlockSpec((tm, 128), row),
                  pl.BlockSpec((1, FOX_HD), fix), pl.BlockSpec((1, FOX_HD), fix), pl.BlockSpec((1, 128), fix)],
        out_specs=[pl.BlockSpec((tm, W), row), pl.BlockSpec((tm, W), row), pl.BlockSpec((tm, 128), row)],
        out_shape=[jax.ShapeDtypeStruct((M, W), BF16), jax.ShapeDtypeStruct((M, W), F32),
                   jax.ShapeDtypeStruct((M, 128), F32)],
        compiler_params=_cparams(("parallel",)),
        name="fox_prep",
    )(hq, hk, hgate, P['fox_q_g'].reshape(1, FOX_HD), P['fox_k_g'].reshape(1, FOX_HD), gate_bias)


def _fox_prep_prompt_body(q_ref, k_ref, v_ref, gt_ref, qg_ref, kg_ref, gb_ref, tri_ref,
                          qa_ref, kn_ref, vc_ref, kt_ref, va_ref, lf_ref, carry):
    i = pl.program_id(1)
    tm = q_ref.shape[0]
    HD = FOX_HD

    @pl.when(i == 0)
    def _():
        carry[...] = jnp.zeros_like(carry)

    lf = _log_sigmoid(gt_ref[...] + gb_ref[...])
    lf_ref[...] = lf
    c = _exact_left_dot(tri_ref[...], lf) + carry[...]
    carry[...] = c[tm - 1:tm, :]
    c2t = (c * (-LOG2E)).T
    lane = lax.broadcasted_iota(jnp.int32, (tm, HD), 1)
    q_ones = jnp.where(lane < 3, 1.0, 0.0).astype(BF16)
    v_ones = jnp.where(lane < 1, 1.0, 0.0).astype(BF16)
    row16 = lax.broadcasted_iota(jnp.int32, (16, tm), 0)
    q, k, v = q_ref[...], k_ref[...], v_ref[...]
    for h in range(FOX_HEADS):
        hs = slice(h * HD, (h + 1) * HD)
        qh, kh = q[:, hs], k[:, hs]
        qh = qh * lax.rsqrt(jnp.mean(qh * qh, axis=-1, keepdims=True) + EPS) * qg_ref[...]
        kh = kh * lax.rsqrt(jnp.mean(kh * kh, axis=-1, keepdims=True) + EPS) * kg_ref[...]
        qa_ref[:, 2 * h * HD:(2 * h + 1) * HD] = (qh * (HD ** -0.5 * LOG2E)).astype(BF16)
        qa_ref[:, (2 * h + 1) * HD:(2 * h + 2) * HD] = q_ones
        kn_ref[pl.ds(h, tm, stride=FOX_HEADS), :] = kh
        vc_ref[pl.ds(h, tm, stride=FOX_HEADS), :] = v[:, hs]
        kt_ref[0, h, 0, 0:HD, :] = kh.T.astype(BF16)
        bias = c2t[2 * MLSTM_HEADS + h:2 * MLSTM_HEADS + h + 1, :]
        hi = bias.astype(BF16).astype(F32)
        mid = (bias - hi).astype(BF16).astype(F32)
        lo = bias - hi - mid
        blk = jnp.where(row16 == 0, hi, jnp.where(row16 == 1, mid, jnp.where(row16 == 2, lo, 0.0)))
        kt_ref[0, h, 0, HD:HD + 16, :] = blk.astype(BF16)
        kt_ref[0, h, 0, HD + 16:2 * HD, :] = jnp.zeros((HD - 16, tm), BF16)
        va_ref[:, 2 * h * HD:(2 * h + 1) * HD] = v[:, hs].astype(BF16)
        va_ref[:, (2 * h + 1) * HD:(2 * h + 2) * HD] = v_ones


def _fox_prep_prompt(hq, hk, hv, hgate, P, gate_bias, B, T):
    M, W = hq.shape
    H, HD = FOX_HEADS, FOX_HD
    tm = _tile(T, FOX_TQ)
    nT = T // tm
    idx = np.arange(tm)
    tri = jnp.asarray(idx[:, None] >= idx[None, :], BF16)
    row = lambda b, i: (b * nT + i, 0)
    fix = lambda b, i: (0, 0)
    return pl.pallas_call(
        _fox_prep_prompt_body,
        grid=(B, nT),
        in_specs=[pl.BlockSpec((tm, W), row), pl.BlockSpec((tm, W), row), pl.BlockSpec((tm, W), row),
                  pl.BlockSpec((tm, 128), row),
                  pl.BlockSpec((1, HD), fix), pl.BlockSpec((1, HD), fix), pl.BlockSpec((1, 128), fix),
                  pl.BlockSpec((tm, tm), fix)],
        out_specs=[pl.BlockSpec((tm, 2 * W), row), pl.BlockSpec((tm * H, HD), row), pl.BlockSpec((tm * H, HD), row),
                   pl.BlockSpec((1, H, 1, 2 * HD, tm), lambda b, i: (b, 0, i, 0, 0)),
                   pl.BlockSpec((tm, 2 * W), row), pl.BlockSpec((tm, 128), row)],
        out_shape=[jax.ShapeDtypeStruct((M, 2 * W), BF16), jax.ShapeDtypeStruct((M * H, HD), F32),
                   jax.ShapeDtypeStruct((M * H, HD), F32),
                   jax.ShapeDtypeStruct((B, H, nT, 2 * HD, tm), BF16),
                   jax.ShapeDtypeStruct((M, 2 * W), BF16), jax.ShapeDtypeStruct((M, 128), F32)],
        scratch_shapes=[pltpu.VMEM((1, 128), F32)],
        compiler_params=_cparams(("parallel", "arbitrary")),
        name="fox_prep_prompt",
    )(hq, hk, hv, hgate, P['fox_q_g'].reshape(1, HD), P['fox_k_g'].reshape(1, HD), gate_bias, tri)


def _fox_prompt_body(q_ref, kt_ref, va_ref, o_ref, sa_scr, sb_scr, m_scr, acc_scr, *, tq):
    qi = pl.program_id(2)
    nq = pl.num_programs(2)
    m_scr[...] = jnp.full_like(m_scr, NEG_INF)
    acc_scr[...] = jnp.zeros_like(acc_scr)

    def scores(s_ref, kj, qt=None):
        q0 = pl.multiple_of((qi if qt is None else qt) * tq, tq)
        s_ref[...] = jnp.dot(q_ref[0, pl.ds(q0, tq), :], kt_ref[0, 0, kj], preferred_element_type=F32)

    def update(s_ref, kj, masked):
        vb = va_ref[0, pl.ds(pl.multiple_of(kj * tq, tq), tq), :]
        s = s_ref[...]
        if masked:
            ii = lax.broadcasted_iota(jnp.int32, (tq, tq), 0)
            jj = lax.broadcasted_iota(jnp.int32, (tq, tq), 1)
            s = jnp.where(jj <= ii, s, NEG_INF)
        m_old = m_scr[...]
        m_new = jnp.maximum(m_old, jnp.max(s, axis=-1, keepdims=True))
        p = jnp.exp2(s - m_new)
        acc_scr[...] = jnp.exp2(m_old - m_new) * acc_scr[...] + jnp.dot(p.astype(BF16), vb, preferred_element_type=F32)
        m_scr[...] = m_new

    @pl.when(qi == 0)
    def _():
        scores(sa_scr, 0)

    nxt = jnp.minimum(qi + 1, nq - 1)

    def run(first, second):
        def body(j, carry):
            scores(second, 2 * j + 1)
            update(first, 2 * j, False)
            scores(first, 2 * j + 2)
            update(second, 2 * j + 1, False)
            return carry

        lax.fori_loop(0, lax.shift_right_logical(qi, 1), body, 0)

        @pl.when((qi & 1) == 0)
        def _():
            scores(second, 0, nxt)
            update(first, qi, True)

        @pl.when((qi & 1) == 1)
        def _():
            scores(second, qi)
            update(first, qi - 1, False)
            scores(first, 0, nxt)
            update(second, qi, True)

    starts_in_a = (lax.shift_right_logical(qi + 1, 1) & 1) == 0

    @pl.when(starts_in_a)
    def _():
        run(sa_scr, sb_scr)

    @pl.when(jnp.logical_not(starts_in_a))
    def _():
        run(sb_scr, sa_scr)

    acc = acc_scr[...]
    o_ref[0] = acc[:, :FOX_HD] / acc[:, FOX_HD:FOX_HD + 1]


def _fox_prompt(qa, kt, va, B, T):
    H, HD = FOX_HEADS, FOX_HD
    W = H * HD
    tq = kt.shape[-1]
    nQ = T // tq
    out = pl.pallas_call(
        functools.partial(_fox_prompt_body, tq=tq),
        grid=(B, H, nQ),
        in_specs=[pl.BlockSpec((1, T, 2 * HD), lambda b, h, i: (b, 0, h), pipeline_mode=pl.Buffered(1)),
                  pl.BlockSpec((1, 1, nQ, 2 * HD, tq), lambda b, h, i: (b, h, 0, 0, 0),
                               pipeline_mode=pl.Buffered(1)),
                  pl.BlockSpec((1, T, 2 * HD), lambda b, h, i: (b, 0, h), pipeline_mode=pl.Buffered(1))],
        out_specs=pl.BlockSpec((1, tq, HD), lambda b, h, i: (b, i, h)),
        out_shape=jax.ShapeDtypeStruct((B, T, W), F32),
        scratch_shapes=[pltpu.VMEM((tq, tq), F32), pltpu.VMEM((tq, tq), F32),
                        pltpu.VMEM((tq, 1), F32), pltpu.VMEM((tq, 2 * HD), F32)],
        compiler_params=_cparams(("parallel", "parallel", "arbitrary")),
        name="fox_prompt",
    )(qa.reshape(B, T, 2 * W), kt, va.reshape(B, T, 2 * W))
    return out.reshape(B * T, W)


def _fox_sample_body(q_ref, kc_ref, vc_ref, kn_ref, vn_ref, cq_ref, ckc_ref, ckn_ref, o_ref, *, T, past):
    H, HD = FOX_HEADS, FOX_HD
    q = q_ref[0]
    kn, vn = kn_ref[0], vn_ref[0]
    ii = lax.broadcasted_iota(jnp.int32, (T, T), 0)
    jj = lax.broadcasted_iota(jnp.int32, (T, T), 1)
    outs = []
    for h in range(H):
        cs = slice(h * HD, (h + 1) * HD)
        kc = kc_ref[0, pl.ds(h, past, stride=H), :]
        vc = vc_ref[0, pl.ds(h, past, stride=H), :]
        cq = cq_ref[0, h]
        s1 = _dot_nt(q[:, cs], kc) + (cq - ckc_ref[0, h])
        s2 = _dot_nt(q[:, cs], kn[:, cs]) + (cq - ckn_ref[0, h])
        s2 = jnp.where(jj <= ii, s2, NEG_INF)
        m = jnp.maximum(jnp.max(s1, axis=-1, keepdims=True), jnp.max(s2, axis=-1, keepdims=True))
        p1 = jnp.exp(s1 - m)
        p2 = jnp.exp(s2 - m)
        den = jnp.sum(p1, axis=-1, keepdims=True) + jnp.sum(p2, axis=-1, keepdims=True)
        outs.append((_dot(p1, vc) + _dot(p2, vn[:, cs])) / den)
    o_ref[0] = jnp.concatenate(outs, axis=1)


def _fox_sample(qn, kn, vn, k_cache, v_cache, c_all, B, T):
    H, HD = FOX_HEADS, FOX_HD
    W = H * HD
    past = k_cache.shape[1]
    ct = jnp.transpose(c_all, (0, 2, 1))
    cq = ct[:, :, past:].reshape(B, H, T, 1)
    ckc = ct[:, :, :past].reshape(B, H, 1, past)
    ckn = ct[:, :, past:].reshape(B, H, 1, T)
    b3 = lambda b: (b, 0, 0)
    b4 = lambda b: (b, 0, 0, 0)
    out = pl.pallas_call(
        functools.partial(_fox_sample_body, T=T, past=past),
        grid=(B,),
        in_specs=[pl.BlockSpec((1, T, W), b3), pl.BlockSpec((1, past * H, HD), b3), pl.BlockSpec((1, past * H, HD), b3),
                  pl.BlockSpec((1, T, W), b3), pl.BlockSpec((1, T, W), b3),
                  pl.BlockSpec((1, H, T, 1), b4), pl.BlockSpec((1, H, 1, past), b4), pl.BlockSpec((1, H, 1, T), b4)],
        out_specs=pl.BlockSpec((1, T, W), b3),
        out_shape=jax.ShapeDtypeStruct((B, T, W), F32),
        compiler_params=_cparams(("parallel",)),
        name="fox_sample",
    )(qn.reshape(B, T, W), k_cache.reshape(B, past * H, HD), v_cache.reshape(B, past * H, HD),
      kn.reshape(B, T, W), vn.reshape(B, T, W), cq, ckc, ckn)
    return out.reshape(B * T, W)


def _mem_attn_body(x_ref, mk_ref, mv_ref, wq_ref, wo_ref, g_ref, b_ref, o_ref, *, prec):
    x = x_ref[...]
    q = _wdot(x, wq_ref, prec)
    nseq = mk_ref.shape[0]
    rows = x.shape[0] // nseq
    per_seq = []
    for j in range(nseq):
        qj = q[j * rows:(j + 1) * rows]
        outs = []
        for h in range(MEM_HEADS):
            cs = slice(h * MEM_HD, (h + 1) * MEM_HD)
            mkh, mvh = mk_ref[j, :, cs], mv_ref[j, :, cs]
            s = _dot_nt(qj[:, cs], mkh, prec) * (MEM_HD ** -0.5)
            m = jnp.max(s, axis=-1, keepdims=True)
            p = jnp.exp(s - m)
            outs.append(_dot(p, mvh, prec) / jnp.sum(p, axis=-1, keepdims=True))
        per_seq.append(jnp.concatenate(outs, axis=1))
    o = per_seq[0] if nseq == 1 else jnp.concatenate(per_seq, axis=0)
    att = _wdot(o, wo_ref, prec)
    y = _layer_norm(ALPHA * x + att, g_ref[...], b_ref[...])
    o_ref[...] = y


def _mem_attn_ln(x, mk, mv, l, wq, wo, g, bias, B, T, prec=False):
    D = D_MODEL
    Mm = mk.shape[1]
    tm = _tile(T, ROW_TILE_WIDE)
    nT = T // tm
    nseq = MEM_SEQS if (nT == 1 and B % MEM_SEQS == 0) else 1
    row = lambda b, i: (b * nT + i, 0)
    fix = lambda b, i: (0, 0)
    mem_spec = pl.BlockSpec((nseq, Mm, D), lambda b, i: (l * (B // nseq) + b, 0, 0))
    return pl.pallas_call(
        functools.partial(_mem_attn_body, prec=prec),
        grid=(B // nseq, nT),
        in_specs=[pl.BlockSpec((nseq * tm, D), row), mem_spec, mem_spec,
                  _wspec(wq), _wspec(wo),
                  pl.BlockSpec((1, D), fix), pl.BlockSpec((1, D), fix)],
        out_specs=pl.BlockSpec((nseq * tm, D), row),
        out_shape=jax.ShapeDtypeStruct((B * T, D), F32),
        compiler_params=_cparams(("parallel", "parallel")),
        name="mem_attn_ln",
    )(x, mk, mv, wq, wo, g.reshape(1, D), bias.reshape(1, D))


def _router_body(x_ref, w_ref, b_ref, tri_ref, o_ref, cnt_ref, cnt_scr):
    x = x_ref[...]
    w = w_ref[...]
    x1, x2, x3 = _split3(x)
    w1, w2, w3 = _split3(w)
    nt = lambda a, c: lax.dot_general(a, c, _NT, preferred_element_type=F32)
    logits = (nt(w1, x1) + (nt(w1, x2) + nt(w2, x1)) + (nt(w1, x3) + nt(w2, x2) + nt(w3, x1))) + b_ref[...]
    m = jnp.max(logits, axis=0, keepdims=True)
    e = jnp.exp(logits - m)
    p = e / jnp.sum(e, axis=0, keepdims=True)
    rows = [p[j:j + 1, :] for j in range(N_EXPERTS)]
    best = None
    sel = None
    for g in range(N_GROUPS):
        a, b, c, d = rows[4 * g:4 * g + 4]
        top2 = jnp.maximum(jnp.maximum(jnp.maximum(a + b, a + c), jnp.maximum(a + d, b + c)),
                           jnp.maximum(b + d, c + d))
        if g == 0:
            best, sel = top2, jnp.zeros_like(top2, dtype=jnp.int32)
        else:
            upd = top2 > best
            sel = jnp.where(upd, g, sel)
            best = jnp.maximum(best, top2)
    pin = []
    for kk in range(EXPERTS_PER_GROUP):
        v = rows[kk]
        for g in range(1, N_GROUPS):
            v = jnp.where(sel == g, rows[4 * g + kk], v)
        pin.append(v)
    v1, i1 = pin[0], jnp.zeros_like(sel)
    for kk in range(1, EXPERTS_PER_GROUP):
        upd = pin[kk] > v1
        i1 = jnp.where(upd, kk, i1)
        v1 = jnp.maximum(v1, pin[kk])
    v2, i2 = None, None
    for kk in range(EXPERTS_PER_GROUP):
        cand = jnp.where(i1 == kk, -1.0, pin[kk])
        if v2 is None:
            v2, i2 = cand, jnp.zeros_like(sel)
        else:
            upd = cand > v2
            i2 = jnp.where(upd, kk, i2)
            v2 = jnp.maximum(v2, cand)
    tot = v1 + v2
    e1 = sel * EXPERTS_PER_GROUP + i1
    e2 = sel * EXPERTS_PER_GROUP + i2
    @pl.when(pl.program_id(0) == 0)
    def _():
        cnt_scr[...] = jnp.zeros_like(cnt_scr)

    eidx = lax.broadcasted_iota(jnp.int32, logits.shape, 0)
    oh1 = jnp.where(eidx == e1, 1.0, 0.0)
    oh2 = jnp.where(eidx == e2, 1.0, 0.0)
    oh = oh1 + oh2
    base = cnt_scr[...] + jnp.dot(oh.astype(BF16), tri_ref[...], preferred_element_type=F32)
    r1 = jnp.sum(oh1 * base, axis=0, keepdims=True)
    r2 = jnp.sum(oh2 * base, axis=0, keepdims=True)
    cnt = cnt_scr[...] + jnp.sum(oh, axis=1, keepdims=True)
    cnt_scr[...] = cnt
    cnt_ref[...] = jnp.broadcast_to(cnt, cnt_ref.shape)
    zero = jnp.zeros_like(v1)
    o_ref[...] = jnp.concatenate([e1.astype(F32), e2.astype(F32), v1 / tot, v2 / tot, r1, r2, zero, zero], axis=0)


def _router(x, w_router, b_router):
    M, D = x.shape
    tm = _tile(M, ROW_TILE)
    idx = np.arange(tm)
    tri = jnp.asarray(idx[:, None] < idx[None, :], BF16)
    r, cnt = pl.pallas_call(
        _router_body,
        grid=(M // tm,),
        in_specs=[pl.BlockSpec((tm, D), lambda i: (i, 0)),
                  pl.BlockSpec((N_EXPERTS, D), lambda i: (0, 0)),
                  pl.BlockSpec((N_EXPERTS, 1), lambda i: (0, 0)),
                  pl.BlockSpec((tm, tm), lambda i: (0, 0))],
        out_specs=[pl.BlockSpec((8, tm), lambda i: (0, i)), pl.BlockSpec((N_EXPERTS, 128), lambda i: (0, 0))],
        out_shape=[jax.ShapeDtypeStruct((8, M), F32), jax.ShapeDtypeStruct((N_EXPERTS, 128), F32)],
        scratch_shapes=[pltpu.VMEM((N_EXPERTS, 1), F32)],
        compiler_params=_cparams(("arbitrary",)),
        name="router",
    )(x, w_router.T, b_router.reshape(N_EXPERTS, 1), tri)
    return r, cnt[:, 0].astype(jnp.int32)


def _expert_body(be_ref, x_ref, wg_ref, wu_ref, wd_ref, o_ref, wg_s, wu_s, wd_s):
    i = pl.program_id(0)
    prev = be_ref[jnp.maximum(i - 1, 0)]

    @pl.when((i == 0) | (be_ref[i] != prev))
    def _():
        wg_s[...] = wg_ref[0, 0].astype(BF16)
        wu_s[...] = wu_ref[0, 0].astype(BF16)
        wd_s[...] = wd_ref[0, 0].astype(BF16)

    x = x_ref[...].astype(BF16)
    hg = jnp.dot(x, wg_s[...], preferred_element_type=F32)
    hu = jnp.dot(x, wu_s[...], preferred_element_type=F32)
    hb = (_silu(hg) * hu).astype(BF16)
    o_ref[...] = jnp.dot(hb, wd_s[...], preferred_element_type=F32)


def _experts(xg, blk_exp, wg, wu, wd, l):
    n_blocks = blk_exp.shape[0]
    blk = xg.shape[0] // n_blocks
    D, DE = D_MODEL, D_EXPERT
    grid_spec = pltpu.PrefetchScalarGridSpec(
        num_scalar_prefetch=1,
        grid=(n_blocks,),
        in_specs=[pl.BlockSpec((blk, D), lambda i, be: (i, 0)),
                  pl.BlockSpec((1, 1, D, DE), lambda i, be: (l, be[i], 0, 0)),
                  pl.BlockSpec((1, 1, D, DE), lambda i, be: (l, be[i], 0, 0)),
                  pl.BlockSpec((1, 1, DE, D), lambda i, be: (l, be[i], 0, 0))],
        out_specs=pl.BlockSpec((blk, D), lambda i, be: (i, 0)),
        scratch_shapes=[pltpu.VMEM((D, DE), BF16), pltpu.VMEM((D, DE), BF16), pltpu.VMEM((DE, D), BF16)],
    )
    return pl.pallas_call(
        _expert_body,
        grid_spec=grid_spec,
        out_shape=jax.ShapeDtypeStruct((n_blocks * blk, D), F32),
        compiler_params=_cparams(("arbitrary",)),
        name="experts",
    )(blk_exp, xg, wg, wu, wd)


def _combine_ln_body(x_ref, y0_ref, y1_ref, gt_ref, g_ref, b_ref, o_ref):
    gt = gt_ref[...]
    ffn = y0_ref[...] * gt[:, 0:1] + y1_ref[...] * gt[:, 1:2]
    o_ref[...] = _layer_norm(ALPHA * x_ref[...] + ffn, g_ref[...], b_ref[...])


def _combine_ln(x, y0, y1, gates, g, bias):
    M, D = x.shape
    tm = _tile(M, ROW_TILE_WIDE)
    row = lambda i: (i, 0)
    fix = lambda i: (0, 0)
    return pl.pallas_call(
        _combine_ln_body,
        grid=(M // tm,),
        in_specs=[pl.BlockSpec((tm, D), row), pl.BlockSpec((tm, D), row), pl.BlockSpec((tm, D), row),
                  pl.BlockSpec((tm, 2), row), pl.BlockSpec((1, D), fix), pl.BlockSpec((1, D), fix)],
        out_specs=pl.BlockSpec((tm, D), row),
        out_shape=jax.ShapeDtypeStruct((M, D), F32),
        compiler_params=_cparams(("parallel",)),
        name="combine_ln",
    )(x, y0, y1, gates, g.reshape(1, D), bias.reshape(1, D))


def _moe_ln(x, P, l):
    M = x.shape[0]
    r, counts = _router(x, P['w_router'], P['b_router'])
    e = r[0:2].astype(jnp.int32).T.reshape(-1)
    gates = r[2:4].T
    rank = r[4:6].astype(jnp.int32).T.reshape(-1)
    blk = MOE_BLOCK if 2 * M >= N_EXPERTS * MOE_BLOCK else MOE_BLOCK_SMALL
    padded = (counts + blk - 1) // blk * blk
    p_ends = jnp.cumsum(padded)
    p_starts = p_ends - padded
    dest = p_starts[e] + rank
    n_blocks = -(-2 * M // blk) + N_EXPERTS
    blk_start = jnp.arange(n_blocks, dtype=jnp.int32) * blk
    blk_exp = jnp.minimum(jnp.sum((p_ends[None, :] <= blk_start[:, None]).astype(jnp.int32), axis=1), N_EXPERTS - 1)
    order = jnp.argsort(e, stable=True).astype(jnp.int32)
    starts = jnp.cumsum(counts) - counts
    slot = jnp.arange(n_blocks * blk, dtype=jnp.int32)
    slot_e = jnp.repeat(blk_exp, blk)
    r_in_e = slot - p_starts[slot_e]
    src = order[jnp.clip(starts[slot_e] + r_in_e, 0, 2 * M - 1)] // 2
    slot_tok = jnp.where(r_in_e < counts[slot_e], src, slot % M)
    xg = x[slot_tok]
    yb = _experts(xg, blk_exp, P['w_exp_gate'], P['w_exp_up'], P['w_exp_down'], l)
    d2 = dest.reshape(M, 2)
    return _combine_ln(x, yb[d2[:, 0]], yb[d2[:, 1]], gates, P['ln_g'][l, 2], P['ln_b'][l, 2])


def _trunk(x, pos0, mem_k, mem_v, states, P, Wc, Wf, is_prompt, prec0):
    B, T, D = x.shape
    ret_S, rwkv_S, shift, mC, mn, mm, conv, fk, fv, flf = states
    xf = x.reshape(B * T, D)
    W0 = Wf if prec0 else Wc
    hq, hk, hv, hg, hb = _project(xf, W0['w_in0'], (256, 256, 512, 512, RWKV_PROJ), prec0)
    L_ret = _tile(T, RET_CHUNK)
    out_a, ret_S = _retention(hq, hk, hv, hg, ret_S, P['ret_gn_g'], B, T, pos0, L_ret, prec0)
    hb3 = hb.reshape(B, T, RWKV_PROJ)
    new_shift = hb3[:, -1:]
    L_rwkv = min(RWKV_CHUNK, T)
    if T > L_rwkv:
        assert not prec0
        pre = _rwkv_prep(hb, shift, P, L_rwkv, False, seqs=(B, T))
        out_b, rwkv_S = _rwkv_scan(pre, rwkv_S, P['rwkv_gn_g'], P['rwkv_gn_b'], B, T, L_rwkv)
    else:
        prev = jnp.concatenate([shift, hb3[:, :-1]], axis=1).reshape(B * T, RWKV_PROJ)
        pre = _rwkv_prep(hb, prev, P, T, prec0)
        out_b, rwkv_S = _rwkv_single_chunk(pre, rwkv_S, P['rwkv_gn_g'], P['rwkv_gn_b'], B, T, prec0)
    xf = _out_proj_ln(xf, out_a, out_b, W0['w_out0a'], W0['w_out0b'], P['ln_g'][0, 0], P['ln_b'][0, 0], prec0)
    xf = _mem_attn_ln(xf, mem_k, mem_v, 0, W0['w_mem_q'][0], W0['w_mem_o'][0], P['ln_g'][0, 1], P['ln_b'][0, 1],
                      B, T, prec0)
    xf = _moe_ln(xf, P, 0)
    hqk, hv1, ho, fq, fkk, fvv, hgate = _project(xf, Wc['w_in1'], (512, 512, 512, 512, 512, 512, 128))
    out_c, mC, mn, mm, conv = _mlstm(hqk, hv1, ho, hgate, conv, mC, mn, mm, P, Wc['gate_bias'], B, T, L_ret)
    if is_prompt:
        qa, kn, vc, kt, va, lf = _fox_prep_prompt(fq, fkk, fvv, hgate, P, Wc['gate_bias'], B, T)
        out_d = _fox_prompt(qa, kt, va, B, T)
        logf = lf[:, 2 * MLSTM_HEADS:2 * MLSTM_HEADS + FOX_HEADS].reshape(B, T, FOX_HEADS)
    else:
        qn, kn, lf = _fox_prep(fq, fkk, hgate, P, Wc['gate_bias'])
        vc = fvv
        logf = lf[:, 2 * MLSTM_HEADS:2 * MLSTM_HEADS + FOX_HEADS].reshape(B, T, FOX_HEADS)
        c_all = jnp.cumsum(jnp.concatenate([flf, logf], axis=1), axis=1)
        out_d = _fox_sample(qn, kn, fvv, fk, fv, c_all, B, T)
    xf = _out_proj_ln(xf, out_c, out_d, Wc['w_out1a'], Wc['w_out1b'], P['ln_g'][1, 0], P['ln_b'][1, 0])
    xf = _mem_attn_ln(xf, mem_k, mem_v, 1, Wc['w_mem_q'][1], Wc['w_mem_o'][1], P['ln_g'][1, 1], P['ln_b'][1, 1],
                      B, T)
    xf = _moe_ln(xf, P, 1)
    fk_new = kn.reshape(B, T, FOX_HEADS, FOX_HD)
    fv_new = vc.reshape(B, T, FOX_HEADS, FOX_HD)
    return (xf.reshape(B, T, D), ret_S, rwkv_S, new_shift, mC, mn, mm, conv, fk_new, fv_new, logf)


def kernel(x_prompt, x_sample, mem_prompt, state_ret, state_rwkv, cache_rwkv_shift, state_mlstm_c, state_mlstm_n,
           state_mlstm_m, cache_mlstm_conv, cache_fox_k, cache_fox_v, cache_fox_logf, cache_mem_k, cache_mem_v,
           w_in0, ret_gn_g, rwkv_mu, rwkv_w0, rwkv_w2, rwkv_a0, rwkv_a2, rwkv_g2, rwkv_k_k, rwkv_k_a, rwkv_r_k,
           rwkv_gn_g, rwkv_gn_b, w_out0, w_in1, mlstm_conv_w, mlstm_conv_b, mlstm_b_i, mlstm_b_f, mlstm_gn_g,
           fox_q_g, fox_k_g, fox_b_f, w_out1, w_mem_q, w_mem_k, w_mem_v, w_mem_o, w_router, b_router,
           w_exp_gate, w_exp_up, w_exp_down, ln_g, ln_b):
    P = dict(ret_gn_g=ret_gn_g, rwkv_mu=rwkv_mu, rwkv_w0=rwkv_w0, rwkv_w2=rwkv_w2, rwkv_a0=rwkv_a0,
             rwkv_a2=rwkv_a2, rwkv_g2=rwkv_g2, rwkv_k_k=rwkv_k_k, rwkv_k_a=rwkv_k_a, rwkv_r_k=rwkv_r_k,
             rwkv_gn_g=rwkv_gn_g, rwkv_gn_b=rwkv_gn_b, mlstm_conv_w=mlstm_conv_w, mlstm_conv_b=mlstm_conv_b,
             mlstm_gn_g=mlstm_gn_g, fox_q_g=fox_q_g, fox_k_g=fox_k_g, w_router=w_router, b_router=b_router,
             w_exp_gate=w_exp_gate, w_exp_up=w_exp_up, w_exp_down=w_exp_down, ln_g=ln_g, ln_b=ln_b)
    B, M = mem_prompt.shape[0], mem_prompt.shape[1]
    D = D_MODEL
    H = MLSTM_HEADS
    ret_proj = 2 * RET_HEADS * RET_DK + 2 * RET_HEADS * RET_DV
    qk1, w1 = 2 * H * MLSTM_DK, H * MLSTM_DV
    off = qk1 + w1
    mlstm_proj = off + 2 * H + w1
    fw = FOX_HEADS * FOX_HD
    gate_cols = jnp.concatenate([w_in1[:, off:off + 2 * H], w_in1[:, mlstm_proj + 3 * fw:],
                                 jnp.zeros((D, 128 - 2 * H - FOX_HEADS), F32)], axis=1)
    w_in1_c = jnp.concatenate([w_in1[:, :off], w_in1[:, off + 2 * H:mlstm_proj],
                               w_in1[:, mlstm_proj:mlstm_proj + 3 * fw], gate_cols], axis=1)
    gate_bias = jnp.concatenate([mlstm_b_i, mlstm_b_f, fox_b_f, jnp.zeros((128 - 2 * H - FOX_HEADS,), F32)]).reshape(1, 128)
    wa0, wb0 = w_out0[:RET_HEADS * RET_DV], w_out0[RET_HEADS * RET_DV:]
    Wf = dict(w_in0=_hi_lo(w_in0), w_out0a=_hi_lo(wa0), w_out0b=_hi_lo(wb0),
              w_mem_q=[_hi_lo(w_mem_q[0])], w_mem_o=[_hi_lo(w_mem_o[0])])
    Wc = dict(w_in0=w_in0.astype(BF16), w_in1=w_in1_c.astype(BF16), gate_bias=gate_bias,
              w_out0a=wa0.astype(BF16), w_out0b=wb0.astype(BF16),
              w_out1a=w_out1[:w1].astype(BF16), w_out1b=w_out1[w1:].astype(BF16),
              w_mem_q=w_mem_q.astype(BF16), w_mem_o=w_mem_o.astype(BF16))
    w_kv = jnp.concatenate([w_mem_k, w_mem_v], axis=0)
    memkv = _mem_project(mem_prompt.reshape(B * M, D), jnp.swapaxes(_hi_lo(w_kv), 0, 1))
    p_mem_k = memkv[:DEPTH].reshape(DEPTH * B, M, D)
    p_mem_v = memkv[DEPTH:].reshape(DEPTH * B, M, D)
    zeros = lambda *s: jnp.zeros(s, F32)
    prompt_states = (zeros(B, RET_HEADS, RET_DK, RET_DV), zeros(B, RWKV_HEADS, RWKV_HD, RWKV_HD),
                     zeros(B, 1, RWKV_PROJ), zeros(B, H, MLSTM_DK, MLSTM_DV), zeros(B, H, MLSTM_DK), zeros(B, H),
                     zeros(B, MLSTM_CONV - 1, qk1), None, None, None)
    p_out = _trunk(x_prompt, 0, p_mem_k, p_mem_v, prompt_states, P, Wc, Wf, True, False)
    DB = x_sample.shape[0]
    sample_states = (state_ret, state_rwkv, cache_rwkv_shift, state_mlstm_c, state_mlstm_n, state_mlstm_m,
                     cache_mlstm_conv, cache_fox_k, cache_fox_v, cache_fox_logf)
    s_out = _trunk(x_sample, cache_fox_k.shape[1], cache_mem_k.reshape(DEPTH * DB, M, D),
                   cache_mem_v.reshape(DEPTH * DB, M, D), sample_states, P, Wc, Wf, False, True)
    mem_shape = (DEPTH, B, M, MEM_HEADS, MEM_HD)
    return ((p_out[0], s_out[0]) + p_out[1:] + (p_mem_k.reshape(mem_shape), p_mem_v.reshape(mem_shape)) + s_out[1:])
```
